```python
import jax
import jax.numpy as jnp
from jax import lax
import numpy as np

D_MODEL = 2048
BATCH = 1
SEQ = 8192
DEPTH = 2

GRID_W = 64
CTX_LEN = 256
EPS = 1e-6
D_FF = 5632
N_MOD = 9
N_BRANCH = 4
BR_W = 512

HG_HEADS = 4
HG_DK = 128
HG_DV = 128
HG_CHUNK = 64

RW_HEADS = 8
RW_HD = 64
RW_W = RW_HEADS * RW_HD
RW_DECAY_LORA = 64
RW_AAA_LORA = 64
RW_GN_EPS = 64e-5

NA_HEADS = 8
NA_HD = 64
NA_WIN_R = 8
NA_WIN_C = 16

WA_HEADS = 8
WA_KV_HEADS = 2
WA_HD = 64
WA_WINDOW = 128
WA_BLOCK = 128
ROPE_BASE = 10000.0

HG_SPLIT = (HG_HEADS * HG_DK,) * 3 + (HG_HEADS * HG_DV,) * 2
RW_SPLIT = (RW_W,) * 4 + (RW_DECAY_LORA,) * 2 + (RW_AAA_LORA,) * 2
NA_SPLIT = (NA_HEADS * NA_HD,) * 3
WA_SPLIT = (WA_HEADS * WA_HD, WA_KV_HEADS * WA_HD, WA_KV_HEADS * WA_HD)
RW_COLS = sum(RW_SPLIT)
MIX_SPLIT = (sum(HG_SPLIT), RW_COLS, sum(NA_SPLIT), sum(WA_SPLIT), N_BRANCH * D_MODEL)
P_TOTAL = sum(MIX_SPLIT)

kernel_name = 'hybrid_gated_parallel_mixer_dit'


def split_cols(t, widths):
    out, start = [], 0
    for w in widths:
        out.append(t[..., start:start + w])
        start += w
    return out


def rmsnorm(x, g):
    x32 = x.astype(jnp.float32)
    y = x32 * lax.rsqrt(jnp.mean(x32 * x32, axis=-1, keepdims=True) + EPS)
    return (y * g.astype(jnp.float32)).astype(x.dtype)


def modulate(x, g, shift, scale):
    return rmsnorm(x, g) * (1.0 + scale[:, None]) + shift[:, None]


def swiglu_half(x, g, shift, scale, gate, wi, wo):
    a, b = split_cols(modulate(x, g, shift, scale) @ wi, (D_FF, D_FF))
    return x + 0.5 * gate[:, None] * ((jax.nn.silu(a) * b) @ wo)


def rope_half(t, pos):
    n = t.shape[-1] // 2
    inv = ROPE_BASE ** (-jnp.arange(n, dtype=jnp.float32) / n)
    ang = pos.astype(jnp.float32)[:, None] * inv[None, :]
    cos, sin = jnp.cos(ang)[None, :, None, :], jnp.sin(ang)[None, :, None, :]
    t32 = t.astype(jnp.float32)
    t1, t2 = t32[..., :n], t32[..., n:]
    return jnp.concatenate([t1 * cos - t2 * sin, t1 * sin + t2 * cos], -1).astype(t.dtype)


def axial_rope(t, rows, cols):
    h = t.shape[-1] // 2
    return jnp.concatenate([rope_half(t[..., :h], rows), rope_half(t[..., h:], cols)], -1)


def context_attention(qc, kc, vc, sink):
    L = kc.shape[1]
    s = jnp.einsum('blkgd,bmkd->bkglm', qc, kc).astype(jnp.float32)
    if sink is not None:
        s_sink = jnp.broadcast_to(sink.astype(jnp.float32)[None, :, :, None, None], s.shape[:-1] + (1,))
        s = jnp.concatenate([s, s_sink], -1)
    p = jax.nn.softmax(s, axis=-1)[..., :L].astype(vc.dtype)
    return jnp.einsum('bkglm,bmkd->blkgd', p, vc)


def gla_chunk_scan(S0, q, logf, k, v, reverse):
    if reverse:
        q, logf, k, v = [jnp.flip(t, axis=1) for t in (q, logf, k, v)]
    B, T, H, K = q.shape
    V = v.shape[-1]
    n = T // HG_CHUNK

    def to_chunks(t):
        return t.reshape(B, n, HG_CHUNK, H, t.shape[-1]).transpose(1, 0, 3, 2, 4)

    lower = jnp.tril(jnp.ones((HG_CHUNK, HG_CHUNK), dtype=bool))[:, :, None]

    def step(S, inp):
        qc, gc, kc, vc = inp
        b = jnp.cumsum(gc, axis=2)
        diff = b[:, :, :, None, :] - b[:, :, None, :, :]
        dec = jnp.exp(jnp.where(lower, diff, -jnp.inf))
        att = jnp.einsum('bhtk,bhsk,bhtsk->bhts', qc, kc, dec)
        o = jnp.einsum('bhts,bhsv->bhtv', att, vc) + jnp.einsum('bhtk,bhkv->bhtv', qc * jnp.exp(b), S)
        b_end = b[:, :, -1]
        S = jnp.exp(b_end)[..., None] * S + jnp.einsum('bhsk,bhsv->bhkv', kc * jnp.exp(b_end[:, :, None] - b), vc)
        return S, o

    S, o = lax.scan(step, S0, tuple(to_chunks(t) for t in (q, logf, k, v)))
    o = o.transpose(1, 0, 3, 2, 4).reshape(B, T, H, V)
    if reverse:
        o = jnp.flip(o, axis=1)
    return S, o


def hgrn2_branch(u, uc, lb_logits, layer, norm_w, need_ctx):
    lbs = []
    for d in range(2):
        cum = jnp.cumsum(jax.nn.softmax(lb_logits[d].astype(jnp.float32), axis=0), axis=0)
        lbs.append(cum[layer] - cum[0])

    def prep(t):
        B, T = t.shape[:2]
        q, f_fw, f_bw, i, g = split_cols(t.astype(jnp.float32), HG_SPLIT)
        heads = lambda a: a.reshape(B, T, HG_HEADS, -1)
        gates = []
        for fr, lb in ((f_fw, lbs[0]), (f_bw, lbs[1])):
            f = lb + (1.0 - lb) * jax.nn.sigmoid(fr)
            gates.append((heads(jnp.log(f)), heads(1.0 - f)))
        return heads(q), heads(i), g, gates

    def finish(o, g, dtype):
        B, T = g.shape[:2]
        o = rmsnorm(o, norm_w.reshape(HG_HEADS, HG_DV)).reshape(B, T, -1)
        return (o * jax.nn.silu(g)).astype(dtype)

    q_l, i_l, g_l, gates_l = prep(u)
    q_c, i_c, g_c, gates_c = prep(uc)
    S0 = jnp.zeros((u.shape[0], HG_HEADS, HG_DK, HG_DV), jnp.float32)
    outs_l, outs_c = [], []
    for d, reverse in enumerate((False, True)):
        S_c, o_c = gla_chunk_scan(S0, q_c, gates_c[d][0], gates_c[d][1], i_c, reverse)
        _, o_l = gla_chunk_scan(S_c, q_l, gates_l[d][0], gates_l[d][1], i_l, reverse)
        outs_l.append(o_l)
        outs_c.append(o_c)
    y = finish(outs_l[0] + outs_l[1], g_l, u.dtype)
    yc = finish(outs_c[0] + outs_c[1], g_c, uc.dtype) if need_ctx else None
    return y, yc


def token_shift(u, taps):
    up = jnp.pad(u, ((0, 0), (1, 1), (0, 0)))
    return taps[0] * up[:, :-2] + taps[1] * up[:, 1:-1] + taps[2] * up[:, 2:]


def rwkv7_prep(t, taps, w0, w2, a0, a2, k_k, k_a):
    B, T = t.shape[:2]
    t = token_shift(t, taps).astype(jnp.float32)
    r, k, v, g, wd_f, wd_b, ad_f, ad_b = split_cols(t, RW_SPLIT)
    heads = lambda a: a.reshape(B, T, RW_HEADS, RW_HD)
    kk = heads(k * k_k)
    kk = kk * lax.rsqrt(jnp.sum(kk * kk, axis=-1, keepdims=True) + EPS)
    dirs = []
    for d, (wd, ad) in enumerate(((wd_f, ad_f), (wd_b, ad_b))):
        log_w = -jax.nn.softplus(-(w0[d] + jnp.tanh(wd) @ w2[d])) - 0.5
        a = jax.nn.sigmoid(a0[d] + ad @ a2[d])
        k_d = k * (1.0 + (a - 1.0) * k_a)
        dirs.append((heads(jnp.exp(-jnp.exp(log_w))), heads(a) * kk, heads(k_d)))
    return heads(r), kk, heads(v), jax.nn.sigmoid(g), dirs


def rwkv7_scan(S0, r, decay, kk, akk, v, kd, reverse):
    def step(S, inp):
        r_t, w_t, kk_t, akk_t, v_t, k_t = inp
        sk = jnp.einsum('bhvk,bhk->bhv', S, kk_t)
        S = S * w_t[:, :, None, :] - sk[..., None] * akk_t[:, :, None, :] + v_t[..., None] * k_t[:, :, None, :]
        return S, jnp.einsum('bhvk,bhk->bhv', S, r_t)

    xs = tuple(jnp.swapaxes(t, 0, 1) for t in (r, decay, kk, akk, v, kd))
    S, o = lax.scan(step, S0, xs, reverse=reverse)
    return S, jnp.swapaxes(o, 0, 1)


def rwkv7_finish(o, prep, r_k, ln_w, ln_b, dtype):
    r, kk, v, g, dirs = prep
    B, T = g.shape[:2]
    mu = jnp.mean(o, axis=-1, keepdims=True)
    var = jnp.mean(jnp.square(o - mu), axis=-1, keepdims=True)
    o = (o - mu) * lax.rsqrt(var + RW_GN_EPS) * ln_w.reshape(RW_HEADS, RW_HD) + ln_b.reshape(RW_HEADS, RW_HD)
    kd_sum = dirs[0][2] + dirs[1][2]
    bonus = jnp.sum(r * kd_sum * r_k, axis=-1, keepdims=True) * v
    return ((o + bonus).reshape(B, T, -1) * g).astype(dtype)


def rwkv7_branch(u, uc, taps, w0, w2, a0, a2, k_k, k_a, r_k, ln_w, ln_b, need_ctx):
    p_l = rwkv7_prep(u, taps, w0, w2, a0, a2, k_k, k_a)
    p_c = rwkv7_prep(uc, taps, w0, w2, a0, a2, k_k, k_a)
    S0 = jnp.zeros((u.shape[0], RW_HEADS, RW_HD, RW_HD), jnp.float32)
    outs_l, outs_c = [], []
    for d, reverse in enumerate((False, True)):
        dec_c, akk_c, kd_c = p_c[4][d]
        dec_l, akk_l, kd_l = p_l[4][d]
        S_c, o_c = rwkv7_scan(S0, p_c[0], dec_c, p_c[1], akk_c, p_c[2], kd_c, reverse)
        _, o_l = rwkv7_scan(S_c, p_l[0], dec_l, p_l[1], akk_l, p_l[2], kd_l, reverse)
        outs_l.append(o_l)
        outs_c.append(o_c)
    y = rwkv7_finish(outs_l[0] + outs_l[1], p_l, r_k, ln_w, ln_b, u.dtype)
    yc = rwkv7_finish(outs_c[0] + outs_c[1], p_c, r_k, ln_w, ln_b, uc.dtype) if need_ctx else None
    return y, yc


def neighborhood_attention(q, k, v, kc, vc, rpb):
    B, T, H, d = q.shape
    rows = T // GRID_W
    wr = min(NA_WIN_R, rows)
    wc = NA_WIN_C
    qg = q.reshape(B, rows, GRID_W, H, d)
    kg = k.reshape(B, rows, GRID_W, H, d)
    vg = v.reshape(B, rows, GRID_W, H, d)
    col = jnp.arange(GRID_W)
    col_idx = jnp.clip(col - wc // 2, 0, GRID_W - wc)[:, None] + jnp.arange(wc)[None, :]
    col_bias_idx = col_idx - col[:, None] + (NA_WIN_C - 1)

    def one_row(i):
        rs = jnp.clip(i - wr // 2, 0, rows - wr)
        row_bias_idx = rs + jnp.arange(wr) - i + (NA_WIN_R - 1)
        kr = lax.dynamic_slice_in_dim(kg, rs, wr, axis=1)[:, :, col_idx]
        vr = lax.dynamic_slice_in_dim(vg, rs, wr, axis=1)[:, :, col_idx]
        qi = lax.dynamic_index_in_dim(qg, i, axis=1, keepdims=False)
        bias = rpb[:, row_bias_idx[:, None, None], col_bias_idx[None]]
        s_loc = jnp.einsum('bjhd,bajchd->bhjac', qi, kr) + bias.transpose(0, 2, 1, 3)[None]
        s_ctx = jnp.einsum('bjhd,blhd->bhjl', qi, kc)
        s = jnp.concatenate([s_loc.reshape(B, H, GRID_W, wr * wc), s_ctx], -1).astype(jnp.float32)
        p = jax.nn.softmax(s, axis=-1).astype(v.dtype)
        p_loc = p[..., :wr * wc].reshape(B, H, GRID_W, wr, wc)
        return jnp.einsum('bhjac,bajchd->bjhd', p_loc, vr) + jnp.einsum('bhjl,blhd->bjhd', p[..., wr * wc:], vc)

    out = lax.map(one_row, jnp.arange(rows))
    return out.transpose(1, 0, 2, 3, 4).reshape(B, T, H, d)


def na_branch(u, uc, qn, kn, rpb, need_ctx):
    def qkv(t):
        B, T = t.shape[:2]
        q, k, v = [a.reshape(B, T, NA_HEADS, NA_HD) for a in split_cols(t, NA_SPLIT)]
        return rmsnorm(q, qn) * NA_HD ** -0.5, rmsnorm(k, kn), v

    q, k, v = qkv(u)
    qc, kc, vc = qkv(uc)
    B, T = u.shape[:2]
    y = neighborhood_attention(q, k, v, kc, vc, rpb).reshape(B, T, -1)
    yc = context_attention(qc[:, :, :, None], kc, vc, None).reshape(B, uc.shape[1], -1) if need_ctx else None
    return y, yc


def window_attention(q, k, v, kc, vc, sink):
    B, T, Hkv, G, d = q.shape
    nb = T // WA_BLOCK
    qb = q.reshape(B, nb, WA_BLOCK, Hkv, G, d)

    def band(t):
        tp = jnp.pad(t, ((0, 0), (WA_BLOCK, WA_BLOCK), (0, 0), (0, 0))).reshape(B, nb + 2, WA_BLOCK, Hkv, d)
        return jnp.concatenate([tp[:, :-2], tp[:, 1:-1], tp[:, 2:]], axis=2)

    kb, vb = band(k), band(v)
    blk = jnp.arange(nb)[:, None, None] * WA_BLOCK
    qpos = blk + jnp.arange(WA_BLOCK)[None, :, None]
    kpos = blk - WA_BLOCK + jnp.arange(3 * WA_BLOCK)[None, None, :]
    valid = (jnp.abs(qpos - kpos) <= WA_WINDOW) & (kpos >= 0) & (kpos < T)
    s_loc = jnp.einsum('bnqkgd,bnskd->bnkgqs', qb, kb).astype(jnp.float32)
    s_loc = jnp.where(valid[None, :, None, None], s_loc, -jnp.inf)
    s_ctx = jnp.einsum('bnqkgd,blkd->bnkgql', qb, kc).astype(jnp.float32)
    s_sink = jnp.broadcast_to(sink.astype(jnp.float32)[None, None, :, :, None, None], s_loc.shape[:-1] + (1,))
    p = jax.nn.softmax(jnp.concatenate([s_loc, s_ctx, s_sink], -1), axis=-1).astype(v.dtype)
    n_loc = 3 * WA_BLOCK
    L = kc.shape[1]
    o = (jnp.einsum('bnkgqs,bnskd->bnqkgd', p[..., :n_loc], vb)
         + jnp.einsum('bnkgql,blkd->bnqkgd', p[..., n_loc:n_loc + L], vc))
    return o.reshape(B, T, Hkv, G, d)


def wa_branch(u, uc, qn, kn, sink, need_ctx):
    G = WA_HEADS // WA_KV_HEADS

    def qkv(t):
        B, T = t.shape[:2]
        q, k, v = split_cols(t, WA_SPLIT)
        q = rmsnorm(q.reshape(B, T, WA_HEADS, WA_HD), qn) * WA_HD ** -0.5
        k = rmsnorm(k.reshape(B, T, WA_KV_HEADS, WA_HD), kn)
        return q, k, v.reshape(B, T, WA_KV_HEADS, WA_HD)

    q, k, v = qkv(u)
    qc, kc, vc = qkv(uc)
    B, T = u.shape[:2]
    pos = jnp.arange(T)
    rows, cols = pos // GRID_W, pos % GRID_W
    q = axial_rope(q, rows, cols)
    k = axial_rope(k, rows, cols)
    sink_g = sink.reshape(WA_KV_HEADS, G)
    y = window_attention(q.reshape(B, T, WA_KV_HEADS, G, WA_HD), k, v, kc, vc, sink_g).reshape(B, T, -1)
    yc = None
    if need_ctx:
        L = uc.shape[1]
        yc = context_attention(qc.reshape(B, L, WA_KV_HEADS, G, WA_HD), kc, vc, sink_g).reshape(B, L, -1)
    return y, yc


def merge_branches(ys, gate_logits, w_branch, w_out):
    B, T = gate_logits.shape[:2]
    yb = jnp.stack(ys, axis=2)
    proj = jnp.einsum('btnw,nwd->btnd', yb, w_branch)
    gates = jax.nn.sigmoid(gate_logits.reshape(B, T, N_BRANCH, D_MODEL))
    return jnp.sum(gates * proj, axis=2) @ w_out


def token_mixing(p, pc, layer, hg_lb, hg_norm, rw_shift, rw_w0, rw_w2, rw_a0, rw_a2, rw_kk, rw_ka, rw_rk,
                 rw_ln_w, rw_ln_b, na_qn, na_kn, na_rpb, wa_qn, wa_kn, wa_sink, w_branch, w_out, need_ctx):
    hg, rw, na, wa, gl = split_cols(p, MIX_SPLIT)
    hg_c, rw_c, na_c, wa_c, gl_c = split_cols(pc, MIX_SPLIT)
    y_a, yc_a = hgrn2_branch(hg, hg_c, hg_lb, layer, hg_norm, need_ctx)
    y_b, yc_b = rwkv7_branch(rw, rw_c, rw_shift, rw_w0, rw_w2, rw_a0, rw_a2, rw_kk, rw_ka, rw_rk,
                             rw_ln_w, rw_ln_b, need_ctx)
    y_c, yc_c = na_branch(na, na_c, na_qn, na_kn, na_rpb, need_ctx)
    y_d, yc_d = wa_branch(wa, wa_c, wa_qn, wa_kn, wa_sink, need_ctx)
    y = merge_branches((y_a, y_b, y_c, y_d), gl, w_branch, w_out)
    yc = merge_branches((yc_a, yc_b, yc_c, yc_d), gl_c, w_branch, w_out) if need_ctx else None
    return y, yc


def setup_inputs(seed: int = 0) -> dict:
    key = jax.random.key(seed)
    ks = iter(jax.random.split(key, 40))
    nrm = lambda shape, s=1.0: s * jax.random.normal(next(ks), shape, jnp.float32)
    L, D = DEPTH, D_MODEL
    shift_base = jnp.array([0.25, 0.5, 0.25], dtype=jnp.float32)[None, :, None]
    return {
        'x': nrm((BATCH, SEQ, D)),
        'c': nrm((BATCH, D)),
        'ctx': nrm((BATCH, CTX_LEN, D)),
        'c_ctx': nrm((D,)),
        'ada_w': nrm((L, D, N_MOD * D), D ** -0.5),
        'ada_b': nrm((L, N_MOD * D), 0.02),
        'norm_ffn1': 1.0 + nrm((L, D), 0.1),
        'norm_mix': 1.0 + nrm((L, D), 0.1),
        'norm_ffn2': 1.0 + nrm((L, D), 0.1),
        'ffn1_wi': nrm((L, D, 2 * D_FF), D ** -0.5),
        'ffn1_wo': nrm((L, D_FF, D), D_FF ** -0.5),
        'ffn2_wi': nrm((L, D, 2 * D_FF), D ** -0.5),
        'ffn2_wo': nrm((L, D_FF, D), D_FF ** -0.5),
        'w_in': nrm((L, D, P_TOTAL), D ** -0.5),
        'hg_lb': nrm((2, L, HG_HEADS * HG_DK)),
        'hg_norm': 1.0 + nrm((L, HG_HEADS * HG_DV), 0.1),
        'rw_shift': shift_base + nrm((L, 3, RW_COLS), 0.05),
        'rw_w0': nrm((L, 2, RW_W), 0.5),
        'rw_w2': nrm((L, 2, RW_DECAY_LORA, RW_W), RW_DECAY_LORA ** -0.5),
        'rw_a0': nrm((L, 2, RW_W), 0.5),
        'rw_a2': nrm((L, 2, RW_AAA_LORA, RW_W), RW_AAA_LORA ** -0.5),
        'rw_kk': 0.85 + nrm((L, RW_W), 0.05),
        'rw_ka': 1.0 + nrm((L, RW_W), 0.05),
        'rw_rk': nrm((L, RW_HEADS, RW_HD), 0.1),
        'rw_ln_w': 1.0 + nrm((L, RW_W), 0.1),
        'rw_ln_b': nrm((L, RW_W), 0.02),
        'na_qn': 1.0 + nrm((L, NA_HD), 0.1),
        'na_kn': 1.0 + nrm((L, NA_HD), 0.1),
        'na_rpb': nrm((L, NA_HEADS, 2 * NA_WIN_R - 1, 2 * NA_WIN_C - 1), 0.1),
        'wa_qn': 1.0 + nrm((L, WA_HD), 0.1),
        'wa_kn': 1.0 + nrm((L, WA_HD), 0.1),
        'wa_sink': nrm((L, WA_HEADS)),
        'w_branch': nrm((L, N_BRANCH, BR_W, D), BR_W ** -0.5),
        'w_out': nrm((L, D, D), D ** -0.5),
    }


def reference(x, c, ctx, c_ctx, ada_w, ada_b, norm_ffn1, norm_mix, norm_ffn2, ffn1_wi, ffn1_wo, ffn2_wi,
              ffn2_wo, w_in, hg_lb, hg_norm, rw_shift, rw_w0, rw_w2, rw_a0, rw_a2, rw_kk, rw_ka, rw_rk,
              rw_ln_w, rw_ln_b, na_qn, na_kn, na_rpb, wa_qn, wa_kn, wa_sink, w_branch, w_out):
    xc = ctx
    mod_splits = (D_MODEL,) * N_MOD
    for l in range(DEPTH):
        need_ctx = l < DEPTH - 1
        m = split_cols(jax.nn.silu(c) @ ada_w[l] + ada_b[l], mod_splits)
        mc = split_cols((jax.nn.silu(c_ctx) @ ada_w[l] + ada_b[l])[None], mod_splits)
        x = swiglu_half(x, norm_ffn1[l], m[0], m[1], m[2], ffn1_wi[l], ffn1_wo[l])
        xc = swiglu_half(xc, norm_ffn1[l], mc[0], mc[1], mc[2], ffn1_wi[l], ffn1_wo[l])
        p = modulate(x, norm_mix[l], m[3], m[4]) @ w_in[l]
        pc = modulate(xc, norm_mix[l], mc[3], mc[4]) @ w_in[l]
        y, yc = token_mixing(p, pc, l, hg_lb, hg_norm[l], rw_shift[l], rw_w0[l], rw_w2[l], rw_a0[l], rw_a2[l],
                             rw_kk[l], rw_ka[l], rw_rk[l], rw_ln_w[l], rw_ln_b[l], na_qn[l], na_kn[l], na_rpb[l],
                             wa_qn[l], wa_kn[l], wa_sink[l], w_branch[l], w_out[l], need_ctx)
        x = x + m[5][:, None] * y
        x = swiglu_half(x, norm_ffn2[l], m[6], m[7], m[8], ffn2_wi[l], ffn2_wo[l])
        if need_ctx:
            xc = xc + mc[5][:, None] * yc
            xc = swiglu_half(xc, norm_ffn2[l], mc[6], mc[7], mc[8], ffn2_wi[l], ffn2_wo[l])
    return x
```

```python
import functools

import numpy as np
import jax
import jax.numpy as jnp
from jax import lax
from jax.experimental import pallas as pl
from jax.experimental.pallas import tpu as pltpu

F32 = jnp.float32
BF16 = jnp.bfloat16

D = 2048
DEPTH = 2
GRID_W = 64
EPS = 1e-6
D_FF = 5632
N_MOD = 9
N_BRANCH = 4
BR_W = 512
HG_HEADS, HG_DK = 4, 128
RW_HEADS, RW_HD = 8, 64
RW_LORA = 64
RW_GN_EPS = 64e-5
NA_HEADS, NA_HD = 8, 64
NA_WIN_R, NA_WIN_C = 8, 16
WA_HEADS, WA_KV_HEADS, WA_HD = 8, 2, 64
WA_WINDOW = 128
ROPE_BASE = 10000.0

OFF_RW, OFF_HG, OFF_NA, OFF_GL, OFF_WA, OFF_LORA = 0, 2048, 4608, 6144, 14336, 15104
P_TOTAL = 15360

MIX_ROWS = 256
GLA_CHUNK = 16
RW_CHUNK = 64
NEG = -1e30
VMEM_LIMIT = 56 * 1024 * 1024

NT = (((1,), (1,)), ((), ()))
TN = (((0,), (0,)), ((), ()))


def _cparams(sem):
    return pltpu.CompilerParams(dimension_semantics=sem, vmem_limit_bytes=VMEM_LIMIT)


def _mm(a, b, dims=None):
    a = a.astype(BF16)
    b = b.astype(BF16)
    if dims is None:
        return jnp.dot(a, b, preferred_element_type=F32)
    return lax.dot_general(a, b, dims, preferred_element_type=F32)


def _mm2(x, w_bf16):
    hi = x.astype(BF16)
    lo = (x - hi.astype(F32)).astype(BF16)
    return (jnp.dot(hi, w_bf16, preferred_element_type=F32)
            + jnp.dot(lo, w_bf16, preferred_element_type=F32))


def _mm_f32(a, b):
    return jnp.dot(a, b, preferred_element_type=F32, precision=lax.Precision.HIGHEST)


def _sigmoid(x):
    return 1.0 / (1.0 + jnp.exp(-x))


def _silu(x):
    return x * _sigmoid(x)


def _seg_cumsum(x, seg, rev):
    rows = x.shape[0]
    pos = lax.broadcasted_iota(jnp.int32, x.shape, 0) & (seg - 1)
    sh = 1
    while sh < seg:
        if rev:
            x = x + jnp.where(pos < seg - sh, pltpu.roll(x, rows - sh, axis=0), 0.0)
        else:
            x = x + jnp.where(pos >= sh, pltpu.roll(x, sh, axis=0), 0.0)
        sh *= 2
    return x


def _ada_kernel(cc_ref, w_ref, b_ref, o_ref):
    s = _silu(cc_ref[...])
    w = w_ref[...]
    r0 = jnp.sum(w * s[:, 0:1], axis=0, keepdims=True)
    r1 = jnp.sum(w * s[:, 1:2], axis=0, keepdims=True)
    o_ref[...] = jnp.concatenate([r0, r1], axis=0) + b_ref[...]


def _ada_mods(cc_t, ada_w, ada_b):
    tn = 512
    nmod = ada_w.shape[-1]
    return pl.pallas_call(
        _ada_kernel,
        grid=(DEPTH, nmod // tn),
        in_specs=[pl.BlockSpec((D, 2), lambda l, j: (0, 0)),
                  pl.BlockSpec((None, D, tn), lambda l, j: (l, 0, j)),
                  pl.BlockSpec((None, 1, tn), lambda l, j: (l, 0, j))],
        out_specs=pl.BlockSpec((None, 2, tn), lambda l, j: (l, 0, j)),
        out_shape=jax.ShapeDtypeStruct((DEPTH, 2, nmod), F32),
        compiler_params=_cparams(("arbitrary", "arbitrary")),
        name="ada_mod",
    )(cc_t, ada_w, ada_b.reshape(DEPTH, 1, nmod))


def _row_is_ctx(i, tm, lc):
    return (i * tm + lax.broadcasted_iota(jnp.int32, (tm, 1), 0)) < lc


def _mod_pick(mod_ref, j, is_ctx):
    return jnp.where(is_ctx, mod_ref[1, j:j + 1, :], mod_ref[0, j:j + 1, :])


def _modulated(x, g, mod_ref, is_ctx):
    y = x * lax.rsqrt(jnp.mean(x * x, axis=-1, keepdims=True) + EPS) * g
    return y * (1.0 + _mod_pick(mod_ref, 1, is_ctx)) + _mod_pick(mod_ref, 0, is_ctx)


def _ffn_kernel(lc, tm, nf, x_ref, mod_ref, g_ref, wa_ref, wb_ref, wo_ref, o_ref, xn_ref):
    i = pl.program_id(0)
    j = pl.program_id(1)
    is_ctx = _row_is_ctx(i, tm, lc)

    @pl.when(j == 0)
    def _():
        xn_ref[...] = _modulated(x_ref[...], g_ref[...], mod_ref, is_ctx).astype(BF16)
        o_ref[...] = jnp.zeros_like(o_ref)

    xn = xn_ref[...]
    a = jnp.dot(xn, wa_ref[...], preferred_element_type=F32)
    b = jnp.dot(xn, wb_ref[...], preferred_element_type=F32)
    h = (_silu(a) * b).astype(BF16)
    o_ref[...] += jnp.dot(h, wo_ref[...], preferred_element_type=F32)

    @pl.when(j == nf - 1)
    def _():
        o_ref[...] = x_ref[...] + 0.5 * _mod_pick(mod_ref, 2, is_ctx) * o_ref[...]


def _dense_tm(rows):
    for tm in (768, 512, 256):
        if rows % tm == 0:
            return tm
    raise ValueError(rows)


def _ffn(x, mod3, g, wi, wo, lc):
    rows = x.shape[0]
    tm = _dense_tm(rows)
    tf = 512
    nf = D_FF // tf
    return pl.pallas_call(
        functools.partial(_ffn_kernel, lc, tm, nf),
        grid=(rows // tm, nf),
        in_specs=[pl.BlockSpec((tm, D), lambda i, j: (i, 0)),
                  pl.BlockSpec((2, 3, D), lambda i, j: (0, 0, 0)),
                  pl.BlockSpec((1, D), lambda i, j: (0, 0)),
                  pl.BlockSpec((D, tf), lambda i, j: (0, j)),
                  pl.BlockSpec((D, tf), lambda i, j: (0, j + nf)),
                  pl.BlockSpec((tf, D), lambda i, j: (j, 0))],
        out_specs=pl.BlockSpec((tm, D), lambda i, j: (i, 0)),
        out_shape=jax.ShapeDtypeStruct((rows, D), F32),
        scratch_shapes=[pltpu.VMEM((tm, D), BF16)],
        compiler_params=_cparams(("arbitrary", "arbitrary")),
        name="ffn",
    )(x, mod3, g, wi, wi, wo)


def _win_kernel(lc, tm, x_ref, mod_ref, g_ref, w_ref, o_ref, xn_ref):
    i = pl.program_id(0)

    @pl.when(pl.program_id(1) == 0)
    def _():
        is_ctx = _row_is_ctx(i, tm, lc)
        xn_ref[...] = _modulated(x_ref[...], g_ref[...], mod_ref, is_ctx).astype(BF16)

    o_ref[...] = jnp.dot(xn_ref[...], w_ref[...], preferred_element_type=F32)


def _win(x, mod3, g, w, lc):
    rows = x.shape[0]
    tm = _dense_tm(rows)
    tn = 512
    return pl.pallas_call(
        functools.partial(_win_kernel, lc, tm),
        grid=(rows // tm, P_TOTAL // tn),
        in_specs=[pl.BlockSpec((tm, D), lambda i, j: (i, 0)),
                  pl.BlockSpec((2, 3, D), lambda i, j: (0, 0, 0)),
                  pl.BlockSpec((1, D), lambda i, j: (0, 0)),
                  pl.BlockSpec((D, tn), lambda i, j: (0, j))],
        out_specs=pl.BlockSpec((tm, tn), lambda i, j: (i, j)),
        out_shape=jax.ShapeDtypeStruct((rows, P_TOTAL), F32),
        scratch_shapes=[pltpu.VMEM((tm, D), BF16)],
        compiler_params=_cparams(("arbitrary", "arbitrary")),
        name="w_in",
    )(x, mod3, g, w)


def _merge_kernel(lc, tm, x_ref, mod_ref, y_ref, gl_ref, wb_ref, wo_ref, o_ref, acc_ref):
    i = pl.program_id(0)
    n = pl.program_id(1)

    @pl.when(n == 0)
    def _():
        acc_ref[...] = jnp.zeros_like(acc_ref)

    proj = jnp.dot(y_ref[...].astype(BF16), wb_ref[...], preferred_element_type=F32)
    acc_ref[...] += _sigmoid(gl_ref[...]) * proj

    @pl.when(n == N_BRANCH - 1)
    def _():
        is_ctx = _row_is_ctx(i, tm, lc)
        y = jnp.dot(acc_ref[...].astype(BF16), wo_ref[...], preferred_element_type=F32)
        o_ref[...] = x_ref[...] + _mod_pick(mod_ref, 2, is_ctx) * y


def _merge(x, mod3, yb, p, wb, wo, lc):
    rows = x.shape[0]
    tm = MIX_ROWS
    glb = OFF_GL // D
    return pl.pallas_call(
        functools.partial(_merge_kernel, lc, tm),
        grid=(rows // tm, N_BRANCH),
        in_specs=[pl.BlockSpec((tm, D), lambda i, n: (i, 0)),
                  pl.BlockSpec((2, 3, D), lambda i, n: (0, 0, 0)),
                  pl.BlockSpec((None, tm, BR_W), lambda i, n: (n, i, 0)),
                  pl.BlockSpec((tm, D), lambda i, n: (i, glb + n)),
                  pl.BlockSpec((None, BR_W, D), lambda i, n: (n, 0, 0)),
                  pl.BlockSpec((D, D), lambda i, n: (0, 0))],
        out_specs=pl.BlockSpec((tm, D), lambda i, n: (i, 0)),
        out_shape=jax.ShapeDtypeStruct((rows, D), F32),
        scratch_shapes=[pltpu.VMEM((tm, D), F32)],
        compiler_params=_cparams(("arbitrary", "arbitrary")),
        name="merge",
    )(x, mod3, yb, p, wb, wo)


def _gla_kernel(rev, *refs):
    if rev:
        (q_ref, f_ref, i_ref, lb_ref, of_ref, g_ref, nw_ref, y_ref,
         st_ref, b_s, q_s, k_s, v_s, o_s) = refs
    else:
        q_ref, f_ref, i_ref, lb_ref, y_ref, st_ref, b_s, q_s, k_s, v_s, o_s = refs
    C = GLA_CHUNK

    @pl.when(pl.program_id(1) == 0)
    def _():
        st_ref[...] = jnp.zeros_like(st_ref)

    lb = lb_ref[...]
    f = lb + (1.0 - lb) * _sigmoid(f_ref[...])
    b_s[...] = _seg_cumsum(jnp.log(f), C, rev)
    k_s[...] = 1.0 - f
    q_s[...] = q_ref[...]
    v_s[...] = i_ref[...]
    sidx = lax.broadcasted_iota(jnp.int32, (C, 1), 0)

    def chunk(ci, carry):
        c = (MIX_ROWS // C - 1 - ci) if rev else ci
        r0 = pl.multiple_of(c * C, C)
        bc = b_s[pl.ds(r0, C), :]
        qc = q_s[pl.ds(r0, C), :]
        kc = k_s[pl.ds(r0, C), :]
        vc = v_s[pl.ds(r0, C), :]
        b_end = bc[0:1] if rev else bc[C - 1:C]
        st = st_ref[...]
        o_inter = _mm(qc * jnp.exp(bc), st, NT)
        kb = kc * jnp.exp(b_end - bc)
        st_ref[...] = st * jnp.exp(b_end) + _mm(vc, kb, TN)
        rows = []
        for t in range(C):
            mask = (sidx >= t) if rev else (sidx <= t)
            e = jnp.exp(jnp.where(mask, bc[t:t + 1] - bc, NEG))
            col = jnp.sum(e * kc * qc[t:t + 1], axis=-1, keepdims=True)
            rows.append(jnp.sum(col * vc, axis=0, keepdims=True))
        o_s[pl.ds(r0, C), :] = o_inter + jnp.concatenate(rows, axis=0)
        return carry

    lax.fori_loop(0, MIX_ROWS // C, chunk, 0)

    if rev:
        o = of_ref[...] + o_s[...]
        o = o * lax.rsqrt(jnp.mean(o * o, axis=-1, keepdims=True) + EPS) * nw_ref[...]
        y_ref[...] = o * _silu(g_ref[...])
    else:
        y_ref[...] = o_s[...]


def _blk_order(rev, nb):
    if rev:
        return lambda b: jnp.where(b == 0, 0, nb - b)
    return lambda b: b


def _gla_dir(p, lb, rev, o_fwd=None, norm_w=None):
    rows = p.shape[0]
    nb = rows // MIX_ROWS
    order = _blk_order(rev, nb)
    cb = OFF_HG // HG_DK
    nh = HG_HEADS

    def col(off):
        return pl.BlockSpec((MIX_ROWS, HG_DK), lambda h, b: (order(b), cb + off + h))

    in_specs = [col(0), col(nh * (2 if rev else 1)), col(3 * nh),
                pl.BlockSpec((1, HG_DK), lambda h, b: (0, h))]
    args = [p, p, p, lb]
    if rev:
        in_specs += [pl.BlockSpec((MIX_ROWS, HG_DK), lambda h, b: (order(b), h)),
                     col(4 * nh),
                     pl.BlockSpec((1, HG_DK), lambda h, b: (0, h))]
        args += [o_fwd, p, norm_w]
    return pl.pallas_call(
        functools.partial(_gla_kernel, rev),
        grid=(nh, nb),
        in_specs=in_specs,
        out_specs=pl.BlockSpec((MIX_ROWS, HG_DK), lambda h, b: (order(b), h)),
        out_shape=jax.ShapeDtypeStruct((rows, nh * HG_DK), F32),
        scratch_shapes=[pltpu.VMEM((HG_DK, HG_DK), F32)] + [pltpu.VMEM((MIX_ROWS, HG_DK), F32)] * 5,
        compiler_params=_cparams(("arbitrary", "arbitrary")),
        name="gla_rev" if rev else "gla_fwd",
    )(*args)


def _rw_prep_kernel(nb, main_ref, lora_ref, pm_ref, nm_ref, pl_ref, nl_ref, tm_ref, tl_ref,
                    w0_ref, w2_ref, a0_ref, a2_ref, kkw_ref, ka_ref, rk_ref, bd_ref,
                    r_o, kk_o, v_o, gs_o, bonus_o, lw0_o, lw1_o, akk0_o, akk1_o, kd0_o, kd1_o):
    i = pl.program_id(0)
    has_prev = (i >= 2).astype(F32)
    has_next = jnp.logical_and(i != 0, i != nb - 1).astype(F32)
    rowi = lax.broadcasted_iota(jnp.int32, (MIX_ROWS, 1), 0)

    def shift(x, prev_blk, next_blk, taps):
        up = jnp.where(rowi == 0, prev_blk[7:8, :] * has_prev, pltpu.roll(x, 1, axis=0))
        dn = jnp.where(rowi == MIX_ROWS - 1, next_blk[0:1, :] * has_next,
                       pltpu.roll(x, MIX_ROWS - 1, axis=0))
        return taps[0:1] * up + taps[1:2] * x + taps[2:3] * dn

    main = shift(main_ref[...], pm_ref[...], nm_ref[...], tm_ref[...])
    lora = shift(lora_ref[...], pl_ref[...], nl_ref[...], tl_ref[...])
    W = RW_HEADS * RW_HD
    r, k, v, g = (main[:, n * W:(n + 1) * W] for n in range(4))
    bd = bd_ref[...]
    kk = k * kkw_ref[...]
    kk = kk * lax.rsqrt(_mm2(kk * kk, bd) + EPS)
    tl = jnp.tanh(lora)
    kds = []
    for d, (lw_o, akk_o, kd_o) in enumerate(((lw0_o, akk0_o, kd0_o), (lw1_o, akk1_o, kd1_o))):
        z = -(w0_ref[d:d + 1, :] + _mm_f32(tl, w2_ref[d]))
        softplus = jnp.maximum(z, 0.0) + jnp.log(1.0 + jnp.exp(-jnp.abs(z)))
        lw_o[...] = -jnp.exp(-softplus - 0.5)
        a = _sigmoid(a0_ref[d:d + 1, :] + _mm_f32(lora, a2_ref[d]))
        kd = k * (1.0 + (a - 1.0) * ka_ref[...])
        kds.append(kd)
        kd_o[...] = kd
        akk_o[...] = a * kk
    r_o[...] = r
    kk_o[...] = kk
    v_o[...] = v
    gs_o[...] = _sigmoid(g)
    bonus_o[...] = _mm2(r * (kds[0] + kds[1]) * rk_ref[...], bd) * v


def _rw_prep(p, taps_m, taps_l, w0, w2p, a0, a2p, kkw, ka, rk, bd512):
    rows = p.shape[0]
    nb = rows // MIX_ROWS
    W = RW_HEADS * RW_HD
    n8 = rows // 8
    lb = OFF_LORA // 256
    per = MIX_ROWS // 8
    full = lambda shape: pl.BlockSpec(shape, lambda i: (0,) * len(shape))
    in_specs = [pl.BlockSpec((MIX_ROWS, 4 * W), lambda i: (i, 0)),
                pl.BlockSpec((MIX_ROWS, 256), lambda i: (i, lb)),
                pl.BlockSpec((8, 4 * W), lambda i: (jnp.maximum(i * per - 1, 0), 0)),
                pl.BlockSpec((8, 4 * W), lambda i: (jnp.minimum((i + 1) * per, n8 - 1), 0)),
                pl.BlockSpec((8, 256), lambda i: (jnp.maximum(i * per - 1, 0), lb)),
                pl.BlockSpec((8, 256), lambda i: (jnp.minimum((i + 1) * per, n8 - 1), lb)),
                full((3, 4 * W)), full((3, 256)),
                full((2, W)), full((2, 256, W)), full((2, W)), full((2, 256, W)),
                full((1, W)), full((1, W)), full((1, W)), full((W, W))]
    out = jax.ShapeDtypeStruct((rows, W), F32)
    return pl.pallas_call(
        functools.partial(_rw_prep_kernel, nb),
        grid=(nb,),
        in_specs=in_specs,
        out_specs=[pl.BlockSpec((MIX_ROWS, W), lambda i: (i, 0))] * 11,
        out_shape=[out] * 11,
        compiler_params=_cparams(("arbitrary",)),
        name="rw_prep",
    )(p, p, p, p, p, p, taps_m, taps_l, w0, w2p, a0, a2p, kkw, ka, rk, bd512)


def _rw_scan_kernel(rev, *refs):
    if rev:
        (r_ref, kk_ref, v_ref, lw_ref, akk_ref, kd_ref, of_ref, bonus_ref, gs_ref,
         lnw_ref, lnb_ref, bd_ref, y_ref, s_ref) = refs
    else:
        r_ref, kk_ref, v_ref, lw_ref, akk_ref, kd_ref, y_ref, s_ref = refs
    C = RW_CHUNK
    HD = RW_HD
    P2 = 2 * HD

    @pl.when(pl.program_id(1) == 0)
    def _():
        s_ref[...] = jnp.zeros_like(s_ref)

    lw = lw_ref[...]
    cum = _seg_cumsum(lw, C, rev)
    cum_prev = cum - lw

    lane = lax.broadcasted_iota(jnp.int32, (1, P2), 1)
    h0 = lane < HD
    ri = lax.broadcasted_iota(jnp.int32, (P2, P2), 0)
    ci = lax.broadcasted_iota(jnp.int32, (P2, P2), 1)
    same = (ri // C) == (ci // C)
    rt, cs = ri % C, ci % C
    strict = jnp.logical_and(same, (rt < cs) if rev else (rt > cs))
    incl = jnp.logical_and(same, (rt <= cs) if rev else (rt >= cs))
    eye = (ri == ci).astype(F32)
    bdiag = ((ri // HD) == (ci // HD)).astype(F32)

    def stack(x):
        return jnp.concatenate([x, x], axis=0)

    def split_heads(x):
        return jnp.concatenate([jnp.where(h0, x, 0.0), jnp.where(h0, 0.0, x)], axis=0)

    def unstack(x):
        return jnp.where(h0, x[:C], x[C:])

    order = range(MIX_ROWS // C - 1, -1, -1) if rev else range(MIX_ROWS // C)
    for c in order:
        sl = slice(c * C, (c + 1) * C)
        cum_c = cum[sl]
        cend = cum_c[0:1] if rev else cum_c[C - 1:C]
        e_neg = jnp.exp(-cum_c)
        e_end = jnp.exp(cend - cum_c)
        a_t = kk_ref[sl, :] * jnp.exp(cum_prev[sl])
        r_t = r_ref[sl, :] * jnp.exp(cum_c)
        akk = akk_ref[sl, :]
        kd = kd_ref[sl, :]
        v = v_ref[sl, :]
        b_st = stack(akk * e_neg)
        k_st = stack(kd * e_neg)
        la = split_heads(a_t)
        lr = split_heads(r_t)
        n_ab = jnp.where(strict, _mm(la, b_st, NT), 0.0)
        a_ak = jnp.where(strict, _mm(la, k_st, NT), 0.0)
        a_rb = jnp.where(incl, _mm(lr, b_st, NT), 0.0)
        a_rk = jnp.where(incl, _mm(lr, k_st, NT), 0.0)
        tinv = eye - n_ab
        m = n_ab
        for _ in range(5):
            m = _mm(m, m)
            tinv = tinv + _mm(tinv, m)
        s = s_ref[...]
        x = _mm(jnp.concatenate([a_t, r_t], axis=0), s, NT)
        v_st = stack(v)
        u_st = _mm(tinv, stack(x[:C]) + _mm(a_ak, v_st))
        o_st = stack(x[C:]) + _mm(a_rk, v_st) - _mm(a_rb, u_st)
        u = unstack(u_st)
        y_ref[sl, :] = unstack(o_st)
        upd = _mm(jnp.concatenate([v, -u], axis=0),
                  jnp.concatenate([kd * e_end, akk * e_end], axis=0), TN)
        s_ref[...] = s * jnp.exp(cend) + bdiag * upd

    if rev:
        o = of_ref[...] + y_ref[...]
        bd = bd_ref[...]
        mu = _mm2(o, bd) * (1.0 / HD)
        oc = o - mu
        var = _mm2(oc * oc, bd) * (1.0 / HD)
        o = oc * lax.rsqrt(var + RW_GN_EPS) * lnw_ref[...] + lnb_ref[...]
        y_ref[...] = (o + bonus_ref[...]) * gs_ref[...]


def _rw_scan(rev, r, kk, v, lw, akk, kd, extra=()):
    rows = r.shape[0]
    nb = rows // MIX_ROWS
    order = _blk_order(rev, nb)
    P2 = 2 * RW_HD
    blk = pl.BlockSpec((MIX_ROWS, P2), lambda h, b: (order(b), h))
    in_specs = [blk] * 6
    args = [r, kk, v, lw, akk, kd]
    if rev:
        o_fwd, bonus, gs, lnw, lnb, bd128 = extra
        vec = pl.BlockSpec((1, P2), lambda h, b: (0, h))
        in_specs += [blk, blk, blk, vec, vec, pl.BlockSpec((P2, P2), lambda h, b: (0, 0))]
        args += [o_fwd, bonus, gs, lnw, lnb, bd128]
    return pl.pallas_call(
        functools.partial(_rw_scan_kernel, rev),
        grid=(RW_HEADS // 2, nb),
        in_specs=in_specs,
        out_specs=blk,
        out_shape=jax.ShapeDtypeStruct((rows, RW_HEADS * RW_HD), F32),
        scratch_shapes=[pltpu.VMEM((P2, P2), F32)],
        compiler_params=_cparams(("arbitrary", "arbitrary")),
        name="rw_rev" if rev else "rw_fwd",
    )(*args)


def _qk_prep_kernel(nq_ref, nk_ref, nv_ref, wq_ref, wkv_ref, nqn_ref, nkn_ref, wqn_ref, wkn_ref,
                    cos_ref, sin_ref, bd512_ref, bd128_ref,
                    naq_o, nak_o, nav_o, waq_o, wak_o, wav_o):
    bd512 = bd512_ref[...]
    bd128 = bd128_ref[...]

    def hnorm(x, g, bd, hd):
        return x * lax.rsqrt(_mm2(x * x, bd) * (1.0 / hd) + EPS) * g

    def rope(x, cos, sin):
        lane = lax.broadcasted_iota(jnp.int32, x.shape, 1)
        w = x.shape[1]
        partner = jnp.where((lane & 31) < 16, pltpu.roll(x, w - 16, axis=1), pltpu.roll(x, 16, axis=1))
        return x * cos + partner * sin

    def heads_out(o_ref, x, nh, hd):
        for h in range(nh):
            o_ref[h] = x[:, h * hd:(h + 1) * hd].astype(o_ref.dtype)

    heads_out(naq_o, hnorm(nq_ref[...], nqn_ref[...], bd512, NA_HD) * NA_HD ** -0.5, NA_HEADS, NA_HD)
    heads_out(nak_o, hnorm(nk_ref[...], nkn_ref[...], bd512, NA_HD), NA_HEADS, NA_HD)
    heads_out(nav_o, nv_ref[...], NA_HEADS, NA_HD)
    cos = cos_ref[...]
    sin = sin_ref[...]
    wq = hnorm(wq_ref[...], wqn_ref[...], bd512, WA_HD) * WA_HD ** -0.5
    wq = rope(wq, jnp.concatenate([cos] * 4, axis=1), jnp.concatenate([sin] * 4, axis=1))
    heads_out(waq_o, wq, WA_HEADS, WA_HD)
    kv = wkv_ref[...]
    wk = rope(hnorm(kv[:, :128], wkn_ref[...], bd128, WA_HD), cos, sin)
    heads_out(wak_o, wk, WA_KV_HEADS, WA_HD)
    heads_out(wav_o, kv[:, 128:], WA_KV_HEADS, WA_HD)


def _qk_prep(p, nqn, nkn, wqn, wkn, cos, sin, bd512, bd128):
    rows = p.shape[0]
    nb = rows // MIX_ROWS
    nab = OFF_NA // 512
    full = lambda shape: pl.BlockSpec(shape, lambda i: (0,) * len(shape))
    in_specs = [pl.BlockSpec((MIX_ROWS, 512), lambda i: (i, nab)),
                pl.BlockSpec((MIX_ROWS, 512), lambda i: (i, nab + 1)),
                pl.BlockSpec((MIX_ROWS, 512), lambda i: (i, nab + 2)),
                pl.BlockSpec((MIX_ROWS, 512), lambda i: (i, OFF_WA // 512)),
                pl.BlockSpec((MIX_ROWS, 256), lambda i: (i, (OFF_WA + 512) // 256)),
                full((1, 512)), full((1, 512)), full((1, 512)), full((1, 128)),
                pl.BlockSpec((MIX_ROWS, 128), lambda i: (i, 0)),
                pl.BlockSpec((MIX_ROWS, 128), lambda i: (i, 0)),
                full((512, 512)), full((128, 128))]
    hspec = lambda nh: pl.BlockSpec((nh, MIX_ROWS, 64), lambda i: (0, i, 0))
    hshape = lambda nh: jax.ShapeDtypeStruct((nh, rows, 64), BF16)
    return pl.pallas_call(
        _qk_prep_kernel,
        grid=(nb,),
        in_specs=in_specs,
        out_specs=[hspec(8), hspec(8), hspec(8), hspec(8), hspec(2), hspec(2)],
        out_shape=[hshape(8), hshape(8), hshape(8), hshape(8), hshape(2), hshape(2)],
        compiler_params=_cparams(("arbitrary",)),
        name="qk_prep",
    )(p, p, p, p, p, nqn, nkn, wqn, wkn, cos, sin, bd512, bd128)


def _na_kernel(lc, nrows, q_ref, k_ref, v_ref, bias_ref, o_ref):
    W = GRID_W
    nk = NA_WIN_R * W
    kc = k_ref[0:lc, :]
    vc = v_ref[0:lc, :]

    s = _mm(q_ref[0:lc, :], kc, NT)
    e = jnp.exp(s - jnp.max(s, axis=-1, keepdims=True))
    o_ref[0:lc, :] = _mm(e, vc) / jnp.sum(e, axis=-1, keepdims=True)

    def row(i, carry):
        rs = jnp.clip(i - NA_WIN_R // 2, 0, nrows - NA_WIN_R)
        q0 = pl.multiple_of(lc + i * W, W)
        k0 = pl.multiple_of(lc + rs * W, W)
        qi = q_ref[pl.ds(q0, W), :]
        s_loc = _mm(qi, k_ref[pl.ds(k0, nk), :], NT) + bias_ref[i - rs]
        s_ctx = _mm(qi, kc, NT)
        m = jnp.maximum(jnp.max(s_loc, axis=-1, keepdims=True), jnp.max(s_ctx, axis=-1, keepdims=True))
        p_loc = jnp.exp(s_loc - m)
        p_ctx = jnp.exp(s_ctx - m)
        den = jnp.sum(p_loc, axis=-1, keepdims=True) + jnp.sum(p_ctx, axis=-1, keepdims=True)
        o_ref[pl.ds(q0, W), :] = (_mm(p_loc, v_ref[pl.ds(k0, nk), :]) + _mm(p_ctx, vc)) / den
        return carry

    lax.fori_loop(0, nrows, row, 0)


def _na(q, k, v, bias, lc):
    nh, rows, hd = q.shape
    nrows = (rows - lc) // GRID_W
    hblk = pl.BlockSpec((None, rows, hd), lambda h: (h, 0, 0))
    return pl.pallas_call(
        functools.partial(_na_kernel, lc, nrows),
        grid=(nh,),
        in_specs=[hblk, hblk, hblk,
                  pl.BlockSpec((None, NA_WIN_R, GRID_W, NA_WIN_R * GRID_W), lambda h: (h, 0, 0, 0))],
        out_specs=hblk,
        out_shape=jax.ShapeDtypeStruct((nh, rows, hd), F32),
        compiler_params=_cparams(("arbitrary",)),
        name="na_attn",
    )(q, k, v, bias)


def _wa_kernel(lc, t_len, q_ref, k_ref, v_ref, sink_ref, o_ref):
    G = WA_HEADS // WA_KV_HEADS
    B = WA_WINDOW
    nband = 3 * B
    b = pl.program_id(1)
    kc = k_ref[0:lc, :]
    vc = v_ref[0:lc, :]
    sink = sink_ref[...]

    @pl.when(b == 0)
    def _():
        for g in range(G):
            s = _mm(q_ref[g], kc, NT)
            sk = sink[g * B:g * B + 1, :]
            m = jnp.maximum(jnp.max(s, axis=-1, keepdims=True), sk)
            e = jnp.exp(s - m)
            o_ref[g] = _mm(e, vc) / (jnp.sum(e, axis=-1, keepdims=True) + jnp.exp(sk - m))

    @pl.when(b > 0)
    def _():
        qoff = lax.broadcasted_iota(jnp.int32, (G * B, nband), 0) & (B - 1)
        koff = lax.broadcasted_iota(jnp.int32, (G * B, nband), 1)
        for j in range(MIX_ROWS // B):
            n = (b - 1) * (MIX_ROWS // B) + j
            start = jnp.clip((n - 1) * B, 0, t_len - nband)
            k0 = pl.multiple_of(lc + start, B)
            qs = jnp.concatenate([q_ref[g, j * B:(j + 1) * B, :] for g in range(G)], axis=0)
            valid = jnp.abs((n * B + qoff) - (start + koff)) <= WA_WINDOW
            s_loc = jnp.where(valid, _mm(qs, k_ref[pl.ds(k0, nband), :], NT), NEG)
            s_ctx = _mm(qs, kc, NT)
            m = jnp.maximum(jnp.maximum(jnp.max(s_loc, axis=-1, keepdims=True),
                                        jnp.max(s_ctx, axis=-1, keepdims=True)), sink)
            p_loc = jnp.exp(s_loc - m)
            p_ctx = jnp.exp(s_ctx - m)
            den = (jnp.sum(p_loc, axis=-1, keepdims=True) + jnp.sum(p_ctx, axis=-1, keepdims=True)
                   + jnp.exp(sink - m))
            o = (_mm(p_loc, v_ref[pl.ds(k0, nband), :]) + _mm(p_ctx, vc)) / den
            for g in range(G):
                o_ref[g, j * B:(j + 1) * B, :] = o[g * B:(g + 1) * B]


def _wa(q, k, v, sink_col, lc):
    nh, rows, hd = q.shape
    G = WA_HEADS // WA_KV_HEADS
    kvblk = pl.BlockSpec((None, rows, hd), lambda h, b: (h, 0, 0))
    qblk = pl.BlockSpec((G, MIX_ROWS, hd), lambda h, b: (h, b, 0))
    return pl.pallas_call(
        functools.partial(_wa_kernel, lc, rows - lc),
        grid=(WA_KV_HEADS, rows // MIX_ROWS),
        in_specs=[qblk, kvblk, kvblk,
                  pl.BlockSpec((None, G * WA_WINDOW, 1), lambda h, b: (h, 0, 0))],
        out_specs=qblk,
        out_shape=jax.ShapeDtypeStruct((nh, rows, hd), F32),
        compiler_params=_cparams(("arbitrary", "arbitrary")),
        name="wa_attn",
    )(q, k, v, sink_col)


def _block_diag(n, blk):
    idx = np.arange(n) // blk
    return jnp.asarray(idx[:, None] == idx[None, :], dtype=BF16)


def _na_bias_table(rpb):
    W, wr, wc = GRID_W, NA_WIN_R, NA_WIN_C
    j = np.arange(W)
    cstart = np.clip(j - wc // 2, 0, W - wc)
    cabs = np.arange(W)
    inwin = (cabs[None, :] >= cstart[:, None]) & (cabs[None, :] < cstart[:, None] + wc)
    cb = np.clip(cabs[None, :] - j[:, None] + wc - 1, 0, 2 * wc - 2)
    cfg = np.arange(wr)
    a = np.arange(wr)
    rb = a[None, :] - cfg[:, None] + wr - 1
    tab = rpb[:, rb[:, None, :, None], cb[None, :, None, :]]
    tab = jnp.where(inwin[None, None, :, None, :], tab, NEG)
    return tab.reshape(rpb.shape[0], wr, W, wr * W).astype(F32)


def _rope_tables(lc, t_len):
    pos = np.arange(t_len)
    n = WA_HD // 4
    inv = ROPE_BASE ** (-np.arange(n, dtype=np.float64) / n)
    ang_r = (pos // GRID_W)[:, None] * inv[None, :]
    ang_c = (pos % GRID_W)[:, None] * inv[None, :]
    cos = np.concatenate([np.cos(ang_r)] * 2 + [np.cos(ang_c)] * 2, axis=1)
    sin = np.concatenate([-np.sin(ang_r), np.sin(ang_r), -np.sin(ang_c), np.sin(ang_c)], axis=1)
    cos = np.concatenate([np.ones((lc, WA_HD)), cos], axis=0)
    sin = np.concatenate([np.zeros((lc, WA_HD)), sin], axis=0)
    return (jnp.asarray(np.tile(cos, (1, 2)), dtype=F32), jnp.asarray(np.tile(sin, (1, 2)), dtype=F32))


def _permute_w_in(w):
    hg, rw, na, wa = 2560, 2304, 1536, 768
    s_rw, s_na, s_wa, s_gl = hg, hg + rw, hg + rw + na, hg + rw + na + wa
    return jnp.concatenate([w[:, s_rw:s_rw + 2048], w[:, :hg], w[:, s_na:s_wa], w[:, s_gl:],
                            w[:, s_wa:s_gl], w[:, s_rw + 2048:s_na]], axis=1)


def _lora_pad(w, row0):
    out = jnp.zeros((2, 256, w.shape[-1]), F32)
    for d in range(2):
        out = out.at[d, row0 + 64 * d:row0 + 64 * (d + 1)].set(w[d])
    return out


def _token_mixing(p, lc, layer, hg_lb, hg_norm, rw_shift, rw_w0, rw_w2, rw_a0, rw_a2, rw_kk, rw_ka, rw_rk,
                  rw_ln_w, rw_ln_b, na_qn, na_kn, na_rpb, wa_qn, wa_kn, wa_sink, tables):
    rows = p.shape[0]
    bd512, bd128, cos, sin = tables
    cum = jnp.cumsum(jax.nn.softmax(hg_lb.astype(F32), axis=1), axis=1)
    lbs = cum[:, layer] - cum[:, 0]
    o_f = _gla_dir(p, lbs[0:1], False)
    y_a = _gla_dir(p, lbs[1:2], True, o_f, hg_norm[None])
    prep = _rw_prep(p, rw_shift[:, :2048], rw_shift[:, 2048:], rw_w0, _lora_pad(rw_w2, 0), rw_a0,
                    _lora_pad(rw_a2, 128), rw_kk[None], rw_ka[None], rw_rk.reshape(1, -1), bd512)
    r, kk, v, gs, bonus, lw0, lw1, akk0, akk1, kd0, kd1 = prep
    o_f = _rw_scan(False, r, kk, v, lw0, akk0, kd0)
    y_b = _rw_scan(True, r, kk, v, lw1, akk1, kd1,
                   (o_f, bonus, gs, rw_ln_w[None], rw_ln_b[None], bd128))
    tile = lambda g, n: jnp.tile(g, n)[None]
    naq, nak, nav, waq, wak, wav = _qk_prep(p, tile(na_qn, 8), tile(na_kn, 8), tile(wa_qn, 8),
                                            tile(wa_kn, 2), cos, sin, bd512, bd128)
    y_c = _na(naq, nak, nav, _na_bias_table(na_rpb), lc)
    G = WA_HEADS // WA_KV_HEADS
    sink_col = jnp.repeat(wa_sink.reshape(WA_KV_HEADS, G), WA_WINDOW, axis=1)[..., None]
    y_d = _wa(waq, wak, wav, sink_col, lc)
    flat = lambda y: y.transpose(1, 0, 2).reshape(rows, -1)
    return jnp.stack([y_a, y_b, flat(y_c), flat(y_d)], axis=0)


def kernel(x, c, ctx, c_ctx, ada_w, ada_b, norm_ffn1, norm_mix, norm_ffn2, ffn1_wi, ffn1_wo, ffn2_wi, ffn2_wo, w_in, hg_lb, hg_norm, rw_shift, rw_w0, rw_w2, rw_a0, rw_a2, rw_kk, rw_ka, rw_rk, rw_ln_w, rw_ln_b, na_qn, na_kn, na_rpb, wa_qn, wa_kn, wa_sink, w_branch, w_out):
    assert x.shape[0] == 1 and ctx.shape[1] == MIX_ROWS
    lc = ctx.shape[1]
    t_len = x.shape[1]
    xa = jnp.concatenate([ctx[0], x[0]], axis=0)
    cc_t = jnp.stack([c[0], c_ctx], axis=1)
    mods = _ada_mods(cc_t, ada_w, ada_b).reshape(DEPTH, 2, N_MOD, D)
    tables = (_block_diag(512, 64), _block_diag(128, 64)) + _rope_tables(lc, t_len)
    for l in range(DEPTH):
        m = mods[l]
        xa = _ffn(xa, m[:, 0:3], norm_ffn1[l][None], ffn1_wi[l].astype(BF16), ffn1_wo[l].astype(BF16), lc)
        p = _win(xa, m[:, 3:6], norm_mix[l][None], _permute_w_in(w_in[l]).astype(BF16), lc)
        yb = _token_mixing(p, lc, l, hg_lb, hg_norm[l], rw_shift[l], rw_w0[l], rw_w2[l], rw_a0[l], rw_a2[l],
                           rw_kk[l], rw_ka[l], rw_rk[l], rw_ln_w[l], rw_ln_b[l], na_qn[l], na_kn[l],
                           na_rpb[l], wa_qn[l], wa_kn[l], wa_sink[l], tables)
        xa = _merge(xa, m[:, 3:6], yb, p, w_branch[l].astype(BF16), w_out[l].astype(BF16), lc)
        xa = _ffn(xa, m[:, 6:9], norm_ffn2[l][None], ffn2_wi[l].astype(BF16), ffn2_wo[l].astype(BF16), lc)
    return xa[lc:][None]
```

```python
import functools

import numpy as np
import jax
import jax.numpy as jnp
from jax import lax
from jax.experimental import pallas as pl
from jax.experimental.pallas import tpu as pltpu

F32 = jnp.float32
BF16 = jnp.bfloat16

D = 2048
DEPTH = 2
GRID_W = 64
EPS = 1e-6
D_FF = 5632
N_MOD = 9
N_BRANCH = 4
BR_W = 512
HG_HEADS, HG_DK = 4, 128
RW_HEADS, RW_HD = 8, 64
RW_LORA = 64
RW_GN_EPS = 64e-5
NA_HEADS, NA_HD = 8, 64
NA_WIN_R, NA_WIN_C = 8, 16
WA_HEADS, WA_KV_HEADS, WA_HD = 8, 2, 64
WA_WINDOW = 128
ROPE_BASE = 10000.0

OFF_RW, OFF_HG, OFF_NA, OFF_GL, OFF_WA, OFF_LORA = 0, 2048, 4608, 6144, 14336, 15104
P_TOTAL = 15360

MIX_ROWS = 256
GLA_CHUNK = 16
RW_CHUNK = 64
NEG = -1e30
VMEM_LIMIT = 56 * 1024 * 1024

NT = (((1,), (1,)), ((), ()))
TN = (((0,), (0,)), ((), ()))


def _cparams(sem):
    return pltpu.CompilerParams(dimension_semantics=sem, vmem_limit_bytes=VMEM_LIMIT)


def _mm(a, b, dims=None):
    a = a.astype(BF16)
    b = b.astype(BF16)
    if dims is None:
        return jnp.dot(a, b, preferred_element_type=F32)
    return lax.dot_general(a, b, dims, preferred_element_type=F32)


def _mm2(x, w_bf16):
    hi = x.astype(BF16)
    lo = (x - hi.astype(F32)).astype(BF16)
    return (jnp.dot(hi, w_bf16, preferred_element_type=F32)
            + jnp.dot(lo, w_bf16, preferred_element_type=F32))


def _mm_f32(a, b):
    return jnp.dot(a, b, preferred_element_type=F32, precision=lax.Precision.HIGHEST)


def _sigmoid(x):
    return 1.0 / (1.0 + jnp.exp(-x))


def _silu(x):
    return x * _sigmoid(x)


def _seg_cumsum(x, seg, rev):
    rows = x.shape[0]
    pos = lax.broadcasted_iota(jnp.int32, x.shape, 0) & (seg - 1)
    sh = 1
    while sh < seg:
        if rev:
            x = x + jnp.where(pos < seg - sh, pltpu.roll(x, rows - sh, axis=0), 0.0)
        else:
            x = x + jnp.where(pos >= sh, pltpu.roll(x, sh, axis=0), 0.0)
        sh *= 2
    return x


def _ada_kernel(cc_ref, w_ref, b_ref, o_ref):
    s = _silu(cc_ref[...])
    w = w_ref[...]
    r0 = jnp.sum(w * s[:, 0:1], axis=0, keepdims=True)
    r1 = jnp.sum(w * s[:, 1:2], axis=0, keepdims=True)
    o_ref[...] = jnp.concatenate([r0, r1], axis=0) + b_ref[...]


def _ada_mods(cc_t, ada_w, ada_b):
    tn = 512
    nmod = ada_w.shape[-1]
    return pl.pallas_call(
        _ada_kernel,
        grid=(DEPTH, nmod // tn),
        in_specs=[pl.BlockSpec((D, 2), lambda l, j: (0, 0)),
                  pl.BlockSpec((None, D, tn), lambda l, j: (l, 0, j)),
                  pl.BlockSpec((None, 1, tn), lambda l, j: (l, 0, j))],
        out_specs=pl.BlockSpec((None, 2, tn), lambda l, j: (l, 0, j)),
        out_shape=jax.ShapeDtypeStruct((DEPTH, 2, nmod), F32),
        compiler_params=_cparams(("arbitrary", "arbitrary")),
        name="ada_mod",
    )(cc_t, ada_w, ada_b.reshape(DEPTH, 1, nmod))


def _row_is_ctx(i, tm, lc):
    return (i * tm + lax.broadcasted_iota(jnp.int32, (tm, 1), 0)) < lc


def _mod_pick(mod_ref, j, is_ctx):
    return jnp.where(is_ctx, mod_ref[1, j:j + 1, :], mod_ref[0, j:j + 1, :])


def _modulated(x, g, mod_ref, is_ctx):
    y = x * lax.rsqrt(jnp.mean(x * x, axis=-1, keepdims=True) + EPS) * g
    return y * (1.0 + _mod_pick(mod_ref, 1, is_ctx)) + _mod_pick(mod_ref, 0, is_ctx)


def _ffn_kernel(lc, tm, nf, x_ref, mod_ref, g_ref, wa_ref, wb_ref, wo_ref, o_ref, xn_ref):
    i = pl.program_id(0)
    j = pl.program_id(1)
    is_ctx = _row_is_ctx(i, tm, lc)

    @pl.when(j == 0)
    def _():
        xn_ref[...] = _modulated(x_ref[...], g_ref[...], mod_ref, is_ctx).astype(BF16)
        o_ref[...] = jnp.zeros_like(o_ref)

    xn = xn_ref[...]
    a = jnp.dot(xn, wa_ref[...], preferred_element_type=F32)
    b = jnp.dot(xn, wb_ref[...], preferred_element_type=F32)
    h = (_silu(a) * b).astype(BF16)
    o_ref[...] += jnp.dot(h, wo_ref[...], preferred_element_type=F32)

    @pl.when(j == nf - 1)
    def _():
        o_ref[...] = x_ref[...] + 0.5 * _mod_pick(mod_ref, 2, is_ctx) * o_ref[...]


def _dense_tm(rows):
    for tm in (768, 512, 256):
        if rows % tm == 0:
            return tm
    raise ValueError(rows)


def _ffn(x, mod3, g, wi, wo, lc):
    rows = x.shape[0]
    tm = _dense_tm(rows)
    tf = 512
    nf = D_FF // tf
    return pl.pallas_call(
        functools.partial(_ffn_kernel, lc, tm, nf),
        grid=(rows // tm, nf),
        in_specs=[pl.BlockSpec((tm, D), lambda i, j: (i, 0)),
                  pl.BlockSpec((2, 3, D), lambda i, j: (0, 0, 0)),
                  pl.BlockSpec((1, D), lambda i, j: (0, 0)),
                  pl.BlockSpec((D, tf), lambda i, j: (0, j)),
                  pl.BlockSpec((D, tf), lambda i, j: (0, j + nf)),
                  pl.BlockSpec((tf, D), lambda i, j: (j, 0))],
        out_specs=pl.BlockSpec((tm, D), lambda i, j: (i, 0)),
        out_shape=jax.ShapeDtypeStruct((rows, D), F32),
        scratch_shapes=[pltpu.VMEM((tm, D), BF16)],
        compiler_params=_cparams(("arbitrary", "arbitrary")),
        name="ffn",
    )(x, mod3, g, wi, wi, wo)


def _win_kernel(lc, tm, x_ref, mod_ref, g_ref, w_ref, o_ref, xn_ref):
    i = pl.program_id(0)

    @pl.when(pl.program_id(1) == 0)
    def _():
        is_ctx = _row_is_ctx(i, tm, lc)
        xn_ref[...] = _modulated(x_ref[...], g_ref[...], mod_ref, is_ctx).astype(BF16)

    o_ref[...] = jnp.dot(xn_ref[...], w_ref[...], preferred_element_type=F32)


def _win(x, mod3, g, w, lc):
    rows = x.shape[0]
    tm = _dense_tm(rows)
    tn = 512
    return pl.pallas_call(
        functools.partial(_win_kernel, lc, tm),
        grid=(rows // tm, P_TOTAL // tn),
        in_specs=[pl.BlockSpec((tm, D), lambda i, j: (i, 0)),
                  pl.BlockSpec((2, 3, D), lambda i, j: (0, 0, 0)),
                  pl.BlockSpec((1, D), lambda i, j: (0, 0)),
                  pl.BlockSpec((D, tn), lambda i, j: (0, j))],
        out_specs=pl.BlockSpec((tm, tn), lambda i, j: (i, j)),
        out_shape=jax.ShapeDtypeStruct((rows, P_TOTAL), F32),
        scratch_shapes=[pltpu.VMEM((tm, D), BF16)],
        compiler_params=_cparams(("arbitrary", "arbitrary")),
        name="w_in",
    )(x, mod3, g, w)


def _merge_kernel(lc, tm, x_ref, mod_ref, ya_ref, yb_ref, yc_ref, yd_ref, gl_ref, wb_ref, wo_ref, o_ref, acc_ref):
    i = pl.program_id(0)
    n = pl.program_id(1)

    @pl.when(n == 0)
    def _():
        acc_ref[...] = jnp.zeros_like(acc_ref)

    for nn, y_ref in enumerate((ya_ref, yb_ref, yc_ref, yd_ref)):
        @pl.when(n == nn)
        def _():
            proj = jnp.dot(y_ref[...].astype(BF16), wb_ref[...], preferred_element_type=F32)
            acc_ref[...] += _sigmoid(gl_ref[...]) * proj

    @pl.when(n == N_BRANCH - 1)
    def _():
        is_ctx = _row_is_ctx(i, tm, lc)
        y = jnp.dot(acc_ref[...].astype(BF16), wo_ref[...], preferred_element_type=F32)
        o_ref[...] = x_ref[...] + _mod_pick(mod_ref, 2, is_ctx) * y


def _merge(x, mod3, ys, p, wb, wo, lc):
    rows = x.shape[0]
    tm = MIX_ROWS
    glb = OFF_GL // D
    yspec = pl.BlockSpec((tm, BR_W), lambda i, n: (i, 0))
    return pl.pallas_call(
        functools.partial(_merge_kernel, lc, tm),
        grid=(rows // tm, N_BRANCH),
        in_specs=[pl.BlockSpec((tm, D), lambda i, n: (i, 0)),
                  pl.BlockSpec((2, 3, D), lambda i, n: (0, 0, 0)),
                  yspec, yspec, yspec, yspec,
                  pl.BlockSpec((tm, D), lambda i, n: (i, glb + n)),
                  pl.BlockSpec((None, BR_W, D), lambda i, n: (n, 0, 0)),
                  pl.BlockSpec((D, D), lambda i, n: (0, 0))],
        out_specs=pl.BlockSpec((tm, D), lambda i, n: (i, 0)),
        out_shape=jax.ShapeDtypeStruct((rows, D), F32),
        scratch_shapes=[pltpu.VMEM((tm, D), F32)],
        compiler_params=_cparams(("arbitrary", "arbitrary")),
        name="merge",
    )(x, mod3, *ys, p, wb, wo)


def _gla_kernel(rev, *refs):
    if rev:
        q_ref, f_ref, i_ref, lb_ref, of_ref, g_ref, nw_ref, y_ref, st_ref, o_s = refs
    else:
        q_ref, f_ref, i_ref, lb_ref, y_ref, st_ref, o_s = refs
    C = GLA_CHUNK
    R = MIX_ROWS

    @pl.when(pl.program_id(1) == 0)
    def _():
        st_ref[...] = jnp.zeros_like(st_ref)

    lb = lb_ref[...]
    f = lb + (1.0 - lb) * _sigmoid(f_ref[...])
    b = _seg_cumsum(jnp.log(f), C, rev)
    k = 1.0 - f
    q = q_ref[...]
    v = i_ref[...]
    pos = lax.broadcasted_iota(jnp.int32, (R, 1), 0) & (C - 1)

    o = jnp.sum(q * k, axis=-1, keepdims=True) * v
    for r in range(8):
        sh = ((R - r) if rev else r) % R
        br, kr, vr = (pltpu.roll(t, sh, axis=0) if r else t for t in (b, k, v))
        for d in (r, r + 8):
            if d == 0:
                continue
            if d >= 8:
                sh8 = (R - 8) if rev else 8
                bd_, kd_, vd_ = (pltpu.roll(t, sh8, axis=0) for t in (br, kr, vr))
            else:
                bd_, kd_, vd_ = br, kr, vr
            valid = (pos < C - d) if rev else (pos >= d)
            e = jnp.exp(jnp.where(valid, b - bd_, NEG))
            att = jnp.sum(q * kd_ * e, axis=-1, keepdims=True)
            o = o + att * vd_
    o_s[...] = o

    qb = q * jnp.exp(b)
    st = st_ref[...]
    order = range(R // C - 1, -1, -1) if rev else range(R // C)
    for c in order:
        sl = slice(c * C, (c + 1) * C)
        bc = b[sl]
        b_end = bc[0:1] if rev else bc[C - 1:C]
        o_s[sl, :] += _mm(qb[sl], st, NT)
        st = st * jnp.exp(b_end) + _mm(v[sl], k[sl] * jnp.exp(b_end - bc), TN)
    st_ref[...] = st

    if rev:
        o = of_ref[...] + o_s[...]
        o = o * lax.rsqrt(jnp.mean(o * o, axis=-1, keepdims=True) + EPS) * nw_ref[...]
        y_ref[...] = o * _silu(g_ref[...])
    else:
        y_ref[...] = o_s[...]


def _blk_order(rev, nb):
    if rev:
        return lambda b: jnp.where(b == 0, 0, nb - b)
    return lambda b: b


def _gla_dir(p, lb, rev, o_fwd=None, norm_w=None):
    rows = p.shape[0]
    nb = rows // MIX_ROWS
    order = _blk_order(rev, nb)
    cb = OFF_HG // HG_DK
    nh = HG_HEADS

    def col(off):
        return pl.BlockSpec((MIX_ROWS, HG_DK), lambda h, b: (order(b), cb + off + h))

    in_specs = [col(0), col(nh * (2 if rev else 1)), col(3 * nh),
                pl.BlockSpec((1, HG_DK), lambda h, b: (0, h))]
    args = [p, p, p, lb]
    if rev:
        in_specs += [pl.BlockSpec((MIX_ROWS, HG_DK), lambda h, b: (order(b), h)),
                     col(4 * nh),
                     pl.BlockSpec((1, HG_DK), lambda h, b: (0, h))]
        args += [o_fwd, p, norm_w]
    return pl.pallas_call(
        functools.partial(_gla_kernel, rev),
        grid=(nh, nb),
        in_specs=in_specs,
        out_specs=pl.BlockSpec((MIX_ROWS, HG_DK), lambda h, b: (order(b), h)),
        out_shape=jax.ShapeDtypeStruct((rows, nh * HG_DK), F32),
        scratch_shapes=[pltpu.VMEM((HG_DK, HG_DK), F32), pltpu.VMEM((MIX_ROWS, HG_DK), F32)],
        compiler_params=_cparams(("arbitrary", "arbitrary")),
        name="gla_rev" if rev else "gla_fwd",
    )(*args)


def _rw_prep_kernel(nb, main_ref, lora_ref, pm_ref, nm_ref, pl_ref, nl_ref, tm_ref, tl_ref,
                    w0_ref, w2_ref, a0_ref, a2_ref, kkw_ref, ka_ref, rk_ref, bd_ref,
                    r_o, kk_o, v_o, gs_o, bonus_o, lw0_o, lw1_o, akk0_o, akk1_o, kd0_o, kd1_o):
    i = pl.program_id(0)
    has_prev = (i >= 2).astype(F32)
    has_next = jnp.logical_and(i != 0, i != nb - 1).astype(F32)
    rowi = lax.broadcasted_iota(jnp.int32, (MIX_ROWS, 1), 0)

    def shift(x, prev_blk, next_blk, taps):
        up = jnp.where(rowi == 0, prev_blk[7:8, :] * has_prev, pltpu.roll(x, 1, axis=0))
        dn = jnp.where(rowi == MIX_ROWS - 1, next_blk[0:1, :] * has_next,
                       pltpu.roll(x, MIX_ROWS - 1, axis=0))
        return taps[0:1] * up + taps[1:2] * x + taps[2:3] * dn

    main = shift(main_ref[...], pm_ref[...], nm_ref[...], tm_ref[...])
    lora = shift(lora_ref[...], pl_ref[...], nl_ref[...], tl_ref[...])
    W = RW_HEADS * RW_HD
    r, k, v, g = (main[:, n * W:(n + 1) * W] for n in range(4))
    bd = bd_ref[...]
    kk = k * kkw_ref[...]
    kk = kk * lax.rsqrt(_mm2(kk * kk, bd) + EPS)
    tl = jnp.tanh(lora)
    kds = []
    for d, (lw_o, akk_o, kd_o) in enumerate(((lw0_o, akk0_o, kd0_o), (lw1_o, akk1_o, kd1_o))):
        z = -(w0_ref[d:d + 1, :] + _mm_f32(tl, w2_ref[d]))
        softplus = jnp.maximum(z, 0.0) + jnp.log(1.0 + jnp.exp(-jnp.abs(z)))
        lw_o[...] = -jnp.exp(-softplus - 0.5)
        a = _sigmoid(a0_ref[d:d + 1, :] + _mm_f32(lora, a2_ref[d]))
        kd = k * (1.0 + (a - 1.0) * ka_ref[...])
        kds.append(kd)
        kd_o[...] = kd
        akk_o[...] = a * kk
    r_o[...] = r
    kk_o[...] = kk
    v_o[...] = v
    gs_o[...] = _sigmoid(g)
    bonus_o[...] = _mm2(r * (kds[0] + kds[1]) * rk_ref[...], bd) * v


def _rw_prep(p, taps_m, taps_l, w0, w2p, a0, a2p, kkw, ka, rk, bd512):
    rows = p.shape[0]
    nb = rows // MIX_ROWS
    W = RW_HEADS * RW_HD
    n8 = rows // 8
    lb = OFF_LORA // 256
    per = MIX_ROWS // 8
    full = lambda shape: pl.BlockSpec(shape, lambda i: (0,) * len(shape))
    in_specs = [pl.BlockSpec((MIX_ROWS, 4 * W), lambda i: (i, 0)),
                pl.BlockSpec((MIX_ROWS, 256), lambda i: (i, lb)),
                pl.BlockSpec((8, 4 * W), lambda i: (jnp.maximum(i * per - 1, 0), 0)),
                pl.BlockSpec((8, 4 * W), lambda i: (jnp.minimum((i + 1) * per, n8 - 1), 0)),
                pl.BlockSpec((8, 256), lambda i: (jnp.maximum(i * per - 1, 0), lb)),
                pl.BlockSpec((8, 256), lambda i: (jnp.minimum((i + 1) * per, n8 - 1), lb)),
                full((3, 4 * W)), full((3, 256)),
                full((2, W)), full((2, 256, W)), full((2, W)), full((2, 256, W)),
                full((1, W)), full((1, W)), full((1, W)), full((W, W))]
    out = jax.ShapeDtypeStruct((rows, W), F32)
    return pl.pallas_call(
        functools.partial(_rw_prep_kernel, nb),
        grid=(nb,),
        in_specs=in_specs,
        out_specs=[pl.BlockSpec((MIX_ROWS, W), lambda i: (i, 0))] * 11,
        out_shape=[out] * 11,
        compiler_params=_cparams(("arbitrary",)),
        name="rw_prep",
    )(p, p, p, p, p, p, taps_m, taps_l, w0, w2p, a0, a2p, kkw, ka, rk, bd512)


def _rw_scan_kernel(rev, *refs):
    if rev:
        (r_ref, kk_ref, v_ref, lw_ref, akk_ref, kd_ref, of_ref, bonus_ref, gs_ref,
         lnw_ref, lnb_ref, bd_ref, y_ref, s_ref) = refs
    else:
        r_ref, kk_ref, v_ref, lw_ref, akk_ref, kd_ref, y_ref, s_ref = refs
    C = RW_CHUNK
    HD = RW_HD
    P2 = 2 * HD

    NP = RW_HEADS // 2

    @pl.when(pl.program_id(0) == 0)
    def _():
        s_ref[...] = jnp.zeros_like(s_ref)

    lw = lw_ref[...]
    cum = _seg_cumsum(lw, C, rev)
    cum_prev = cum - lw

    lane = lax.broadcasted_iota(jnp.int32, (1, P2), 1)
    h0 = lane < HD
    ri = lax.broadcasted_iota(jnp.int32, (P2, P2), 0)
    ci = lax.broadcasted_iota(jnp.int32, (P2, P2), 1)
    same = (ri // C) == (ci // C)
    rt, cs = ri % C, ci % C
    strict = jnp.logical_and(same, (rt < cs) if rev else (rt > cs))
    incl = jnp.logical_and(same, (rt <= cs) if rev else (rt >= cs))
    eye = (ri == ci).astype(F32)
    bdiag = ((ri // HD) == (ci // HD)).astype(F32)

    def stack(x):
        return jnp.concatenate([x, x], axis=0)

    def split_heads(x):
        return jnp.concatenate([jnp.where(h0, x, 0.0), jnp.where(h0, 0.0, x)], axis=0)

    def unstack(x):
        return jnp.where(h0, x[:C], x[C:])

    chunks = list(range(MIX_ROWS // C - 1, -1, -1) if rev else range(MIX_ROWS // C))
    order = [(c, p) for c in chunks for p in range(NP)]

    pre = {}
    for c, p in order:
        sl = slice(c * C, (c + 1) * C)
        pp = slice(p * P2, (p + 1) * P2)
        cum_c = cum[sl, pp]
        cend = cum_c[0:1] if rev else cum_c[C - 1:C]
        e_neg = jnp.exp(-cum_c)
        e_end = jnp.exp(cend - cum_c)
        a_t = kk_ref[sl, pp] * jnp.exp(cum_prev[sl, pp])
        r_t = r_ref[sl, pp] * jnp.exp(cum_c)
        akk = akk_ref[sl, pp]
        kd = kd_ref[sl, pp]
        v = v_ref[sl, pp]
        g = _mm(jnp.concatenate([split_heads(a_t), split_heads(r_t)], axis=0),
                jnp.concatenate([stack(akk * e_neg), stack(kd * e_neg)], axis=0), NT)
        n_ab = jnp.where(strict, g[:P2, :P2], 0.0)
        a_ak = jnp.where(strict, g[:P2, P2:], 0.0)
        a_rb = jnp.where(incl, g[P2:, :P2], 0.0)
        a_rk = jnp.where(incl, g[P2:, P2:], 0.0)
        av = _mm(jnp.concatenate([a_ak, a_rk], axis=0), stack(v))
        pre[c, p] = dict(lhs=jnp.concatenate([a_t, r_t], axis=0), v=v, a_rb=a_rb, av=av[:P2], rkv=av[P2:],
                         kb=jnp.concatenate([kd * e_end, akk * e_end], axis=0), dec=jnp.exp(cend),
                         p=eye - n_ab, m=n_ab)
    for w in pre.values():
        w["m"] = _mm(w["m"], w["m"])
    for _ in range(4):
        for w in pre.values():
            pm = _mm(jnp.concatenate([w["p"], w["m"]], axis=0), w["m"])
            w["p"] = w["p"] + pm[:P2]
            w["m"] = pm[P2:]
    for w in pre.values():
        w["p"] = w["p"] + _mm(w["p"], w["m"])

    s = [s_ref[p] for p in range(NP)]
    for c, p in order:
        w = pre[c, p]
        sl = slice(c * C, (c + 1) * C)
        x = _mm(w["lhs"], s[p], NT)
        u_st = _mm(w["p"], stack(x[:C]) + w["av"])
        o_st = stack(x[C:]) + w["rkv"] - _mm(w["a_rb"], u_st)
        y_ref[sl, p * P2:(p + 1) * P2] = unstack(o_st)
        upd = _mm(jnp.concatenate([w["v"], -unstack(u_st)], axis=0), w["kb"], TN)
        s[p] = s[p] * w["dec"] + bdiag * upd
    for p in range(NP):
        s_ref[p] = s[p]

    if rev:
        o = of_ref[...] + y_ref[...]
        bd = bd_ref[...]
        mu = _mm2(o, bd) * (1.0 / HD)
        oc = o - mu
        var = _mm2(oc * oc, bd) * (1.0 / HD)
        o = oc * lax.rsqrt(var + RW_GN_EPS) * lnw_ref[...] + lnb_ref[...]
        y_ref[...] = (o + bonus_ref[...]) * gs_ref[...]


def _rw_scan(rev, r, kk, v, lw, akk, kd, extra=()):
    rows = r.shape[0]
    nb = rows // MIX_ROWS
    order = _blk_order(rev, nb)
    P2 = 2 * RW_HD
    W = RW_HEADS * RW_HD
    blk = pl.BlockSpec((MIX_ROWS, W), lambda b: (order(b), 0))
    in_specs = [blk] * 6
    args = [r, kk, v, lw, akk, kd]
    if rev:
        o_fwd, bonus, gs, lnw, lnb, bd512 = extra
        vec = pl.BlockSpec((1, W), lambda b: (0, 0))
        in_specs += [blk, blk, blk, vec, vec, pl.BlockSpec((W, W), lambda b: (0, 0))]
        args += [o_fwd, bonus, gs, lnw, lnb, bd512]
    return pl.pallas_call(
        functools.partial(_rw_scan_kernel, rev),
        grid=(nb,),
        in_specs=in_specs,
        out_specs=blk,
        out_shape=jax.ShapeDtypeStruct((rows, W), F32),
        scratch_shapes=[pltpu.VMEM((RW_HEADS // 2, P2, P2), F32)],
        compiler_params=_cparams(("arbitrary",)),
        name="rw_rev" if rev else "rw_fwd",
    )(*args)


def _qk_prep_kernel(nq_ref, nk_ref, nv_ref, wq_ref, wkv_ref, nqn_ref, nkn_ref, wqn_ref, wkn_ref,
                    cos_ref, sin_ref, bd512_ref, bd128_ref,
                    naq_o, nak_o, nav_o, waq_o, wak_o, wav_o):
    bd512 = bd512_ref[...]
    bd128 = bd128_ref[...]

    def hnorm(x, g, bd, hd):
        return x * lax.rsqrt(_mm2(x * x, bd) * (1.0 / hd) + EPS) * g

    def rope(x, cos, sin):
        lane = lax.broadcasted_iota(jnp.int32, x.shape, 1)
        w = x.shape[1]
        partner = jnp.where((lane & 31) < 16, pltpu.roll(x, w - 16, axis=1), pltpu.roll(x, 16, axis=1))
        return x * cos + partner * sin

    def heads_out(o_ref, x, nh, hd):
        for h in range(nh):
            o_ref[h] = x[:, h * hd:(h + 1) * hd].astype(o_ref.dtype)

    heads_out(naq_o, hnorm(nq_ref[...], nqn_ref[...], bd512, NA_HD) * NA_HD ** -0.5, NA_HEADS, NA_HD)
    heads_out(nak_o, hnorm(nk_ref[...], nkn_ref[...], bd512, NA_HD), NA_HEADS, NA_HD)
    heads_out(nav_o, nv_ref[...], NA_HEADS, NA_HD)
    cos = cos_ref[...]
    sin = sin_ref[...]
    wq = hnorm(wq_ref[...], wqn_ref[...], bd512, WA_HD) * WA_HD ** -0.5
    wq = rope(wq, jnp.concatenate([cos] * 4, axis=1), jnp.concatenate([sin] * 4, axis=1))
    heads_out(waq_o, wq, WA_HEADS, WA_HD)
    kv = wkv_ref[...]
    wk = rope(hnorm(kv[:, :128], wkn_ref[...], bd128, WA_HD), cos, sin)
    heads_out(wak_o, wk, WA_KV_HEADS, WA_HD)
    heads_out(wav_o, kv[:, 128:], WA_KV_HEADS, WA_HD)


def _qk_prep(p, nqn, nkn, wqn, wkn, cos, sin, bd512, bd128):
    rows = p.shape[0]
    nb = rows // MIX_ROWS
    nab = OFF_NA // 512
    full = lambda shape: pl.BlockSpec(shape, lambda i: (0,) * len(shape))
    in_specs = [pl.BlockSpec((MIX_ROWS, 512), lambda i: (i, nab)),
                pl.BlockSpec((MIX_ROWS, 512), lambda i: (i, nab + 1)),
                pl.BlockSpec((MIX_ROWS, 512), lambda i: (i, nab + 2)),
                pl.BlockSpec((MIX_ROWS, 512), lambda i: (i, OFF_WA // 512)),
                pl.BlockSpec((MIX_ROWS, 256), lambda i: (i, (OFF_WA + 512) // 256)),
                full((1, 512)), full((1, 512)), full((1, 512)), full((1, 128)),
                pl.BlockSpec((MIX_ROWS, 128), lambda i: (i, 0)),
                pl.BlockSpec((MIX_ROWS, 128), lambda i: (i, 0)),
                full((512, 512)), full((128, 128))]
    hspec = lambda nh: pl.BlockSpec((nh, MIX_ROWS, 64), lambda i: (0, i, 0))
    hshape = lambda nh: jax.ShapeDtypeStruct((nh, rows, 64), BF16)
    return pl.pallas_call(
        _qk_prep_kernel,
        grid=(nb,),
        in_specs=in_specs,
        out_specs=[hspec(8), hspec(8), hspec(8), hspec(8), hspec(2), hspec(2)],
        out_shape=[hshape(8), hshape(8), hshape(8), hshape(8), hshape(2), hshape(2)],
        compiler_params=_cparams(("arbitrary",)),
        name="qk_prep",
    )(p, p, p, p, p, nqn, nkn, wqn, wkn, cos, sin, bd512, bd128)


def _na_kernel(lc, nrows, q_ref, k_ref, v_ref, bias_ref, o_ref):
    W = GRID_W
    nk = NA_WIN_R * W
    kc = k_ref[0:lc, :]
    vc = v_ref[0:lc, :]

    s = _mm(q_ref[0:lc, :], kc, NT)
    e = jnp.exp(s - jnp.max(s, axis=-1, keepdims=True))
    o_ref[0:lc, :] = _mm(e, vc) / jnp.sum(e, axis=-1, keepdims=True)

    def row(i, carry):
        rs = jnp.clip(i - NA_WIN_R // 2, 0, nrows - NA_WIN_R)
        q0 = pl.multiple_of(lc + i * W, W)
        k0 = pl.multiple_of(lc + rs * W, W)
        qi = q_ref[pl.ds(q0, W), :]
        s_loc = _mm(qi, k_ref[pl.ds(k0, nk), :], NT) + bias_ref[i - rs]
        s_ctx = _mm(qi, kc, NT)
        m = jnp.maximum(jnp.max(s_loc, axis=-1, keepdims=True), jnp.max(s_ctx, axis=-1, keepdims=True))
        p_loc = jnp.exp(s_loc - m)
        p_ctx = jnp.exp(s_ctx - m)
        den = jnp.sum(p_loc, axis=-1, keepdims=True) + jnp.sum(p_ctx, axis=-1, keepdims=True)
        o_ref[pl.ds(q0, W), :] = (_mm(p_loc, v_ref[pl.ds(k0, nk), :]) + _mm(p_ctx, vc)) / den
        return carry

    lax.fori_loop(0, nrows, row, 0, unroll=2)


def _na(q, k, v, bias, lc):
    nh, rows, hd = q.shape
    nrows = (rows - lc) // GRID_W
    hblk = pl.BlockSpec((None, rows, hd), lambda h: (h, 0, 0))
    return pl.pallas_call(
        functools.partial(_na_kernel, lc, nrows),
        grid=(nh,),
        in_specs=[hblk, hblk, hblk,
                  pl.BlockSpec((None, NA_WIN_R, GRID_W, NA_WIN_R * GRID_W), lambda h: (h, 0, 0, 0))],
        out_specs=hblk,
        out_shape=jax.ShapeDtypeStruct((nh, rows, hd), F32),
        compiler_params=_cparams(("arbitrary",)),
        name="na_attn",
    )(q, k, v, bias)


def _wa_kernel(lc, t_len, q_ref, k_ref, v_ref, sink_ref, o_ref):
    G = WA_HEADS // WA_KV_HEADS
    B = WA_WINDOW
    nband = 3 * B
    b = pl.program_id(1)
    kc = k_ref[0:lc, :]
    vc = v_ref[0:lc, :]
    sink = sink_ref[...]

    @pl.when(b == 0)
    def _():
        for g in range(G):
            s = _mm(q_ref[g], kc, NT)
            sk = sink[g * B:g * B + 1, :]
            m = jnp.maximum(jnp.max(s, axis=-1, keepdims=True), sk)
            e = jnp.exp(s - m)
            o_ref[g] = _mm(e, vc) / (jnp.sum(e, axis=-1, keepdims=True) + jnp.exp(sk - m))

    @pl.when(b > 0)
    def _():
        qoff = lax.broadcasted_iota(jnp.int32, (G * B, nband), 0) & (B - 1)
        koff = lax.broadcasted_iota(jnp.int32, (G * B, nband), 1)
        for j in range(MIX_ROWS // B):
            n = (b - 1) * (MIX_ROWS // B) + j
            start = jnp.clip((n - 1) * B, 0, t_len - nband)
            k0 = pl.multiple_of(lc + start, B)
            qs = jnp.concatenate([q_ref[g, j * B:(j + 1) * B, :] for g in range(G)], axis=0)
            valid = jnp.abs((n * B + qoff) - (start + koff)) <= WA_WINDOW
            s_loc = jnp.where(valid, _mm(qs, k_ref[pl.ds(k0, nband), :], NT), NEG)
            s_ctx = _mm(qs, kc, NT)
            m = jnp.maximum(jnp.maximum(jnp.max(s_loc, axis=-1, keepdims=True),
                                        jnp.max(s_ctx, axis=-1, keepdims=True)), sink)
            p_loc = jnp.exp(s_loc - m)
            p_ctx = jnp.exp(s_ctx - m)
            den = (jnp.sum(p_loc, axis=-1, keepdims=True) + jnp.sum(p_ctx, axis=-1, keepdims=True)
                   + jnp.exp(sink - m))
            o = (_mm(p_loc, v_ref[pl.ds(k0, nband), :]) + _mm(p_ctx, vc)) / den
            for g in range(G):
                o_ref[g, j * B:(j + 1) * B, :] = o[g * B:(g + 1) * B]


def _wa(q, k, v, sink_col, lc):
    nh, rows, hd = q.shape
    G = WA_HEADS // WA_KV_HEADS
    kvblk = pl.BlockSpec((None, rows, hd), lambda h, b: (h, 0, 0))
    qblk = pl.BlockSpec((G, MIX_ROWS, hd), lambda h, b: (h, b, 0))
    return pl.pallas_call(
        functools.partial(_wa_kernel, lc, rows - lc),
        grid=(WA_KV_HEADS, rows // MIX_ROWS),
        in_specs=[qblk, kvblk, kvblk,
                  pl.BlockSpec((None, G * WA_WINDOW, 1), lambda h, b: (h, 0, 0))],
        out_specs=qblk,
        out_shape=jax.ShapeDtypeStruct((nh, rows, hd), F32),
        compiler_params=_cparams(("arbitrary", "arbitrary")),
        name="wa_attn",
    )(q, k, v, sink_col)


def _block_diag(n, blk):
    idx = np.arange(n) // blk
    return jnp.asarray(idx[:, None] == idx[None, :], dtype=BF16)


def _na_bias_table(rpb):
    W, wr, wc = GRID_W, NA_WIN_R, NA_WIN_C
    j = np.arange(W)
    cstart = np.clip(j - wc // 2, 0, W - wc)
    cabs = np.arange(W)
    inwin = (cabs[None, :] >= cstart[:, None]) & (cabs[None, :] < cstart[:, None] + wc)
    cb = cabs[None, :] - j[:, None] + wc - 1
    nh, nro, nco = rpb.shape
    onehot = (cb[None] == np.arange(nco)[:, None, None]) & inwin[None]
    t = jnp.dot(rpb.reshape(nh * nro, nco), jnp.asarray(onehot.reshape(nco, W * W), F32),
                precision=lax.Precision.HIGHEST).reshape(nh, nro, W, W)
    t = t + jnp.asarray(np.where(inwin, 0.0, NEG), F32)
    tab = jnp.stack([t[:, wr - 1 - cfg:2 * wr - 1 - cfg] for cfg in range(wr)], axis=1)
    return tab.transpose(0, 1, 3, 2, 4).reshape(nh, wr, W, wr * W)


def _rope_tables(lc, t_len):
    pos = np.arange(t_len)
    n = WA_HD // 4
    inv = ROPE_BASE ** (-np.arange(n, dtype=np.float64) / n)
    ang_r = (pos // GRID_W)[:, None] * inv[None, :]
    ang_c = (pos % GRID_W)[:, None] * inv[None, :]
    cos = np.concatenate([np.cos(ang_r)] * 2 + [np.cos(ang_c)] * 2, axis=1)
    sin = np.concatenate([-np.sin(ang_r), np.sin(ang_r), -np.sin(ang_c), np.sin(ang_c)], axis=1)
    cos = np.concatenate([np.ones((lc, WA_HD)), cos], axis=0)
    sin = np.concatenate([np.zeros((lc, WA_HD)), sin], axis=0)
    return (jnp.asarray(np.tile(cos, (1, 2)), dtype=F32), jnp.asarray(np.tile(sin, (1, 2)), dtype=F32))


def _permute_w_in(w):
    hg, rw, na, wa = 2560, 2304, 1536, 768
    s_rw, s_na, s_wa, s_gl = hg, hg + rw, hg + rw + na, hg + rw + na + wa
    return jnp.concatenate([w[:, s_rw:s_rw + 2048], w[:, :hg], w[:, s_na:s_wa], w[:, s_gl:],
                            w[:, s_wa:s_gl], w[:, s_rw + 2048:s_na]], axis=1)


def _lora_pad(w, row0):
    out = jnp.zeros((2, 256, w.shape[-1]), F32)
    for d in range(2):
        out = out.at[d, row0 + 64 * d:row0 + 64 * (d + 1)].set(w[d])
    return out


def _token_mixing(p, lc, layer, hg_lb, hg_norm, rw_shift, rw_w0, rw_w2, rw_a0, rw_a2, rw_kk, rw_ka, rw_rk,
                  rw_ln_w, rw_ln_b, na_qn, na_kn, na_rpb, wa_qn, wa_kn, wa_sink, tables):
    rows = p.shape[0]
    bd512, bd128, cos, sin = tables
    cum = jnp.cumsum(jax.nn.softmax(hg_lb.astype(F32), axis=1), axis=1)
    lbs = cum[:, layer] - cum[:, 0]
    o_f = _gla_dir(p, lbs[0:1], False)
    y_a = _gla_dir(p, lbs[1:2], True, o_f, hg_norm[None])
    prep = _rw_prep(p, rw_shift[:, :2048], rw_shift[:, 2048:], rw_w0, _lora_pad(rw_w2, 0), rw_a0,
                    _lora_pad(rw_a2, 128), rw_kk[None], rw_ka[None], rw_rk.reshape(1, -1), bd512)
    r, kk, v, gs, bonus, lw0, lw1, akk0, akk1, kd0, kd1 = prep
    o_f = _rw_scan(False, r, kk, v, lw0, akk0, kd0)
    y_b = _rw_scan(True, r, kk, v, lw1, akk1, kd1,
                   (o_f, bonus, gs, rw_ln_w[None], rw_ln_b[None], bd512))
    tile = lambda g, n: jnp.tile(g, n)[None]
    naq, nak, nav, waq, wak, wav = _qk_prep(p, tile(na_qn, 8), tile(na_kn, 8), tile(wa_qn, 8),
                                            tile(wa_kn, 2), cos, sin, bd512, bd128)
    y_c = _na(naq, nak, nav, _na_bias_table(na_rpb), lc)
    G = WA_HEADS // WA_KV_HEADS
    sink_col = jnp.repeat(wa_sink.reshape(WA_KV_HEADS, G), WA_WINDOW, axis=1)[..., None]
    y_d = _wa(waq, wak, wav, sink_col, lc)
    flat = lambda y: y.transpose(1, 0, 2).reshape(rows, -1)
    return y_a, y_b, flat(y_c), flat(y_d)


def kernel(x, c, ctx, c_ctx, ada_w, ada_b, norm_ffn1, norm_mix, norm_ffn2, ffn1_wi, ffn1_wo, ffn2_wi, ffn2_wo, w_in, hg_lb, hg_norm, rw_shift, rw_w0, rw_w2, rw_a0, rw_a2, rw_kk, rw_ka, rw_rk, rw_ln_w, rw_ln_b, na_qn, na_kn, na_rpb, wa_qn, wa_kn, wa_sink, w_branch, w_out):
    assert x.shape[0] == 1 and ctx.shape[1] == MIX_ROWS
    lc = ctx.shape[1]
    t_len = x.shape[1]
    xa = jnp.concatenate([ctx[0], x[0]], axis=0)
    cc_t = jnp.stack([c[0], c_ctx], axis=1)
    mods = _ada_mods(cc_t, ada_w, ada_b).reshape(DEPTH, 2, N_MOD, D)
    tables = (_block_diag(512, 64), _block_diag(128, 64)) + _rope_tables(lc, t_len)
    for l in range(DEPTH):
        m = mods[l]
        xa = _ffn(xa, m[:, 0:3], norm_ffn1[l][None], ffn1_wi[l].astype(BF16), ffn1_wo[l].astype(BF16), lc)
        p = _win(xa, m[:, 3:6], norm_mix[l][None], _permute_w_in(w_in[l]).astype(BF16), lc)
        yb = _token_mixing(p, lc, l, hg_lb, hg_norm[l], rw_shift[l], rw_w0[l], rw_w2[l], rw_a0[l], rw_a2[l],
                           rw_kk[l], rw_ka[l], rw_rk[l], rw_ln_w[l], rw_ln_b[l], na_qn[l], na_kn[l],
                           na_rpb[l], wa_qn[l], wa_kn[l], wa_sink[l], tables)
        xa = _merge(xa, m[:, 3:6], yb, p, w_branch[l].astype(BF16), w_out[l].astype(BF16), lc)
        xa = _ffn(xa, m[:, 6:9], norm_ffn2[l][None], ffn2_wi[l].astype(BF16), ffn2_wo[l].astype(BF16), lc)
    return xa[lc:][None]
```

```python
import functools

import numpy as np
import jax
import jax.numpy as jnp
from jax import lax
from jax.experimental import pallas as pl
from jax.experimental.pallas import tpu as pltpu

F32 = jnp.float32
BF16 = jnp.bfloat16

D = 2048
DEPTH = 2
GRID_W = 64
EPS = 1e-6
D_FF = 5632
N_MOD = 9
N_BRANCH = 4
BR_W = 512
HG_HEADS, HG_DK = 4, 128
RW_HEADS, RW_HD = 8, 64
RW_LORA = 64
RW_GN_EPS = 64e-5
NA_HEADS, NA_HD = 8, 64
NA_WIN_R, NA_WIN_C = 8, 16
WA_HEADS, WA_KV_HEADS, WA_HD = 8, 2, 64
WA_WINDOW = 128
ROPE_BASE = 10000.0

OFF_RW, OFF_HG, OFF_NA, OFF_GL, OFF_WA, OFF_LORA = 0, 2048, 4608, 6144, 14336, 15104
P_TOTAL = 15360

MIX_ROWS = 256
GLA_CHUNK = 16
RW_CHUNK = 64
NEG = -1e30
VMEM_LIMIT = 56 * 1024 * 1024

NT = (((1,), (1,)), ((), ()))
TN = (((0,), (0,)), ((), ()))


def _cparams(sem):
    return pltpu.CompilerParams(dimension_semantics=sem, vmem_limit_bytes=VMEM_LIMIT)


def _mm(a, b, dims=None):
    a = a.astype(BF16)
    b = b.astype(BF16)
    if dims is None:
        return jnp.dot(a, b, preferred_element_type=F32)
    return lax.dot_general(a, b, dims, preferred_element_type=F32)


def _mm2(x, w_bf16):
    hi = x.astype(BF16)
    lo = (x - hi.astype(F32)).astype(BF16)
    return (jnp.dot(hi, w_bf16, preferred_element_type=F32)
            + jnp.dot(lo, w_bf16, preferred_element_type=F32))


def _mm_f32(a, b):
    return jnp.dot(a, b, preferred_element_type=F32, precision=lax.Precision.HIGHEST)


def _sigmoid(x):
    return 1.0 / (1.0 + jnp.exp(-x))


def _silu(x):
    return x * _sigmoid(x)


def _seg_cumsum(x, tri):
    w = x.shape[1]
    hi = x.astype(BF16)
    r1 = x - hi.astype(F32)
    mid = r1.astype(BF16)
    lo = (r1 - mid.astype(F32)).astype(BF16)
    y = jnp.dot(tri, jnp.concatenate([hi, mid, lo], axis=1), preferred_element_type=F32)
    return y[:, :w] + y[:, w:2 * w] + y[:, 2 * w:]


def _seg_tri(rows, seg, rev):
    r = np.arange(rows)
    same = (r[:, None] // seg) == (r[None, :] // seg)
    tri = (r[None, :] >= r[:, None]) if rev else (r[None, :] <= r[:, None])
    return jnp.asarray(same & tri, dtype=BF16)


def _ada_kernel(cc_ref, w_ref, b_ref, o_ref):
    s = _silu(cc_ref[...])
    w = w_ref[...]
    r0 = jnp.sum(w * s[:, 0:1], axis=0, keepdims=True)
    r1 = jnp.sum(w * s[:, 1:2], axis=0, keepdims=True)
    o_ref[...] = jnp.concatenate([r0, r1], axis=0) + b_ref[...]


def _ada_mods(cc_t, ada_w, ada_b):
    tn = 512
    nmod = ada_w.shape[-1]
    return pl.pallas_call(
        _ada_kernel,
        grid=(DEPTH, nmod // tn),
        in_specs=[pl.BlockSpec((D, 2), lambda l, j: (0, 0)),
                  pl.BlockSpec((None, D, tn), lambda l, j: (l, 0, j)),
                  pl.BlockSpec((None, 1, tn), lambda l, j: (l, 0, j))],
        out_specs=pl.BlockSpec((None, 2, tn), lambda l, j: (l, 0, j)),
        out_shape=jax.ShapeDtypeStruct((DEPTH, 2, nmod), F32),
        compiler_params=_cparams(("arbitrary", "arbitrary")),
        name="ada_mod",
    )(cc_t, ada_w, ada_b.reshape(DEPTH, 1, nmod))


def _row_is_ctx(i, tm, lc):
    return (i * tm + lax.broadcasted_iota(jnp.int32, (tm, 1), 0)) < lc


def _mod_pick(mod_ref, j, is_ctx):
    return jnp.where(is_ctx, mod_ref[1, j:j + 1, :], mod_ref[0, j:j + 1, :])


def _modulated(x, g, mod_ref, is_ctx):
    y = x * lax.rsqrt(jnp.mean(x * x, axis=-1, keepdims=True) + EPS) * g
    return y * (1.0 + _mod_pick(mod_ref, 1, is_ctx)) + _mod_pick(mod_ref, 0, is_ctx)


def _ffn_kernel(lc, tm, nf, x_ref, mod_ref, g_ref, wa_ref, wb_ref, wo_ref, o_ref, xn_ref):
    i = pl.program_id(0)
    j = pl.program_id(1)
    is_ctx = _row_is_ctx(i, tm, lc)

    @pl.when(j == 0)
    def _():
        xn_ref[...] = _modulated(x_ref[...], g_ref[...], mod_ref, is_ctx).astype(BF16)
        o_ref[...] = jnp.zeros_like(o_ref)

    xn = xn_ref[...]
    a = jnp.dot(xn, wa_ref[...], preferred_element_type=F32)
    b = jnp.dot(xn, wb_ref[...], preferred_element_type=F32)
    h = (_silu(a) * b).astype(BF16)
    o_ref[...] += jnp.dot(h, wo_ref[...], preferred_element_type=F32)

    @pl.when(j == nf - 1)
    def _():
        o_ref[...] = x_ref[...] + 0.5 * _mod_pick(mod_ref, 2, is_ctx) * o_ref[...]


def _dense_tm(rows):
    for tm in (768, 512, 256):
        if rows % tm == 0:
            return tm
    raise ValueError(rows)


def _ffn(x, mod3, g, wi, wo, lc):
    rows = x.shape[0]
    tm = _dense_tm(rows)
    tf = 512
    nf = D_FF // tf
    return pl.pallas_call(
        functools.partial(_ffn_kernel, lc, tm, nf),
        grid=(rows // tm, nf),
        in_specs=[pl.BlockSpec((tm, D), lambda i, j: (i, 0)),
                  pl.BlockSpec((2, 3, D), lambda i, j: (0, 0, 0)),
                  pl.BlockSpec((1, D), lambda i, j: (0, 0)),
                  pl.BlockSpec((D, tf), lambda i, j: (0, j)),
                  pl.BlockSpec((D, tf), lambda i, j: (0, j + nf)),
                  pl.BlockSpec((tf, D), lambda i, j: (j, 0))],
        out_specs=pl.BlockSpec((tm, D), lambda i, j: (i, 0)),
        out_shape=jax.ShapeDtypeStruct((rows, D), F32),
        scratch_shapes=[pltpu.VMEM((tm, D), BF16)],
        compiler_params=_cparams(("arbitrary", "arbitrary")),
        name="ffn",
    )(x, mod3, g, wi, wi, wo)


def _win_kernel(lc, tm, x_ref, mod_ref, g_ref, w_ref, o_ref, xn_ref):
    i = pl.program_id(0)

    @pl.when(pl.program_id(1) == 0)
    def _():
        is_ctx = _row_is_ctx(i, tm, lc)
        xn_ref[...] = _modulated(x_ref[...], g_ref[...], mod_ref, is_ctx).astype(BF16)

    o_ref[...] = jnp.dot(xn_ref[...], w_ref[...], preferred_element_type=F32)


def _win(x, mod3, g, w, lc):
    rows = x.shape[0]
    tm = _dense_tm(rows)
    tn = 1536
    return pl.pallas_call(
        functools.partial(_win_kernel, lc, tm),
        grid=(rows // tm, P_TOTAL // tn),
        in_specs=[pl.BlockSpec((tm, D), lambda i, j: (i, 0)),
                  pl.BlockSpec((2, 3, D), lambda i, j: (0, 0, 0)),
                  pl.BlockSpec((1, D), lambda i, j: (0, 0)),
                  pl.BlockSpec((D, tn), lambda i, j: (0, j))],
        out_specs=pl.BlockSpec((tm, tn), lambda i, j: (i, j)),
        out_shape=jax.ShapeDtypeStruct((rows, P_TOTAL), F32),
        scratch_shapes=[pltpu.VMEM((tm, D), BF16)],
        compiler_params=_cparams(("arbitrary", "arbitrary")),
        name="w_in",
    )(x, mod3, g, w)


def _merge_kernel(lc, tm, x_ref, mod_ref, ya_ref, yb_ref, yc_ref, yd_ref, gl_ref, wb_ref, wo_ref, o_ref, acc_ref):
    i = pl.program_id(0)
    n = pl.program_id(1)

    @pl.when(n == 0)
    def _():
        acc_ref[...] = jnp.zeros_like(acc_ref)

    for nn, y_ref in enumerate((ya_ref, yb_ref, yc_ref, yd_ref)):
        @pl.when(n == nn)
        def _():
            proj = jnp.dot(y_ref[...].astype(BF16), wb_ref[...], preferred_element_type=F32)
            acc_ref[...] += _sigmoid(gl_ref[...]) * proj

    @pl.when(n == N_BRANCH - 1)
    def _():
        is_ctx = _row_is_ctx(i, tm, lc)
        y = jnp.dot(acc_ref[...].astype(BF16), wo_ref[...], preferred_element_type=F32)
        o_ref[...] = x_ref[...] + _mod_pick(mod_ref, 2, is_ctx) * y


def _merge(x, mod3, ys, p, wb, wo, lc):
    rows = x.shape[0]
    tm = MIX_ROWS
    glb = OFF_GL // D
    yspec = pl.BlockSpec((tm, BR_W), lambda i, n: (i, 0))
    return pl.pallas_call(
        functools.partial(_merge_kernel, lc, tm),
        grid=(rows // tm, N_BRANCH),
        in_specs=[pl.BlockSpec((tm, D), lambda i, n: (i, 0)),
                  pl.BlockSpec((2, 3, D), lambda i, n: (0, 0, 0)),
                  yspec, yspec, yspec, yspec,
                  pl.BlockSpec((tm, D), lambda i, n: (i, glb + n)),
                  pl.BlockSpec((None, BR_W, D), lambda i, n: (n, 0, 0)),
                  pl.BlockSpec((D, D), lambda i, n: (0, 0))],
        out_specs=pl.BlockSpec((tm, D), lambda i, n: (i, 0)),
        out_shape=jax.ShapeDtypeStruct((rows, D), F32),
        scratch_shapes=[pltpu.VMEM((tm, D), F32)],
        compiler_params=_cparams(("arbitrary", "arbitrary")),
        name="merge",
    )(x, mod3, *ys, p, wb, wo)


def _gla_kernel(rev, *refs):
    if rev:
        q_ref, f_ref, i_ref, lb_ref, tri_ref, of_ref, g_ref, nw_ref, y_ref, st_ref = refs
    else:
        q_ref, f_ref, i_ref, lb_ref, tri_ref, y_ref, st_ref = refs
    C = GLA_CHUNK
    R = MIX_ROWS
    NH, DK = HG_HEADS, HG_DK

    @pl.when(pl.program_id(0) == 0)
    def _():
        st_ref[...] = jnp.zeros_like(st_ref)

    lb = lb_ref[...]
    f_all = lb + (1.0 - lb) * _sigmoid(f_ref[...])
    b_all = _seg_cumsum(jnp.log(f_all), tri_ref[...])
    pos = lax.broadcasted_iota(jnp.int32, (R, 1), 0) & (C - 1)
    chunks = list(range(R // C - 1, -1, -1) if rev else range(R // C))

    heads = []
    for h in range(NH):
        hs = slice(h * DK, (h + 1) * DK)
        b = b_all[:, hs]
        k = 1.0 - f_all[:, hs]
        q = q_ref[:, hs]
        v = i_ref[:, hs]
        o = jnp.sum(q * k, axis=-1, keepdims=True) * v
        for r in range(8):
            sh = ((R - r) if rev else r) % R
            br, kr, vr = (pltpu.roll(t, sh, axis=0) if r else t for t in (b, k, v))
            for d in (r, r + 8):
                if d == 0:
                    continue
                if d >= 8:
                    sh8 = (R - 8) if rev else 8
                    bd_, kd_, vd_ = (pltpu.roll(t, sh8, axis=0) for t in (br, kr, vr))
                else:
                    bd_, kd_, vd_ = br, kr, vr
                valid = (pos < C - d) if rev else (pos >= d)
                e = jnp.exp(jnp.where(valid, b - bd_, NEG))
                att = jnp.sum(q * kd_ * e, axis=-1, keepdims=True)
                o = o + att * vd_
        upd, dec = {}, {}
        for c in chunks:
            sl = slice(c * C, (c + 1) * C)
            b_end = b[sl][0:1] if rev else b[sl][C - 1:C]
            dec[c] = jnp.exp(b_end)
            upd[c] = _mm(v[sl], k[sl] * jnp.exp(b_end - b[sl]), TN)
        heads.append(dict(o=o, qb=q * jnp.exp(b), upd=upd, dec=dec, st=st_ref[h]))

    for c in chunks:
        for w in heads:
            w[c] = w["st"]
            w["st"] = w["st"] * w["dec"][c] + w["upd"][c]
    outs = []
    for h, w in enumerate(heads):
        st_ref[h] = w["st"]
        inter = {c: _mm(w["qb"][c * C:(c + 1) * C], w[c], NT) for c in chunks}
        outs.append(w["o"] + jnp.concatenate([inter[c] for c in range(R // C)], axis=0))

    for h, o in enumerate(outs):
        hs = slice(h * DK, (h + 1) * DK)
        if rev:
            o = of_ref[:, hs] + o
            o = o * lax.rsqrt(jnp.mean(o * o, axis=-1, keepdims=True) + EPS) * nw_ref[:, hs]
            o = o * _silu(g_ref[:, hs])
        y_ref[:, hs] = o


def _blk_order(rev, nb):
    if rev:
        return lambda b: jnp.where(b == 0, 0, nb - b)
    return lambda b: b


def _gla_dir(p, lb, rev, o_fwd=None, norm_w=None):
    rows = p.shape[0]
    nb = rows // MIX_ROWS
    order = _blk_order(rev, nb)
    W = HG_HEADS * HG_DK
    cb = OFF_HG // W

    def col(n):
        return pl.BlockSpec((MIX_ROWS, W), lambda b: (order(b), cb + n))

    vec = pl.BlockSpec((1, W), lambda b: (0, 0))
    in_specs = [col(0), col(2 if rev else 1), col(3), vec,
                pl.BlockSpec((MIX_ROWS, MIX_ROWS), lambda b: (0, 0))]
    args = [p, p, p, lb, _seg_tri(MIX_ROWS, GLA_CHUNK, rev)]
    if rev:
        in_specs += [pl.BlockSpec((MIX_ROWS, W), lambda b: (order(b), 0)), col(4), vec]
        args += [o_fwd, p, norm_w]
    return pl.pallas_call(
        functools.partial(_gla_kernel, rev),
        grid=(nb,),
        in_specs=in_specs,
        out_specs=pl.BlockSpec((MIX_ROWS, W), lambda b: (order(b), 0)),
        out_shape=jax.ShapeDtypeStruct((rows, W), F32),
        scratch_shapes=[pltpu.VMEM((HG_HEADS, HG_DK, HG_DK), F32)],
        compiler_params=_cparams(("arbitrary",)),
        name="gla_rev" if rev else "gla_fwd",
    )(*args)


def _rw_prep_kernel(nb, main_ref, lora_ref, pm_ref, nm_ref, pl_ref, nl_ref, tm_ref, tl_ref,
                    w0_ref, w2_ref, a0_ref, a2_ref, kkw_ref, ka_ref, rk_ref, bd_ref,
                    r_o, kk_o, v_o, gs_o, bonus_o, lw0_o, lw1_o, akk0_o, akk1_o, kd0_o, kd1_o):
    i = pl.program_id(0)
    has_prev = (i >= 2).astype(F32)
    has_next = jnp.logical_and(i != 0, i != nb - 1).astype(F32)
    rowi = lax.broadcasted_iota(jnp.int32, (MIX_ROWS, 1), 0)

    def shift(x, prev_blk, next_blk, taps):
        up = jnp.where(rowi == 0, prev_blk[7:8, :] * has_prev, pltpu.roll(x, 1, axis=0))
        dn = jnp.where(rowi == MIX_ROWS - 1, next_blk[0:1, :] * has_next,
                       pltpu.roll(x, MIX_ROWS - 1, axis=0))
        return taps[0:1] * up + taps[1:2] * x + taps[2:3] * dn

    main = shift(main_ref[...], pm_ref[...], nm_ref[...], tm_ref[...])
    lora = shift(lora_ref[...], pl_ref[...], nl_ref[...], tl_ref[...])
    W = RW_HEADS * RW_HD
    r, k, v, g = (main[:, n * W:(n + 1) * W] for n in range(4))
    bd = bd_ref[...]
    kk = k * kkw_ref[...]
    kk = kk * lax.rsqrt(_mm2(kk * kk, bd) + EPS)
    tl = jnp.tanh(lora)
    kds = []
    for d, (lw_o, akk_o, kd_o) in enumerate(((lw0_o, akk0_o, kd0_o), (lw1_o, akk1_o, kd1_o))):
        z = -(w0_ref[d:d + 1, :] + _mm_f32(tl, w2_ref[d]))
        softplus = jnp.maximum(z, 0.0) + jnp.log(1.0 + jnp.exp(-jnp.abs(z)))
        lw_o[...] = -jnp.exp(-softplus - 0.5)
        a = _sigmoid(a0_ref[d:d + 1, :] + _mm_f32(lora, a2_ref[d]))
        kd = k * (1.0 + (a - 1.0) * ka_ref[...])
        kds.append(kd)
        kd_o[...] = kd
        akk_o[...] = a * kk
    r_o[...] = r
    kk_o[...] = kk
    v_o[...] = v
    gs_o[...] = _sigmoid(g)
    bonus_o[...] = _mm2(r * (kds[0] + kds[1]) * rk_ref[...], bd) * v


def _rw_prep(p, taps_m, taps_l, w0, w2p, a0, a2p, kkw, ka, rk, bd512):
    rows = p.shape[0]
    nb = rows // MIX_ROWS
    W = RW_HEADS * RW_HD
    n8 = rows // 8
    lb = OFF_LORA // 256
    per = MIX_ROWS // 8
    full = lambda shape: pl.BlockSpec(shape, lambda i: (0,) * len(shape))
    in_specs = [pl.BlockSpec((MIX_ROWS, 4 * W), lambda i: (i, 0)),
                pl.BlockSpec((MIX_ROWS, 256), lambda i: (i, lb)),
                pl.BlockSpec((8, 4 * W), lambda i: (jnp.maximum(i * per - 1, 0), 0)),
                pl.BlockSpec((8, 4 * W), lambda i: (jnp.minimum((i + 1) * per, n8 - 1), 0)),
                pl.BlockSpec((8, 256), lambda i: (jnp.maximum(i * per - 1, 0), lb)),
                pl.BlockSpec((8, 256), lambda i: (jnp.minimum((i + 1) * per, n8 - 1), lb)),
                full((3, 4 * W)), full((3, 256)),
                full((2, W)), full((2, 256, W)), full((2, W)), full((2, 256, W)),
                full((1, W)), full((1, W)), full((1, W)), full((W, W))]
    out = jax.ShapeDtypeStruct((rows, W), F32)
    return pl.pallas_call(
        functools.partial(_rw_prep_kernel, nb),
        grid=(nb,),
        in_specs=in_specs,
        out_specs=[pl.BlockSpec((MIX_ROWS, W), lambda i: (i, 0))] * 11,
        out_shape=[out] * 11,
        compiler_params=_cparams(("arbitrary",)),
        name="rw_prep",
    )(p, p, p, p, p, p, taps_m, taps_l, w0, w2p, a0, a2p, kkw, ka, rk, bd512)


def _rw_scan_kernel(rev, *refs):
    if rev:
        (r_ref, kk_ref, v_ref, lw_ref, akk_ref, kd_ref, tri_ref, of_ref, bonus_ref, gs_ref,
         lnw_ref, lnb_ref, bd_ref, y_ref, s_ref) = refs
    else:
        r_ref, kk_ref, v_ref, lw_ref, akk_ref, kd_ref, tri_ref, y_ref, s_ref = refs
    C = RW_CHUNK
    HD = RW_HD
    P2 = 2 * HD

    NP = RW_HEADS // 2

    @pl.when(pl.program_id(0) == 0)
    def _():
        s_ref[...] = jnp.zeros_like(s_ref)

    lw = lw_ref[...]
    cum = _seg_cumsum(lw, tri_ref[...])
    cum_prev = cum - lw

    lane = lax.broadcasted_iota(jnp.int32, (1, P2), 1)
    h0 = lane < HD
    ri = lax.broadcasted_iota(jnp.int32, (P2, P2), 0)
    ci = lax.broadcasted_iota(jnp.int32, (P2, P2), 1)
    same = (ri // C) == (ci // C)
    rt, cs = ri % C, ci % C
    strict = jnp.logical_and(same, (rt < cs) if rev else (rt > cs))
    incl = jnp.logical_and(same, (rt <= cs) if rev else (rt >= cs))
    eye = (ri == ci).astype(F32)
    bdiag = ((ri // HD) == (ci // HD)).astype(F32)

    def stack(x):
        return jnp.concatenate([x, x], axis=0)

    def split_heads(x):
        return jnp.concatenate([jnp.where(h0, x, 0.0), jnp.where(h0, 0.0, x)], axis=0)

    def unstack(x):
        return jnp.where(h0, x[:C], x[C:])

    chunks = list(range(MIX_ROWS // C - 1, -1, -1) if rev else range(MIX_ROWS // C))
    order = [(c, p) for c in chunks for p in range(NP)]

    pre = {}
    for c, p in order:
        sl = slice(c * C, (c + 1) * C)
        pp = slice(p * P2, (p + 1) * P2)
        cum_c = cum[sl, pp]
        cend = cum_c[0:1] if rev else cum_c[C - 1:C]
        e_neg = jnp.exp(-cum_c)
        e_end = jnp.exp(cend - cum_c)
        a_t = kk_ref[sl, pp] * jnp.exp(cum_prev[sl, pp])
        r_t = r_ref[sl, pp] * jnp.exp(cum_c)
        akk = akk_ref[sl, pp]
        kd = kd_ref[sl, pp]
        v = v_ref[sl, pp]
        pre[c, p] = dict(a_t=a_t, r_t=r_t, v=v, kh=kd * e_end, bh=akk * e_end, dec=jnp.exp(cend),
                         lhs=jnp.concatenate([split_heads(a_t), split_heads(r_t)], axis=0),
                         rhs=jnp.concatenate([stack(akk * e_neg), stack(kd * e_neg)], axis=0))
    for w in pre.values():
        g = _mm(w["lhs"], w["rhs"], NT)
        w["m"] = jnp.where(strict, g[:P2, :P2], 0.0)
        w["a_ak"] = jnp.where(strict, g[:P2, P2:], 0.0)
        w["a_rb"] = jnp.where(incl, g[P2:, :P2], 0.0)
        w["a_rk"] = jnp.where(incl, g[P2:, P2:], 0.0)
        w["p"] = eye - w["m"]
    for w in pre.values():
        av = _mm(jnp.concatenate([w["a_ak"], w["a_rk"]], axis=0), stack(w["v"]))
        w["av"], w["rkv"] = av[:P2], av[P2:]
    for w in pre.values():
        w["m"] = _mm(w["m"], w["m"])
    for _ in range(4):
        for w in pre.values():
            pm = _mm(jnp.concatenate([w["p"], w["m"]], axis=0), w["m"])
            w["p"] = w["p"] + pm[:P2]
            w["m"] = pm[P2:]
    for w in pre.values():
        w["p"] = w["p"] + _mm(w["p"], w["m"])
    for w in pre.values():
        w["tt"] = _mm(w["p"], jnp.concatenate([stack(w["a_t"]), w["av"]], axis=1))
        w["z"] = _mm(w["p"], stack(w["bh"]), TN)
    for w in pre.values():
        ar = _mm(w["a_rb"], w["tt"])
        w["oa"] = stack(w["r_t"]) - ar[:, :P2]
        w["oc"] = w["rkv"] - ar[:, P2:]
        w["pz"] = bdiag * _mm(split_heads(w["a_t"]), w["z"], TN)
        w["kz"] = stack(w["kh"]) - _mm(w["a_ak"], w["z"], TN)
    for w in pre.values():
        w["q"] = bdiag * _mm(split_heads(w["v"]), w["kz"], TN)

    s = [s_ref[p] for p in range(NP)]
    for c, p in order:
        w = pre[c, p]
        w["s0"] = s[p]
        s[p] = s[p] * w["dec"] - _mm(s[p], w["pz"]) + w["q"]
    for c, p in order:
        w = pre[c, p]
        o_st = _mm(w["oa"], w["s0"], NT) + w["oc"]
        y_ref[c * C:(c + 1) * C, p * P2:(p + 1) * P2] = unstack(o_st)
    for p in range(NP):
        s_ref[p] = s[p]

    if rev:
        o = of_ref[...] + y_ref[...]
        bd = bd_ref[...]
        mu = _mm2(o, bd) * (1.0 / HD)
        oc = o - mu
        var = _mm2(oc * oc, bd) * (1.0 / HD)
        o = oc * lax.rsqrt(var + RW_GN_EPS) * lnw_ref[...] + lnb_ref[...]
        y_ref[...] = (o + bonus_ref[...]) * gs_ref[...]


def _rw_scan(rev, r, kk, v, lw, akk, kd, extra=()):
    rows = r.shape[0]
    nb = rows // MIX_ROWS
    order = _blk_order(rev, nb)
    P2 = 2 * RW_HD
    W = RW_HEADS * RW_HD
    blk = pl.BlockSpec((MIX_ROWS, W), lambda b: (order(b), 0))
    in_specs = [blk] * 6 + [pl.BlockSpec((MIX_ROWS, MIX_ROWS), lambda b: (0, 0))]
    args = [r, kk, v, lw, akk, kd, _seg_tri(MIX_ROWS, RW_CHUNK, rev)]
    if rev:
        o_fwd, bonus, gs, lnw, lnb, bd512 = extra
        vec = pl.BlockSpec((1, W), lambda b: (0, 0))
        in_specs += [blk, blk, blk, vec, vec, pl.BlockSpec((W, W), lambda b: (0, 0))]
        args += [o_fwd, bonus, gs, lnw, lnb, bd512]
    return pl.pallas_call(
        functools.partial(_rw_scan_kernel, rev),
        grid=(nb,),
        in_specs=in_specs,
        out_specs=blk,
        out_shape=jax.ShapeDtypeStruct((rows, W), F32),
        scratch_shapes=[pltpu.VMEM((RW_HEADS // 2, P2, P2), F32)],
        compiler_params=_cparams(("arbitrary",)),
        name="rw_rev" if rev else "rw_fwd",
    )(*args)


def _qk_prep_kernel(nq_ref, nk_ref, nv_ref, wq_ref, wkv_ref, nqn_ref, nkn_ref, wqn_ref, wkn_ref,
                    cos_ref, sin_ref, bd512_ref, bd128_ref,
                    naq_o, nak_o, nav_o, waq_o, wak_o, wav_o):
    bd512 = bd512_ref[...]
    bd128 = bd128_ref[...]

    def hnorm(x, g, bd, hd):
        return x * lax.rsqrt(_mm2(x * x, bd) * (1.0 / hd) + EPS) * g

    def rope(x, cos, sin):
        lane = lax.broadcasted_iota(jnp.int32, x.shape, 1)
        w = x.shape[1]
        partner = jnp.where((lane & 31) < 16, pltpu.roll(x, w - 16, axis=1), pltpu.roll(x, 16, axis=1))
        return x * cos + partner * sin

    def heads_out(o_ref, x, nh, hd):
        for h in range(nh):
            o_ref[h] = x[:, h * hd:(h + 1) * hd].astype(o_ref.dtype)

    heads_out(naq_o, hnorm(nq_ref[...], nqn_ref[...], bd512, NA_HD) * NA_HD ** -0.5, NA_HEADS, NA_HD)
    heads_out(nak_o, hnorm(nk_ref[...], nkn_ref[...], bd512, NA_HD), NA_HEADS, NA_HD)
    heads_out(nav_o, nv_ref[...], NA_HEADS, NA_HD)
    cos = cos_ref[...]
    sin = sin_ref[...]
    wq = hnorm(wq_ref[...], wqn_ref[...], bd512, WA_HD) * WA_HD ** -0.5
    wq = rope(wq, jnp.concatenate([cos] * 4, axis=1), jnp.concatenate([sin] * 4, axis=1))
    heads_out(waq_o, wq, WA_HEADS, WA_HD)
    kv = wkv_ref[...]
    wk = rope(hnorm(kv[:, :128], wkn_ref[...], bd128, WA_HD), cos, sin)
    heads_out(wak_o, wk, WA_KV_HEADS, WA_HD)
    heads_out(wav_o, kv[:, 128:], WA_KV_HEADS, WA_HD)


def _qk_prep(p, nqn, nkn, wqn, wkn, cos, sin, bd512, bd128):
    rows = p.shape[0]
    nb = rows // MIX_ROWS
    nab = OFF_NA // 512
    full = lambda shape: pl.BlockSpec(shape, lambda i: (0,) * len(shape))
    in_specs = [pl.BlockSpec((MIX_ROWS, 512), lambda i: (i, nab)),
                pl.BlockSpec((MIX_ROWS, 512), lambda i: (i, nab + 1)),
                pl.BlockSpec((MIX_ROWS, 512), lambda i: (i, nab + 2)),
                pl.BlockSpec((MIX_ROWS, 512), lambda i: (i, OFF_WA // 512)),
                pl.BlockSpec((MIX_ROWS, 256), lambda i: (i, (OFF_WA + 512) // 256)),
                full((1, 512)), full((1, 512)), full((1, 512)), full((1, 128)),
                pl.BlockSpec((MIX_ROWS, 128), lambda i: (i, 0)),
                pl.BlockSpec((MIX_ROWS, 128), lambda i: (i, 0)),
                full((512, 512)), full((128, 128))]
    hspec = lambda nh: pl.BlockSpec((nh, MIX_ROWS, 64), lambda i: (0, i, 0))
    hshape = lambda nh: jax.ShapeDtypeStruct((nh, rows, 64), BF16)
    return pl.pallas_call(
        _qk_prep_kernel,
        grid=(nb,),
        in_specs=in_specs,
        out_specs=[hspec(8), hspec(8), hspec(8), hspec(8), hspec(2), hspec(2)],
        out_shape=[hshape(8), hshape(8), hshape(8), hshape(8), hshape(2), hshape(2)],
        compiler_params=_cparams(("arbitrary",)),
        name="qk_prep",
    )(p, p, p, p, p, nqn, nkn, wqn, wkn, cos, sin, bd512, bd128)


def _na_kernel(lc, nrows, q_ref, k_ref, v_ref, bias_ref, o_ref):
    W = GRID_W
    nk = NA_WIN_R * W
    kc = k_ref[0:lc, :]
    vc = v_ref[0:lc, :]

    s = _mm(q_ref[0:lc, :], kc, NT)
    e = jnp.exp(s - jnp.max(s, axis=-1, keepdims=True))
    o_ref[0:lc, :] = _mm(e, vc) / jnp.sum(e, axis=-1, keepdims=True)

    def row(i, carry):
        rs = jnp.clip(i - NA_WIN_R // 2, 0, nrows - NA_WIN_R)
        q0 = pl.multiple_of(lc + i * W, W)
        k0 = pl.multiple_of(lc + rs * W, W)
        qi = q_ref[pl.ds(q0, W), :]
        s_loc = _mm(qi, k_ref[pl.ds(k0, nk), :], NT) + bias_ref[i - rs]
        s_ctx = _mm(qi, kc, NT)
        m = jnp.maximum(jnp.max(s_loc, axis=-1, keepdims=True), jnp.max(s_ctx, axis=-1, keepdims=True))
        p_loc = jnp.exp(s_loc - m)
        p_ctx = jnp.exp(s_ctx - m)
        den = jnp.sum(p_loc, axis=-1, keepdims=True) + jnp.sum(p_ctx, axis=-1, keepdims=True)
        o_ref[pl.ds(q0, W), :] = (_mm(p_loc, v_ref[pl.ds(k0, nk), :]) + _mm(p_ctx, vc)) / den
        return carry

    lax.fori_loop(0, nrows, row, 0, unroll=4)


def _na(q, k, v, bias, lc):
    nh, rows, hd = q.shape
    nrows = (rows - lc) // GRID_W
    hblk = pl.BlockSpec((None, rows, hd), lambda h: (h, 0, 0))
    return pl.pallas_call(
        functools.partial(_na_kernel, lc, nrows),
        grid=(nh,),
        in_specs=[hblk, hblk, hblk,
                  pl.BlockSpec((None, NA_WIN_R, GRID_W, NA_WIN_R * GRID_W), lambda h: (h, 0, 0, 0))],
        out_specs=hblk,
        out_shape=jax.ShapeDtypeStruct((nh, rows, hd), F32),
        compiler_params=_cparams(("arbitrary",)),
        name="na_attn",
    )(q, k, v, bias)


def _wa_kernel(lc, t_len, q_ref, k_ref, v_ref, sink_ref, o_ref):
    G = WA_HEADS // WA_KV_HEADS
    B = WA_WINDOW
    nband = 3 * B
    b = pl.program_id(1)
    kc = k_ref[0:lc, :]
    vc = v_ref[0:lc, :]
    sink = sink_ref[...]

    @pl.when(b == 0)
    def _():
        for g in range(G):
            s = _mm(q_ref[g], kc, NT)
            sk = sink[g * B:g * B + 1, :]
            m = jnp.maximum(jnp.max(s, axis=-1, keepdims=True), sk)
            e = jnp.exp(s - m)
            o_ref[g] = _mm(e, vc) / (jnp.sum(e, axis=-1, keepdims=True) + jnp.exp(sk - m))

    @pl.when(b > 0)
    def _():
        qoff = lax.broadcasted_iota(jnp.int32, (G * B, nband), 0) & (B - 1)
        koff = lax.broadcasted_iota(jnp.int32, (G * B, nband), 1)
        for j in range(MIX_ROWS // B):
            n = (b - 1) * (MIX_ROWS // B) + j
            start = jnp.clip((n - 1) * B, 0, t_len - nband)
            k0 = pl.multiple_of(lc + start, B)
            qs = jnp.concatenate([q_ref[g, j * B:(j + 1) * B, :] for g in range(G)], axis=0)
            valid = jnp.abs((n * B + qoff) - (start + koff)) <= WA_WINDOW
            s_loc = jnp.where(valid, _mm(qs, k_ref[pl.ds(k0, nband), :], NT), NEG)
            s_ctx = _mm(qs, kc, NT)
            m = jnp.maximum(jnp.maximum(jnp.max(s_loc, axis=-1, keepdims=True),
                                        jnp.max(s_ctx, axis=-1, keepdims=True)), sink)
            p_loc = jnp.exp(s_loc - m)
            p_ctx = jnp.exp(s_ctx - m)
            den = (jnp.sum(p_loc, axis=-1, keepdims=True) + jnp.sum(p_ctx, axis=-1, keepdims=True)
                   + jnp.exp(sink - m))
            o = (_mm(p_loc, v_ref[pl.ds(k0, nband), :]) + _mm(p_ctx, vc)) / den
            for g in range(G):
                o_ref[g, j * B:(j + 1) * B, :] = o[g * B:(g + 1) * B]


def _wa(q, k, v, sink_col, lc):
    nh, rows, hd = q.shape
    G = WA_HEADS // WA_KV_HEADS
    kvblk = pl.BlockSpec((None, rows, hd), lambda h, b: (h, 0, 0))
    qblk = pl.BlockSpec((G, MIX_ROWS, hd), lambda h, b: (h, b, 0))
    return pl.pallas_call(
        functools.partial(_wa_kernel, lc, rows - lc),
        grid=(WA_KV_HEADS, rows // MIX_ROWS),
        in_specs=[qblk, kvblk, kvblk,
                  pl.BlockSpec((None, G * WA_WINDOW, 1), lambda h, b: (h, 0, 0))],
        out_specs=qblk,
        out_shape=jax.ShapeDtypeStruct((nh, rows, hd), F32),
        compiler_params=_cparams(("arbitrary", "arbitrary")),
        name="wa_attn",
    )(q, k, v, sink_col)


def _block_diag(n, blk):
    idx = np.arange(n) // blk
    return jnp.asarray(idx[:, None] == idx[None, :], dtype=BF16)


def _na_bias_table(rpb):
    W, wr, wc = GRID_W, NA_WIN_R, NA_WIN_C
    j = np.arange(W)
    cstart = np.clip(j - wc // 2, 0, W - wc)
    cabs = np.arange(W)
    inwin = (cabs[None, :] >= cstart[:, None]) & (cabs[None, :] < cstart[:, None] + wc)
    cb = cabs[None, :] - j[:, None] + wc - 1
    nh, nro, nco = rpb.shape
    onehot = (cb[None] == np.arange(nco)[:, None, None]) & inwin[None]
    t = jnp.dot(rpb.reshape(nh * nro, nco), jnp.asarray(onehot.reshape(nco, W * W), F32),
                precision=lax.Precision.HIGHEST).reshape(nh, nro, W, W)
    t = t + jnp.asarray(np.where(inwin, 0.0, NEG), F32)
    tab = jnp.stack([t[:, wr - 1 - cfg:2 * wr - 1 - cfg] for cfg in range(wr)], axis=1)
    return tab.transpose(0, 1, 3, 2, 4).reshape(nh, wr, W, wr * W)


def _rope_tables(lc, t_len):
    pos = np.arange(t_len)
    n = WA_HD // 4
    inv = ROPE_BASE ** (-np.arange(n, dtype=np.float64) / n)
    ang_r = (pos // GRID_W)[:, None] * inv[None, :]
    ang_c = (pos % GRID_W)[:, None] * inv[None, :]
    cos = np.concatenate([np.cos(ang_r)] * 2 + [np.cos(ang_c)] * 2, axis=1)
    sin = np.concatenate([-np.sin(ang_r), np.sin(ang_r), -np.sin(ang_c), np.sin(ang_c)], axis=1)
    cos = np.concatenate([np.ones((lc, WA_HD)), cos], axis=0)
    sin = np.concatenate([np.zeros((lc, WA_HD)), sin], axis=0)
    return (jnp.asarray(np.tile(cos, (1, 2)), dtype=F32), jnp.asarray(np.tile(sin, (1, 2)), dtype=F32))


def _permute_w_in(w):
    hg, rw, na, wa = 2560, 2304, 1536, 768
    s_rw, s_na, s_wa, s_gl = hg, hg + rw, hg + rw + na, hg + rw + na + wa
    return jnp.concatenate([w[:, s_rw:s_rw + 2048], w[:, :hg], w[:, s_na:s_wa], w[:, s_gl:],
                            w[:, s_wa:s_gl], w[:, s_rw + 2048:s_na]], axis=1)


def _lora_pad(w, row0):
    out = jnp.zeros((2, 256, w.shape[-1]), F32)
    for d in range(2):
        out = out.at[d, row0 + 64 * d:row0 + 64 * (d + 1)].set(w[d])
    return out


def _token_mixing(p, lc, layer, hg_lb, hg_norm, rw_shift, rw_w0, rw_w2, rw_a0, rw_a2, rw_kk, rw_ka, rw_rk,
                  rw_ln_w, rw_ln_b, na_qn, na_kn, na_rpb, wa_qn, wa_kn, wa_sink, tables):
    rows = p.shape[0]
    bd512, bd128, cos, sin = tables
    cum = jnp.cumsum(jax.nn.softmax(hg_lb.astype(F32), axis=1), axis=1)
    lbs = cum[:, layer] - cum[:, 0]
    o_f = _gla_dir(p, lbs[0:1], False)
    y_a = _gla_dir(p, lbs[1:2], True, o_f, hg_norm[None])
    prep = _rw_prep(p, rw_shift[:, :2048], rw_shift[:, 2048:], rw_w0, _lora_pad(rw_w2, 0), rw_a0,
                    _lora_pad(rw_a2, 128), rw_kk[None], rw_ka[None], rw_rk.reshape(1, -1), bd512)
    r, kk, v, gs, bonus, lw0, lw1, akk0, akk1, kd0, kd1 = prep
    o_f = _rw_scan(False, r, kk, v, lw0, akk0, kd0)
    y_b = _rw_scan(True, r, kk, v, lw1, akk1, kd1,
                   (o_f, bonus, gs, rw_ln_w[None], rw_ln_b[None], bd512))
    tile = lambda g, n: jnp.tile(g, n)[None]
    naq, nak, nav, waq, wak, wav = _qk_prep(p, tile(na_qn, 8), tile(na_kn, 8), tile(wa_qn, 8),
                                            tile(wa_kn, 2), cos, sin, bd512, bd128)
    y_c = _na(naq, nak, nav, _na_bias_table(na_rpb), lc)
    G = WA_HEADS // WA_KV_HEADS
    sink_col = jnp.repeat(wa_sink.reshape(WA_KV_HEADS, G), WA_WINDOW, axis=1)[..., None]
    y_d = _wa(waq, wak, wav, sink_col, lc)
    flat = lambda y: y.transpose(1, 0, 2).reshape(rows, -1)
    return y_a, y_b, flat(y_c), flat(y_d)


def kernel(x, c, ctx, c_ctx, ada_w, ada_b, norm_ffn1, norm_mix, norm_ffn2, ffn1_wi, ffn1_wo, ffn2_wi, ffn2_wo, w_in, hg_lb, hg_norm, rw_shift, rw_w0, rw_w2, rw_a0, rw_a2, rw_kk, rw_ka, rw_rk, rw_ln_w, rw_ln_b, na_qn, na_kn, na_rpb, wa_qn, wa_kn, wa_sink, w_branch, w_out):
    assert x.shape[0] == 1 and ctx.shape[1] == MIX_ROWS
    lc = ctx.shape[1]
    t_len = x.shape[1]
    xa = jnp.concatenate([ctx[0], x[0]], axis=0)
    cc_t = jnp.stack([c[0], c_ctx], axis=1)
    mods = _ada_mods(cc_t, ada_w, ada_b).reshape(DEPTH, 2, N_MOD, D)
    tables = (_block_diag(512, 64), _block_diag(128, 64)) + _rope_tables(lc, t_len)
    for l in range(DEPTH):
        m = mods[l]
        xa = _ffn(xa, m[:, 0:3], norm_ffn1[l][None], ffn1_wi[l].astype(BF16), ffn1_wo[l].astype(BF16), lc)
        p = _win(xa, m[:, 3:6], norm_mix[l][None], _permute_w_in(w_in[l]).astype(BF16), lc)
        yb = _token_mixing(p, lc, l, hg_lb, hg_norm[l], rw_shift[l], rw_w0[l], rw_w2[l], rw_a0[l], rw_a2[l],
                           rw_kk[l], rw_ka[l], rw_rk[l], rw_ln_w[l], rw_ln_b[l], na_qn[l], na_kn[l],
                           na_rpb[l], wa_qn[l], wa_kn[l], wa_sink[l], tables)
        xa = _merge(xa, m[:, 3:6], yb, p, w_branch[l].astype(BF16), w_out[l].astype(BF16), lc)
        xa = _ffn(xa, m[:, 6:9], norm_ffn2[l][None], ffn2_wi[l].astype(BF16), ffn2_wo[l].astype(BF16), lc)
    return xa[lc:][None]
```

```python
import functools

import numpy as np
import jax
import jax.numpy as jnp
from jax import lax
from jax.experimental import pallas as pl
from jax.experimental.pallas import tpu as pltpu

F32 = jnp.float32
BF16 = jnp.bfloat16

D = 2048
DEPTH = 2
GRID_W = 64
EPS = 1e-6
D_FF = 5632
N_MOD = 9
N_BRANCH = 4
BR_W = 512
HG_HEADS, HG_DK = 4, 128
RW_HEADS, RW_HD = 8, 64
RW_LORA = 64
RW_GN_EPS = 64e-5
NA_HEADS, NA_HD = 8, 64
NA_WIN_R, NA_WIN_C = 8, 16
NA_GROUP = 4
NA_KEY_ROWS = NA_WIN_R + NA_GROUP - 1
WA_HEADS, WA_KV_HEADS, WA_HD = 8, 2, 64
WA_WINDOW = 128
ROPE_BASE = 10000.0

OFF_RW, OFF_HG, OFF_NA, OFF_GL, OFF_WA, OFF_LORA = 0, 2048, 4608, 6144, 14336, 15104
P_TOTAL = 15360

MIX_ROWS = 256
GLA_CHUNK = 16
RW_CHUNK = 64
NEG = -1e30
VMEM_LIMIT = 56 * 1024 * 1024

NT = (((1,), (1,)), ((), ()))
TN = (((0,), (0,)), ((), ()))


def _cparams(sem):
    return pltpu.CompilerParams(dimension_semantics=sem, vmem_limit_bytes=VMEM_LIMIT)


def _mm(a, b, dims=None):
    a = a.astype(BF16)
    b = b.astype(BF16)
    if dims is None:
        return jnp.dot(a, b, preferred_element_type=F32)
    return lax.dot_general(a, b, dims, preferred_element_type=F32)


def _mm2(x, w_bf16):
    hi = x.astype(BF16)
    lo = (x - hi.astype(F32)).astype(BF16)
    return (jnp.dot(hi, w_bf16, preferred_element_type=F32)
            + jnp.dot(lo, w_bf16, preferred_element_type=F32))


def _mm_f32(a, b):
    return jnp.dot(a, b, preferred_element_type=F32, precision=lax.Precision.HIGHEST)


def _sigmoid(x):
    return 1.0 / (1.0 + jnp.exp(-x))


def _silu(x):
    return x * _sigmoid(x)


def _seg_cumsum(x, tri):
    w = x.shape[1]
    hi = x.astype(BF16)
    r1 = x - hi.astype(F32)
    mid = r1.astype(BF16)
    lo = (r1 - mid.astype(F32)).astype(BF16)
    y = jnp.dot(tri, jnp.concatenate([hi, mid, lo], axis=1), preferred_element_type=F32)
    return y[:, :w] + y[:, w:2 * w] + y[:, 2 * w:]


def _seg_tri(rows, seg, rev):
    r = np.arange(rows)
    same = (r[:, None] // seg) == (r[None, :] // seg)
    tri = (r[None, :] >= r[:, None]) if rev else (r[None, :] <= r[:, None])
    return jnp.asarray(same & tri, dtype=BF16)


def _ada_kernel(cc_ref, w_ref, b_ref, o_ref):
    s = _silu(cc_ref[...])
    w = w_ref[...]
    r0 = jnp.sum(w * s[:, 0:1], axis=0, keepdims=True)
    r1 = jnp.sum(w * s[:, 1:2], axis=0, keepdims=True)
    o_ref[...] = jnp.concatenate([r0, r1], axis=0) + b_ref[...]


def _ada_mods(cc_t, ada_w, ada_b):
    tn = 512
    nmod = ada_w.shape[-1]
    return pl.pallas_call(
        _ada_kernel,
        grid=(DEPTH, nmod // tn),
        in_specs=[pl.BlockSpec((D, 2), lambda l, j: (0, 0)),
                  pl.BlockSpec((None, D, tn), lambda l, j: (l, 0, j)),
                  pl.BlockSpec((None, 1, tn), lambda l, j: (l, 0, j))],
        out_specs=pl.BlockSpec((None, 2, tn), lambda l, j: (l, 0, j)),
        out_shape=jax.ShapeDtypeStruct((DEPTH, 2, nmod), F32),
        compiler_params=_cparams(("arbitrary", "arbitrary")),
        name="ada_mod",
    )(cc_t, ada_w, ada_b.reshape(DEPTH, 1, nmod))


def _row_is_ctx(i, tm, lc):
    return (i * tm + lax.broadcasted_iota(jnp.int32, (tm, 1), 0)) < lc


def _mod_pick(mod_ref, j, is_ctx):
    return jnp.where(is_ctx, mod_ref[1, j:j + 1, :], mod_ref[0, j:j + 1, :])


def _modulated(x, g, mod_ref, is_ctx):
    y = x * lax.rsqrt(jnp.mean(x * x, axis=-1, keepdims=True) + EPS) * g
    return y * (1.0 + _mod_pick(mod_ref, 1, is_ctx)) + _mod_pick(mod_ref, 0, is_ctx)


def _ffn_kernel(lc, tm, nf, x_ref, mod_ref, g_ref, wa_ref, wb_ref, wo_ref, o_ref, xn_ref):
    i = pl.program_id(0)
    j = pl.program_id(1)
    is_ctx = _row_is_ctx(i, tm, lc)

    @pl.when(j == 0)
    def _():
        xn_ref[...] = _modulated(x_ref[...], g_ref[...], mod_ref, is_ctx).astype(BF16)
        o_ref[...] = jnp.zeros_like(o_ref)

    xn = xn_ref[...]
    a = jnp.dot(xn, wa_ref[...], preferred_element_type=F32)
    b = jnp.dot(xn, wb_ref[...], preferred_element_type=F32)
    h = (_silu(a) * b).astype(BF16)
    o_ref[...] += jnp.dot(h, wo_ref[...], preferred_element_type=F32)

    @pl.when(j == nf - 1)
    def _():
        o_ref[...] = x_ref[...] + 0.5 * _mod_pick(mod_ref, 2, is_ctx) * o_ref[...]


def _dense_tm(rows):
    for tm in (768, 512, 256):
        if rows % tm == 0:
            return tm
    raise ValueError(rows)


def _ffn(x, mod3, g, wi, wo, lc):
    rows = x.shape[0]
    tm = _dense_tm(rows)
    tf = 512
    nf = D_FF // tf
    return pl.pallas_call(
        functools.partial(_ffn_kernel, lc, tm, nf),
        grid=(rows // tm, nf),
        in_specs=[pl.BlockSpec((tm, D), lambda i, j: (i, 0)),
                  pl.BlockSpec((2, 3, D), lambda i, j: (0, 0, 0)),
                  pl.BlockSpec((1, D), lambda i, j: (0, 0)),
                  pl.BlockSpec((D, tf), lambda i, j: (0, j)),
                  pl.BlockSpec((D, tf), lambda i, j: (0, j + nf)),
                  pl.BlockSpec((tf, D), lambda i, j: (j, 0))],
        out_specs=pl.BlockSpec((tm, D), lambda i, j: (i, 0)),
        out_shape=jax.ShapeDtypeStruct((rows, D), F32),
        scratch_shapes=[pltpu.VMEM((tm, D), BF16)],
        compiler_params=_cparams(("arbitrary", "arbitrary")),
        name="ffn",
    )(x, mod3, g, wi, wi, wo)


def _win_kernel(lc, tm, x_ref, mod_ref, g_ref, w_ref, o_ref, xn_ref):
    i = pl.program_id(0)

    @pl.when(pl.program_id(1) == 0)
    def _():
        is_ctx = _row_is_ctx(i, tm, lc)
        xn_ref[...] = _modulated(x_ref[...], g_ref[...], mod_ref, is_ctx).astype(BF16)

    o_ref[...] = jnp.dot(xn_ref[...], w_ref[...], preferred_element_type=F32)


def _win(x, mod3, g, w, lc):
    rows = x.shape[0]
    tm = _dense_tm(rows)
    tn = 1536
    return pl.pallas_call(
        functools.partial(_win_kernel, lc, tm),
        grid=(rows // tm, P_TOTAL // tn),
        in_specs=[pl.BlockSpec((tm, D), lambda i, j: (i, 0)),
                  pl.BlockSpec((2, 3, D), lambda i, j: (0, 0, 0)),
                  pl.BlockSpec((1, D), lambda i, j: (0, 0)),
                  pl.BlockSpec((D, tn), lambda i, j: (0, j))],
        out_specs=pl.BlockSpec((tm, tn), lambda i, j: (i, j)),
        out_shape=jax.ShapeDtypeStruct((rows, P_TOTAL), F32),
        scratch_shapes=[pltpu.VMEM((tm, D), BF16)],
        compiler_params=_cparams(("arbitrary", "arbitrary")),
        name="w_in",
    )(x, mod3, g, w)


def _merge_kernel(lc, tm, x_ref, mod_ref, ya_ref, yb_ref, yc_ref, yd_ref, gl_ref, wb_ref, wo_ref, o_ref, acc_ref):
    i = pl.program_id(0)
    n = pl.program_id(1)

    @pl.when(n == 0)
    def _():
        acc_ref[...] = jnp.zeros_like(acc_ref)

    for nn, y_ref in enumerate((ya_ref, yb_ref, yc_ref, yd_ref)):
        @pl.when(n == nn)
        def _():
            proj = jnp.dot(y_ref[...].astype(BF16), wb_ref[...], preferred_element_type=F32)
            acc_ref[...] += _sigmoid(gl_ref[...]) * proj

    @pl.when(n == N_BRANCH - 1)
    def _():
        is_ctx = _row_is_ctx(i, tm, lc)
        y = jnp.dot(acc_ref[...].astype(BF16), wo_ref[...], preferred_element_type=F32)
        o_ref[...] = x_ref[...] + _mod_pick(mod_ref, 2, is_ctx) * y


def _merge(x, mod3, ys, p, wb, wo, lc):
    rows = x.shape[0]
    tm = 384 if rows % 384 == 0 else MIX_ROWS
    glb = OFF_GL // D
    yspec = pl.BlockSpec((tm, BR_W), lambda i, n: (i, 0))
    return pl.pallas_call(
        functools.partial(_merge_kernel, lc, tm),
        grid=(rows // tm, N_BRANCH),
        in_specs=[pl.BlockSpec((tm, D), lambda i, n: (i, 0)),
                  pl.BlockSpec((2, 3, D), lambda i, n: (0, 0, 0)),
                  yspec, yspec, yspec, yspec,
                  pl.BlockSpec((tm, D), lambda i, n: (i, glb + n)),
                  pl.BlockSpec((None, BR_W, D), lambda i, n: (n, 0, 0)),
                  pl.BlockSpec((D, D), lambda i, n: (0, 0))],
        out_specs=pl.BlockSpec((tm, D), lambda i, n: (i, 0)),
        out_shape=jax.ShapeDtypeStruct((rows, D), F32),
        scratch_shapes=[pltpu.VMEM((tm, D), F32)],
        compiler_params=_cparams(("arbitrary", "arbitrary")),
        name="merge",
    )(x, mod3, *ys, p, wb, wo)


def _gla_kernel(rev, *refs):
    if rev:
        q_ref, f_ref, i_ref, lb_ref, tri_ref, of_ref, g_ref, nw_ref, y_ref, st_ref = refs
    else:
        q_ref, f_ref, i_ref, lb_ref, tri_ref, y_ref, st_ref = refs
    C = GLA_CHUNK
    R = MIX_ROWS
    NH, DK = HG_HEADS, HG_DK

    @pl.when(pl.program_id(0) == 0)
    def _():
        st_ref[...] = jnp.zeros_like(st_ref)

    lb = lb_ref[...]
    f_all = lb + (1.0 - lb) * _sigmoid(f_ref[...])
    b_all = _seg_cumsum(jnp.log(f_all), tri_ref[...])
    pos = lax.broadcasted_iota(jnp.int32, (R, 1), 0) & (C - 1)
    chunks = list(range(R // C - 1, -1, -1) if rev else range(R // C))

    heads = []
    for h in range(NH):
        hs = slice(h * DK, (h + 1) * DK)
        b = b_all[:, hs]
        k = 1.0 - f_all[:, hs]
        q = q_ref[:, hs]
        v = i_ref[:, hs]
        o = jnp.sum(q * k, axis=-1, keepdims=True) * v
        for r in range(8):
            sh = ((R - r) if rev else r) % R
            br, kr, vr = (pltpu.roll(t, sh, axis=0) if r else t for t in (b, k, v))
            for d in (r, r + 8):
                if d == 0:
                    continue
                if d >= 8:
                    sh8 = (R - 8) if rev else 8
                    bd_, kd_, vd_ = (pltpu.roll(t, sh8, axis=0) for t in (br, kr, vr))
                else:
                    bd_, kd_, vd_ = br, kr, vr
                valid = (pos < C - d) if rev else (pos >= d)
                e = jnp.exp(jnp.where(valid, b - bd_, NEG))
                att = jnp.sum(q * kd_ * e, axis=-1, keepdims=True)
                o = o + att * vd_
        upd, dec = {}, {}
        for c in chunks:
            sl = slice(c * C, (c + 1) * C)
            b_end = b[sl][0:1] if rev else b[sl][C - 1:C]
            dec[c] = jnp.exp(b_end)
            upd[c] = _mm(v[sl], k[sl] * jnp.exp(b_end - b[sl]), TN)
        heads.append(dict(o=o, qb=q * jnp.exp(b), upd=upd, dec=dec, st=st_ref[h]))

    for c in chunks:
        for w in heads:
            w[c] = w["st"]
            w["st"] = w["st"] * w["dec"][c] + w["upd"][c]
    outs = []
    for h, w in enumerate(heads):
        st_ref[h] = w["st"]
        inter = {c: _mm(w["qb"][c * C:(c + 1) * C], w[c], NT) for c in chunks}
        outs.append(w["o"] + jnp.concatenate([inter[c] for c in range(R // C)], axis=0))

    for h, o in enumerate(outs):
        hs = slice(h * DK, (h + 1) * DK)
        if rev:
            o = of_ref[:, hs] + o
            o = o * lax.rsqrt(jnp.mean(o * o, axis=-1, keepdims=True) + EPS) * nw_ref[:, hs]
            o = o * _silu(g_ref[:, hs])
        y_ref[:, hs] = o


def _blk_order(rev, nb):
    if rev:
        return lambda b: jnp.where(b == 0, 0, nb - b)
    return lambda b: b


def _gla_dir(p, lb, rev, o_fwd=None, norm_w=None):
    rows = p.shape[0]
    nb = rows // MIX_ROWS
    order = _blk_order(rev, nb)
    W = HG_HEADS * HG_DK
    cb = OFF_HG // W

    def col(n):
        return pl.BlockSpec((MIX_ROWS, W), lambda b: (order(b), cb + n))

    vec = pl.BlockSpec((1, W), lambda b: (0, 0))
    in_specs = [col(0), col(2 if rev else 1), col(3), vec,
                pl.BlockSpec((MIX_ROWS, MIX_ROWS), lambda b: (0, 0))]
    args = [p, p, p, lb, _seg_tri(MIX_ROWS, GLA_CHUNK, rev)]
    if rev:
        in_specs += [pl.BlockSpec((MIX_ROWS, W), lambda b: (order(b), 0)), col(4), vec]
        args += [o_fwd, p, norm_w]
    return pl.pallas_call(
        functools.partial(_gla_kernel, rev),
        grid=(nb,),
        in_specs=in_specs,
        out_specs=pl.BlockSpec((MIX_ROWS, W), lambda b: (order(b), 0)),
        out_shape=jax.ShapeDtypeStruct((rows, W), F32),
        scratch_shapes=[pltpu.VMEM((HG_HEADS, HG_DK, HG_DK), F32)],
        compiler_params=_cparams(("arbitrary",)),
        name="gla_rev" if rev else "gla_fwd",
    )(*args)


def _rw_prep_kernel(nb, main_ref, lora_ref, pm_ref, nm_ref, pl_ref, nl_ref, tm_ref, tl_ref,
                    w0_ref, w2_ref, a0_ref, a2_ref, kkw_ref, ka_ref, rk_ref, bd_ref,
                    r_o, kk_o, v_o, gs_o, bonus_o, lw0_o, lw1_o, akk0_o, akk1_o, kd0_o, kd1_o):
    i = pl.program_id(0)
    has_prev = (i >= 2).astype(F32)
    has_next = jnp.logical_and(i != 0, i != nb - 1).astype(F32)
    rowi = lax.broadcasted_iota(jnp.int32, (MIX_ROWS, 1), 0)

    def shift(x, prev_blk, next_blk, taps):
        up = jnp.where(rowi == 0, prev_blk[7:8, :] * has_prev, pltpu.roll(x, 1, axis=0))
        dn = jnp.where(rowi == MIX_ROWS - 1, next_blk[0:1, :] * has_next,
                       pltpu.roll(x, MIX_ROWS - 1, axis=0))
        return taps[0:1] * up + taps[1:2] * x + taps[2:3] * dn

    main = shift(main_ref[...], pm_ref[...], nm_ref[...], tm_ref[...])
    lora = shift(lora_ref[...], pl_ref[...], nl_ref[...], tl_ref[...])
    W = RW_HEADS * RW_HD
    r, k, v, g = (main[:, n * W:(n + 1) * W] for n in range(4))
    bd = bd_ref[...]
    kk = k * kkw_ref[...]
    kk = kk * lax.rsqrt(_mm2(kk * kk, bd) + EPS)
    tl = jnp.tanh(lora)
    kds = []
    for d, (lw_o, akk_o, kd_o) in enumerate(((lw0_o, akk0_o, kd0_o), (lw1_o, akk1_o, kd1_o))):
        z = -(w0_ref[d:d + 1, :] + _mm_f32(tl, w2_ref[d]))
        softplus = jnp.maximum(z, 0.0) + jnp.log(1.0 + jnp.exp(-jnp.abs(z)))
        lw_o[...] = -jnp.exp(-softplus - 0.5)
        a = _sigmoid(a0_ref[d:d + 1, :] + _mm_f32(lora, a2_ref[d]))
        kd = k * (1.0 + (a - 1.0) * ka_ref[...])
        kds.append(kd)
        kd_o[...] = kd
        akk_o[...] = a * kk
    r_o[...] = r
    kk_o[...] = kk
    v_o[...] = v
    gs_o[...] = _sigmoid(g)
    bonus_o[...] = _mm2(r * (kds[0] + kds[1]) * rk_ref[...], bd) * v


def _rw_prep(p, taps_m, taps_l, w0, w2p, a0, a2p, kkw, ka, rk, bd512):
    rows = p.shape[0]
    nb = rows // MIX_ROWS
    W = RW_HEADS * RW_HD
    n8 = rows // 8
    lb = OFF_LORA // 256
    per = MIX_ROWS // 8
    full = lambda shape: pl.BlockSpec(shape, lambda i: (0,) * len(shape))
    in_specs = [pl.BlockSpec((MIX_ROWS, 4 * W), lambda i: (i, 0)),
                pl.BlockSpec((MIX_ROWS, 256), lambda i: (i, lb)),
                pl.BlockSpec((8, 4 * W), lambda i: (jnp.maximum(i * per - 1, 0), 0)),
                pl.BlockSpec((8, 4 * W), lambda i: (jnp.minimum((i + 1) * per, n8 - 1), 0)),
                pl.BlockSpec((8, 256), lambda i: (jnp.maximum(i * per - 1, 0), lb)),
                pl.BlockSpec((8, 256), lambda i: (jnp.minimum((i + 1) * per, n8 - 1), lb)),
                full((3, 4 * W)), full((3, 256)),
                full((2, W)), full((2, 256, W)), full((2, W)), full((2, 256, W)),
                full((1, W)), full((1, W)), full((1, W)), full((W, W))]
    out = jax.ShapeDtypeStruct((rows, W), F32)
    return pl.pallas_call(
        functools.partial(_rw_prep_kernel, nb),
        grid=(nb,),
        in_specs=in_specs,
        out_specs=[pl.BlockSpec((MIX_ROWS, W), lambda i: (i, 0))] * 11,
        out_shape=[out] * 11,
        compiler_params=_cparams(("arbitrary",)),
        name="rw_prep",
    )(p, p, p, p, p, p, taps_m, taps_l, w0, w2p, a0, a2p, kkw, ka, rk, bd512)


def _rw_scan_kernel(rev, *refs):
    if rev:
        (r_ref, kk_ref, v_ref, lw_ref, akk_ref, kd_ref, tri_ref, of_ref, bonus_ref, gs_ref,
         lnw_ref, lnb_ref, bd_ref, y_ref, s_ref) = refs
    else:
        r_ref, kk_ref, v_ref, lw_ref, akk_ref, kd_ref, tri_ref, y_ref, s_ref = refs
    C = RW_CHUNK
    HD = RW_HD
    P2 = 2 * HD

    NP = RW_HEADS // 2

    @pl.when(pl.program_id(0) == 0)
    def _():
        s_ref[...] = jnp.zeros_like(s_ref)

    lw = lw_ref[...]
    cum = _seg_cumsum(lw, tri_ref[...])
    cum_prev = cum - lw

    lane = lax.broadcasted_iota(jnp.int32, (1, P2), 1)
    h0 = lane < HD
    ri = lax.broadcasted_iota(jnp.int32, (P2, P2), 0)
    ci = lax.broadcasted_iota(jnp.int32, (P2, P2), 1)
    same = (ri // C) == (ci // C)
    rt, cs = ri % C, ci % C
    strict = jnp.logical_and(same, (rt < cs) if rev else (rt > cs))
    incl = jnp.logical_and(same, (rt <= cs) if rev else (rt >= cs))
    eye = (ri == ci).astype(F32)
    bdiag = ((ri // HD) == (ci // HD)).astype(F32)

    def stack(x):
        return jnp.concatenate([x, x], axis=0)

    def split_heads(x):
        return jnp.concatenate([jnp.where(h0, x, 0.0), jnp.where(h0, 0.0, x)], axis=0)

    def unstack(x):
        return jnp.where(h0, x[:C], x[C:])

    chunks = list(range(MIX_ROWS // C - 1, -1, -1) if rev else range(MIX_ROWS // C))
    order = [(c, p) for c in chunks for p in range(NP)]

    pre = {}
    for c, p in order:
        sl = slice(c * C, (c + 1) * C)
        pp = slice(p * P2, (p + 1) * P2)
        cum_c = cum[sl, pp]
        cend = cum_c[0:1] if rev else cum_c[C - 1:C]
        e_neg = jnp.exp(-cum_c)
        e_end = jnp.exp(cend - cum_c)
        a_t = kk_ref[sl, pp] * jnp.exp(cum_prev[sl, pp])
        r_t = r_ref[sl, pp] * jnp.exp(cum_c)
        akk = akk_ref[sl, pp]
        kd = kd_ref[sl, pp]
        v = v_ref[sl, pp]
        pre[c, p] = dict(a_t=a_t, r_t=r_t, v=v, kh=kd * e_end, bh=akk * e_end, dec=jnp.exp(cend),
                         lhs=jnp.concatenate([split_heads(a_t), split_heads(r_t)], axis=0),
                         rhs=jnp.concatenate([stack(akk * e_neg), stack(kd * e_neg)], axis=0))
    for w in pre.values():
        g = _mm(w["lhs"], w["rhs"], NT)
        w["m"] = jnp.where(strict, g[:P2, :P2], 0.0)
        w["a_ak"] = jnp.where(strict, g[:P2, P2:], 0.0)
        w["a_rb"] = jnp.where(incl, g[P2:, :P2], 0.0)
        w["a_rk"] = jnp.where(incl, g[P2:, P2:], 0.0)
        w["p"] = eye - w["m"]
    for w in pre.values():
        av = _mm(jnp.concatenate([w["a_ak"], w["a_rk"]], axis=0), stack(w["v"]))
        w["av"], w["rkv"] = av[:P2], av[P2:]
    for w in pre.values():
        w["m"] = _mm(w["m"], w["m"])
    for _ in range(4):
        for w in pre.values():
            pm = _mm(jnp.concatenate([w["p"], w["m"]], axis=0), w["m"])
            w["p"] = w["p"] + pm[:P2]
            w["m"] = pm[P2:]
    for w in pre.values():
        w["p"] = w["p"] + _mm(w["p"], w["m"])
    for w in pre.values():
        w["tt"] = _mm(w["p"], jnp.concatenate([stack(w["a_t"]), w["av"]], axis=1))
        w["z"] = _mm(w["p"], stack(w["bh"]), TN)
    for w in pre.values():
        ar = _mm(w["a_rb"], w["tt"])
        w["oa"] = stack(w["r_t"]) - ar[:, :P2]
        w["oc"] = w["rkv"] - ar[:, P2:]
        w["pz"] = bdiag * _mm(split_heads(w["a_t"]), w["z"], TN)
        w["kz"] = stack(w["kh"]) - _mm(w["a_ak"], w["z"], TN)
    for w in pre.values():
        w["q"] = bdiag * _mm(split_heads(w["v"]), w["kz"], TN)

    s = [s_ref[p] for p in range(NP)]
    for c, p in order:
        w = pre[c, p]
        w["s0"] = s[p]
        s[p] = s[p] * w["dec"] - _mm(s[p], w["pz"]) + w["q"]
    for c, p in order:
        w = pre[c, p]
        o_st = _mm(w["oa"], w["s0"], NT) + w["oc"]
        y_ref[c * C:(c + 1) * C, p * P2:(p + 1) * P2] = unstack(o_st)
    for p in range(NP):
        s_ref[p] = s[p]

    if rev:
        o = of_ref[...] + y_ref[...]
        bd = bd_ref[...]
        mu = _mm2(o, bd) * (1.0 / HD)
        oc = o - mu
        var = _mm2(oc * oc, bd) * (1.0 / HD)
        o = oc * lax.rsqrt(var + RW_GN_EPS) * lnw_ref[...] + lnb_ref[...]
        y_ref[...] = (o + bonus_ref[...]) * gs_ref[...]


def _rw_scan(rev, r, kk, v, lw, akk, kd, extra=()):
    rows = r.shape[0]
    nb = rows // MIX_ROWS
    order = _blk_order(rev, nb)
    P2 = 2 * RW_HD
    W = RW_HEADS * RW_HD
    blk = pl.BlockSpec((MIX_ROWS, W), lambda b: (order(b), 0))
    in_specs = [blk] * 6 + [pl.BlockSpec((MIX_ROWS, MIX_ROWS), lambda b: (0, 0))]
    args = [r, kk, v, lw, akk, kd, _seg_tri(MIX_ROWS, RW_CHUNK, rev)]
    if rev:
        o_fwd, bonus, gs, lnw, lnb, bd512 = extra
        vec = pl.BlockSpec((1, W), lambda b: (0, 0))
        in_specs += [blk, blk, blk, vec, vec, pl.BlockSpec((W, W), lambda b: (0, 0))]
        args += [o_fwd, bonus, gs, lnw, lnb, bd512]
    return pl.pallas_call(
        functools.partial(_rw_scan_kernel, rev),
        grid=(nb,),
        in_specs=in_specs,
        out_specs=blk,
        out_shape=jax.ShapeDtypeStruct((rows, W), F32),
        scratch_shapes=[pltpu.VMEM((RW_HEADS // 2, P2, P2), F32)],
        compiler_params=_cparams(("arbitrary",)),
        name="rw_rev" if rev else "rw_fwd",
    )(*args)


def _qk_prep_kernel(nq_ref, nk_ref, nv_ref, wq_ref, wkv_ref, nqn_ref, nkn_ref, wqn_ref, wkn_ref,
                    cos_ref, sin_ref, bd512_ref, bd128_ref,
                    naq_o, nak_o, nav_o, waq_o, wak_o, wav_o):
    bd512 = bd512_ref[...]
    bd128 = bd128_ref[...]

    def hnorm(x, g, bd, hd):
        return x * lax.rsqrt(_mm2(x * x, bd) * (1.0 / hd) + EPS) * g

    def rope(x, cos, sin):
        lane = lax.broadcasted_iota(jnp.int32, x.shape, 1)
        w = x.shape[1]
        partner = jnp.where((lane & 31) < 16, pltpu.roll(x, w - 16, axis=1), pltpu.roll(x, 16, axis=1))
        return x * cos + partner * sin

    def heads_out(o_ref, x, nh, hd):
        for h in range(nh):
            o_ref[h] = x[:, h * hd:(h + 1) * hd].astype(o_ref.dtype)

    heads_out(naq_o, hnorm(nq_ref[...], nqn_ref[...], bd512, NA_HD) * NA_HD ** -0.5, NA_HEADS, NA_HD)
    heads_out(nak_o, hnorm(nk_ref[...], nkn_ref[...], bd512, NA_HD), NA_HEADS, NA_HD)
    heads_out(nav_o, nv_ref[...], NA_HEADS, NA_HD)
    cos = cos_ref[...]
    sin = sin_ref[...]
    wq = hnorm(wq_ref[...], wqn_ref[...], bd512, WA_HD) * WA_HD ** -0.5
    wq = rope(wq, jnp.concatenate([cos] * 4, axis=1), jnp.concatenate([sin] * 4, axis=1))
    heads_out(waq_o, wq, WA_HEADS, WA_HD)
    kv = wkv_ref[...]
    wk = rope(hnorm(kv[:, :128], wkn_ref[...], bd128, WA_HD), cos, sin)
    heads_out(wak_o, wk, WA_KV_HEADS, WA_HD)
    heads_out(wav_o, kv[:, 128:], WA_KV_HEADS, WA_HD)


def _qk_prep(p, nqn, nkn, wqn, wkn, cos, sin, bd512, bd128):
    rows = p.shape[0]
    nb = rows // MIX_ROWS
    nab = OFF_NA // 512
    full = lambda shape: pl.BlockSpec(shape, lambda i: (0,) * len(shape))
    in_specs = [pl.BlockSpec((MIX_ROWS, 512), lambda i: (i, nab)),
                pl.BlockSpec((MIX_ROWS, 512), lambda i: (i, nab + 1)),
                pl.BlockSpec((MIX_ROWS, 512), lambda i: (i, nab + 2)),
                pl.BlockSpec((MIX_ROWS, 512), lambda i: (i, OFF_WA // 512)),
                pl.BlockSpec((MIX_ROWS, 256), lambda i: (i, (OFF_WA + 512) // 256)),
                full((1, 512)), full((1, 512)), full((1, 512)), full((1, 128)),
                pl.BlockSpec((MIX_ROWS, 128), lambda i: (i, 0)),
                pl.BlockSpec((MIX_ROWS, 128), lambda i: (i, 0)),
                full((512, 512)), full((128, 128))]
    hspec = lambda nh: pl.BlockSpec((nh, MIX_ROWS, 64), lambda i: (0, i, 0))
    hshape = lambda nh: jax.ShapeDtypeStruct((nh, rows, 64), BF16)
    return pl.pallas_call(
        _qk_prep_kernel,
        grid=(nb,),
        in_specs=in_specs,
        out_specs=[hspec(8), hspec(8), hspec(8), hspec(8), hspec(2), hspec(2)],
        out_shape=[hshape(8), hshape(8), hshape(8), hshape(8), hshape(2), hshape(2)],
        compiler_params=_cparams(("arbitrary",)),
        name="qk_prep",
    )(p, p, p, p, p, nqn, nkn, wqn, wkn, cos, sin, bd512, bd128)


def _na_kernel(lc, nrows, q_ref, k_ref, v_ref, bias_ref, o_ref):
    W = GRID_W
    nq = NA_GROUP * W
    nk = NA_KEY_ROWS * W
    ngroups = nrows // NA_GROUP
    kc = k_ref[0:lc, :]
    vc = v_ref[0:lc, :]

    s = _mm(q_ref[0:lc, :], kc, NT)
    e = jnp.exp(s - jnp.max(s, axis=-1, keepdims=True))
    o_ref[0:lc, :] = _mm(e, vc) / jnp.sum(e, axis=-1, keepdims=True)

    def group(gi, carry):
        base = jnp.clip(gi * NA_GROUP - NA_WIN_R // 2, 0, nrows - NA_KEY_ROWS)
        kind = jnp.where(gi == 0, 0, jnp.where(gi == ngroups - 1, 2, 1))
        q0 = pl.multiple_of(lc + gi * nq, nq)
        k0 = pl.multiple_of(lc + base * W, W)
        keys = jnp.concatenate([kc, k_ref[pl.ds(k0, nk), :]], axis=0)
        vals = jnp.concatenate([vc, v_ref[pl.ds(k0, nk), :]], axis=0)
        s = _mm(q_ref[pl.ds(q0, nq), :], keys, NT) + bias_ref[kind]
        p = jnp.exp(s - jnp.max(s, axis=-1, keepdims=True))
        o_ref[pl.ds(q0, nq), :] = _mm(p, vals) / jnp.sum(p, axis=-1, keepdims=True)
        return carry

    lax.fori_loop(0, ngroups, group, 0, unroll=2)


def _na(q, k, v, bias, lc):
    nh, rows, hd = q.shape
    nrows = (rows - lc) // GRID_W
    hblk = pl.BlockSpec((None, rows, hd), lambda h: (h, 0, 0))
    return pl.pallas_call(
        functools.partial(_na_kernel, lc, nrows),
        grid=(nh,),
        in_specs=[hblk, hblk, hblk,
                  pl.BlockSpec((None,) + bias.shape[1:], lambda h: (h, 0, 0, 0))],
        out_specs=hblk,
        out_shape=jax.ShapeDtypeStruct((nh, rows, hd), F32),
        compiler_params=_cparams(("arbitrary",)),
        name="na_attn",
    )(q, k, v, bias)


def _wa_kernel(lc, t_len, q_ref, k_ref, v_ref, sink_ref, o_ref):
    G = WA_HEADS // WA_KV_HEADS
    B = WA_WINDOW
    nband = 3 * B
    b = pl.program_id(1)
    kc = k_ref[0:lc, :]
    vc = v_ref[0:lc, :]
    sink = sink_ref[...]

    @pl.when(b == 0)
    def _():
        for g in range(G):
            s = _mm(q_ref[g], kc, NT)
            sk = sink[g * B:g * B + 1, :]
            m = jnp.maximum(jnp.max(s, axis=-1, keepdims=True), sk)
            e = jnp.exp(s - m)
            o_ref[g] = _mm(e, vc) / (jnp.sum(e, axis=-1, keepdims=True) + jnp.exp(sk - m))

    @pl.when(b > 0)
    def _():
        qoff = lax.broadcasted_iota(jnp.int32, (G * B, lc + nband), 0) & (B - 1)
        koff = lax.broadcasted_iota(jnp.int32, (G * B, lc + nband), 1) - lc
        for j in range(MIX_ROWS // B):
            n = (b - 1) * (MIX_ROWS // B) + j
            start = jnp.clip((n - 1) * B, 0, t_len - nband)
            k0 = pl.multiple_of(lc + start, B)
            qs = jnp.concatenate([q_ref[g, j * B:(j + 1) * B, :] for g in range(G)], axis=0)
            keys = jnp.concatenate([kc, k_ref[pl.ds(k0, nband), :]], axis=0)
            vals = jnp.concatenate([vc, v_ref[pl.ds(k0, nband), :]], axis=0)
            valid = jnp.logical_or(koff < 0, jnp.abs((n * B + qoff) - (start + koff)) <= WA_WINDOW)
            s = jnp.where(valid, _mm(qs, keys, NT), NEG)
            m = jnp.maximum(jnp.max(s, axis=-1, keepdims=True), sink)
            p = jnp.exp(s - m)
            o = _mm(p, vals) / (jnp.sum(p, axis=-1, keepdims=True) + jnp.exp(sink - m))
            for g in range(G):
                o_ref[g, j * B:(j + 1) * B, :] = o[g * B:(g + 1) * B]


def _wa(q, k, v, sink_col, lc):
    nh, rows, hd = q.shape
    G = WA_HEADS // WA_KV_HEADS
    kvblk = pl.BlockSpec((None, rows, hd), lambda h, b: (h, 0, 0))
    qblk = pl.BlockSpec((G, MIX_ROWS, hd), lambda h, b: (h, b, 0))
    return pl.pallas_call(
        functools.partial(_wa_kernel, lc, rows - lc),
        grid=(WA_KV_HEADS, rows // MIX_ROWS),
        in_specs=[qblk, kvblk, kvblk,
                  pl.BlockSpec((None, G * WA_WINDOW, 1), lambda h, b: (h, 0, 0))],
        out_specs=qblk,
        out_shape=jax.ShapeDtypeStruct((nh, rows, hd), F32),
        compiler_params=_cparams(("arbitrary", "arbitrary")),
        name="wa_attn",
    )(q, k, v, sink_col)


def _block_diag(n, blk):
    idx = np.arange(n) // blk
    return jnp.asarray(idx[:, None] == idx[None, :], dtype=BF16)


def _na_bias_table(rpb, lc):
    W, wr, wc = GRID_W, NA_WIN_R, NA_WIN_C
    G, KR = NA_GROUP, NA_KEY_ROWS
    j = np.arange(W)
    cstart = np.clip(j - wc // 2, 0, W - wc)
    cabs = np.arange(W)
    inwin = (cabs[None, :] >= cstart[:, None]) & (cabs[None, :] < cstart[:, None] + wc)
    cb = cabs[None, :] - j[:, None] + wc - 1
    nh, nro, nco = rpb.shape
    onehot = (cb[None] == np.arange(nco)[:, None, None]) & inwin[None]
    t = jnp.dot(rpb.reshape(nh * nro, nco), jnp.asarray(onehot.reshape(nco, W * W), F32),
                precision=lax.Precision.HIGHEST).reshape(nh, nro, W, W)
    t = t + jnp.asarray(np.where(inwin, 0.0, NEG), F32)
    off = np.array([0, -(wr // 2), -(KR - G)])[:, None, None]
    wstart = np.stack([np.zeros(G, int), np.arange(G), np.full(G, KR - wr)])[:, :, None]
    a = np.arange(KR)[None, None, :]
    r = np.arange(G)[None, :, None]
    valid = (a >= wstart) & (a < wstart + wr)
    ro = np.clip(off + a - r + wr - 1, 0, nro - 1)
    tab = jnp.where(jnp.asarray(valid)[None, :, :, :, None, None], t[:, ro], NEG)
    tab = tab.transpose(0, 1, 2, 4, 3, 5).reshape(nh, 3, G * W, KR * W)
    return jnp.concatenate([jnp.zeros((nh, 3, G * W, lc), F32), tab], axis=-1)


def _rope_tables(lc, t_len):
    pos = np.arange(t_len)
    n = WA_HD // 4
    inv = ROPE_BASE ** (-np.arange(n, dtype=np.float64) / n)
    ang_r = (pos // GRID_W)[:, None] * inv[None, :]
    ang_c = (pos % GRID_W)[:, None] * inv[None, :]
    cos = np.concatenate([np.cos(ang_r)] * 2 + [np.cos(ang_c)] * 2, axis=1)
    sin = np.concatenate([-np.sin(ang_r), np.sin(ang_r), -np.sin(ang_c), np.sin(ang_c)], axis=1)
    cos = np.concatenate([np.ones((lc, WA_HD)), cos], axis=0)
    sin = np.concatenate([np.zeros((lc, WA_HD)), sin], axis=0)
    return (jnp.asarray(np.tile(cos, (1, 2)), dtype=F32), jnp.asarray(np.tile(sin, (1, 2)), dtype=F32))


def _permute_w_in(w):
    hg, rw, na, wa = 2560, 2304, 1536, 768
    s_rw, s_na, s_wa, s_gl = hg, hg + rw, hg + rw + na, hg + rw + na + wa
    return jnp.concatenate([w[:, s_rw:s_rw + 2048], w[:, :hg], w[:, s_na:s_wa], w[:, s_gl:],
                            w[:, s_wa:s_gl], w[:, s_rw + 2048:s_na]], axis=1)


def _lora_pad(w, row0):
    out = jnp.zeros((2, 256, w.shape[-1]), F32)
    for d in range(2):
        out = out.at[d, row0 + 64 * d:row0 + 64 * (d + 1)].set(w[d])
    return out


def _token_mixing(p, lc, layer, hg_lb, hg_norm, rw_shift, rw_w0, rw_w2, rw_a0, rw_a2, rw_kk, rw_ka, rw_rk,
                  rw_ln_w, rw_ln_b, na_qn, na_kn, na_rpb, wa_qn, wa_kn, wa_sink, tables):
    rows = p.shape[0]
    bd512, bd128, cos, sin = tables
    cum = jnp.cumsum(jax.nn.softmax(hg_lb.astype(F32), axis=1), axis=1)
    lbs = cum[:, layer] - cum[:, 0]
    o_f = _gla_dir(p, lbs[0:1], False)
    y_a = _gla_dir(p, lbs[1:2], True, o_f, hg_norm[None])
    prep = _rw_prep(p, rw_shift[:, :2048], rw_shift[:, 2048:], rw_w0, _lora_pad(rw_w2, 0), rw_a0,
                    _lora_pad(rw_a2, 128), rw_kk[None], rw_ka[None], rw_rk.reshape(1, -1), bd512)
    r, kk, v, gs, bonus, lw0, lw1, akk0, akk1, kd0, kd1 = prep
    o_f = _rw_scan(False, r, kk, v, lw0, akk0, kd0)
    y_b = _rw_scan(True, r, kk, v, lw1, akk1, kd1,
                   (o_f, bonus, gs, rw_ln_w[None], rw_ln_b[None], bd512))
    tile = lambda g, n: jnp.tile(g, n)[None]
    naq, nak, nav, waq, wak, wav = _qk_prep(p, tile(na_qn, 8), tile(na_kn, 8), tile(wa_qn, 8),
                                            tile(wa_kn, 2), cos, sin, bd512, bd128)
    y_c = _na(naq, nak, nav, _na_bias_table(na_rpb, lc), lc)
    G = WA_HEADS // WA_KV_HEADS
    sink_col = jnp.repeat(wa_sink.reshape(WA_KV_HEADS, G), WA_WINDOW, axis=1)[..., None]
    y_d = _wa(waq, wak, wav, sink_col, lc)
    flat = lambda y: y.transpose(1, 0, 2).reshape(rows, -1)
    return y_a, y_b, flat(y_c), flat(y_d)


def kernel(x, c, ctx, c_ctx, ada_w, ada_b, norm_ffn1, norm_mix, norm_ffn2, ffn1_wi, ffn1_wo, ffn2_wi, ffn2_wo, w_in, hg_lb, hg_norm, rw_shift, rw_w0, rw_w2, rw_a0, rw_a2, rw_kk, rw_ka, rw_rk, rw_ln_w, rw_ln_b, na_qn, na_kn, na_rpb, wa_qn, wa_kn, wa_sink, w_branch, w_out):
    assert x.shape[0] == 1 and ctx.shape[1] == MIX_ROWS
    lc = ctx.shape[1]
    t_len = x.shape[1]
    xa = jnp.concatenate([ctx[0], x[0]], axis=0)
    cc_t = jnp.stack([c[0], c_ctx], axis=1)
    mods = _ada_mods(cc_t, ada_w, ada_b).reshape(DEPTH, 2, N_MOD, D)
    tables = (_block_diag(512, 64), _block_diag(128, 64)) + _rope_tables(lc, t_len)
    for l in range(DEPTH):
        m = mods[l]
        xa = _ffn(xa, m[:, 0:3], norm_ffn1[l][None], ffn1_wi[l].astype(BF16), ffn1_wo[l].astype(BF16), lc)
        p = _win(xa, m[:, 3:6], norm_mix[l][None], _permute_w_in(w_in[l]).astype(BF16), lc)
        yb = _token_mixing(p, lc, l, hg_lb, hg_norm[l], rw_shift[l], rw_w0[l], rw_w2[l], rw_a0[l], rw_a2[l],
                           rw_kk[l], rw_ka[l], rw_rk[l], rw_ln_w[l], rw_ln_b[l], na_qn[l], na_kn[l],
                           na_rpb[l], wa_qn[l], wa_kn[l], wa_sink[l], tables)
        xa = _merge(xa, m[:, 3:6], yb, p, w_branch[l].astype(BF16), w_out[l].astype(BF16), lc)
        xa = _ffn(xa, m[:, 6:9], norm_ffn2[l][None], ffn2_wi[l].astype(BF16), ffn2_wo[l].astype(BF16), lc)
    return xa[lc:][None]
```

```python
import functools

import numpy as np
import jax
import jax.numpy as jnp
from jax import lax
from jax.experimental import pallas as pl
from jax.experimental.pallas import tpu as pltpu

F32 = jnp.float32
BF16 = jnp.bfloat16

D = 2048
DEPTH = 2
GRID_W = 64
EPS = 1e-6
D_FF = 5632
N_MOD = 9
N_BRANCH = 4
BR_W = 512
HG_HEADS, HG_DK = 4, 128
RW_HEADS, RW_HD = 8, 64
RW_LORA = 64
RW_GN_EPS = 64e-5
NA_HEADS, NA_HD = 8, 64
NA_WIN_R, NA_WIN_C = 8, 16
NA_GROUP = 4
NA_KEY_ROWS = NA_WIN_R + NA_GROUP - 1
WA_HEADS, WA_KV_HEADS, WA_HD = 8, 2, 64
WA_WINDOW = 128
ROPE_BASE = 10000.0

OFF_RW, OFF_HG, OFF_NA, OFF_GL, OFF_WA, OFF_LORA = 0, 2048, 4608, 6144, 14336, 15104
P_TOTAL = 15360

MIX_ROWS = 256
GLA_CHUNK = 16
RW_CHUNK = 64
NEG = -1e30
VMEM_LIMIT = 56 * 1024 * 1024

NT = (((1,), (1,)), ((), ()))
TN = (((0,), (0,)), ((), ()))


def _cparams(sem):
    return pltpu.CompilerParams(dimension_semantics=sem, vmem_limit_bytes=VMEM_LIMIT)


def _mm(a, b, dims=None):
    a = a.astype(BF16)
    b = b.astype(BF16)
    if dims is None:
        return jnp.dot(a, b, preferred_element_type=F32)
    return lax.dot_general(a, b, dims, preferred_element_type=F32)


def _mm2(x, w_bf16):
    hi = x.astype(BF16)
    lo = (x - hi.astype(F32)).astype(BF16)
    return (jnp.dot(hi, w_bf16, preferred_element_type=F32)
            + jnp.dot(lo, w_bf16, preferred_element_type=F32))


def _mm_f32(a, b):
    return jnp.dot(a, b, preferred_element_type=F32, precision=lax.Precision.HIGHEST)


def _sigmoid(x):
    return 1.0 / (1.0 + jnp.exp(-x))


def _silu(x):
    return x * _sigmoid(x)


def _seg_cumsum(x, tri):
    w = x.shape[1]
    hi = x.astype(BF16)
    r1 = x - hi.astype(F32)
    mid = r1.astype(BF16)
    lo = (r1 - mid.astype(F32)).astype(BF16)
    y = jnp.dot(tri, jnp.concatenate([hi, mid, lo], axis=1), preferred_element_type=F32)
    return y[:, :w] + y[:, w:2 * w] + y[:, 2 * w:]


def _seg_tri(rows, seg, rev):
    r = np.arange(rows)
    same = (r[:, None] // seg) == (r[None, :] // seg)
    tri = (r[None, :] >= r[:, None]) if rev else (r[None, :] <= r[:, None])
    return jnp.asarray(same & tri, dtype=BF16)


def _ada_kernel(cc_ref, w_ref, b_ref, o_ref):
    s = _silu(cc_ref[...])
    w = w_ref[...]
    r0 = jnp.sum(w * s[:, 0:1], axis=0, keepdims=True)
    r1 = jnp.sum(w * s[:, 1:2], axis=0, keepdims=True)
    o_ref[...] = jnp.concatenate([r0, r1], axis=0) + b_ref[...]


def _ada_mods(cc_t, ada_w, ada_b):
    tn = 512
    nmod = ada_w.shape[-1]
    return pl.pallas_call(
        _ada_kernel,
        grid=(DEPTH, nmod // tn),
        in_specs=[pl.BlockSpec((D, 2), lambda l, j: (0, 0)),
                  pl.BlockSpec((None, D, tn), lambda l, j: (l, 0, j)),
                  pl.BlockSpec((None, 1, tn), lambda l, j: (l, 0, j))],
        out_specs=pl.BlockSpec((None, 2, tn), lambda l, j: (l, 0, j)),
        out_shape=jax.ShapeDtypeStruct((DEPTH, 2, nmod), F32),
        compiler_params=_cparams(("arbitrary", "arbitrary")),
        name="ada_mod",
    )(cc_t, ada_w, ada_b.reshape(DEPTH, 1, nmod))


def _row_is_ctx(i, tm, lc):
    return (i * tm + lax.broadcasted_iota(jnp.int32, (tm, 1), 0)) < lc


def _mod_pick(mod_ref, j, is_ctx):
    return jnp.where(is_ctx, mod_ref[1, j:j + 1, :], mod_ref[0, j:j + 1, :])


def _modulated(x, g, mod_ref, is_ctx):
    y = x * lax.rsqrt(jnp.mean(x * x, axis=-1, keepdims=True) + EPS) * g
    return y * (1.0 + _mod_pick(mod_ref, 1, is_ctx)) + _mod_pick(mod_ref, 0, is_ctx)


def _ffn_kernel(lc, tm, nf, x_ref, mod_ref, g_ref, wa_ref, wb_ref, wo_ref, o_ref, xn_ref):
    i = pl.program_id(0)
    j = pl.program_id(1)
    is_ctx = _row_is_ctx(i, tm, lc)

    @pl.when(j == 0)
    def _():
        xn_ref[...] = _modulated(x_ref[...], g_ref[...], mod_ref, is_ctx).astype(BF16)
        o_ref[...] = jnp.zeros_like(o_ref)

    xn = xn_ref[...]
    a = jnp.dot(xn, wa_ref[...], preferred_element_type=F32)
    b = jnp.dot(xn, wb_ref[...], preferred_element_type=F32)
    h = (_silu(a) * b).astype(BF16)
    o_ref[...] += jnp.dot(h, wo_ref[...], preferred_element_type=F32)

    @pl.when(j == nf - 1)
    def _():
        o_ref[...] = x_ref[...] + 0.5 * _mod_pick(mod_ref, 2, is_ctx) * o_ref[...]


def _dense_tm(rows):
    for tm in (768, 512, 256):
        if rows % tm == 0:
            return tm
    raise ValueError(rows)


def _ffn(x, mod3, g, wi, wo, lc):
    rows = x.shape[0]
    tm = _dense_tm(rows)
    tf = 512
    nf = D_FF // tf
    return pl.pallas_call(
        functools.partial(_ffn_kernel, lc, tm, nf),
        grid=(rows // tm, nf),
        in_specs=[pl.BlockSpec((tm, D), lambda i, j: (i, 0)),
                  pl.BlockSpec((2, 3, D), lambda i, j: (0, 0, 0)),
                  pl.BlockSpec((1, D), lambda i, j: (0, 0)),
                  pl.BlockSpec((D, tf), lambda i, j: (0, j)),
                  pl.BlockSpec((D, tf), lambda i, j: (0, j + nf)),
                  pl.BlockSpec((tf, D), lambda i, j: (j, 0))],
        out_specs=pl.BlockSpec((tm, D), lambda i, j: (i, 0)),
        out_shape=jax.ShapeDtypeStruct((rows, D), F32),
        scratch_shapes=[pltpu.VMEM((tm, D), BF16)],
        compiler_params=_cparams(("arbitrary", "arbitrary")),
        name="ffn",
    )(x, mod3, g, wi, wi, wo)


def _win_kernel(lc, tm, x_ref, mod_ref, g_ref, w_ref, o_ref, xn_ref):
    i = pl.program_id(0)

    @pl.when(pl.program_id(1) == 0)
    def _():
        is_ctx = _row_is_ctx(i, tm, lc)
        xn_ref[...] = _modulated(x_ref[...], g_ref[...], mod_ref, is_ctx).astype(BF16)

    o_ref[...] = jnp.dot(xn_ref[...], w_ref[...], preferred_element_type=F32)


def _win(x, mod3, g, w, lc):
    rows = x.shape[0]
    tm = _dense_tm(rows)
    tn = 1536
    return pl.pallas_call(
        functools.partial(_win_kernel, lc, tm),
        grid=(rows // tm, P_TOTAL // tn),
        in_specs=[pl.BlockSpec((tm, D), lambda i, j: (i, 0)),
                  pl.BlockSpec((2, 3, D), lambda i, j: (0, 0, 0)),
                  pl.BlockSpec((1, D), lambda i, j: (0, 0)),
                  pl.BlockSpec((D, tn), lambda i, j: (0, j))],
        out_specs=pl.BlockSpec((tm, tn), lambda i, j: (i, j)),
        out_shape=jax.ShapeDtypeStruct((rows, P_TOTAL), F32),
        scratch_shapes=[pltpu.VMEM((tm, D), BF16)],
        compiler_params=_cparams(("arbitrary", "arbitrary")),
        name="w_in",
    )(x, mod3, g, w)


def _merge_kernel(lc, tm, x_ref, mod_ref, ya_ref, yb_ref, yc_ref, yd_ref, gl_ref, wb_ref, wo_ref, o_ref, acc_ref):
    i = pl.program_id(0)
    n = pl.program_id(1)

    @pl.when(n == 0)
    def _():
        acc_ref[...] = jnp.zeros_like(acc_ref)

    for nn, y_ref in enumerate((ya_ref, yb_ref, yc_ref, yd_ref)):
        @pl.when(n == nn)
        def _():
            proj = jnp.dot(y_ref[...].astype(BF16), wb_ref[...], preferred_element_type=F32)
            acc_ref[...] += _sigmoid(gl_ref[...]) * proj

    @pl.when(n == N_BRANCH - 1)
    def _():
        is_ctx = _row_is_ctx(i, tm, lc)
        y = jnp.dot(acc_ref[...].astype(BF16), wo_ref[...], preferred_element_type=F32)
        o_ref[...] = x_ref[...] + _mod_pick(mod_ref, 2, is_ctx) * y


def _merge(x, mod3, ys, p, wb, wo, lc):
    rows = x.shape[0]
    tm = 384 if rows % 384 == 0 else MIX_ROWS
    glb = OFF_GL // D
    yspec = pl.BlockSpec((tm, BR_W), lambda i, n: (i, 0))
    return pl.pallas_call(
        functools.partial(_merge_kernel, lc, tm),
        grid=(rows // tm, N_BRANCH),
        in_specs=[pl.BlockSpec((tm, D), lambda i, n: (i, 0)),
                  pl.BlockSpec((2, 3, D), lambda i, n: (0, 0, 0)),
                  yspec, yspec, yspec, yspec,
                  pl.BlockSpec((tm, D), lambda i, n: (i, glb + n)),
                  pl.BlockSpec((None, BR_W, D), lambda i, n: (n, 0, 0)),
                  pl.BlockSpec((D, D), lambda i, n: (0, 0))],
        out_specs=pl.BlockSpec((tm, D), lambda i, n: (i, 0)),
        out_shape=jax.ShapeDtypeStruct((rows, D), F32),
        scratch_shapes=[pltpu.VMEM((tm, D), F32)],
        compiler_params=_cparams(("arbitrary", "arbitrary")),
        name="merge",
    )(x, mod3, *ys, p, wb, wo)


def _gla_kernel(rev, *refs):
    if rev:
        q_ref, f_ref, i_ref, lb_ref, tri_ref, of_ref, g_ref, nw_ref, y_ref, st_ref = refs
    else:
        q_ref, f_ref, i_ref, lb_ref, tri_ref, y_ref, st_ref = refs
    C = GLA_CHUNK
    R = MIX_ROWS
    NH, DK = HG_HEADS, HG_DK

    @pl.when(pl.program_id(0) == 0)
    def _():
        st_ref[...] = jnp.zeros_like(st_ref)

    lb = lb_ref[...]
    f_all = lb + (1.0 - lb) * _sigmoid(f_ref[...])
    b_all = _seg_cumsum(jnp.log(f_all), tri_ref[...])
    pos = lax.broadcasted_iota(jnp.int32, (R, 1), 0) & (C - 1)
    chunks = list(range(R // C - 1, -1, -1) if rev else range(R // C))

    heads = []
    for h in range(NH):
        hs = slice(h * DK, (h + 1) * DK)
        b = b_all[:, hs]
        k = 1.0 - f_all[:, hs]
        q = q_ref[:, hs]
        v = i_ref[:, hs]
        o = jnp.sum(q * k, axis=-1, keepdims=True) * v
        for r in range(8):
            sh = ((R - r) if rev else r) % R
            br, kr, vr = (pltpu.roll(t, sh, axis=0) if r else t for t in (b, k, v))
            for d in (r, r + 8):
                if d == 0:
                    continue
                if d >= 8:
                    sh8 = (R - 8) if rev else 8
                    bd_, kd_, vd_ = (pltpu.roll(t, sh8, axis=0) for t in (br, kr, vr))
                else:
                    bd_, kd_, vd_ = br, kr, vr
                valid = (pos < C - d) if rev else (pos >= d)
                e = jnp.exp(jnp.where(valid, b - bd_, NEG))
                att = jnp.sum(q * kd_ * e, axis=-1, keepdims=True)
                o = o + att * vd_
        upd, dec = {}, {}
        for c in chunks:
            sl = slice(c * C, (c + 1) * C)
            b_end = b[sl][0:1] if rev else b[sl][C - 1:C]
            dec[c] = jnp.exp(b_end)
            upd[c] = _mm(v[sl], k[sl] * jnp.exp(b_end - b[sl]), TN)
        heads.append(dict(o=o, qb=q * jnp.exp(b), upd=upd, dec=dec, st=st_ref[h]))

    for c in chunks:
        for w in heads:
            w[c] = w["st"]
            w["st"] = w["st"] * w["dec"][c] + w["upd"][c]
    outs = []
    for h, w in enumerate(heads):
        st_ref[h] = w["st"]
        inter = {c: _mm(w["qb"][c * C:(c + 1) * C], w[c], NT) for c in chunks}
        outs.append(w["o"] + jnp.concatenate([inter[c] for c in range(R // C)], axis=0))

    for h, o in enumerate(outs):
        hs = slice(h * DK, (h + 1) * DK)
        if rev:
            o = of_ref[:, hs] + o
            o = o * lax.rsqrt(jnp.mean(o * o, axis=-1, keepdims=True) + EPS) * nw_ref[:, hs]
            o = o * _silu(g_ref[:, hs])
        y_ref[:, hs] = o


def _blk_order(rev, nb):
    if rev:
        return lambda b: jnp.where(b == 0, 0, nb - b)
    return lambda b: b


def _gla_dir(p, lb, rev, o_fwd=None, norm_w=None):
    rows = p.shape[0]
    nb = rows // MIX_ROWS
    order = _blk_order(rev, nb)
    W = HG_HEADS * HG_DK
    cb = OFF_HG // W

    def col(n):
        return pl.BlockSpec((MIX_ROWS, W), lambda b: (order(b), cb + n))

    vec = pl.BlockSpec((1, W), lambda b: (0, 0))
    in_specs = [col(0), col(2 if rev else 1), col(3), vec,
                pl.BlockSpec((MIX_ROWS, MIX_ROWS), lambda b: (0, 0))]
    args = [p, p, p, lb, _seg_tri(MIX_ROWS, GLA_CHUNK, rev)]
    if rev:
        in_specs += [pl.BlockSpec((MIX_ROWS, W), lambda b: (order(b), 0)), col(4), vec]
        args += [o_fwd, p, norm_w]
    return pl.pallas_call(
        functools.partial(_gla_kernel, rev),
        grid=(nb,),
        in_specs=in_specs,
        out_specs=pl.BlockSpec((MIX_ROWS, W), lambda b: (order(b), 0)),
        out_shape=jax.ShapeDtypeStruct((rows, W), F32),
        scratch_shapes=[pltpu.VMEM((HG_HEADS, HG_DK, HG_DK), F32)],
        compiler_params=_cparams(("arbitrary",)),
        name="gla_rev" if rev else "gla_fwd",
    )(*args)


def _rw_prep_kernel(nb, main_ref, lora_ref, pm_ref, nm_ref, pl_ref, nl_ref, tm_ref, tl_ref,
                    w0_ref, w2_ref, a0_ref, a2_ref, kkw_ref, ka_ref, rk_ref, bd_ref,
                    r_o, kk_o, v_o, gs_o, bonus_o, lw0_o, lw1_o, akk0_o, akk1_o, kd0_o, kd1_o):
    i = pl.program_id(0)
    has_prev = (i >= 2).astype(F32)
    has_next = jnp.logical_and(i != 0, i != nb - 1).astype(F32)
    rowi = lax.broadcasted_iota(jnp.int32, (MIX_ROWS, 1), 0)

    def shift(x, prev_blk, next_blk, taps):
        up = jnp.where(rowi == 0, prev_blk[7:8, :] * has_prev, pltpu.roll(x, 1, axis=0))
        dn = jnp.where(rowi == MIX_ROWS - 1, next_blk[0:1, :] * has_next,
                       pltpu.roll(x, MIX_ROWS - 1, axis=0))
        return taps[0:1] * up + taps[1:2] * x + taps[2:3] * dn

    main = shift(main_ref[...], pm_ref[...], nm_ref[...], tm_ref[...])
    lora = shift(lora_ref[...], pl_ref[...], nl_ref[...], tl_ref[...])
    W = RW_HEADS * RW_HD
    r, k, v, g = (main[:, n * W:(n + 1) * W] for n in range(4))
    bd = bd_ref[...]
    kk = k * kkw_ref[...]
    kk = kk * lax.rsqrt(_mm2(kk * kk, bd) + EPS)
    tl = jnp.tanh(lora)
    kds = []
    for d, (lw_o, akk_o, kd_o) in enumerate(((lw0_o, akk0_o, kd0_o), (lw1_o, akk1_o, kd1_o))):
        z = -(w0_ref[d:d + 1, :] + _mm_f32(tl, w2_ref[d]))
        softplus = jnp.maximum(z, 0.0) + jnp.log(1.0 + jnp.exp(-jnp.abs(z)))
        lw_o[...] = -jnp.exp(-softplus - 0.5)
        a = _sigmoid(a0_ref[d:d + 1, :] + _mm_f32(lora, a2_ref[d]))
        kd = k * (1.0 + (a - 1.0) * ka_ref[...])
        kds.append(kd)
        kd_o[...] = kd
        akk_o[...] = a * kk
    r_o[...] = r
    kk_o[...] = kk
    v_o[...] = v
    gs_o[...] = _sigmoid(g)
    bonus_o[...] = _mm2(r * (kds[0] + kds[1]) * rk_ref[...], bd) * v


def _rw_prep(p, taps_m, taps_l, w0, w2p, a0, a2p, kkw, ka, rk, bd512):
    rows = p.shape[0]
    nb = rows // MIX_ROWS
    W = RW_HEADS * RW_HD
    n8 = rows // 8
    lb = OFF_LORA // 256
    per = MIX_ROWS // 8
    full = lambda shape: pl.BlockSpec(shape, lambda i: (0,) * len(shape))
    in_specs = [pl.BlockSpec((MIX_ROWS, 4 * W), lambda i: (i, 0)),
                pl.BlockSpec((MIX_ROWS, 256), lambda i: (i, lb)),
                pl.BlockSpec((8, 4 * W), lambda i: (jnp.maximum(i * per - 1, 0), 0)),
                pl.BlockSpec((8, 4 * W), lambda i: (jnp.minimum((i + 1) * per, n8 - 1), 0)),
                pl.BlockSpec((8, 256), lambda i: (jnp.maximum(i * per - 1, 0), lb)),
                pl.BlockSpec((8, 256), lambda i: (jnp.minimum((i + 1) * per, n8 - 1), lb)),
                full((3, 4 * W)), full((3, 256)),
                full((2, W)), full((2, 256, W)), full((2, W)), full((2, 256, W)),
                full((1, W)), full((1, W)), full((1, W)), full((W, W))]
    out = jax.ShapeDtypeStruct((rows, W), F32)
    return pl.pallas_call(
        functools.partial(_rw_prep_kernel, nb),
        grid=(nb,),
        in_specs=in_specs,
        out_specs=[pl.BlockSpec((MIX_ROWS, W), lambda i: (i, 0))] * 11,
        out_shape=[out] * 11,
        compiler_params=_cparams(("arbitrary",)),
        name="rw_prep",
    )(p, p, p, p, p, p, taps_m, taps_l, w0, w2p, a0, a2p, kkw, ka, rk, bd512)


def _rw_scan_kernel(rev, *refs):
    if rev:
        (r_ref, kk_ref, v_ref, lw_ref, akk_ref, kd_ref, tri_ref, of_ref, bonus_ref, gs_ref,
         lnw_ref, lnb_ref, bd_ref, y_ref, s_ref) = refs
    else:
        r_ref, kk_ref, v_ref, lw_ref, akk_ref, kd_ref, tri_ref, y_ref, s_ref = refs
    C = RW_CHUNK
    HD = RW_HD
    P2 = 2 * HD

    NP = RW_HEADS // 2

    @pl.when(pl.program_id(0) == 0)
    def _():
        s_ref[...] = jnp.zeros_like(s_ref)

    lw = lw_ref[...]
    cum = _seg_cumsum(lw, tri_ref[...])
    cum_prev = cum - lw

    lane = lax.broadcasted_iota(jnp.int32, (1, P2), 1)
    h0 = lane < HD
    ri = lax.broadcasted_iota(jnp.int32, (P2, P2), 0)
    ci = lax.broadcasted_iota(jnp.int32, (P2, P2), 1)
    same = (ri // C) == (ci // C)
    rt, cs = ri % C, ci % C
    strict = jnp.logical_and(same, (rt < cs) if rev else (rt > cs))
    incl = jnp.logical_and(same, (rt <= cs) if rev else (rt >= cs))
    eye = (ri == ci).astype(F32)
    bdiag = ((ri // HD) == (ci // HD)).astype(F32)

    def stack(x):
        return jnp.concatenate([x, x], axis=0)

    def split_heads(x):
        return jnp.concatenate([jnp.where(h0, x, 0.0), jnp.where(h0, 0.0, x)], axis=0)

    def unstack(x):
        return jnp.where(h0, x[:C], x[C:])

    chunks = list(range(MIX_ROWS // C - 1, -1, -1) if rev else range(MIX_ROWS // C))
    order = [(c, p) for c in chunks for p in range(NP)]

    pre = {}
    for c, p in order:
        sl = slice(c * C, (c + 1) * C)
        pp = slice(p * P2, (p + 1) * P2)
        cum_c = cum[sl, pp]
        cend = cum_c[0:1] if rev else cum_c[C - 1:C]
        e_neg = jnp.exp(-cum_c)
        e_end = jnp.exp(cend - cum_c)
        a_t = kk_ref[sl, pp] * jnp.exp(cum_prev[sl, pp])
        r_t = r_ref[sl, pp] * jnp.exp(cum_c)
        akk = akk_ref[sl, pp]
        kd = kd_ref[sl, pp]
        v = v_ref[sl, pp]
        pre[c, p] = dict(a_t=a_t, r_t=r_t, v=v, kh=kd * e_end, bh=akk * e_end, dec=jnp.exp(cend),
                         lhs=jnp.concatenate([split_heads(a_t), split_heads(r_t)], axis=0),
                         rhs=jnp.concatenate([stack(akk * e_neg), stack(kd * e_neg)], axis=0))
    for w in pre.values():
        g = _mm(w["lhs"], w["rhs"], NT)
        w["m"] = jnp.where(strict, g[:P2, :P2], 0.0)
        w["a_ak"] = jnp.where(strict, g[:P2, P2:], 0.0)
        w["a_rb"] = jnp.where(incl, g[P2:, :P2], 0.0)
        w["a_rk"] = jnp.where(incl, g[P2:, P2:], 0.0)
        w["p"] = eye - w["m"]
    for w in pre.values():
        av = _mm(jnp.concatenate([w["a_ak"], w["a_rk"]], axis=0), stack(w["v"]))
        w["av"], w["rkv"] = av[:P2], av[P2:]
    for w in pre.values():
        w["m"] = _mm(w["m"], w["m"])
    for _ in range(4):
        for w in pre.values():
            pm = _mm(jnp.concatenate([w["p"], w["m"]], axis=0), w["m"])
            w["p"] = w["p"] + pm[:P2]
            w["m"] = pm[P2:]
    for w in pre.values():
        w["p"] = w["p"] + _mm(w["p"], w["m"])
    for w in pre.values():
        w["tt"] = _mm(w["p"], jnp.concatenate([stack(w["a_t"]), w["av"]], axis=1))
        w["z"] = _mm(w["p"], stack(w["bh"]), TN)
    for w in pre.values():
        ar = _mm(w["a_rb"], w["tt"])
        w["oa"] = stack(w["r_t"]) - ar[:, :P2]
        w["oc"] = w["rkv"] - ar[:, P2:]
        w["pz"] = bdiag * _mm(split_heads(w["a_t"]), w["z"], TN)
        w["kz"] = stack(w["kh"]) - _mm(w["a_ak"], w["z"], TN)
    for w in pre.values():
        w["q"] = bdiag * _mm(split_heads(w["v"]), w["kz"], TN)

    s = [s_ref[p] for p in range(NP)]
    for c, p in order:
        w = pre[c, p]
        w["s0"] = s[p]
        s[p] = s[p] * w["dec"] - _mm(s[p], w["pz"]) + w["q"]
    for c, p in order:
        w = pre[c, p]
        o_st = _mm(w["oa"], w["s0"], NT) + w["oc"]
        y_ref[c * C:(c + 1) * C, p * P2:(p + 1) * P2] = unstack(o_st)
    for p in range(NP):
        s_ref[p] = s[p]

    if rev:
        o = of_ref[...] + y_ref[...]
        bd = bd_ref[...]
        mu = _mm2(o, bd) * (1.0 / HD)
        oc = o - mu
        var = _mm2(oc * oc, bd) * (1.0 / HD)
        o = oc * lax.rsqrt(var + RW_GN_EPS) * lnw_ref[...] + lnb_ref[...]
        y_ref[...] = (o + bonus_ref[...]) * gs_ref[...]


def _rw_scan(rev, r, kk, v, lw, akk, kd, extra=()):
    rows = r.shape[0]
    nb = rows // MIX_ROWS
    order = _blk_order(rev, nb)
    P2 = 2 * RW_HD
    W = RW_HEADS * RW_HD
    blk = pl.BlockSpec((MIX_ROWS, W), lambda b: (order(b), 0))
    in_specs = [blk] * 6 + [pl.BlockSpec((MIX_ROWS, MIX_ROWS), lambda b: (0, 0))]
    args = [r, kk, v, lw, akk, kd, _seg_tri(MIX_ROWS, RW_CHUNK, rev)]
    if rev:
        o_fwd, bonus, gs, lnw, lnb, bd512 = extra
        vec = pl.BlockSpec((1, W), lambda b: (0, 0))
        in_specs += [blk, blk, blk, vec, vec, pl.BlockSpec((W, W), lambda b: (0, 0))]
        args += [o_fwd, bonus, gs, lnw, lnb, bd512]
    return pl.pallas_call(
        functools.partial(_rw_scan_kernel, rev),
        grid=(nb,),
        in_specs=in_specs,
        out_specs=blk,
        out_shape=jax.ShapeDtypeStruct((rows, W), F32),
        scratch_shapes=[pltpu.VMEM((RW_HEADS // 2, P2, P2), F32)],
        compiler_params=_cparams(("arbitrary",)),
        name="rw_rev" if rev else "rw_fwd",
    )(*args)


def _qk_prep_kernel(nq_ref, nk_ref, nv_ref, wq_ref, wkv_ref, nqn_ref, nkn_ref, wqn_ref, wkn_ref,
                    cos_ref, sin_ref, bd512_ref, bd128_ref,
                    naq_o, nak_o, nav_o, waq_o, wak_o, wav_o):
    bd512 = bd512_ref[...]
    bd128 = bd128_ref[...]

    def hnorm(x, g, bd, hd):
        return x * lax.rsqrt(_mm2(x * x, bd) * (1.0 / hd) + EPS) * g

    def rope(x, cos, sin):
        lane = lax.broadcasted_iota(jnp.int32, x.shape, 1)
        w = x.shape[1]
        partner = jnp.where((lane & 31) < 16, pltpu.roll(x, w - 16, axis=1), pltpu.roll(x, 16, axis=1))
        return x * cos + partner * sin

    def dup_heads(o_ref, x):
        lower = lax.broadcasted_iota(jnp.int32, x.shape, 1) < WA_HD
        swapped = pltpu.roll(x, WA_HD, axis=1)
        o_ref[0] = jnp.where(lower, x, swapped).astype(o_ref.dtype)
        o_ref[1] = jnp.where(lower, swapped, x).astype(o_ref.dtype)

    naq_o[...] = (hnorm(nq_ref[...], nqn_ref[...], bd512, NA_HD) * NA_HD ** -0.5).astype(BF16)
    nak_o[...] = hnorm(nk_ref[...], nkn_ref[...], bd512, NA_HD).astype(BF16)
    nav_o[...] = nv_ref[...].astype(BF16)
    cos = cos_ref[...]
    sin = sin_ref[...]
    wq = hnorm(wq_ref[...], wqn_ref[...], bd512, WA_HD) * WA_HD ** -0.5
    wq = rope(wq, jnp.concatenate([cos] * 4, axis=1), jnp.concatenate([sin] * 4, axis=1))
    waq_o[...] = wq.astype(BF16)
    kv = wkv_ref[...]
    dup_heads(wak_o, rope(hnorm(kv[:, :128], wkn_ref[...], bd128, WA_HD), cos, sin))
    dup_heads(wav_o, kv[:, 128:])


def _qk_prep(p, nqn, nkn, wqn, wkn, cos, sin, bd512, bd128):
    rows = p.shape[0]
    nb = rows // MIX_ROWS
    nab = OFF_NA // 512
    full = lambda shape: pl.BlockSpec(shape, lambda i: (0,) * len(shape))
    in_specs = [pl.BlockSpec((MIX_ROWS, 512), lambda i: (i, nab)),
                pl.BlockSpec((MIX_ROWS, 512), lambda i: (i, nab + 1)),
                pl.BlockSpec((MIX_ROWS, 512), lambda i: (i, nab + 2)),
                pl.BlockSpec((MIX_ROWS, 512), lambda i: (i, OFF_WA // 512)),
                pl.BlockSpec((MIX_ROWS, 256), lambda i: (i, (OFF_WA + 512) // 256)),
                full((1, 512)), full((1, 512)), full((1, 512)), full((1, 128)),
                pl.BlockSpec((MIX_ROWS, 128), lambda i: (i, 0)),
                pl.BlockSpec((MIX_ROWS, 128), lambda i: (i, 0)),
                full((512, 512)), full((128, 128))]
    flat = pl.BlockSpec((MIX_ROWS, 512), lambda i: (i, 0))
    flat_shape = jax.ShapeDtypeStruct((rows, 512), BF16)
    dup = pl.BlockSpec((WA_KV_HEADS, MIX_ROWS, 128), lambda i: (0, i, 0))
    dup_shape = jax.ShapeDtypeStruct((WA_KV_HEADS, rows, 128), BF16)
    return pl.pallas_call(
        _qk_prep_kernel,
        grid=(nb,),
        in_specs=in_specs,
        out_specs=[flat, flat, flat, flat, dup, dup],
        out_shape=[flat_shape, flat_shape, flat_shape, flat_shape, dup_shape, dup_shape],
        compiler_params=_cparams(("arbitrary",)),
        name="qk_prep",
    )(p, p, p, p, p, nqn, nkn, wqn, wkn, cos, sin, bd512, bd128)


def _na_bias_layout():
    wr, G, KR = NA_WIN_R, NA_GROUP, NA_KEY_ROWS
    off = (0, -(wr // 2), -(KR - G))
    out = {}
    for kind in range(3):
        for r in range(G):
            wstart = (0, r, KR - wr)[kind]
            for a in range(KR):
                inside = wstart <= a < wstart + wr
                out[kind, r, a] = (off[kind] + a - r + wr - 1) if inside else None
    return out


def _na_kernel(lc, nrows, q_ref, k_ref, v_ref, t_ref, o_ref, bias_s):
    W = GRID_W
    HD = NA_HD
    nq = NA_GROUP * W
    nk = NA_KEY_ROWS * W
    ngroups = nrows // NA_GROUP
    kc = k_ref[0:lc, :]
    vc = v_ref[0:lc, :]
    half = [lax.broadcasted_iota(jnp.int32, (1, 2 * HD), 1) // HD == hh for hh in range(2)]

    bias_s[:, :, :, 0:lc] = jnp.zeros((2, 3, nq, lc), F32)
    for (kind, r, a), ro in _na_bias_layout().items():
        c0 = lc + a * W
        for hh in range(2):
            tile = jnp.full((W, W), NEG, F32) if ro is None else t_ref[hh, ro, :, (a % 2) * W:(a % 2 + 1) * W]
            bias_s[hh, kind, r * W:(r + 1) * W, c0:c0 + W] = tile

    def attend(q, keys, vals, bias):
        out = None
        for hh in range(2):
            s = _mm(jnp.where(half[hh], q, 0), keys, NT)
            if bias is not None:
                s = s + bias(hh)
            p = jnp.exp(s - jnp.max(s, axis=-1, keepdims=True))
            o = _mm(p, jnp.where(half[hh], vals, 0)) / jnp.sum(p, axis=-1, keepdims=True)
            out = o if out is None else out + o
        return out

    o_ref[0:lc, :] = attend(q_ref[0:lc, :], kc, vc, None)

    def group(gi, carry):
        base = jnp.clip(gi * NA_GROUP - NA_WIN_R // 2, 0, nrows - NA_KEY_ROWS)
        kind = jnp.where(gi == 0, 0, jnp.where(gi == ngroups - 1, 2, 1))
        q0 = pl.multiple_of(lc + gi * nq, nq)
        k0 = pl.multiple_of(lc + base * W, W)
        keys = jnp.concatenate([kc, k_ref[pl.ds(k0, nk), :]], axis=0)
        vals = jnp.concatenate([vc, v_ref[pl.ds(k0, nk), :]], axis=0)
        o_ref[pl.ds(q0, nq), :] = attend(q_ref[pl.ds(q0, nq), :], keys, vals, lambda hh: bias_s[hh, kind])
        return carry

    lax.fori_loop(0, ngroups, group, 0, unroll=2)


def _na(q, k, v, t2, lc):
    rows = q.shape[0]
    nrows = (rows - lc) // GRID_W
    pblk = pl.BlockSpec((rows, 2 * NA_HD), lambda p: (0, p))
    return pl.pallas_call(
        functools.partial(_na_kernel, lc, nrows),
        grid=(NA_HEADS // 2,),
        in_specs=[pblk, pblk, pblk, pl.BlockSpec((2,) + t2.shape[1:], lambda p: (p, 0, 0, 0))],
        out_specs=pblk,
        out_shape=jax.ShapeDtypeStruct((rows, NA_HEADS * NA_HD), F32),
        scratch_shapes=[pltpu.VMEM((2, 3, NA_GROUP * GRID_W, lc + NA_KEY_ROWS * GRID_W), F32)],
        compiler_params=_cparams(("arbitrary",)),
        name="na_attn",
    )(q, k, v, t2)


def _wa_kernel(lc, t_len, q_ref, k_ref, v_ref, sink_ref, o_ref):
    G = WA_HEADS // WA_KV_HEADS
    B = WA_WINDOW
    nband = 3 * B
    HD = WA_HD
    b = pl.program_id(1)
    kc = k_ref[0:lc, :]
    vc = v_ref[0:lc, :]
    sink = sink_ref[...]
    lower = lax.broadcasted_iota(jnp.int32, (1, 2 * HD), 1) < HD

    def stacked_queries(r0, n):
        parts = []
        for g in range(G):
            qp = q_ref[r0:r0 + n, (g // 2) * 2 * HD:(g // 2 + 1) * 2 * HD]
            parts.append(jnp.where(lower if g % 2 == 0 else jnp.logical_not(lower), qp, 0))
        return jnp.concatenate(parts, axis=0)

    def store(r0, n, o):
        for gp in range(G // 2):
            o_ref[r0:r0 + n, gp * 2 * HD:(gp + 1) * 2 * HD] = jnp.where(
                lower, o[2 * gp * n:(2 * gp + 1) * n], o[(2 * gp + 1) * n:(2 * gp + 2) * n])

    @pl.when(b == 0)
    def _():
        sk = jnp.concatenate([jnp.broadcast_to(sink[g * B:g * B + 1, :], (lc, 1)) for g in range(G)], axis=0)
        s = _mm(stacked_queries(0, lc), kc, NT)
        m = jnp.maximum(jnp.max(s, axis=-1, keepdims=True), sk)
        e = jnp.exp(s - m)
        store(0, lc, _mm(e, vc) / (jnp.sum(e, axis=-1, keepdims=True) + jnp.exp(sk - m)))

    @pl.when(b > 0)
    def _():
        qoff = lax.broadcasted_iota(jnp.int32, (G * B, lc + nband), 0) & (B - 1)
        koff = lax.broadcasted_iota(jnp.int32, (G * B, lc + nband), 1) - lc
        for j in range(MIX_ROWS // B):
            n = (b - 1) * (MIX_ROWS // B) + j
            start = jnp.clip((n - 1) * B, 0, t_len - nband)
            k0 = pl.multiple_of(lc + start, B)
            keys = jnp.concatenate([kc, k_ref[pl.ds(k0, nband), :]], axis=0)
            vals = jnp.concatenate([vc, v_ref[pl.ds(k0, nband), :]], axis=0)
            valid = jnp.logical_or(koff < 0, jnp.abs((n * B + qoff) - (start + koff)) <= WA_WINDOW)
            s = jnp.where(valid, _mm(stacked_queries(j * B, B), keys, NT), NEG)
            m = jnp.maximum(jnp.max(s, axis=-1, keepdims=True), sink)
            p = jnp.exp(s - m)
            store(j * B, B, _mm(p, vals) / (jnp.sum(p, axis=-1, keepdims=True) + jnp.exp(sink - m)))


def _wa(q, k, v, sink_col, lc):
    rows = q.shape[0]
    G = WA_HEADS // WA_KV_HEADS
    kvblk = pl.BlockSpec((None, rows, 2 * WA_HD), lambda h, b: (h, 0, 0))
    qblk = pl.BlockSpec((MIX_ROWS, G * WA_HD), lambda h, b: (b, h))
    return pl.pallas_call(
        functools.partial(_wa_kernel, lc, rows - lc),
        grid=(WA_KV_HEADS, rows // MIX_ROWS),
        in_specs=[qblk, kvblk, kvblk,
                  pl.BlockSpec((None, G * WA_WINDOW, 1), lambda h, b: (h, 0, 0))],
        out_specs=qblk,
        out_shape=jax.ShapeDtypeStruct((rows, WA_HEADS * WA_HD), F32),
        compiler_params=_cparams(("arbitrary", "arbitrary")),
        name="wa_attn",
    )(q, k, v, sink_col)


def _block_diag(n, blk):
    idx = np.arange(n) // blk
    return jnp.asarray(idx[:, None] == idx[None, :], dtype=BF16)


def _na_bias_table(rpb):
    W, wc = GRID_W, NA_WIN_C
    j = np.arange(W)
    cstart = np.clip(j - wc // 2, 0, W - wc)
    cabs = np.arange(W)
    inwin = (cabs[None, :] >= cstart[:, None]) & (cabs[None, :] < cstart[:, None] + wc)
    cb = cabs[None, :] - j[:, None] + wc - 1
    nh, nro, nco = rpb.shape
    onehot = (cb[None] == np.arange(nco)[:, None, None]) & inwin[None]
    t = jnp.dot(rpb.reshape(nh * nro, nco), jnp.asarray(onehot.reshape(nco, W * W), F32),
                precision=lax.Precision.HIGHEST).reshape(nh, nro, W, W)
    t = t + jnp.asarray(np.where(inwin, 0.0, NEG), F32)
    return jnp.concatenate([t, t], axis=-1)


def _rope_tables(lc, t_len):
    pos = np.arange(t_len)
    n = WA_HD // 4
    inv = ROPE_BASE ** (-np.arange(n, dtype=np.float64) / n)
    ang_r = (pos // GRID_W)[:, None] * inv[None, :]
    ang_c = (pos % GRID_W)[:, None] * inv[None, :]
    cos = np.concatenate([np.cos(ang_r)] * 2 + [np.cos(ang_c)] * 2, axis=1)
    sin = np.concatenate([-np.sin(ang_r), np.sin(ang_r), -np.sin(ang_c), np.sin(ang_c)], axis=1)
    cos = np.concatenate([np.ones((lc, WA_HD)), cos], axis=0)
    sin = np.concatenate([np.zeros((lc, WA_HD)), sin], axis=0)
    return (jnp.asarray(np.tile(cos, (1, 2)), dtype=F32), jnp.asarray(np.tile(sin, (1, 2)), dtype=F32))


def _permute_w_in(w):
    hg, rw, na, wa = 2560, 2304, 1536, 768
    s_rw, s_na, s_wa, s_gl = hg, hg + rw, hg + rw + na, hg + rw + na + wa
    return jnp.concatenate([w[:, s_rw:s_rw + 2048], w[:, :hg], w[:, s_na:s_wa], w[:, s_gl:],
                            w[:, s_wa:s_gl], w[:, s_rw + 2048:s_na]], axis=1)


def _lora_pad(w, row0):
    out = jnp.zeros((2, 256, w.shape[-1]), F32)
    for d in range(2):
        out = out.at[d, row0 + 64 * d:row0 + 64 * (d + 1)].set(w[d])
    return out


def _token_mixing(p, lc, layer, hg_lb, hg_norm, rw_shift, rw_w0, rw_w2, rw_a0, rw_a2, rw_kk, rw_ka, rw_rk,
                  rw_ln_w, rw_ln_b, na_qn, na_kn, na_rpb, wa_qn, wa_kn, wa_sink, tables):
    rows = p.shape[0]
    bd512, bd128, cos, sin = tables
    cum = jnp.cumsum(jax.nn.softmax(hg_lb.astype(F32), axis=1), axis=1)
    lbs = cum[:, layer] - cum[:, 0]
    o_f = _gla_dir(p, lbs[0:1], False)
    y_a = _gla_dir(p, lbs[1:2], True, o_f, hg_norm[None])
    prep = _rw_prep(p, rw_shift[:, :2048], rw_shift[:, 2048:], rw_w0, _lora_pad(rw_w2, 0), rw_a0,
                    _lora_pad(rw_a2, 128), rw_kk[None], rw_ka[None], rw_rk.reshape(1, -1), bd512)
    r, kk, v, gs, bonus, lw0, lw1, akk0, akk1, kd0, kd1 = prep
    o_f = _rw_scan(False, r, kk, v, lw0, akk0, kd0)
    y_b = _rw_scan(True, r, kk, v, lw1, akk1, kd1,
                   (o_f, bonus, gs, rw_ln_w[None], rw_ln_b[None], bd512))
    tile = lambda g, n: jnp.tile(g, n)[None]
    naq, nak, nav, waq, wak, wav = _qk_prep(p, tile(na_qn, 8), tile(na_kn, 8), tile(wa_qn, 8),
                                            tile(wa_kn, 2), cos, sin, bd512, bd128)
    y_c = _na(naq, nak, nav, _na_bias_table(na_rpb), lc)
    G = WA_HEADS // WA_KV_HEADS
    sink_col = jnp.repeat(wa_sink.reshape(WA_KV_HEADS, G), WA_WINDOW, axis=1)[..., None]
    y_d = _wa(waq, wak, wav, sink_col, lc)
    return y_a, y_b, y_c, y_d


def kernel(x, c, ctx, c_ctx, ada_w, ada_b, norm_ffn1, norm_mix, norm_ffn2, ffn1_wi, ffn1_wo, ffn2_wi, ffn2_wo, w_in, hg_lb, hg_norm, rw_shift, rw_w0, rw_w2, rw_a0, rw_a2, rw_kk, rw_ka, rw_rk, rw_ln_w, rw_ln_b, na_qn, na_kn, na_rpb, wa_qn, wa_kn, wa_sink, w_branch, w_out):
    assert x.shape[0] == 1 and ctx.shape[1] == MIX_ROWS
    lc = ctx.shape[1]
    t_len = x.shape[1]
    xa = jnp.concatenate([ctx[0], x[0]], axis=0)
    cc_t = jnp.stack([c[0], c_ctx], axis=1)
    mods = _ada_mods(cc_t, ada_w, ada_b).reshape(DEPTH, 2, N_MOD, D)
    tables = (_block_diag(512, 64), _block_diag(128, 64)) + _rope_tables(lc, t_len)
    for l in range(DEPTH):
        m = mods[l]
        xa = _ffn(xa, m[:, 0:3], norm_ffn1[l][None], ffn1_wi[l].astype(BF16), ffn1_wo[l].astype(BF16), lc)
        p = _win(xa, m[:, 3:6], norm_mix[l][None], _permute_w_in(w_in[l]).astype(BF16), lc)
        yb = _token_mixing(p, lc, l, hg_lb, hg_norm[l], rw_shift[l], rw_w0[l], rw_w2[l], rw_a0[l], rw_a2[l],
                           rw_kk[l], rw_ka[l], rw_rk[l], rw_ln_w[l], rw_ln_b[l], na_qn[l], na_kn[l],
                           na_rpb[l], wa_qn[l], wa_kn[l], wa_sink[l], tables)
        xa = _merge(xa, m[:, 3:6], yb, p, w_branch[l].astype(BF16), w_out[l].astype(BF16), lc)
        xa = _ffn(xa, m[:, 6:9], norm_ffn2[l][None], ffn2_wi[l].astype(BF16), ffn2_wo[l].astype(BF16), lc)
    return xa[lc:][None]
```

```python
import functools

import numpy as np
import jax
import jax.numpy as jnp
from jax import lax
from jax.experimental import pallas as pl
from jax.experimental.pallas import tpu as pltpu

F32 = jnp.float32
BF16 = jnp.bfloat16

D = 2048
DEPTH = 2
GRID_W = 64
EPS = 1e-6
D_FF = 5632
N_MOD = 9
N_BRANCH = 4
BR_W = 512
HG_HEADS, HG_DK = 4, 128
RW_HEADS, RW_HD = 8, 64
RW_LORA = 64
RW_GN_EPS = 64e-5
NA_HEADS, NA_HD = 8, 64
NA_WIN_R, NA_WIN_C = 8, 16
NA_GROUP = 4
NA_KEY_ROWS = NA_WIN_R + NA_GROUP - 1
WA_HEADS, WA_KV_HEADS, WA_HD = 8, 2, 64
WA_WINDOW = 128
ROPE_BASE = 10000.0

OFF_RW, OFF_HG, OFF_NA, OFF_GL, OFF_WA, OFF_LORA = 0, 2048, 4608, 6144, 14336, 15104
P_TOTAL = 15360

MIX_ROWS = 256
HALO = 16
GLA_CHUNK = 16
RW_CHUNK = 64
NEG = -1e30
VMEM_LIMIT = 56 * 1024 * 1024

NT = (((1,), (1,)), ((), ()))
TN = (((0,), (0,)), ((), ()))


def _cparams(sem):
    return pltpu.CompilerParams(dimension_semantics=sem, vmem_limit_bytes=VMEM_LIMIT)


def _mm(a, b, dims=None):
    a = a.astype(BF16)
    b = b.astype(BF16)
    if dims is None:
        return jnp.dot(a, b, preferred_element_type=F32)
    return lax.dot_general(a, b, dims, preferred_element_type=F32)


def _mm2(x, w_bf16):
    hi = x.astype(BF16)
    lo = (x - hi.astype(F32)).astype(BF16)
    return (jnp.dot(hi, w_bf16, preferred_element_type=F32)
            + jnp.dot(lo, w_bf16, preferred_element_type=F32))


def _mm_f32(a, b):
    return jnp.dot(a, b, preferred_element_type=F32, precision=lax.Precision.HIGHEST)


def _sigmoid(x):
    return 1.0 / (1.0 + jnp.exp(-x))


def _silu(x):
    return x * _sigmoid(x)


def _seg_cumsum(x, tri):
    w = x.shape[1]
    hi = x.astype(BF16)
    r1 = x - hi.astype(F32)
    mid = r1.astype(BF16)
    lo = (r1 - mid.astype(F32)).astype(BF16)
    y = jnp.dot(tri, jnp.concatenate([hi, mid, lo], axis=1), preferred_element_type=F32)
    return y[:, :w] + y[:, w:2 * w] + y[:, 2 * w:]


def _seg_tri(rows, seg, rev):
    r = np.arange(rows)
    same = (r[:, None] // seg) == (r[None, :] // seg)
    tri = (r[None, :] >= r[:, None]) if rev else (r[None, :] <= r[:, None])
    return jnp.asarray(same & tri, dtype=BF16)


def _ada_kernel(cc_ref, w_ref, b_ref, o_ref):
    s = _silu(cc_ref[...])
    w = w_ref[...]
    r0 = jnp.sum(w * s[:, 0:1], axis=0, keepdims=True)
    r1 = jnp.sum(w * s[:, 1:2], axis=0, keepdims=True)
    o_ref[...] = jnp.concatenate([r0, r1], axis=0) + b_ref[...]


def _ada_mods(cc_t, ada_w, ada_b):
    tn = 512
    nmod = ada_w.shape[-1]
    return pl.pallas_call(
        _ada_kernel,
        grid=(DEPTH, nmod // tn),
        in_specs=[pl.BlockSpec((D, 2), lambda l, j: (0, 0)),
                  pl.BlockSpec((None, D, tn), lambda l, j: (l, 0, j)),
                  pl.BlockSpec((None, 1, tn), lambda l, j: (l, 0, j))],
        out_specs=pl.BlockSpec((None, 2, tn), lambda l, j: (l, 0, j)),
        out_shape=jax.ShapeDtypeStruct((DEPTH, 2, nmod), F32),
        compiler_params=_cparams(("arbitrary", "arbitrary")),
        name="ada_mod",
    )(cc_t, ada_w, ada_b.reshape(DEPTH, 1, nmod))


def _row_is_ctx(i, tm, lc):
    return (i * tm + lax.broadcasted_iota(jnp.int32, (tm, 1), 0)) < lc


def _mod_pick(mod_ref, j, is_ctx):
    return jnp.where(is_ctx, mod_ref[1, j:j + 1, :], mod_ref[0, j:j + 1, :])


def _modulated(x, g, mod_ref, is_ctx):
    y = x * lax.rsqrt(jnp.mean(x * x, axis=-1, keepdims=True) + EPS) * g
    return y * (1.0 + _mod_pick(mod_ref, 1, is_ctx)) + _mod_pick(mod_ref, 0, is_ctx)


def _ffn_kernel(lc, tm, nf, x_ref, mod_ref, g_ref, wa_ref, wb_ref, wo_ref, o_ref, xn_ref):
    i = pl.program_id(0)
    j = pl.program_id(1)
    is_ctx = _row_is_ctx(i, tm, lc)

    @pl.when(j == 0)
    def _():
        xn_ref[...] = _modulated(x_ref[...], g_ref[...], mod_ref, is_ctx).astype(BF16)
        o_ref[...] = jnp.zeros_like(o_ref)

    xn = xn_ref[...]
    a = jnp.dot(xn, wa_ref[...], preferred_element_type=F32)
    b = jnp.dot(xn, wb_ref[...], preferred_element_type=F32)
    h = (_silu(a) * b).astype(BF16)
    o_ref[...] += jnp.dot(h, wo_ref[...], preferred_element_type=F32)

    @pl.when(j == nf - 1)
    def _():
        o_ref[...] = x_ref[...] + 0.5 * _mod_pick(mod_ref, 2, is_ctx) * o_ref[...]


def _dense_tm(rows):
    for tm in (768, 512, 256):
        if rows % tm == 0:
            return tm
    raise ValueError(rows)


def _ffn(x, mod3, g, wi, wo, lc):
    rows = x.shape[0]
    tm = _dense_tm(rows)
    tf = 512
    nf = D_FF // tf
    return pl.pallas_call(
        functools.partial(_ffn_kernel, lc, tm, nf),
        grid=(rows // tm, nf),
        in_specs=[pl.BlockSpec((tm, D), lambda i, j: (i, 0)),
                  pl.BlockSpec((2, 3, D), lambda i, j: (0, 0, 0)),
                  pl.BlockSpec((1, D), lambda i, j: (0, 0)),
                  pl.BlockSpec((D, tf), lambda i, j: (0, j)),
                  pl.BlockSpec((D, tf), lambda i, j: (0, j + nf)),
                  pl.BlockSpec((tf, D), lambda i, j: (j, 0))],
        out_specs=pl.BlockSpec((tm, D), lambda i, j: (i, 0)),
        out_shape=jax.ShapeDtypeStruct((rows, D), F32),
        scratch_shapes=[pltpu.VMEM((tm, D), BF16)],
        compiler_params=_cparams(("arbitrary", "arbitrary")),
        name="ffn",
    )(x, mod3, g, wi, wi, wo)


def _win_kernel(lc, tm, x_ref, mod_ref, g_ref, w_ref, o_ref, xn_ref):
    i = pl.program_id(0)

    @pl.when(pl.program_id(1) == 0)
    def _():
        is_ctx = _row_is_ctx(i, tm, lc)
        xn_ref[...] = _modulated(x_ref[...], g_ref[...], mod_ref, is_ctx).astype(BF16)

    o_ref[...] = jnp.dot(xn_ref[...], w_ref[...], preferred_element_type=F32).astype(o_ref.dtype)


def _win(x, mod3, g, w, lc):
    rows = x.shape[0]
    tm = _dense_tm(rows)
    tn = 1536
    return pl.pallas_call(
        functools.partial(_win_kernel, lc, tm),
        grid=(rows // tm, P_TOTAL // tn),
        in_specs=[pl.BlockSpec((tm, D), lambda i, j: (i, 0)),
                  pl.BlockSpec((2, 3, D), lambda i, j: (0, 0, 0)),
                  pl.BlockSpec((1, D), lambda i, j: (0, 0)),
                  pl.BlockSpec((D, tn), lambda i, j: (0, j))],
        out_specs=pl.BlockSpec((tm, tn), lambda i, j: (i, j)),
        out_shape=jax.ShapeDtypeStruct((rows, P_TOTAL), BF16),
        scratch_shapes=[pltpu.VMEM((tm, D), BF16)],
        compiler_params=_cparams(("arbitrary", "arbitrary")),
        name="w_in",
    )(x, mod3, g, w)


def _merge_kernel(lc, tm, x_ref, mod_ref, ya_ref, yb_ref, yc_ref, yd_ref, gl_ref, wb_ref, wo_ref, o_ref, acc_ref):
    i = pl.program_id(0)
    n = pl.program_id(1)

    @pl.when(n == 0)
    def _():
        acc_ref[...] = jnp.zeros_like(acc_ref)

    for nn, y_ref in enumerate((ya_ref, yb_ref, yc_ref, yd_ref)):
        @pl.when(n == nn)
        def _():
            proj = jnp.dot(y_ref[...].astype(BF16), wb_ref[...], preferred_element_type=F32)
            acc_ref[...] += _sigmoid(gl_ref[...].astype(F32)) * proj

    @pl.when(n == N_BRANCH - 1)
    def _():
        is_ctx = _row_is_ctx(i, tm, lc)
        y = jnp.dot(acc_ref[...].astype(BF16), wo_ref[...], preferred_element_type=F32)
        o_ref[...] = x_ref[...] + _mod_pick(mod_ref, 2, is_ctx) * y


def _merge(x, mod3, ys, p, wb, wo, lc):
    rows = x.shape[0]
    tm = 384 if rows % 384 == 0 else MIX_ROWS
    glb = OFF_GL // D
    yspec = pl.BlockSpec((tm, BR_W), lambda i, n: (i, 0))
    return pl.pallas_call(
        functools.partial(_merge_kernel, lc, tm),
        grid=(rows // tm, N_BRANCH),
        in_specs=[pl.BlockSpec((tm, D), lambda i, n: (i, 0)),
                  pl.BlockSpec((2, 3, D), lambda i, n: (0, 0, 0)),
                  yspec, yspec, yspec, yspec,
                  pl.BlockSpec((tm, D), lambda i, n: (i, glb + n)),
                  pl.BlockSpec((None, BR_W, D), lambda i, n: (n, 0, 0)),
                  pl.BlockSpec((D, D), lambda i, n: (0, 0))],
        out_specs=pl.BlockSpec((tm, D), lambda i, n: (i, 0)),
        out_shape=jax.ShapeDtypeStruct((rows, D), F32),
        scratch_shapes=[pltpu.VMEM((tm, D), F32)],
        compiler_params=_cparams(("arbitrary", "arbitrary")),
        name="merge",
    )(x, mod3, *ys, p, wb, wo)


def _gla_kernel(rev, *refs):
    if rev:
        q_ref, f_ref, i_ref, lb_ref, tri_ref, of_ref, g_ref, nw_ref, y_ref, st_ref = refs
    else:
        q_ref, f_ref, i_ref, lb_ref, tri_ref, y_ref, st_ref = refs
    C = GLA_CHUNK
    R = MIX_ROWS
    NH, DK = HG_HEADS, HG_DK

    @pl.when(pl.program_id(0) == 0)
    def _():
        st_ref[...] = jnp.zeros_like(st_ref)

    lb = lb_ref[...]
    f_all = lb + (1.0 - lb) * _sigmoid(f_ref[...].astype(F32))
    b_all = _seg_cumsum(jnp.log(f_all), tri_ref[...])
    pos = lax.broadcasted_iota(jnp.int32, (R, 1), 0) & (C - 1)
    chunks = list(range(R // C - 1, -1, -1) if rev else range(R // C))

    heads = []
    for h in range(NH):
        hs = slice(h * DK, (h + 1) * DK)
        b = b_all[:, hs]
        k = 1.0 - f_all[:, hs]
        q = q_ref[:, hs].astype(F32)
        v = i_ref[:, hs].astype(F32)
        o = jnp.sum(q * k, axis=-1, keepdims=True) * v
        for r in range(8):
            sh = ((R - r) if rev else r) % R
            br, kr, vr = (pltpu.roll(t, sh, axis=0) if r else t for t in (b, k, v))
            for d in (r, r + 8):
                if d == 0:
                    continue
                if d >= 8:
                    sh8 = (R - 8) if rev else 8
                    bd_, kd_, vd_ = (pltpu.roll(t, sh8, axis=0) for t in (br, kr, vr))
                else:
                    bd_, kd_, vd_ = br, kr, vr
                valid = (pos < C - d) if rev else (pos >= d)
                e = jnp.exp(jnp.where(valid, b - bd_, NEG))
                att = jnp.sum(q * kd_ * e, axis=-1, keepdims=True)
                o = o + att * vd_
        upd, dec = {}, {}
        for c in chunks:
            sl = slice(c * C, (c + 1) * C)
            b_end = b[sl][0:1] if rev else b[sl][C - 1:C]
            dec[c] = jnp.exp(b_end)
            upd[c] = _mm(v[sl], k[sl] * jnp.exp(b_end - b[sl]), TN)
        heads.append(dict(o=o, qb=q * jnp.exp(b), upd=upd, dec=dec, st=st_ref[h]))

    for c in chunks:
        for w in heads:
            w[c] = w["st"]
            w["st"] = w["st"] * w["dec"][c] + w["upd"][c]
    outs = []
    for h, w in enumerate(heads):
        st_ref[h] = w["st"]
        inter = {c: _mm(w["qb"][c * C:(c + 1) * C], w[c], NT) for c in chunks}
        outs.append(w["o"] + jnp.concatenate([inter[c] for c in range(R // C)], axis=0))

    for h, o in enumerate(outs):
        hs = slice(h * DK, (h + 1) * DK)
        if rev:
            o = of_ref[:, hs] + o
            o = o * lax.rsqrt(jnp.mean(o * o, axis=-1, keepdims=True) + EPS) * nw_ref[:, hs]
            o = o * _silu(g_ref[:, hs].astype(F32))
        y_ref[:, hs] = o


def _blk_order(rev, nb):
    if rev:
        return lambda b: jnp.where(b == 0, 0, nb - b)
    return lambda b: b


def _gla_dir(p, lb, rev, o_fwd=None, norm_w=None):
    rows = p.shape[0]
    nb = rows // MIX_ROWS
    order = _blk_order(rev, nb)
    W = HG_HEADS * HG_DK
    cb = OFF_HG // W

    def col(n):
        return pl.BlockSpec((MIX_ROWS, W), lambda b: (order(b), cb + n))

    vec = pl.BlockSpec((1, W), lambda b: (0, 0))
    in_specs = [col(0), col(2 if rev else 1), col(3), vec,
                pl.BlockSpec((MIX_ROWS, MIX_ROWS), lambda b: (0, 0))]
    args = [p, p, p, lb, _seg_tri(MIX_ROWS, GLA_CHUNK, rev)]
    if rev:
        in_specs += [pl.BlockSpec((MIX_ROWS, W), lambda b: (order(b), 0)), col(4), vec]
        args += [o_fwd, p, norm_w]
    return pl.pallas_call(
        functools.partial(_gla_kernel, rev),
        grid=(nb,),
        in_specs=in_specs,
        out_specs=pl.BlockSpec((MIX_ROWS, W), lambda b: (order(b), 0)),
        out_shape=jax.ShapeDtypeStruct((rows, W), F32),
        scratch_shapes=[pltpu.VMEM((HG_HEADS, HG_DK, HG_DK), F32)],
        compiler_params=_cparams(("arbitrary",)),
        name="gla_rev" if rev else "gla_fwd",
    )(*args)


def _rw_prep_kernel(nb, main_ref, lora_ref, pm_ref, nm_ref, pl_ref, nl_ref, tm_ref, tl_ref,
                    w0_ref, w2_ref, a0_ref, a2_ref, kkw_ref, ka_ref, rk_ref, bd_ref,
                    r_o, kk_o, v_o, gs_o, bonus_o, lw0_o, lw1_o, akk0_o, akk1_o, kd0_o, kd1_o):
    i = pl.program_id(0)
    has_prev = (i >= 2).astype(F32)
    has_next = jnp.logical_and(i != 0, i != nb - 1).astype(F32)
    rowi = lax.broadcasted_iota(jnp.int32, (MIX_ROWS, 1), 0)

    def shift(x, prev_blk, next_blk, taps):
        up = jnp.where(rowi == 0, prev_blk[HALO - 1:HALO, :] * has_prev, pltpu.roll(x, 1, axis=0))
        dn = jnp.where(rowi == MIX_ROWS - 1, next_blk[0:1, :] * has_next,
                       pltpu.roll(x, MIX_ROWS - 1, axis=0))
        return taps[0:1] * up + taps[1:2] * x + taps[2:3] * dn

    f32 = lambda ref: ref[...].astype(F32)
    main = shift(f32(main_ref), f32(pm_ref), f32(nm_ref), tm_ref[...])
    lora = shift(f32(lora_ref), f32(pl_ref), f32(nl_ref), tl_ref[...])
    W = RW_HEADS * RW_HD
    r, k, v, g = (main[:, n * W:(n + 1) * W] for n in range(4))
    bd = bd_ref[...]
    kk = k * kkw_ref[...]
    kk = kk * lax.rsqrt(_mm2(kk * kk, bd) + EPS)
    tl = jnp.tanh(lora)
    kds = []
    for d, (lw_o, akk_o, kd_o) in enumerate(((lw0_o, akk0_o, kd0_o), (lw1_o, akk1_o, kd1_o))):
        z = -(w0_ref[d:d + 1, :] + _mm_f32(tl, w2_ref[d]))
        softplus = jnp.maximum(z, 0.0) + jnp.log(1.0 + jnp.exp(-jnp.abs(z)))
        lw_o[...] = -jnp.exp(-softplus - 0.5)
        a = _sigmoid(a0_ref[d:d + 1, :] + _mm_f32(lora, a2_ref[d]))
        kd = k * (1.0 + (a - 1.0) * ka_ref[...])
        kds.append(kd)
        kd_o[...] = kd
        akk_o[...] = a * kk
    r_o[...] = r
    kk_o[...] = kk
    v_o[...] = v
    gs_o[...] = _sigmoid(g)
    bonus_o[...] = _mm2(r * (kds[0] + kds[1]) * rk_ref[...], bd) * v


def _rw_prep(p, taps_m, taps_l, w0, w2p, a0, a2p, kkw, ka, rk, bd512):
    rows = p.shape[0]
    nb = rows // MIX_ROWS
    W = RW_HEADS * RW_HD
    n8 = rows // HALO
    lb = OFF_LORA // 256
    per = MIX_ROWS // HALO
    full = lambda shape: pl.BlockSpec(shape, lambda i: (0,) * len(shape))
    in_specs = [pl.BlockSpec((MIX_ROWS, 4 * W), lambda i: (i, 0)),
                pl.BlockSpec((MIX_ROWS, 256), lambda i: (i, lb)),
                pl.BlockSpec((HALO, 4 * W), lambda i: (jnp.maximum(i * per - 1, 0), 0)),
                pl.BlockSpec((HALO, 4 * W), lambda i: (jnp.minimum((i + 1) * per, n8 - 1), 0)),
                pl.BlockSpec((HALO, 256), lambda i: (jnp.maximum(i * per - 1, 0), lb)),
                pl.BlockSpec((HALO, 256), lambda i: (jnp.minimum((i + 1) * per, n8 - 1), lb)),
                full((3, 4 * W)), full((3, 256)),
                full((2, W)), full((2, 256, W)), full((2, W)), full((2, 256, W)),
                full((1, W)), full((1, W)), full((1, W)), full((W, W))]
    out = jax.ShapeDtypeStruct((rows, W), F32)
    return pl.pallas_call(
        functools.partial(_rw_prep_kernel, nb),
        grid=(nb,),
        in_specs=in_specs,
        out_specs=[pl.BlockSpec((MIX_ROWS, W), lambda i: (i, 0))] * 11,
        out_shape=[out] * 11,
        compiler_params=_cparams(("arbitrary",)),
        name="rw_prep",
    )(p, p, p, p, p, p, taps_m, taps_l, w0, w2p, a0, a2p, kkw, ka, rk, bd512)


def _rw_scan_kernel(rev, *refs):
    if rev:
        (r_ref, kk_ref, v_ref, lw_ref, akk_ref, kd_ref, tri_ref, of_ref, bonus_ref, gs_ref,
         lnw_ref, lnb_ref, bd_ref, y_ref, s_ref) = refs
    else:
        r_ref, kk_ref, v_ref, lw_ref, akk_ref, kd_ref, tri_ref, y_ref, s_ref = refs
    C = RW_CHUNK
    HD = RW_HD
    P2 = 2 * HD

    NP = RW_HEADS // 2

    @pl.when(pl.program_id(0) == 0)
    def _():
        s_ref[...] = jnp.zeros_like(s_ref)

    lw = lw_ref[...]
    cum = _seg_cumsum(lw, tri_ref[...])
    cum_prev = cum - lw

    lane = lax.broadcasted_iota(jnp.int32, (1, P2), 1)
    h0 = lane < HD
    ri = lax.broadcasted_iota(jnp.int32, (P2, P2), 0)
    ci = lax.broadcasted_iota(jnp.int32, (P2, P2), 1)
    same = (ri // C) == (ci // C)
    rt, cs = ri % C, ci % C
    strict = jnp.logical_and(same, (rt < cs) if rev else (rt > cs))
    incl = jnp.logical_and(same, (rt <= cs) if rev else (rt >= cs))
    eye = (ri == ci).astype(F32)
    bdiag = ((ri // HD) == (ci // HD)).astype(F32)

    def stack(x):
        return jnp.concatenate([x, x], axis=0)

    def split_heads(x):
        return jnp.concatenate([jnp.where(h0, x, 0.0), jnp.where(h0, 0.0, x)], axis=0)

    def unstack(x):
        return jnp.where(h0, x[:C], x[C:])

    chunks = list(range(MIX_ROWS // C - 1, -1, -1) if rev else range(MIX_ROWS // C))
    order = [(c, p) for c in chunks for p in range(NP)]

    pre = {}
    for c, p in order:
        sl = slice(c * C, (c + 1) * C)
        pp = slice(p * P2, (p + 1) * P2)
        cum_c = cum[sl, pp]
        cend = cum_c[0:1] if rev else cum_c[C - 1:C]
        e_neg = jnp.exp(-cum_c)
        e_end = jnp.exp(cend - cum_c)
        a_t = kk_ref[sl, pp] * jnp.exp(cum_prev[sl, pp])
        r_t = r_ref[sl, pp] * jnp.exp(cum_c)
        akk = akk_ref[sl, pp]
        kd = kd_ref[sl, pp]
        v = v_ref[sl, pp]
        pre[c, p] = dict(a_t=a_t, r_t=r_t, v=v, kh=kd * e_end, bh=akk * e_end, dec=jnp.exp(cend),
                         lhs=jnp.concatenate([split_heads(a_t), split_heads(r_t)], axis=0),
                         rhs=jnp.concatenate([stack(akk * e_neg), stack(kd * e_neg)], axis=0))
    for w in pre.values():
        g = _mm(w["lhs"], w["rhs"], NT)
        w["m"] = jnp.where(strict, g[:P2, :P2], 0.0)
        w["a_ak"] = jnp.where(strict, g[:P2, P2:], 0.0)
        w["a_rb"] = jnp.where(incl, g[P2:, :P2], 0.0)
        w["a_rk"] = jnp.where(incl, g[P2:, P2:], 0.0)
        w["p"] = eye - w["m"]
    for w in pre.values():
        av = _mm(jnp.concatenate([w["a_ak"], w["a_rk"]], axis=0), stack(w["v"]))
        w["av"], w["rkv"] = av[:P2], av[P2:]
    for w in pre.values():
        w["m"] = _mm(w["m"], w["m"])
    for _ in range(4):
        for w in pre.values():
            pm = _mm(jnp.concatenate([w["p"], w["m"]], axis=0), w["m"])
            w["p"] = w["p"] + pm[:P2]
            w["m"] = pm[P2:]
    for w in pre.values():
        w["p"] = w["p"] + _mm(w["p"], w["m"])
    for w in pre.values():
        w["tt"] = _mm(w["p"], jnp.concatenate([stack(w["a_t"]), w["av"]], axis=1))
        w["z"] = _mm(w["p"], stack(w["bh"]), TN)
    for w in pre.values():
        ar = _mm(w["a_rb"], w["tt"])
        w["oa"] = stack(w["r_t"]) - ar[:, :P2]
        w["oc"] = w["rkv"] - ar[:, P2:]
        w["pz"] = bdiag * _mm(split_heads(w["a_t"]), w["z"], TN)
        w["kz"] = stack(w["kh"]) - _mm(w["a_ak"], w["z"], TN)
    for w in pre.values():
        w["q"] = bdiag * _mm(split_heads(w["v"]), w["kz"], TN)

    s = [s_ref[p] for p in range(NP)]
    for c, p in order:
        w = pre[c, p]
        w["s0"] = s[p]
        s[p] = s[p] * w["dec"] - _mm(s[p], w["pz"]) + w["q"]
    for c, p in order:
        w = pre[c, p]
        o_st = _mm(w["oa"], w["s0"], NT) + w["oc"]
        y_ref[c * C:(c + 1) * C, p * P2:(p + 1) * P2] = unstack(o_st)
    for p in range(NP):
        s_ref[p] = s[p]

    if rev:
        o = of_ref[...] + y_ref[...]
        bd = bd_ref[...]
        mu = _mm2(o, bd) * (1.0 / HD)
        oc = o - mu
        var = _mm2(oc * oc, bd) * (1.0 / HD)
        o = oc * lax.rsqrt(var + RW_GN_EPS) * lnw_ref[...] + lnb_ref[...]
        y_ref[...] = (o + bonus_ref[...]) * gs_ref[...]


def _rw_scan(rev, r, kk, v, lw, akk, kd, extra=()):
    rows = r.shape[0]
    nb = rows // MIX_ROWS
    order = _blk_order(rev, nb)
    P2 = 2 * RW_HD
    W = RW_HEADS * RW_HD
    blk = pl.BlockSpec((MIX_ROWS, W), lambda b: (order(b), 0))
    in_specs = [blk] * 6 + [pl.BlockSpec((MIX_ROWS, MIX_ROWS), lambda b: (0, 0))]
    args = [r, kk, v, lw, akk, kd, _seg_tri(MIX_ROWS, RW_CHUNK, rev)]
    if rev:
        o_fwd, bonus, gs, lnw, lnb, bd512 = extra
        vec = pl.BlockSpec((1, W), lambda b: (0, 0))
        in_specs += [blk, blk, blk, vec, vec, pl.BlockSpec((W, W), lambda b: (0, 0))]
        args += [o_fwd, bonus, gs, lnw, lnb, bd512]
    return pl.pallas_call(
        functools.partial(_rw_scan_kernel, rev),
        grid=(nb,),
        in_specs=in_specs,
        out_specs=blk,
        out_shape=jax.ShapeDtypeStruct((rows, W), F32),
        scratch_shapes=[pltpu.VMEM((RW_HEADS // 2, P2, P2), F32)],
        compiler_params=_cparams(("arbitrary",)),
        name="rw_rev" if rev else "rw_fwd",
    )(*args)


def _qk_prep_kernel(nq_ref, nk_ref, nv_ref, wq_ref, wkv_ref, nqn_ref, nkn_ref, wqn_ref, wkn_ref,
                    cos_ref, sin_ref, bd512_ref, bd128_ref,
                    naq_o, nak_o, nav_o, waq_o, wak_o, wav_o):
    bd512 = bd512_ref[...]
    bd128 = bd128_ref[...]

    def hnorm(x, g, bd, hd):
        return x * lax.rsqrt(_mm2(x * x, bd) * (1.0 / hd) + EPS) * g

    def rope(x, cos, sin):
        lane = lax.broadcasted_iota(jnp.int32, x.shape, 1)
        w = x.shape[1]
        partner = jnp.where((lane & 31) < 16, pltpu.roll(x, w - 16, axis=1), pltpu.roll(x, 16, axis=1))
        return x * cos + partner * sin

    def dup_heads(o_ref, x):
        lower = lax.broadcasted_iota(jnp.int32, x.shape, 1) < WA_HD
        swapped = pltpu.roll(x, WA_HD, axis=1)
        o_ref[0] = jnp.where(lower, x, swapped).astype(o_ref.dtype)
        o_ref[1] = jnp.where(lower, swapped, x).astype(o_ref.dtype)

    f32 = lambda ref: ref[...].astype(F32)
    naq_o[...] = (hnorm(f32(nq_ref), nqn_ref[...], bd512, NA_HD) * NA_HD ** -0.5).astype(BF16)
    nak_o[...] = hnorm(f32(nk_ref), nkn_ref[...], bd512, NA_HD).astype(BF16)
    nav_o[...] = nv_ref[...]
    cos = cos_ref[...]
    sin = sin_ref[...]
    wq = hnorm(f32(wq_ref), wqn_ref[...], bd512, WA_HD) * WA_HD ** -0.5
    wq = rope(wq, jnp.concatenate([cos] * 4, axis=1), jnp.concatenate([sin] * 4, axis=1))
    waq_o[...] = wq.astype(BF16)
    kv = f32(wkv_ref)
    dup_heads(wak_o, rope(hnorm(kv[:, :128], wkn_ref[...], bd128, WA_HD), cos, sin))
    dup_heads(wav_o, kv[:, 128:])


def _qk_prep(p, nqn, nkn, wqn, wkn, cos, sin, bd512, bd128):
    rows = p.shape[0]
    nb = rows // MIX_ROWS
    nab = OFF_NA // 512
    full = lambda shape: pl.BlockSpec(shape, lambda i: (0,) * len(shape))
    in_specs = [pl.BlockSpec((MIX_ROWS, 512), lambda i: (i, nab)),
                pl.BlockSpec((MIX_ROWS, 512), lambda i: (i, nab + 1)),
                pl.BlockSpec((MIX_ROWS, 512), lambda i: (i, nab + 2)),
                pl.BlockSpec((MIX_ROWS, 512), lambda i: (i, OFF_WA // 512)),
                pl.BlockSpec((MIX_ROWS, 256), lambda i: (i, (OFF_WA + 512) // 256)),
                full((1, 512)), full((1, 512)), full((1, 512)), full((1, 128)),
                pl.BlockSpec((MIX_ROWS, 128), lambda i: (i, 0)),
                pl.BlockSpec((MIX_ROWS, 128), lambda i: (i, 0)),
                full((512, 512)), full((128, 128))]
    flat = pl.BlockSpec((MIX_ROWS, 512), lambda i: (i, 0))
    flat_shape = jax.ShapeDtypeStruct((rows, 512), BF16)
    dup = pl.BlockSpec((WA_KV_HEADS, MIX_ROWS, 128), lambda i: (0, i, 0))
    dup_shape = jax.ShapeDtypeStruct((WA_KV_HEADS, rows, 128), BF16)
    return pl.pallas_call(
        _qk_prep_kernel,
        grid=(nb,),
        in_specs=in_specs,
        out_specs=[flat, flat, flat, flat, dup, dup],
        out_shape=[flat_shape, flat_shape, flat_shape, flat_shape, dup_shape, dup_shape],
        compiler_params=_cparams(("arbitrary",)),
        name="qk_prep",
    )(p, p, p, p, p, nqn, nkn, wqn, wkn, cos, sin, bd512, bd128)


def _na_bias_layout():
    wr, G, KR = NA_WIN_R, NA_GROUP, NA_KEY_ROWS
    off = (0, -(wr // 2), -(KR - G))
    out = {}
    for kind in range(3):
        for r in range(G):
            wstart = (0, r, KR - wr)[kind]
            for a in range(KR):
                inside = wstart <= a < wstart + wr
                out[kind, r, a] = (off[kind] + a - r + wr - 1) if inside else None
    return out


def _na_kernel(lc, nrows, q_ref, k_ref, v_ref, t_ref, o_ref, bias_s):
    W = GRID_W
    HD = NA_HD
    nq = NA_GROUP * W
    nk = NA_KEY_ROWS * W
    ngroups = nrows // NA_GROUP
    kc = k_ref[0:lc, :]
    vc = v_ref[0:lc, :]
    half = [lax.broadcasted_iota(jnp.int32, (1, 2 * HD), 1) // HD == hh for hh in range(2)]

    bias_s[:, :, :, 0:lc] = jnp.zeros((2, 3, nq, lc), F32)
    for (kind, r, a), ro in _na_bias_layout().items():
        c0 = lc + a * W
        for hh in range(2):
            tile = jnp.full((W, W), NEG, F32) if ro is None else t_ref[hh, ro, :, (a % 2) * W:(a % 2 + 1) * W]
            bias_s[hh, kind, r * W:(r + 1) * W, c0:c0 + W] = tile

    def attend(q, keys, vals, bias):
        out = None
        for hh in range(2):
            s = _mm(jnp.where(half[hh], q, 0), keys, NT)
            if bias is not None:
                s = s + bias(hh)
            p = jnp.exp(s - jnp.max(s, axis=-1, keepdims=True))
            o = _mm(p, jnp.where(half[hh], vals, 0)) / jnp.sum(p, axis=-1, keepdims=True)
            out = o if out is None else out + o
        return out

    o_ref[0:lc, :] = attend(q_ref[0:lc, :], kc, vc, None)

    def group(gi, carry):
        base = jnp.clip(gi * NA_GROUP - NA_WIN_R // 2, 0, nrows - NA_KEY_ROWS)
        kind = jnp.where(gi == 0, 0, jnp.where(gi == ngroups - 1, 2, 1))
        q0 = pl.multiple_of(lc + gi * nq, nq)
        k0 = pl.multiple_of(lc + base * W, W)
        keys = jnp.concatenate([kc, k_ref[pl.ds(k0, nk), :]], axis=0)
        vals = jnp.concatenate([vc, v_ref[pl.ds(k0, nk), :]], axis=0)
        o_ref[pl.ds(q0, nq), :] = attend(q_ref[pl.ds(q0, nq), :], keys, vals, lambda hh: bias_s[hh, kind])
        return carry

    lax.fori_loop(0, ngroups, group, 0, unroll=2)


def _na(q, k, v, t2, lc):
    rows = q.shape[0]
    nrows = (rows - lc) // GRID_W
    pblk = pl.BlockSpec((rows, 2 * NA_HD), lambda p: (0, p))
    return pl.pallas_call(
        functools.partial(_na_kernel, lc, nrows),
        grid=(NA_HEADS // 2,),
        in_specs=[pblk, pblk, pblk, pl.BlockSpec((2,) + t2.shape[1:], lambda p: (p, 0, 0, 0))],
        out_specs=pblk,
        out_shape=jax.ShapeDtypeStruct((rows, NA_HEADS * NA_HD), F32),
        scratch_shapes=[pltpu.VMEM((2, 3, NA_GROUP * GRID_W, lc + NA_KEY_ROWS * GRID_W), F32)],
        compiler_params=_cparams(("arbitrary",)),
        name="na_attn",
    )(q, k, v, t2)


def _wa_kernel(lc, t_len, q_ref, k_ref, v_ref, sink_ref, o_ref):
    G = WA_HEADS // WA_KV_HEADS
    B = WA_WINDOW
    nband = 3 * B
    HD = WA_HD
    b = pl.program_id(1)
    kc = k_ref[0:lc, :]
    vc = v_ref[0:lc, :]
    sink = sink_ref[...]
    lower = lax.broadcasted_iota(jnp.int32, (1, 2 * HD), 1) < HD

    def stacked_queries(r0, n):
        parts = []
        for g in range(G):
            qp = q_ref[r0:r0 + n, (g // 2) * 2 * HD:(g // 2 + 1) * 2 * HD]
            parts.append(jnp.where(lower if g % 2 == 0 else jnp.logical_not(lower), qp, 0))
        return jnp.concatenate(parts, axis=0)

    def store(r0, n, o):
        for gp in range(G // 2):
            o_ref[r0:r0 + n, gp * 2 * HD:(gp + 1) * 2 * HD] = jnp.where(
                lower, o[2 * gp * n:(2 * gp + 1) * n], o[(2 * gp + 1) * n:(2 * gp + 2) * n])

    @pl.when(b == 0)
    def _():
        sk = jnp.concatenate([jnp.broadcast_to(sink[g * B:g * B + 1, :], (lc, 1)) for g in range(G)], axis=0)
        s = _mm(stacked_queries(0, lc), kc, NT)
        m = jnp.maximum(jnp.max(s, axis=-1, keepdims=True), sk)
        e = jnp.exp(s - m)
        store(0, lc, _mm(e, vc) / (jnp.sum(e, axis=-1, keepdims=True) + jnp.exp(sk - m)))

    @pl.when(b > 0)
    def _():
        qoff = lax.broadcasted_iota(jnp.int32, (G * B, lc + nband), 0) & (B - 1)
        koff = lax.broadcasted_iota(jnp.int32, (G * B, lc + nband), 1) - lc
        for j in range(MIX_ROWS // B):
            n = (b - 1) * (MIX_ROWS // B) + j
            start = jnp.clip((n - 1) * B, 0, t_len - nband)
            k0 = pl.multiple_of(lc + start, B)
            keys = jnp.concatenate([kc, k_ref[pl.ds(k0, nband), :]], axis=0)
            vals = jnp.concatenate([vc, v_ref[pl.ds(k0, nband), :]], axis=0)
            valid = jnp.logical_or(koff < 0, jnp.abs((n * B + qoff) - (start + koff)) <= WA_WINDOW)
            s = jnp.where(valid, _mm(stacked_queries(j * B, B), keys, NT), NEG)
            m = jnp.maximum(jnp.max(s, axis=-1, keepdims=True), sink)
            p = jnp.exp(s - m)
            store(j * B, B, _mm(p, vals) / (jnp.sum(p, axis=-1, keepdims=True) + jnp.exp(sink - m)))


def _wa(q, k, v, sink_col, lc):
    rows = q.shape[0]
    G = WA_HEADS // WA_KV_HEADS
    kvblk = pl.BlockSpec((None, rows, 2 * WA_HD), lambda h, b: (h, 0, 0))
    qblk = pl.BlockSpec((MIX_ROWS, G * WA_HD), lambda h, b: (b, h))
    return pl.pallas_call(
        functools.partial(_wa_kernel, lc, rows - lc),
        grid=(WA_KV_HEADS, rows // MIX_ROWS),
        in_specs=[qblk, kvblk, kvblk,
                  pl.BlockSpec((None, G * WA_WINDOW, 1), lambda h, b: (h, 0, 0))],
        out_specs=qblk,
        out_shape=jax.ShapeDtypeStruct((rows, WA_HEADS * WA_HD), F32),
        compiler_params=_cparams(("arbitrary", "arbitrary")),
        name="wa_attn",
    )(q, k, v, sink_col)


def _block_diag(n, blk):
    idx = np.arange(n) // blk
    return jnp.asarray(idx[:, None] == idx[None, :], dtype=BF16)


def _na_bias_table(rpb):
    W, wc = GRID_W, NA_WIN_C
    j = np.arange(W)
    cstart = np.clip(j - wc // 2, 0, W - wc)
    cabs = np.arange(W)
    inwin = (cabs[None, :] >= cstart[:, None]) & (cabs[None, :] < cstart[:, None] + wc)
    cb = cabs[None, :] - j[:, None] + wc - 1
    nh, nro, nco = rpb.shape
    onehot = (cb[None] == np.arange(nco)[:, None, None]) & inwin[None]
    t = jnp.dot(rpb.reshape(nh * nro, nco), jnp.asarray(onehot.reshape(nco, W * W), F32),
                precision=lax.Precision.HIGHEST).reshape(nh, nro, W, W)
    t = t + jnp.asarray(np.where(inwin, 0.0, NEG), F32)
    return jnp.concatenate([t, t], axis=-1)


def _rope_tables(lc, t_len):
    pos = np.arange(t_len)
    n = WA_HD // 4
    inv = ROPE_BASE ** (-np.arange(n, dtype=np.float64) / n)
    ang_r = (pos // GRID_W)[:, None] * inv[None, :]
    ang_c = (pos % GRID_W)[:, None] * inv[None, :]
    cos = np.concatenate([np.cos(ang_r)] * 2 + [np.cos(ang_c)] * 2, axis=1)
    sin = np.concatenate([-np.sin(ang_r), np.sin(ang_r), -np.sin(ang_c), np.sin(ang_c)], axis=1)
    cos = np.concatenate([np.ones((lc, WA_HD)), cos], axis=0)
    sin = np.concatenate([np.zeros((lc, WA_HD)), sin], axis=0)
    return (jnp.asarray(np.tile(cos, (1, 2)), dtype=F32), jnp.asarray(np.tile(sin, (1, 2)), dtype=F32))


def _cast_kernel(x_ref, o_ref):
    o_ref[...] = x_ref[...].astype(o_ref.dtype)


def _cast_bf16(w, layer, bk, bn, src_col_block=None):
    _, k, n = w.shape
    col = src_col_block or (lambda j: j)
    return pl.pallas_call(
        _cast_kernel,
        grid=(k // bk, n // bn),
        in_specs=[pl.BlockSpec((None, bk, bn), lambda i, j: (layer, i, col(j)))],
        out_specs=pl.BlockSpec((bk, bn), lambda i, j: (i, j)),
        out_shape=jax.ShapeDtypeStruct((k, n), BF16),
        compiler_params=_cparams(("arbitrary", "arbitrary")),
        name="cast_bf16",
    )(w)


W_IN_BLK = 256
_W_IN_SRC = np.concatenate([np.arange(10, 18), np.arange(0, 10), np.arange(19, 25), np.arange(28, 60),
                            np.arange(25, 28), np.arange(18, 19)])


def _w_in_src_block(j):
    starts = (0, 8, 18, 24, 56, 59)
    src = j + int(_W_IN_SRC[0])
    for s in starts[1:]:
        src = jnp.where(j >= s, j + int(_W_IN_SRC[s]) - s, src)
    return src


def _lora_pad(w, row0):
    out = jnp.zeros((2, 256, w.shape[-1]), F32)
    for d in range(2):
        out = out.at[d, row0 + 64 * d:row0 + 64 * (d + 1)].set(w[d])
    return out


def _token_mixing(p, lc, layer, hg_lb, hg_norm, rw_shift, rw_w0, rw_w2, rw_a0, rw_a2, rw_kk, rw_ka, rw_rk,
                  rw_ln_w, rw_ln_b, na_qn, na_kn, na_rpb, wa_qn, wa_kn, wa_sink, tables):
    rows = p.shape[0]
    bd512, bd128, cos, sin = tables
    cum = jnp.cumsum(jax.nn.softmax(hg_lb.astype(F32), axis=1), axis=1)
    lbs = cum[:, layer] - cum[:, 0]
    o_f = _gla_dir(p, lbs[0:1], False)
    y_a = _gla_dir(p, lbs[1:2], True, o_f, hg_norm[None])
    prep = _rw_prep(p, rw_shift[:, :2048], rw_shift[:, 2048:], rw_w0, _lora_pad(rw_w2, 0), rw_a0,
                    _lora_pad(rw_a2, 128), rw_kk[None], rw_ka[None], rw_rk.reshape(1, -1), bd512)
    r, kk, v, gs, bonus, lw0, lw1, akk0, akk1, kd0, kd1 = prep
    o_f = _rw_scan(False, r, kk, v, lw0, akk0, kd0)
    y_b = _rw_scan(True, r, kk, v, lw1, akk1, kd1,
                   (o_f, bonus, gs, rw_ln_w[None], rw_ln_b[None], bd512))
    tile = lambda g, n: jnp.tile(g, n)[None]
    naq, nak, nav, waq, wak, wav = _qk_prep(p, tile(na_qn, 8), tile(na_kn, 8), tile(wa_qn, 8),
                                            tile(wa_kn, 2), cos, sin, bd512, bd128)
    y_c = _na(naq, nak, nav, _na_bias_table(na_rpb), lc)
    G = WA_HEADS // WA_KV_HEADS
    sink_col = jnp.repeat(wa_sink.reshape(WA_KV_HEADS, G), WA_WINDOW, axis=1)[..., None]
    y_d = _wa(waq, wak, wav, sink_col, lc)
    return y_a, y_b, y_c, y_d


def kernel(x, c, ctx, c_ctx, ada_w, ada_b, norm_ffn1, norm_mix, norm_ffn2, ffn1_wi, ffn1_wo, ffn2_wi, ffn2_wo, w_in, hg_lb, hg_norm, rw_shift, rw_w0, rw_w2, rw_a0, rw_a2, rw_kk, rw_ka, rw_rk, rw_ln_w, rw_ln_b, na_qn, na_kn, na_rpb, wa_qn, wa_kn, wa_sink, w_branch, w_out):
    assert x.shape[0] == 1 and ctx.shape[1] == MIX_ROWS
    lc = ctx.shape[1]
    t_len = x.shape[1]
    xa = jnp.concatenate([ctx[0], x[0]], axis=0)
    cc_t = jnp.stack([c[0], c_ctx], axis=1)
    mods = _ada_mods(cc_t, ada_w, ada_b).reshape(DEPTH, 2, N_MOD, D)
    tables = (_block_diag(512, 64), _block_diag(128, 64)) + _rope_tables(lc, t_len)
    for l in range(DEPTH):
        m = mods[l]
        xa = _ffn(xa, m[:, 0:3], norm_ffn1[l][None], _cast_bf16(ffn1_wi, l, D, 1024),
                  _cast_bf16(ffn1_wo, l, 512, D), lc)
        p = _win(xa, m[:, 3:6], norm_mix[l][None], _cast_bf16(w_in, l, D, W_IN_BLK, _w_in_src_block), lc)
        yb = _token_mixing(p, lc, l, hg_lb, hg_norm[l], rw_shift[l], rw_w0[l], rw_w2[l], rw_a0[l], rw_a2[l],
                           rw_kk[l], rw_ka[l], rw_rk[l], rw_ln_w[l], rw_ln_b[l], na_qn[l], na_kn[l],
                           na_rpb[l], wa_qn[l], wa_kn[l], wa_sink[l], tables)
        wb = _cast_bf16(w_branch.reshape(DEPTH, N_BRANCH * BR_W, D), l, 1024, D).reshape(N_BRANCH, BR_W, D)
        xa = _merge(xa, m[:, 3:6], yb, p, wb, _cast_bf16(w_out, l, 1024, D), lc)
        xa = _ffn(xa, m[:, 6:9], norm_ffn2[l][None], _cast_bf16(ffn2_wi, l, D, 1024),
                  _cast_bf16(ffn2_wo, l, 512, D), lc)
    return xa[lc:][None]
```

```python
import functools

import numpy as np
import jax
import jax.numpy as jnp
from jax import lax
from jax.experimental import pallas as pl
from jax.experimental.pallas import tpu as pltpu

F32 = jnp.float32
BF16 = jnp.bfloat16

D = 2048
DEPTH = 2
GRID_W = 64
EPS = 1e-6
D_FF = 5632
N_MOD = 9
N_BRANCH = 4
BR_W = 512
HG_HEADS, HG_DK = 4, 128
RW_HEADS, RW_HD = 8, 64
RW_LORA = 64
RW_GN_EPS = 64e-5
NA_HEADS, NA_HD = 8, 64
NA_WIN_R, NA_WIN_C = 8, 16
NA_GROUP = 4
NA_KEY_ROWS = NA_WIN_R + NA_GROUP - 1
WA_HEADS, WA_KV_HEADS, WA_HD = 8, 2, 64
WA_WINDOW = 128
ROPE_BASE = 10000.0

OFF_RW, OFF_HG, OFF_NA, OFF_GL, OFF_WA, OFF_LORA = 0, 2048, 4608, 6144, 14336, 15104
P_TOTAL = 15360

MIX_ROWS = 256
HALO = 16
GLA_CHUNK = 16
RW_CHUNK = 64
NEG = -1e30
LOG2E = 1.4426950408889634
VMEM_LIMIT = 56 * 1024 * 1024

NT = (((1,), (1,)), ((), ()))
TN = (((0,), (0,)), ((), ()))


def _cparams(sem):
    return pltpu.CompilerParams(dimension_semantics=sem, vmem_limit_bytes=VMEM_LIMIT)


def _mm(a, b, dims=None):
    a = a.astype(BF16)
    b = b.astype(BF16)
    if dims is None:
        return jnp.dot(a, b, preferred_element_type=F32)
    return lax.dot_general(a, b, dims, preferred_element_type=F32)


def _mm2(x, w_bf16):
    hi = x.astype(BF16)
    lo = (x - hi.astype(F32)).astype(BF16)
    return (jnp.dot(hi, w_bf16, preferred_element_type=F32)
            + jnp.dot(lo, w_bf16, preferred_element_type=F32))


def _mm_f32(a, b):
    return jnp.dot(a, b, preferred_element_type=F32, precision=lax.Precision.HIGHEST)


def _sigmoid(x):
    return 1.0 / (1.0 + jnp.exp(-x))


def _silu(x):
    return x * _sigmoid(x)


def _seg_cumsum(x, tri):
    w = x.shape[1]
    hi = x.astype(BF16)
    r1 = x - hi.astype(F32)
    mid = r1.astype(BF16)
    lo = (r1 - mid.astype(F32)).astype(BF16)
    y = jnp.dot(tri, jnp.concatenate([hi, mid, lo], axis=1), preferred_element_type=F32)
    return y[:, :w] + y[:, w:2 * w] + y[:, 2 * w:]


def _seg_tri(rows, seg, rev):
    r = np.arange(rows)
    same = (r[:, None] // seg) == (r[None, :] // seg)
    tri = (r[None, :] >= r[:, None]) if rev else (r[None, :] <= r[:, None])
    return jnp.asarray(same & tri, dtype=BF16)


def _ada_kernel(cc_ref, w_ref, b_ref, o_ref):
    s = _silu(cc_ref[...])
    w = w_ref[...]
    r0 = jnp.sum(w * s[:, 0:1], axis=0, keepdims=True)
    r1 = jnp.sum(w * s[:, 1:2], axis=0, keepdims=True)
    o_ref[...] = jnp.concatenate([r0, r1], axis=0) + b_ref[...]


def _ada_mods(cc_t, ada_w, ada_b):
    tn = 512
    nmod = ada_w.shape[-1]
    return pl.pallas_call(
        _ada_kernel,
        grid=(DEPTH, nmod // tn),
        in_specs=[pl.BlockSpec((D, 2), lambda l, j: (0, 0)),
                  pl.BlockSpec((None, D, tn), lambda l, j: (l, 0, j)),
                  pl.BlockSpec((None, 1, tn), lambda l, j: (l, 0, j))],
        out_specs=pl.BlockSpec((None, 2, tn), lambda l, j: (l, 0, j)),
        out_shape=jax.ShapeDtypeStruct((DEPTH, 2, nmod), F32),
        compiler_params=_cparams(("arbitrary", "arbitrary")),
        name="ada_mod",
    )(cc_t, ada_w, ada_b.reshape(DEPTH, 1, nmod))


def _row_is_ctx(i, tm, lc):
    return (i * tm + lax.broadcasted_iota(jnp.int32, (tm, 1), 0)) < lc


def _mod_pick(mod_ref, j, is_ctx):
    return jnp.where(is_ctx, mod_ref[1, j:j + 1, :], mod_ref[0, j:j + 1, :])


def _modulated(x, g, mod_ref, is_ctx):
    y = x * lax.rsqrt(jnp.mean(x * x, axis=-1, keepdims=True) + EPS) * g
    return y * (1.0 + _mod_pick(mod_ref, 1, is_ctx)) + _mod_pick(mod_ref, 0, is_ctx)


def _ffn_kernel(lc, tm, nf, x_ref, mod_ref, g_ref, wa_ref, wb_ref, wo_ref, o_ref, xn_ref):
    i = pl.program_id(0)
    j = pl.program_id(1)
    is_ctx = _row_is_ctx(i, tm, lc)

    @pl.when(j == 0)
    def _():
        xn_ref[...] = _modulated(x_ref[...], g_ref[...], mod_ref, is_ctx).astype(BF16)
        o_ref[...] = jnp.zeros_like(o_ref)

    xn = xn_ref[...]
    a = jnp.dot(xn, wa_ref[...], preferred_element_type=F32)
    b = jnp.dot(xn, wb_ref[...], preferred_element_type=F32)
    h = (_silu(a) * b).astype(BF16)
    o_ref[...] += jnp.dot(h, wo_ref[...], preferred_element_type=F32)

    @pl.when(j == nf - 1)
    def _():
        o_ref[...] = x_ref[...] + 0.5 * _mod_pick(mod_ref, 2, is_ctx) * o_ref[...]


def _dense_tm(rows):
    for tm in (768, 512, 256):
        if rows % tm == 0:
            return tm
    raise ValueError(rows)


def _ffn(x, mod3, g, wi, wo, lc):
    rows = x.shape[0]
    tm = _dense_tm(rows)
    tf = 512
    nf = D_FF // tf
    return pl.pallas_call(
        functools.partial(_ffn_kernel, lc, tm, nf),
        grid=(rows // tm, nf),
        in_specs=[pl.BlockSpec((tm, D), lambda i, j: (i, 0)),
                  pl.BlockSpec((2, 3, D), lambda i, j: (0, 0, 0)),
                  pl.BlockSpec((1, D), lambda i, j: (0, 0)),
                  pl.BlockSpec((D, tf), lambda i, j: (0, j)),
                  pl.BlockSpec((D, tf), lambda i, j: (0, j + nf)),
                  pl.BlockSpec((tf, D), lambda i, j: (j, 0))],
        out_specs=pl.BlockSpec((tm, D), lambda i, j: (i, 0)),
        out_shape=jax.ShapeDtypeStruct((rows, D), F32),
        scratch_shapes=[pltpu.VMEM((tm, D), BF16)],
        compiler_params=_cparams(("arbitrary", "arbitrary")),
        name="ffn",
    )(x, mod3, g, wi, wi, wo)


def _win_kernel(lc, tm, x_ref, mod_ref, g_ref, w_ref, o_ref, xn_ref):
    i = pl.program_id(0)

    @pl.when(pl.program_id(1) == 0)
    def _():
        is_ctx = _row_is_ctx(i, tm, lc)
        xn_ref[...] = _modulated(x_ref[...], g_ref[...], mod_ref, is_ctx).astype(BF16)

    o_ref[...] = jnp.dot(xn_ref[...], w_ref[...], preferred_element_type=F32).astype(o_ref.dtype)


def _win(x, mod3, g, w, lc):
    rows = x.shape[0]
    tm = _dense_tm(rows)
    tn = 1536
    return pl.pallas_call(
        functools.partial(_win_kernel, lc, tm),
        grid=(rows // tm, P_TOTAL // tn),
        in_specs=[pl.BlockSpec((tm, D), lambda i, j: (i, 0)),
                  pl.BlockSpec((2, 3, D), lambda i, j: (0, 0, 0)),
                  pl.BlockSpec((1, D), lambda i, j: (0, 0)),
                  pl.BlockSpec((D, tn), lambda i, j: (0, j))],
        out_specs=pl.BlockSpec((tm, tn), lambda i, j: (i, j)),
        out_shape=jax.ShapeDtypeStruct((rows, P_TOTAL), BF16),
        scratch_shapes=[pltpu.VMEM((tm, D), BF16)],
        compiler_params=_cparams(("arbitrary", "arbitrary")),
        name="w_in",
    )(x, mod3, g, w)


def _merge_kernel(lc, tm, x_ref, mod_ref, ya_ref, yb_ref, yc_ref, yd_ref, gl_ref, wb_ref, wo_ref, o_ref, acc_ref):
    i = pl.program_id(0)
    n = pl.program_id(1)

    @pl.when(n == 0)
    def _():
        acc_ref[...] = jnp.zeros_like(acc_ref)

    for nn, y_ref in enumerate((ya_ref, yb_ref, yc_ref, yd_ref)):
        @pl.when(n == nn)
        def _():
            proj = jnp.dot(y_ref[...].astype(BF16), wb_ref[nn], preferred_element_type=F32)
            acc_ref[...] += _sigmoid(gl_ref[...].astype(F32)) * proj

    @pl.when(n == N_BRANCH - 1)
    def _():
        is_ctx = _row_is_ctx(i, tm, lc)
        y = jnp.dot(acc_ref[...].astype(BF16), wo_ref[...], preferred_element_type=F32)
        o_ref[...] = x_ref[...] + _mod_pick(mod_ref, 2, is_ctx) * y


def _merge(x, mod3, ys, p, wb, wo, lc):
    rows = x.shape[0]
    tm = 384 if rows % 384 == 0 else MIX_ROWS
    glb = OFF_GL // D
    yspec = pl.BlockSpec((tm, BR_W), lambda i, n: (i, 0))
    return pl.pallas_call(
        functools.partial(_merge_kernel, lc, tm),
        grid=(rows // tm, N_BRANCH),
        in_specs=[pl.BlockSpec((tm, D), lambda i, n: (i, 0)),
                  pl.BlockSpec((2, 3, D), lambda i, n: (0, 0, 0)),
                  yspec, yspec, yspec, yspec,
                  pl.BlockSpec((tm, D), lambda i, n: (i, glb + n)),
                  pl.BlockSpec((N_BRANCH, BR_W, D), lambda i, n: (0, 0, 0)),
                  pl.BlockSpec((D, D), lambda i, n: (0, 0))],
        out_specs=pl.BlockSpec((tm, D), lambda i, n: (i, 0)),
        out_shape=jax.ShapeDtypeStruct((rows, D), F32),
        scratch_shapes=[pltpu.VMEM((tm, D), F32)],
        compiler_params=_cparams(("arbitrary", "arbitrary")),
        name="merge",
    )(x, mod3, *ys, p, wb, wo)


def _gla_kernel(rev, *refs):
    if rev:
        q_ref, f_ref, i_ref, lb_ref, tri_ref, of_ref, g_ref, nw_ref, y_ref, st_ref = refs
    else:
        q_ref, f_ref, i_ref, lb_ref, tri_ref, y_ref, st_ref = refs
    C = GLA_CHUNK
    R = MIX_ROWS
    NH, DK = HG_HEADS, HG_DK

    @pl.when(pl.program_id(0) == 0)
    def _():
        st_ref[...] = jnp.zeros_like(st_ref)

    lb = lb_ref[...]
    f_all = lb + (1.0 - lb) * _sigmoid(f_ref[...].astype(F32))
    b_all = _seg_cumsum(jnp.log(f_all), tri_ref[...])
    SUB = C // 2
    sub_pos = lax.broadcasted_iota(jnp.int32, (1, SUB, 1), 1)
    chunks = list(range(R // C - 1, -1, -1) if rev else range(R // C))

    heads = []
    for h in range(NH):
        hs = slice(h * DK, (h + 1) * DK)
        b = b_all[:, hs]
        k = 1.0 - f_all[:, hs]
        q = q_ref[:, hs].astype(F32)
        v = i_ref[:, hs].astype(F32)
        def halves(t):
            t3 = t.reshape(R // C, C, DK)
            lo, hi = t3[:, :SUB], t3[:, SUB:]
            return (hi, lo) if rev else (lo, hi)

        def pair_term(qq, bq, keys, ok):
            bk_, kk_, vk_ = keys
            diff = bq - bk_
            e = jnp.exp2(diff if ok is None else jnp.where(ok, diff, NEG))
            return jnp.sum(qq * kk_ * e, axis=-1, keepdims=True) * vk_

        (qn, qf), (bn, bf), (kn, kf), (vn, vf) = halves(q), halves(b * LOG2E), halves(k), halves(v)
        near, far = (bn, kn, vn), (bf, kf, vf)
        o_n = jnp.sum(qn * kn, axis=-1, keepdims=True) * vn
        o_f = jnp.sum(qf * kf, axis=-1, keepdims=True) * vf + pair_term(qf, bf, near, None)
        for d in range(1, SUB):
            sh = (SUB - d) if rev else d
            ok = (sub_pos < SUB - d) if rev else (sub_pos >= d)
            near_r = tuple(pltpu.roll(t, sh, axis=1) for t in near)
            far_r = tuple(jnp.where(ok, pltpu.roll(t, sh, axis=1), tn) for t, tn in zip(far, near_r))
            o_n = o_n + pair_term(qn, bn, near_r, ok)
            o_f = o_f + pair_term(qf, bf, far_r, None) + pair_term(qf, bf, near_r, ok)
        o = jnp.concatenate([o_f, o_n] if rev else [o_n, o_f], axis=1).reshape(R, DK)
        upd, dec = {}, {}
        for c in chunks:
            sl = slice(c * C, (c + 1) * C)
            b_end = b[sl][0:1] if rev else b[sl][C - 1:C]
            dec[c] = jnp.exp(b_end)
            upd[c] = _mm(v[sl], k[sl] * jnp.exp(b_end - b[sl]), TN)
        heads.append(dict(o=o, qb=q * jnp.exp(b), upd=upd, dec=dec, st=st_ref[h]))

    for c in chunks:
        for w in heads:
            w[c] = w["st"]
            w["st"] = w["st"] * w["dec"][c] + w["upd"][c]
    outs = []
    for h, w in enumerate(heads):
        st_ref[h] = w["st"]
        inter = {c: _mm(w["qb"][c * C:(c + 1) * C], w[c], NT) for c in chunks}
        outs.append(w["o"] + jnp.concatenate([inter[c] for c in range(R // C)], axis=0))

    for h, o in enumerate(outs):
        hs = slice(h * DK, (h + 1) * DK)
        if rev:
            o = of_ref[:, hs] + o
            o = o * lax.rsqrt(jnp.mean(o * o, axis=-1, keepdims=True) + EPS) * nw_ref[:, hs]
            o = o * _silu(g_ref[:, hs].astype(F32))
        y_ref[:, hs] = o


def _blk_order(rev, nb):
    if rev:
        return lambda b: jnp.where(b == 0, 0, nb - b)
    return lambda b: b


def _gla_dir(p, lb, rev, o_fwd=None, norm_w=None):
    rows = p.shape[0]
    nb = rows // MIX_ROWS
    order = _blk_order(rev, nb)
    W = HG_HEADS * HG_DK
    cb = OFF_HG // W

    def col(n):
        return pl.BlockSpec((MIX_ROWS, W), lambda b: (order(b), cb + n))

    vec = pl.BlockSpec((1, W), lambda b: (0, 0))
    in_specs = [col(0), col(2 if rev else 1), col(3), vec,
                pl.BlockSpec((MIX_ROWS, MIX_ROWS), lambda b: (0, 0))]
    args = [p, p, p, lb, _seg_tri(MIX_ROWS, GLA_CHUNK, rev)]
    if rev:
        in_specs += [pl.BlockSpec((MIX_ROWS, W), lambda b: (order(b), 0)), col(4), vec]
        args += [o_fwd, p, norm_w]
    return pl.pallas_call(
        functools.partial(_gla_kernel, rev),
        grid=(nb,),
        in_specs=in_specs,
        out_specs=pl.BlockSpec((MIX_ROWS, W), lambda b: (order(b), 0)),
        out_shape=jax.ShapeDtypeStruct((rows, W), F32),
        scratch_shapes=[pltpu.VMEM((HG_HEADS, HG_DK, HG_DK), F32)],
        compiler_params=_cparams(("arbitrary",)),
        name="gla_rev" if rev else "gla_fwd",
    )(*args)


def _rw_prep_kernel(nb, main_ref, lora_ref, pm_ref, nm_ref, pl_ref, nl_ref, tm_ref, tl_ref,
                    w0_ref, w2_ref, a0_ref, a2_ref, kkw_ref, ka_ref, rk_ref, bd_ref,
                    r_o, kk_o, v_o, gs_o, bonus_o, lw0_o, lw1_o, akk0_o, akk1_o, kd0_o, kd1_o):
    i = pl.program_id(0)
    has_prev = (i >= 2).astype(F32)
    has_next = jnp.logical_and(i != 0, i != nb - 1).astype(F32)
    rowi = lax.broadcasted_iota(jnp.int32, (MIX_ROWS, 1), 0)

    def shift(x, prev_blk, next_blk, taps):
        up = jnp.where(rowi == 0, prev_blk[HALO - 1:HALO, :] * has_prev, pltpu.roll(x, 1, axis=0))
        dn = jnp.where(rowi == MIX_ROWS - 1, next_blk[0:1, :] * has_next,
                       pltpu.roll(x, MIX_ROWS - 1, axis=0))
        return taps[0:1] * up + taps[1:2] * x + taps[2:3] * dn

    f32 = lambda ref: ref[...].astype(F32)
    main = shift(f32(main_ref), f32(pm_ref), f32(nm_ref), tm_ref[...])
    lora = shift(f32(lora_ref), f32(pl_ref), f32(nl_ref), tl_ref[...])
    W = RW_HEADS * RW_HD
    r, k, v, g = (main[:, n * W:(n + 1) * W] for n in range(4))
    bd = bd_ref[...]
    kk = k * kkw_ref[...]
    kk = kk * lax.rsqrt(_mm2(kk * kk, bd) + EPS)
    tl = jnp.tanh(lora)
    kds = []
    for d, (lw_o, akk_o, kd_o) in enumerate(((lw0_o, akk0_o, kd0_o), (lw1_o, akk1_o, kd1_o))):
        z = -(w0_ref[d:d + 1, :] + _mm_f32(tl, w2_ref[d]))
        softplus = jnp.maximum(z, 0.0) + jnp.log(1.0 + jnp.exp(-jnp.abs(z)))
        lw_o[...] = -jnp.exp(-softplus - 0.5)
        a = _sigmoid(a0_ref[d:d + 1, :] + _mm_f32(lora, a2_ref[d]))
        kd = k * (1.0 + (a - 1.0) * ka_ref[...])
        kds.append(kd)
        kd_o[...] = kd.astype(kd_o.dtype)
        akk_o[...] = (a * kk).astype(akk_o.dtype)
    r_o[...] = r.astype(r_o.dtype)
    kk_o[...] = kk.astype(kk_o.dtype)
    v_o[...] = v.astype(v_o.dtype)
    gs_o[...] = _sigmoid(g)
    bonus_o[...] = _mm2(r * (kds[0] + kds[1]) * rk_ref[...], bd) * v


def _rw_prep(p, taps_m, taps_l, w0, w2p, a0, a2p, kkw, ka, rk, bd512):
    rows = p.shape[0]
    nb = rows // MIX_ROWS
    W = RW_HEADS * RW_HD
    n8 = rows // HALO
    lb = OFF_LORA // 256
    per = MIX_ROWS // HALO
    full = lambda shape: pl.BlockSpec(shape, lambda i: (0,) * len(shape))
    in_specs = [pl.BlockSpec((MIX_ROWS, 4 * W), lambda i: (i, 0)),
                pl.BlockSpec((MIX_ROWS, 256), lambda i: (i, lb)),
                pl.BlockSpec((HALO, 4 * W), lambda i: (jnp.maximum(i * per - 1, 0), 0)),
                pl.BlockSpec((HALO, 4 * W), lambda i: (jnp.minimum((i + 1) * per, n8 - 1), 0)),
                pl.BlockSpec((HALO, 256), lambda i: (jnp.maximum(i * per - 1, 0), lb)),
                pl.BlockSpec((HALO, 256), lambda i: (jnp.minimum((i + 1) * per, n8 - 1), lb)),
                full((3, 4 * W)), full((3, 256)),
                full((2, W)), full((2, 256, W)), full((2, W)), full((2, 256, W)),
                full((1, W)), full((1, W)), full((1, W)), full((W, W))]
    f32_out = jax.ShapeDtypeStruct((rows, W), F32)
    mxu_out = jax.ShapeDtypeStruct((rows, W), BF16)
    return pl.pallas_call(
        functools.partial(_rw_prep_kernel, nb),
        grid=(nb,),
        in_specs=in_specs,
        out_specs=[pl.BlockSpec((MIX_ROWS, W), lambda i: (i, 0))] * 11,
        out_shape=[mxu_out] * 3 + [f32_out] * 4 + [mxu_out] * 4,
        compiler_params=_cparams(("arbitrary",)),
        name="rw_prep",
    )(p, p, p, p, p, p, taps_m, taps_l, w0, w2p, a0, a2p, kkw, ka, rk, bd512)


def _rw_scan_kernel(rev, *refs):
    if rev:
        (r_ref, kk_ref, v_ref, lw_ref, akk_ref, kd_ref, tri_ref, of_ref, bonus_ref, gs_ref,
         lnw_ref, lnb_ref, bd_ref, y_ref, s_ref) = refs
    else:
        r_ref, kk_ref, v_ref, lw_ref, akk_ref, kd_ref, tri_ref, y_ref, s_ref = refs
    C = RW_CHUNK
    HD = RW_HD
    P2 = 2 * HD

    NP = RW_HEADS // 2

    @pl.when(pl.program_id(0) == 0)
    def _():
        s_ref[...] = jnp.zeros_like(s_ref)

    lw = lw_ref[...]
    cum = _seg_cumsum(lw, tri_ref[...])
    cum_prev = cum - lw

    lane = lax.broadcasted_iota(jnp.int32, (1, P2), 1)
    h0 = lane < HD
    ri = lax.broadcasted_iota(jnp.int32, (P2, P2), 0)
    ci = lax.broadcasted_iota(jnp.int32, (P2, P2), 1)
    same = (ri // C) == (ci // C)
    rt, cs = ri % C, ci % C
    strict = jnp.logical_and(same, (rt < cs) if rev else (rt > cs))
    incl = jnp.logical_and(same, (rt <= cs) if rev else (rt >= cs))
    eye = (ri == ci).astype(F32)
    bdiag = ((ri // HD) == (ci // HD)).astype(F32)

    def stack(x):
        return jnp.concatenate([x, x], axis=0)

    def split_heads(x):
        return jnp.concatenate([jnp.where(h0, x, 0.0), jnp.where(h0, 0.0, x)], axis=0)

    def unstack(x):
        return jnp.where(h0, x[:C], x[C:])

    chunks = list(range(MIX_ROWS // C - 1, -1, -1) if rev else range(MIX_ROWS // C))
    order = [(c, p) for c in chunks for p in range(NP)]

    pre = {}
    for c, p in order:
        sl = slice(c * C, (c + 1) * C)
        pp = slice(p * P2, (p + 1) * P2)
        cum_c = cum[sl, pp]
        cend = cum_c[0:1] if rev else cum_c[C - 1:C]
        e_neg = jnp.exp(-cum_c)
        e_end = jnp.exp(cend - cum_c)
        a_t = kk_ref[sl, pp].astype(F32) * jnp.exp(cum_prev[sl, pp])
        r_t = r_ref[sl, pp].astype(F32) * jnp.exp(cum_c)
        akk = akk_ref[sl, pp].astype(F32)
        kd = kd_ref[sl, pp].astype(F32)
        v = v_ref[sl, pp]
        pre[c, p] = dict(a_t=a_t, r_t=r_t, v=v, kh=kd * e_end, bh=akk * e_end, dec=jnp.exp(cend),
                         lhs=jnp.concatenate([split_heads(a_t), split_heads(r_t)], axis=0),
                         rhs=jnp.concatenate([stack(akk * e_neg), stack(kd * e_neg)], axis=0))
    for w in pre.values():
        g = _mm(w["lhs"], w["rhs"], NT)
        w["m"] = jnp.where(strict, g[:P2, :P2], 0.0)
        w["a_ak"] = jnp.where(strict, g[:P2, P2:], 0.0)
        w["a_rb"] = jnp.where(incl, g[P2:, :P2], 0.0)
        w["a_rk"] = jnp.where(incl, g[P2:, P2:], 0.0)
        w["p"] = eye - w["m"]
    for w in pre.values():
        av = _mm(jnp.concatenate([w["a_ak"], w["a_rk"]], axis=0), stack(w["v"]))
        w["av"], w["rkv"] = av[:P2], av[P2:]
    for w in pre.values():
        w["m"] = _mm(w["m"], w["m"])
    for _ in range(4):
        for w in pre.values():
            pm = _mm(jnp.concatenate([w["p"], w["m"]], axis=0), w["m"])
            w["p"] = w["p"] + pm[:P2]
            w["m"] = pm[P2:]
    for w in pre.values():
        w["p"] = w["p"] + _mm(w["p"], w["m"])
    for w in pre.values():
        w["tt"] = _mm(w["p"], jnp.concatenate([stack(w["a_t"]), w["av"]], axis=1))
        w["z"] = _mm(w["p"], stack(w["bh"]), TN)
    for w in pre.values():
        ar = _mm(w["a_rb"], w["tt"])
        w["oa"] = stack(w["r_t"]) - ar[:, :P2]
        w["oc"] = w["rkv"] - ar[:, P2:]
        w["pz"] = bdiag * _mm(split_heads(w["a_t"]), w["z"], TN)
        w["kz"] = stack(w["kh"]) - _mm(w["a_ak"], w["z"], TN)
    for w in pre.values():
        w["q"] = bdiag * _mm(split_heads(w["v"]), w["kz"], TN)

    s = [s_ref[p] for p in range(NP)]
    for c, p in order:
        w = pre[c, p]
        w["s0"] = s[p]
        s[p] = s[p] * w["dec"] - _mm(s[p], w["pz"]) + w["q"]
    for c, p in order:
        w = pre[c, p]
        o_st = _mm(w["oa"], w["s0"], NT) + w["oc"]
        y_ref[c * C:(c + 1) * C, p * P2:(p + 1) * P2] = unstack(o_st)
    for p in range(NP):
        s_ref[p] = s[p]

    if rev:
        o = of_ref[...] + y_ref[...]
        bd = bd_ref[...]
        mu = _mm2(o, bd) * (1.0 / HD)
        oc = o - mu
        var = _mm2(oc * oc, bd) * (1.0 / HD)
        o = oc * lax.rsqrt(var + RW_GN_EPS) * lnw_ref[...] + lnb_ref[...]
        y_ref[...] = (o + bonus_ref[...]) * gs_ref[...]


def _rw_scan(rev, r, kk, v, lw, akk, kd, extra=()):
    rows = r.shape[0]
    nb = rows // MIX_ROWS
    order = _blk_order(rev, nb)
    P2 = 2 * RW_HD
    W = RW_HEADS * RW_HD
    blk = pl.BlockSpec((MIX_ROWS, W), lambda b: (order(b), 0))
    in_specs = [blk] * 6 + [pl.BlockSpec((MIX_ROWS, MIX_ROWS), lambda b: (0, 0))]
    args = [r, kk, v, lw, akk, kd, _seg_tri(MIX_ROWS, RW_CHUNK, rev)]
    if rev:
        o_fwd, bonus, gs, lnw, lnb, bd512 = extra
        vec = pl.BlockSpec((1, W), lambda b: (0, 0))
        in_specs += [blk, blk, blk, vec, vec, pl.BlockSpec((W, W), lambda b: (0, 0))]
        args += [o_fwd, bonus, gs, lnw, lnb, bd512]
    return pl.pallas_call(
        functools.partial(_rw_scan_kernel, rev),
        grid=(nb,),
        in_specs=in_specs,
        out_specs=blk,
        out_shape=jax.ShapeDtypeStruct((rows, W), F32),
        scratch_shapes=[pltpu.VMEM((RW_HEADS // 2, P2, P2), F32)],
        compiler_params=_cparams(("arbitrary",)),
        name="rw_rev" if rev else "rw_fwd",
    )(*args)


def _qk_prep_kernel(nq_ref, nk_ref, nv_ref, wq_ref, wkv_ref, nqn_ref, nkn_ref, wqn_ref, wkn_ref,
                    cos_ref, sin_ref, bd512_ref, bd128_ref,
                    naq_o, nak_o, nav_o, waq_o, wak_o, wav_o):
    bd512 = bd512_ref[...]
    bd128 = bd128_ref[...]

    def hnorm(x, g, bd, hd):
        return x * lax.rsqrt(_mm2(x * x, bd) * (1.0 / hd) + EPS) * g

    def rope(x, cos, sin):
        lane = lax.broadcasted_iota(jnp.int32, x.shape, 1)
        w = x.shape[1]
        partner = jnp.where((lane & 31) < 16, pltpu.roll(x, w - 16, axis=1), pltpu.roll(x, 16, axis=1))
        return x * cos + partner * sin

    def dup_heads(o_ref, x):
        lower = lax.broadcasted_iota(jnp.int32, x.shape, 1) < WA_HD
        swapped = pltpu.roll(x, WA_HD, axis=1)
        o_ref[0] = jnp.where(lower, x, swapped).astype(o_ref.dtype)
        o_ref[1] = jnp.where(lower, swapped, x).astype(o_ref.dtype)

    f32 = lambda ref: ref[...].astype(F32)
    naq_o[...] = (hnorm(f32(nq_ref), nqn_ref[...], bd512, NA_HD) * NA_HD ** -0.5).astype(BF16)
    nak_o[...] = hnorm(f32(nk_ref), nkn_ref[...], bd512, NA_HD).astype(BF16)
    nav_o[...] = nv_ref[...]
    cos = cos_ref[...]
    sin = sin_ref[...]
    wq = hnorm(f32(wq_ref), wqn_ref[...], bd512, WA_HD) * WA_HD ** -0.5
    wq = rope(wq, jnp.concatenate([cos] * 4, axis=1), jnp.concatenate([sin] * 4, axis=1))
    waq_o[...] = wq.astype(BF16)
    kv = f32(wkv_ref)
    dup_heads(wak_o, rope(hnorm(kv[:, :128], wkn_ref[...], bd128, WA_HD), cos, sin))
    dup_heads(wav_o, kv[:, 128:])


def _qk_prep(p, nqn, nkn, wqn, wkn, cos, sin, bd512, bd128):
    rows = p.shape[0]
    nb = rows // MIX_ROWS
    nab = OFF_NA // 512
    full = lambda shape: pl.BlockSpec(shape, lambda i: (0,) * len(shape))
    in_specs = [pl.BlockSpec((MIX_ROWS, 512), lambda i: (i, nab)),
                pl.BlockSpec((MIX_ROWS, 512), lambda i: (i, nab + 1)),
                pl.BlockSpec((MIX_ROWS, 512), lambda i: (i, nab + 2)),
                pl.BlockSpec((MIX_ROWS, 512), lambda i: (i, OFF_WA // 512)),
                pl.BlockSpec((MIX_ROWS, 256), lambda i: (i, (OFF_WA + 512) // 256)),
                full((1, 512)), full((1, 512)), full((1, 512)), full((1, 128)),
                pl.BlockSpec((MIX_ROWS, 128), lambda i: (i, 0)),
                pl.BlockSpec((MIX_ROWS, 128), lambda i: (i, 0)),
                full((512, 512)), full((128, 128))]
    flat = pl.BlockSpec((MIX_ROWS, 512), lambda i: (i, 0))
    flat_shape = jax.ShapeDtypeStruct((rows, 512), BF16)
    dup = pl.BlockSpec((WA_KV_HEADS, MIX_ROWS, 128), lambda i: (0, i, 0))
    dup_shape = jax.ShapeDtypeStruct((WA_KV_HEADS, rows, 128), BF16)
    return pl.pallas_call(
        _qk_prep_kernel,
        grid=(nb,),
        in_specs=in_specs,
        out_specs=[flat, flat, flat, flat, dup, dup],
        out_shape=[flat_shape, flat_shape, flat_shape, flat_shape, dup_shape, dup_shape],
        compiler_params=_cparams(("arbitrary",)),
        name="qk_prep",
    )(p, p, p, p, p, nqn, nkn, wqn, wkn, cos, sin, bd512, bd128)


def _na_bias_layout():
    wr, G, KR = NA_WIN_R, NA_GROUP, NA_KEY_ROWS
    off = (0, -(wr // 2), -(KR - G))
    out = {}
    for kind in range(3):
        for r in range(G):
            wstart = (0, r, KR - wr)[kind]
            for a in range(KR):
                inside = wstart <= a < wstart + wr
                out[kind, r, a] = (off[kind] + a - r + wr - 1) if inside else None
    return out


def _na_kernel(lc, nrows, q_ref, k_ref, v_ref, t_ref, o_ref, bias_s):
    W = GRID_W
    HD = NA_HD
    nq = NA_GROUP * W
    nk = NA_KEY_ROWS * W
    ngroups = nrows // NA_GROUP
    kc = k_ref[0:lc, :]
    vc = v_ref[0:lc, :]
    half = [lax.broadcasted_iota(jnp.int32, (1, 2 * HD), 1) // HD == hh for hh in range(2)]

    bias_s[:, :, :, 0:lc] = jnp.zeros((2, 3, nq, lc), F32)
    for (kind, r, a), ro in _na_bias_layout().items():
        c0 = lc + a * W
        for hh in range(2):
            tile = jnp.full((W, W), NEG, F32) if ro is None else t_ref[hh, ro, :, (a % 2) * W:(a % 2 + 1) * W]
            bias_s[hh, kind, r * W:(r + 1) * W, c0:c0 + W] = tile

    def attend(q, keys, vals, bias):
        out = None
        for hh in range(2):
            s = _mm(jnp.where(half[hh], q, 0), keys, NT)
            if bias is not None:
                s = s + bias(hh)
            p = jnp.exp(s - jnp.max(s, axis=-1, keepdims=True))
            o = _mm(p, jnp.where(half[hh], vals, 0)) / jnp.sum(p, axis=-1, keepdims=True)
            out = o if out is None else out + o
        return out

    o_ref[0:lc, :] = attend(q_ref[0:lc, :], kc, vc, None)

    def group(gi, carry):
        base = jnp.clip(gi * NA_GROUP - NA_WIN_R // 2, 0, nrows - NA_KEY_ROWS)
        kind = jnp.where(gi == 0, 0, jnp.where(gi == ngroups - 1, 2, 1))
        q0 = pl.multiple_of(lc + gi * nq, nq)
        k0 = pl.multiple_of(lc + base * W, W)
        keys = jnp.concatenate([kc, k_ref[pl.ds(k0, nk), :]], axis=0)
        vals = jnp.concatenate([vc, v_ref[pl.ds(k0, nk), :]], axis=0)
        o_ref[pl.ds(q0, nq), :] = attend(q_ref[pl.ds(q0, nq), :], keys, vals, lambda hh: bias_s[hh, kind])
        return carry

    lax.fori_loop(0, ngroups, group, 0, unroll=2)


def _na(q, k, v, t2, lc):
    rows = q.shape[0]
    nrows = (rows - lc) // GRID_W
    pblk = pl.BlockSpec((rows, 2 * NA_HD), lambda p: (0, p))
    return pl.pallas_call(
        functools.partial(_na_kernel, lc, nrows),
        grid=(NA_HEADS // 2,),
        in_specs=[pblk, pblk, pblk, pl.BlockSpec((2,) + t2.shape[1:], lambda p: (p, 0, 0, 0))],
        out_specs=pblk,
        out_shape=jax.ShapeDtypeStruct((rows, NA_HEADS * NA_HD), F32),
        scratch_shapes=[pltpu.VMEM((2, 3, NA_GROUP * GRID_W, lc + NA_KEY_ROWS * GRID_W), F32)],
        compiler_params=_cparams(("arbitrary",)),
        name="na_attn",
    )(q, k, v, t2)


def _wa_kernel(lc, t_len, q_ref, k_ref, v_ref, sink_ref, o_ref):
    G = WA_HEADS // WA_KV_HEADS
    B = WA_WINDOW
    nband = 3 * B
    HD = WA_HD
    b = pl.program_id(1)
    kc = k_ref[0:lc, :]
    vc = v_ref[0:lc, :]
    sink = sink_ref[...]
    lower = lax.broadcasted_iota(jnp.int32, (1, 2 * HD), 1) < HD

    def stacked_queries(r0, n):
        parts = []
        for g in range(G):
            qp = q_ref[r0:r0 + n, (g // 2) * 2 * HD:(g // 2 + 1) * 2 * HD]
            parts.append(jnp.where(lower if g % 2 == 0 else jnp.logical_not(lower), qp, 0))
        return jnp.concatenate(parts, axis=0)

    def store(r0, n, o):
        for gp in range(G // 2):
            o_ref[r0:r0 + n, gp * 2 * HD:(gp + 1) * 2 * HD] = jnp.where(
                lower, o[2 * gp * n:(2 * gp + 1) * n], o[(2 * gp + 1) * n:(2 * gp + 2) * n])

    @pl.when(b == 0)
    def _():
        sk = jnp.concatenate([jnp.broadcast_to(sink[g * B:g * B + 1, :], (lc, 1)) for g in range(G)], axis=0)
        s = _mm(stacked_queries(0, lc), kc, NT)
        m = jnp.maximum(jnp.max(s, axis=-1, keepdims=True), sk)
        e = jnp.exp(s - m)
        store(0, lc, _mm(e, vc) / (jnp.sum(e, axis=-1, keepdims=True) + jnp.exp(sk - m)))

    @pl.when(b > 0)
    def _():
        qoff = lax.broadcasted_iota(jnp.int32, (G * B, lc + nband), 0) & (B - 1)
        koff = lax.broadcasted_iota(jnp.int32, (G * B, lc + nband), 1) - lc
        for j in range(MIX_ROWS // B):
            n = (b - 1) * (MIX_ROWS // B) + j
            start = jnp.clip((n - 1) * B, 0, t_len - nband)
            k0 = pl.multiple_of(lc + start, B)
            keys = jnp.concatenate([kc, k_ref[pl.ds(k0, nband), :]], axis=0)
            vals = jnp.concatenate([vc, v_ref[pl.ds(k0, nband), :]], axis=0)
            valid = jnp.logical_or(koff < 0, jnp.abs((n * B + qoff) - (start + koff)) <= WA_WINDOW)
            s = jnp.where(valid, _mm(stacked_queries(j * B, B), keys, NT), NEG)
            m = jnp.maximum(jnp.max(s, axis=-1, keepdims=True), sink)
            p = jnp.exp(s - m)
            store(j * B, B, _mm(p, vals) / (jnp.sum(p, axis=-1, keepdims=True) + jnp.exp(sink - m)))


def _wa(q, k, v, sink_col, lc):
    rows = q.shape[0]
    G = WA_HEADS // WA_KV_HEADS
    kvblk = pl.BlockSpec((None, rows, 2 * WA_HD), lambda h, b: (h, 0, 0))
    qblk = pl.BlockSpec((MIX_ROWS, G * WA_HD), lambda h, b: (b, h))
    return pl.pallas_call(
        functools.partial(_wa_kernel, lc, rows - lc),
        grid=(WA_KV_HEADS, rows // MIX_ROWS),
        in_specs=[qblk, kvblk, kvblk,
                  pl.BlockSpec((None, G * WA_WINDOW, 1), lambda h, b: (h, 0, 0))],
        out_specs=qblk,
        out_shape=jax.ShapeDtypeStruct((rows, WA_HEADS * WA_HD), F32),
        compiler_params=_cparams(("arbitrary", "arbitrary")),
        name="wa_attn",
    )(q, k, v, sink_col)


def _block_diag(n, blk):
    idx = np.arange(n) // blk
    return jnp.asarray(idx[:, None] == idx[None, :], dtype=BF16)


def _na_bias_table(rpb):
    W, wc = GRID_W, NA_WIN_C
    j = np.arange(W)
    cstart = np.clip(j - wc // 2, 0, W - wc)
    cabs = np.arange(W)
    inwin = (cabs[None, :] >= cstart[:, None]) & (cabs[None, :] < cstart[:, None] + wc)
    cb = cabs[None, :] - j[:, None] + wc - 1
    nh, nro, nco = rpb.shape
    onehot = (cb[None] == np.arange(nco)[:, None, None]) & inwin[None]
    t = jnp.dot(rpb.reshape(nh * nro, nco), jnp.asarray(onehot.reshape(nco, W * W), F32),
                precision=lax.Precision.HIGHEST).reshape(nh, nro, W, W)
    t = t + jnp.asarray(np.where(inwin, 0.0, NEG), F32)
    return jnp.concatenate([t, t], axis=-1)


def _rope_tables(lc, t_len):
    pos = np.arange(t_len)
    n = WA_HD // 4
    inv = ROPE_BASE ** (-np.arange(n, dtype=np.float64) / n)
    ang_r = (pos // GRID_W)[:, None] * inv[None, :]
    ang_c = (pos % GRID_W)[:, None] * inv[None, :]
    cos = np.concatenate([np.cos(ang_r)] * 2 + [np.cos(ang_c)] * 2, axis=1)
    sin = np.concatenate([-np.sin(ang_r), np.sin(ang_r), -np.sin(ang_c), np.sin(ang_c)], axis=1)
    cos = np.concatenate([np.ones((lc, WA_HD)), cos], axis=0)
    sin = np.concatenate([np.zeros((lc, WA_HD)), sin], axis=0)
    return (jnp.asarray(np.tile(cos, (1, 2)), dtype=F32), jnp.asarray(np.tile(sin, (1, 2)), dtype=F32))


def _cast_kernel(x_ref, o_ref):
    o_ref[...] = x_ref[...].astype(o_ref.dtype)


def _cast_bf16(w, layer, bk, bn, src_col_block=None):
    _, k, n = w.shape
    col = src_col_block or (lambda j: j)
    return pl.pallas_call(
        _cast_kernel,
        grid=(k // bk, n // bn),
        in_specs=[pl.BlockSpec((None, bk, bn), lambda i, j: (layer, i, col(j)))],
        out_specs=pl.BlockSpec((bk, bn), lambda i, j: (i, j)),
        out_shape=jax.ShapeDtypeStruct((k, n), BF16),
        compiler_params=_cparams(("arbitrary", "arbitrary")),
        name="cast_bf16",
    )(w)


W_IN_BLK = 256
_W_IN_SRC = np.concatenate([np.arange(10, 18), np.arange(0, 10), np.arange(19, 25), np.arange(28, 60),
                            np.arange(25, 28), np.arange(18, 19)])


def _w_in_src_block(j):
    starts = (0, 8, 18, 24, 56, 59)
    src = j + int(_W_IN_SRC[0])
    for s in starts[1:]:
        src = jnp.where(j >= s, j + int(_W_IN_SRC[s]) - s, src)
    return src


def _lora_pad(w, row0):
    out = jnp.zeros((2, 256, w.shape[-1]), F32)
    for d in range(2):
        out = out.at[d, row0 + 64 * d:row0 + 64 * (d + 1)].set(w[d])
    return out


def _token_mixing(p, lc, layer, hg_lb, hg_norm, rw_shift, rw_w0, rw_w2, rw_a0, rw_a2, rw_kk, rw_ka, rw_rk,
                  rw_ln_w, rw_ln_b, na_qn, na_kn, na_rpb, wa_qn, wa_kn, wa_sink, tables):
    rows = p.shape[0]
    bd512, bd128, cos, sin = tables
    cum = jnp.cumsum(jax.nn.softmax(hg_lb.astype(F32), axis=1), axis=1)
    lbs = cum[:, layer] - cum[:, 0]
    o_f = _gla_dir(p, lbs[0:1], False)
    y_a = _gla_dir(p, lbs[1:2], True, o_f, hg_norm[None])
    prep = _rw_prep(p, rw_shift[:, :2048], rw_shift[:, 2048:], rw_w0, _lora_pad(rw_w2, 0), rw_a0,
                    _lora_pad(rw_a2, 128), rw_kk[None], rw_ka[None], rw_rk.reshape(1, -1), bd512)
    r, kk, v, gs, bonus, lw0, lw1, akk0, akk1, kd0, kd1 = prep
    o_f = _rw_scan(False, r, kk, v, lw0, akk0, kd0)
    y_b = _rw_scan(True, r, kk, v, lw1, akk1, kd1,
                   (o_f, bonus, gs, rw_ln_w[None], rw_ln_b[None], bd512))
    tile = lambda g, n: jnp.tile(g, n)[None]
    naq, nak, nav, waq, wak, wav = _qk_prep(p, tile(na_qn, 8), tile(na_kn, 8), tile(wa_qn, 8),
                                            tile(wa_kn, 2), cos, sin, bd512, bd128)
    y_c = _na(naq, nak, nav, _na_bias_table(na_rpb), lc)
    G = WA_HEADS // WA_KV_HEADS
    sink_col = jnp.repeat(wa_sink.reshape(WA_KV_HEADS, G), WA_WINDOW, axis=1)[..., None]
    y_d = _wa(waq, wak, wav, sink_col, lc)
    return y_a, y_b, y_c, y_d


def kernel(x, c, ctx, c_ctx, ada_w, ada_b, norm_ffn1, norm_mix, norm_ffn2, ffn1_wi, ffn1_wo, ffn2_wi, ffn2_wo, w_in, hg_lb, hg_norm, rw_shift, rw_w0, rw_w2, rw_a0, rw_a2, rw_kk, rw_ka, rw_rk, rw_ln_w, rw_ln_b, na_qn, na_kn, na_rpb, wa_qn, wa_kn, wa_sink, w_branch, w_out):
    assert x.shape[0] == 1 and ctx.shape[1] == MIX_ROWS
    lc = ctx.shape[1]
    t_len = x.shape[1]
    xa = jnp.concatenate([ctx[0], x[0]], axis=0)
    cc_t = jnp.stack([c[0], c_ctx], axis=1)
    mods = _ada_mods(cc_t, ada_w, ada_b).reshape(DEPTH, 2, N_MOD, D)
    tables = (_block_diag(512, 64), _block_diag(128, 64)) + _rope_tables(lc, t_len)
    for l in range(DEPTH):
        m = mods[l]
        xa = _ffn(xa, m[:, 0:3], norm_ffn1[l][None], _cast_bf16(ffn1_wi, l, D, 1024),
                  _cast_bf16(ffn1_wo, l, 512, D), lc)
        p = _win(xa, m[:, 3:6], norm_mix[l][None], _cast_bf16(w_in, l, D, W_IN_BLK, _w_in_src_block), lc)
        yb = _token_mixing(p, lc, l, hg_lb, hg_norm[l], rw_shift[l], rw_w0[l], rw_w2[l], rw_a0[l], rw_a2[l],
                           rw_kk[l], rw_ka[l], rw_rk[l], rw_ln_w[l], rw_ln_b[l], na_qn[l], na_kn[l],
                           na_rpb[l], wa_qn[l], wa_kn[l], wa_sink[l], tables)
        wb = _cast_bf16(w_branch.reshape(DEPTH, N_BRANCH * BR_W, D), l, 1024, D).reshape(N_BRANCH, BR_W, D)
        xa = _merge(xa, m[:, 3:6], yb, p, wb, _cast_bf16(w_out, l, 1024, D), lc)
        xa = _ffn(xa, m[:, 6:9], norm_ffn2[l][None], _cast_bf16(ffn2_wi, l, D, 1024),
                  _cast_bf16(ffn2_wo, l, 512, D), lc)
    return xa[lc:][None]
```

```python
import functools

import numpy as np
import jax
import jax.numpy as jnp
from jax import lax
from jax.experimental import pallas as pl
from jax.experimental.pallas import tpu as pltpu

F32 = jnp.float32
BF16 = jnp.bfloat16

D = 2048
DEPTH = 2
GRID_W = 64
EPS = 1e-6
D_FF = 5632
N_MOD = 9
N_BRANCH = 4
BR_W = 512
HG_HEADS, HG_DK = 4, 128
RW_HEADS, RW_HD = 8, 64
RW_LORA = 64
RW_GN_EPS = 64e-5
NA_HEADS, NA_HD = 8, 64
NA_WIN_R, NA_WIN_C = 8, 16
NA_GROUP = 4
NA_KEY_ROWS = NA_WIN_R + NA_GROUP - 1
WA_HEADS, WA_KV_HEADS, WA_HD = 8, 2, 64
WA_WINDOW = 128
ROPE_BASE = 10000.0

OFF_RW, OFF_HG, OFF_NA, OFF_GL, OFF_WA, OFF_LORA = 0, 2048, 4608, 6144, 14336, 15104
P_TOTAL = 15360

MIX_ROWS = 256
HALO = 16
GLA_CHUNK = 16
RW_CHUNK = 64
NEG = -1e30
LOG2E = 1.4426950408889634
VMEM_LIMIT = 56 * 1024 * 1024

NT = (((1,), (1,)), ((), ()))
TN = (((0,), (0,)), ((), ()))


def _cparams(sem):
    return pltpu.CompilerParams(dimension_semantics=sem, vmem_limit_bytes=VMEM_LIMIT)


def _mm(a, b, dims=None):
    a = a.astype(BF16)
    b = b.astype(BF16)
    if dims is None:
        return jnp.dot(a, b, preferred_element_type=F32)
    return lax.dot_general(a, b, dims, preferred_element_type=F32)


def _mm2(x, w_bf16):
    hi = x.astype(BF16)
    lo = (x - hi.astype(F32)).astype(BF16)
    return (jnp.dot(hi, w_bf16, preferred_element_type=F32)
            + jnp.dot(lo, w_bf16, preferred_element_type=F32))


def _mm_f32(a, b):
    a_hi, b_hi = a.astype(BF16), b.astype(BF16)
    a_lo = (a - a_hi.astype(F32)).astype(BF16)
    b_lo = (b - b_hi.astype(F32)).astype(BF16)
    dot = functools.partial(jnp.dot, preferred_element_type=F32)
    return dot(a_hi, b_hi) + (dot(a_hi, b_lo) + dot(a_lo, b_hi))


def _sigmoid(x):
    return 1.0 / (1.0 + jnp.exp(-x))


def _silu(x):
    return x * _sigmoid(x)


def _seg_cumsum(x, tri):
    w = x.shape[1]
    hi = x.astype(BF16)
    r1 = x - hi.astype(F32)
    mid = r1.astype(BF16)
    lo = (r1 - mid.astype(F32)).astype(BF16)
    y = jnp.dot(tri, jnp.concatenate([hi, mid, lo], axis=1), preferred_element_type=F32)
    return y[:, :w] + y[:, w:2 * w] + y[:, 2 * w:]


def _seg_tri(rows, seg, rev):
    r = np.arange(rows)
    same = (r[:, None] // seg) == (r[None, :] // seg)
    tri = (r[None, :] >= r[:, None]) if rev else (r[None, :] <= r[:, None])
    return jnp.asarray(same & tri, dtype=BF16)


def _ada_kernel(cc_ref, w_ref, b_ref, o_ref):
    s = _silu(cc_ref[...])
    w = w_ref[...]
    r0 = jnp.sum(w * s[:, 0:1], axis=0, keepdims=True)
    r1 = jnp.sum(w * s[:, 1:2], axis=0, keepdims=True)
    o_ref[...] = jnp.concatenate([r0, r1], axis=0) + b_ref[...]


def _ada_mods(cc_t, ada_w, ada_b):
    tn = 512
    nmod = ada_w.shape[-1]
    return pl.pallas_call(
        _ada_kernel,
        grid=(DEPTH, nmod // tn),
        in_specs=[pl.BlockSpec((D, 2), lambda l, j: (0, 0)),
                  pl.BlockSpec((None, D, tn), lambda l, j: (l, 0, j)),
                  pl.BlockSpec((None, 1, tn), lambda l, j: (l, 0, j))],
        out_specs=pl.BlockSpec((None, 2, tn), lambda l, j: (l, 0, j)),
        out_shape=jax.ShapeDtypeStruct((DEPTH, 2, nmod), F32),
        compiler_params=_cparams(("arbitrary", "arbitrary")),
        name="ada_mod",
    )(cc_t, ada_w, ada_b.reshape(DEPTH, 1, nmod))


def _row_is_ctx(i, tm, lc):
    return (i * tm + lax.broadcasted_iota(jnp.int32, (tm, 1), 0)) < lc


def _mod_pick(mod_ref, j, is_ctx):
    return jnp.where(is_ctx, mod_ref[1, j:j + 1, :], mod_ref[0, j:j + 1, :])


def _store_modulated(xn_ref, x_ref, g_ref, mod_ref, i, tm, lc):
    def normed():
        x = x_ref[...]
        return x * lax.rsqrt(jnp.mean(x * x, axis=-1, keepdims=True) + EPS)

    @pl.when(i * tm < lc)
    def _():
        is_ctx = _row_is_ctx(i, tm, lc)
        y = normed() * g_ref[...]
        xn_ref[...] = (y * (1.0 + _mod_pick(mod_ref, 1, is_ctx)) + _mod_pick(mod_ref, 0, is_ctx)).astype(BF16)

    @pl.when(i * tm >= lc)
    def _():
        gain = g_ref[...] * (1.0 + mod_ref[0, 1:2, :])
        xn_ref[...] = (normed() * gain + mod_ref[0, 0:1, :]).astype(BF16)


def _ffn_kernel(lc, tm, nf, x_ref, mod_ref, g_ref, wa_ref, wb_ref, wo_ref, o_ref, xn_ref):
    i = pl.program_id(0)
    j = pl.program_id(1)

    @pl.when(j == 0)
    def _():
        _store_modulated(xn_ref, x_ref, g_ref, mod_ref, i, tm, lc)
        o_ref[...] = jnp.zeros_like(o_ref)

    xn = xn_ref[...]
    a = jnp.dot(xn, wa_ref[...], preferred_element_type=F32)
    b = jnp.dot(xn, wb_ref[...], preferred_element_type=F32)
    h = (_silu(a) * b).astype(BF16)
    o_ref[...] += jnp.dot(h, wo_ref[...], preferred_element_type=F32)

    @pl.when(j == nf - 1)
    def _():
        o_ref[...] = x_ref[...] + 0.5 * _mod_pick(mod_ref, 2, _row_is_ctx(i, tm, lc)) * o_ref[...]


def _dense_tm(rows):
    for tm in (768, 512, 256):
        if rows % tm == 0:
            return tm
    raise ValueError(rows)


def _ffn(x, mod3, g, wi, wo, lc):
    rows = x.shape[0]
    tm = _dense_tm(rows)
    tf = 512
    nf = D_FF // tf
    return pl.pallas_call(
        functools.partial(_ffn_kernel, lc, tm, nf),
        grid=(rows // tm, nf),
        in_specs=[pl.BlockSpec((tm, D), lambda i, j: (i, 0)),
                  pl.BlockSpec((2, 3, D), lambda i, j: (0, 0, 0)),
                  pl.BlockSpec((1, D), lambda i, j: (0, 0)),
                  pl.BlockSpec((D, tf), lambda i, j: (0, j)),
                  pl.BlockSpec((D, tf), lambda i, j: (0, j + nf)),
                  pl.BlockSpec((tf, D), lambda i, j: (j, 0))],
        out_specs=pl.BlockSpec((tm, D), lambda i, j: (i, 0)),
        out_shape=jax.ShapeDtypeStruct((rows, D), F32),
        scratch_shapes=[pltpu.VMEM((tm, D), BF16)],
        compiler_params=_cparams(("arbitrary", "arbitrary")),
        name="ffn",
    )(x, mod3, g, wi, wi, wo)


def _win_kernel(lc, tm, x_ref, mod_ref, g_ref, w_ref, o_ref, xn_ref):
    i = pl.program_id(0)

    @pl.when(pl.program_id(1) == 0)
    def _():
        _store_modulated(xn_ref, x_ref, g_ref, mod_ref, i, tm, lc)

    o_ref[...] = jnp.dot(xn_ref[...], w_ref[...], preferred_element_type=F32).astype(o_ref.dtype)


def _win(x, mod3, g, w, lc):
    rows = x.shape[0]
    tm = _dense_tm(rows)
    tn = 1536
    return pl.pallas_call(
        functools.partial(_win_kernel, lc, tm),
        grid=(rows // tm, P_TOTAL // tn),
        in_specs=[pl.BlockSpec((tm, D), lambda i, j: (i, 0)),
                  pl.BlockSpec((2, 3, D), lambda i, j: (0, 0, 0)),
                  pl.BlockSpec((1, D), lambda i, j: (0, 0)),
                  pl.BlockSpec((D, tn), lambda i, j: (0, j))],
        out_specs=pl.BlockSpec((tm, tn), lambda i, j: (i, j)),
        out_shape=jax.ShapeDtypeStruct((rows, P_TOTAL), BF16),
        scratch_shapes=[pltpu.VMEM((tm, D), BF16)],
        compiler_params=_cparams(("arbitrary", "arbitrary")),
        name="w_in",
    )(x, mod3, g, w)


def _merge_kernel(lc, tm, x_ref, mod_ref, ya_ref, yb_ref, yc_ref, yd_ref, gl_ref, wb_ref, wo_ref, o_ref, acc_ref):
    i = pl.program_id(0)
    n = pl.program_id(1)

    @pl.when(n == 0)
    def _():
        acc_ref[...] = jnp.zeros_like(acc_ref)

    for nn, y_ref in enumerate((ya_ref, yb_ref, yc_ref, yd_ref)):
        @pl.when(n == nn)
        def _():
            proj = jnp.dot(y_ref[...].astype(BF16), wb_ref[nn], preferred_element_type=F32)
            acc_ref[...] += _sigmoid(gl_ref[...].astype(F32)) * proj

    @pl.when(n == N_BRANCH - 1)
    def _():
        is_ctx = _row_is_ctx(i, tm, lc)
        y = jnp.dot(acc_ref[...].astype(BF16), wo_ref[...], preferred_element_type=F32)
        o_ref[...] = x_ref[...] + _mod_pick(mod_ref, 2, is_ctx) * y


def _merge(x, mod3, ys, p, wb, wo, lc):
    rows = x.shape[0]
    tm = 384 if rows % 384 == 0 else MIX_ROWS
    glb = OFF_GL // D
    yspec = pl.BlockSpec((tm, BR_W), lambda i, n: (i, 0))
    return pl.pallas_call(
        functools.partial(_merge_kernel, lc, tm),
        grid=(rows // tm, N_BRANCH),
        in_specs=[pl.BlockSpec((tm, D), lambda i, n: (i, 0)),
                  pl.BlockSpec((2, 3, D), lambda i, n: (0, 0, 0)),
                  yspec, yspec, yspec, yspec,
                  pl.BlockSpec((tm, D), lambda i, n: (i, glb + n)),
                  pl.BlockSpec((N_BRANCH, BR_W, D), lambda i, n: (0, 0, 0)),
                  pl.BlockSpec((D, D), lambda i, n: (0, 0))],
        out_specs=pl.BlockSpec((tm, D), lambda i, n: (i, 0)),
        out_shape=jax.ShapeDtypeStruct((rows, D), F32),
        scratch_shapes=[pltpu.VMEM((tm, D), F32)],
        compiler_params=_cparams(("arbitrary", "arbitrary")),
        name="merge",
    )(x, mod3, *ys, p, wb, wo)


def _gla_kernel(rev, *refs):
    if rev:
        q_ref, f_ref, i_ref, lb_ref, tri_ref, of_ref, g_ref, nw_ref, y_ref, st_ref = refs
    else:
        q_ref, f_ref, i_ref, lb_ref, tri_ref, y_ref, st_ref = refs
    C = GLA_CHUNK
    R = MIX_ROWS
    NH, DK = HG_HEADS, HG_DK

    @pl.when(pl.program_id(0) == 0)
    def _():
        st_ref[...] = jnp.zeros_like(st_ref)

    lb = lb_ref[...]
    f_all = lb + (1.0 - lb) * _sigmoid(f_ref[...].astype(F32))
    b_all = _seg_cumsum(jnp.log(f_all), tri_ref[...])
    SUB = C // 2
    sub_pos = lax.broadcasted_iota(jnp.int32, (1, SUB, 1), 1)
    chunks = list(range(R // C - 1, -1, -1) if rev else range(R // C))

    heads = []
    for h in range(NH):
        hs = slice(h * DK, (h + 1) * DK)
        b = b_all[:, hs]
        k = 1.0 - f_all[:, hs]
        q = q_ref[:, hs].astype(F32)
        v = i_ref[:, hs].astype(F32)
        def halves(t):
            t3 = t.reshape(R // C, C, DK)
            lo, hi = t3[:, :SUB], t3[:, SUB:]
            return (hi, lo) if rev else (lo, hi)

        def pair_term(qq, bq, keys, ok):
            bk_, kk_, vk_ = keys
            diff = bq - bk_
            e = jnp.exp2(diff if ok is None else jnp.where(ok, diff, NEG))
            return jnp.sum(qq * kk_ * e, axis=-1, keepdims=True) * vk_

        (qn, qf), (bn, bf), (kn, kf), (vn, vf) = halves(q), halves(b * LOG2E), halves(k), halves(v)
        near, far = (bn, kn, vn), (bf, kf, vf)
        o_n = jnp.sum(qn * kn, axis=-1, keepdims=True) * vn
        o_f = jnp.sum(qf * kf, axis=-1, keepdims=True) * vf + pair_term(qf, bf, near, None)
        for d in range(1, SUB):
            sh = (SUB - d) if rev else d
            ok = (sub_pos < SUB - d) if rev else (sub_pos >= d)
            near_r = tuple(pltpu.roll(t, sh, axis=1) for t in near)
            far_r = tuple(jnp.where(ok, pltpu.roll(t, sh, axis=1), tn) for t, tn in zip(far, near_r))
            o_n = o_n + pair_term(qn, bn, near_r, ok)
            o_f = o_f + pair_term(qf, bf, far_r, None) + pair_term(qf, bf, near_r, ok)
        o = jnp.concatenate([o_f, o_n] if rev else [o_n, o_f], axis=1).reshape(R, DK)
        upd, dec = {}, {}
        for c in chunks:
            sl = slice(c * C, (c + 1) * C)
            b_end = b[sl][0:1] if rev else b[sl][C - 1:C]
            dec[c] = jnp.exp(b_end)
            upd[c] = _mm(v[sl], k[sl] * jnp.exp(b_end - b[sl]), TN)
        heads.append(dict(o=o, qb=q * jnp.exp(b), upd=upd, dec=dec, st=st_ref[h]))

    for c in chunks:
        for w in heads:
            w[c] = w["st"]
            w["st"] = w["st"] * w["dec"][c] + w["upd"][c]
    outs = []
    for h, w in enumerate(heads):
        st_ref[h] = w["st"]
        inter = {c: _mm(w["qb"][c * C:(c + 1) * C], w[c], NT) for c in chunks}
        outs.append(w["o"] + jnp.concatenate([inter[c] for c in range(R // C)], axis=0))

    for h, o in enumerate(outs):
        hs = slice(h * DK, (h + 1) * DK)
        if rev:
            o = of_ref[:, hs] + o
            o = o * lax.rsqrt(jnp.mean(o * o, axis=-1, keepdims=True) + EPS) * nw_ref[:, hs]
            o = o * _silu(g_ref[:, hs].astype(F32))
        y_ref[:, hs] = o


def _blk_order(rev, nb):
    if rev:
        return lambda b: jnp.where(b == 0, 0, nb - b)
    return lambda b: b


def _gla_dir(p, lb, rev, o_fwd=None, norm_w=None):
    rows = p.shape[0]
    nb = rows // MIX_ROWS
    order = _blk_order(rev, nb)
    W = HG_HEADS * HG_DK
    cb = OFF_HG // W

    def col(n):
        return pl.BlockSpec((MIX_ROWS, W), lambda b: (order(b), cb + n))

    vec = pl.BlockSpec((1, W), lambda b: (0, 0))
    in_specs = [col(0), col(2 if rev else 1), col(3), vec,
                pl.BlockSpec((MIX_ROWS, MIX_ROWS), lambda b: (0, 0))]
    args = [p, p, p, lb, _seg_tri(MIX_ROWS, GLA_CHUNK, rev)]
    if rev:
        in_specs += [pl.BlockSpec((MIX_ROWS, W), lambda b: (order(b), 0)), col(4), vec]
        args += [o_fwd, p, norm_w]
    return pl.pallas_call(
        functools.partial(_gla_kernel, rev),
        grid=(nb,),
        in_specs=in_specs,
        out_specs=pl.BlockSpec((MIX_ROWS, W), lambda b: (order(b), 0)),
        out_shape=jax.ShapeDtypeStruct((rows, W), F32),
        scratch_shapes=[pltpu.VMEM((HG_HEADS, HG_DK, HG_DK), F32)],
        compiler_params=_cparams(("arbitrary",)),
        name="gla_rev" if rev else "gla_fwd",
    )(*args)


def _rw_prep_kernel(nb, main_ref, lora_ref, pm_ref, nm_ref, pl_ref, nl_ref, tm_ref, tl_ref,
                    w0_ref, w2_ref, a0_ref, a2_ref, kkw_ref, ka_ref, rk_ref, bd_ref,
                    r_o, kk_o, v_o, gs_o, bonus_o, lw0_o, lw1_o, akk0_o, akk1_o, kd0_o, kd1_o):
    i = pl.program_id(0)
    has_prev = (i >= 2).astype(F32)
    has_next = jnp.logical_and(i != 0, i != nb - 1).astype(F32)
    rowi = lax.broadcasted_iota(jnp.int32, (MIX_ROWS, 1), 0)

    def shift(x, prev_blk, next_blk, taps):
        up = jnp.where(rowi == 0, prev_blk[HALO - 1:HALO, :] * has_prev, pltpu.roll(x, 1, axis=0))
        dn = jnp.where(rowi == MIX_ROWS - 1, next_blk[0:1, :] * has_next,
                       pltpu.roll(x, MIX_ROWS - 1, axis=0))
        return taps[0:1] * up + taps[1:2] * x + taps[2:3] * dn

    f32 = lambda ref: ref[...].astype(F32)
    main = shift(f32(main_ref), f32(pm_ref), f32(nm_ref), tm_ref[...])
    lora = shift(f32(lora_ref), f32(pl_ref), f32(nl_ref), tl_ref[...])
    W = RW_HEADS * RW_HD
    r, k, v, g = (main[:, n * W:(n + 1) * W] for n in range(4))
    bd = bd_ref[...]
    kk = k * kkw_ref[...]
    kk = kk * lax.rsqrt(_mm2(kk * kk, bd) + EPS)
    tl = jnp.tanh(lora)
    kds = []
    for d, (lw_o, akk_o, kd_o) in enumerate(((lw0_o, akk0_o, kd0_o), (lw1_o, akk1_o, kd1_o))):
        z = -(w0_ref[d:d + 1, :] + _mm_f32(tl, w2_ref[d]))
        softplus = jnp.maximum(z, 0.0) + jnp.log(1.0 + jnp.exp(-jnp.abs(z)))
        lw_o[...] = -jnp.exp(-softplus - 0.5)
        a = _sigmoid(a0_ref[d:d + 1, :] + _mm_f32(lora, a2_ref[d]))
        kd = k * (1.0 + (a - 1.0) * ka_ref[...])
        kds.append(kd)
        kd_o[...] = kd.astype(kd_o.dtype)
        akk_o[...] = (a * kk).astype(akk_o.dtype)
    r_o[...] = r.astype(r_o.dtype)
    kk_o[...] = kk.astype(kk_o.dtype)
    v_o[...] = v.astype(v_o.dtype)
    gs_o[...] = _sigmoid(g)
    bonus_o[...] = _mm2(r * (kds[0] + kds[1]) * rk_ref[...], bd) * v


def _rw_prep(p, taps_m, taps_l, w0, w2p, a0, a2p, kkw, ka, rk, bd512):
    rows = p.shape[0]
    nb = rows // MIX_ROWS
    W = RW_HEADS * RW_HD
    n8 = rows // HALO
    lb = OFF_LORA // 256
    per = MIX_ROWS // HALO
    full = lambda shape: pl.BlockSpec(shape, lambda i: (0,) * len(shape))
    in_specs = [pl.BlockSpec((MIX_ROWS, 4 * W), lambda i: (i, 0)),
                pl.BlockSpec((MIX_ROWS, 256), lambda i: (i, lb)),
                pl.BlockSpec((HALO, 4 * W), lambda i: (jnp.maximum(i * per - 1, 0), 0)),
                pl.BlockSpec((HALO, 4 * W), lambda i: (jnp.minimum((i + 1) * per, n8 - 1), 0)),
                pl.BlockSpec((HALO, 256), lambda i: (jnp.maximum(i * per - 1, 0), lb)),
                pl.BlockSpec((HALO, 256), lambda i: (jnp.minimum((i + 1) * per, n8 - 1), lb)),
                full((3, 4 * W)), full((3, 256)),
                full((2, W)), full((2, 256, W)), full((2, W)), full((2, 256, W)),
                full((1, W)), full((1, W)), full((1, W)), full((W, W))]
    f32_out = jax.ShapeDtypeStruct((rows, W), F32)
    mxu_out = jax.ShapeDtypeStruct((rows, W), BF16)
    return pl.pallas_call(
        functools.partial(_rw_prep_kernel, nb),
        grid=(nb,),
        in_specs=in_specs,
        out_specs=[pl.BlockSpec((MIX_ROWS, W), lambda i: (i, 0))] * 11,
        out_shape=[mxu_out] * 3 + [f32_out] * 4 + [mxu_out] * 4,
        compiler_params=_cparams(("arbitrary",)),
        name="rw_prep",
    )(p, p, p, p, p, p, taps_m, taps_l, w0, w2p, a0, a2p, kkw, ka, rk, bd512)


def _rw_scan_kernel(rev, *refs):
    if rev:
        (r_ref, kk_ref, v_ref, lw_ref, akk_ref, kd_ref, tri_ref, of_ref, bonus_ref, gs_ref,
         lnw_ref, lnb_ref, bd_ref, y_ref, s_ref) = refs
    else:
        r_ref, kk_ref, v_ref, lw_ref, akk_ref, kd_ref, tri_ref, y_ref, s_ref = refs
    C = RW_CHUNK
    HD = RW_HD
    P2 = 2 * HD

    NP = RW_HEADS // 2

    @pl.when(pl.program_id(0) == 0)
    def _():
        s_ref[...] = jnp.zeros_like(s_ref)

    lw = lw_ref[...]
    cum = _seg_cumsum(lw, tri_ref[...])
    cum_prev = cum - lw

    lane = lax.broadcasted_iota(jnp.int32, (1, P2), 1)
    h0 = lane < HD
    ri = lax.broadcasted_iota(jnp.int32, (P2, P2), 0)
    ci = lax.broadcasted_iota(jnp.int32, (P2, P2), 1)
    same = (ri // C) == (ci // C)
    rt, cs = ri % C, ci % C
    strict = jnp.logical_and(same, (rt < cs) if rev else (rt > cs))
    incl = jnp.logical_and(same, (rt <= cs) if rev else (rt >= cs))
    eye = (ri == ci).astype(F32)
    bdiag = ((ri // HD) == (ci // HD)).astype(F32)

    def stack(x):
        return jnp.concatenate([x, x], axis=0)

    def split_heads(x):
        return jnp.concatenate([jnp.where(h0, x, 0.0), jnp.where(h0, 0.0, x)], axis=0)

    def unstack(x):
        return jnp.where(h0, x[:C], x[C:])

    chunks = list(range(MIX_ROWS // C - 1, -1, -1) if rev else range(MIX_ROWS // C))
    order = [(c, p) for c in chunks for p in range(NP)]

    pre = {}
    for c, p in order:
        sl = slice(c * C, (c + 1) * C)
        pp = slice(p * P2, (p + 1) * P2)
        cum_c = cum[sl, pp]
        cend = cum_c[0:1] if rev else cum_c[C - 1:C]
        e_neg = jnp.exp(-cum_c)
        e_end = jnp.exp(cend - cum_c)
        a_t = kk_ref[sl, pp].astype(F32) * jnp.exp(cum_prev[sl, pp])
        r_t = r_ref[sl, pp].astype(F32) * jnp.exp(cum_c)
        akk = akk_ref[sl, pp].astype(F32)
        kd = kd_ref[sl, pp].astype(F32)
        v = v_ref[sl, pp]
        pre[c, p] = dict(a_t=a_t, r_t=r_t, v=v, kh=kd * e_end, bh=akk * e_end, dec=jnp.exp(cend),
                         lhs=jnp.concatenate([split_heads(a_t), split_heads(r_t)], axis=0),
                         rhs=jnp.concatenate([stack(akk * e_neg), stack(kd * e_neg)], axis=0))
    for w in pre.values():
        g = _mm(w["lhs"], w["rhs"], NT)
        w["m"] = jnp.where(strict, g[:P2, :P2], 0.0)
        w["a_ak"] = jnp.where(strict, g[:P2, P2:], 0.0)
        w["a_rb"] = jnp.where(incl, g[P2:, :P2], 0.0)
        w["a_rk"] = jnp.where(incl, g[P2:, P2:], 0.0)
        w["p"] = eye - w["m"]
    for w in pre.values():
        av = _mm(jnp.concatenate([w["a_ak"], w["a_rk"]], axis=0), stack(w["v"]))
        w["av"], w["rkv"] = av[:P2], av[P2:]
    for w in pre.values():
        w["m"] = _mm(w["m"], w["m"])
    for _ in range(4):
        for w in pre.values():
            pm = _mm(jnp.concatenate([w["p"], w["m"]], axis=0), w["m"])
            w["p"] = w["p"] + pm[:P2]
            w["m"] = pm[P2:]
    for w in pre.values():
        w["p"] = w["p"] + _mm(w["p"], w["m"])
    for w in pre.values():
        w["tt"] = _mm(w["p"], jnp.concatenate([stack(w["a_t"]), w["av"]], axis=1))
        w["z"] = _mm(w["p"], stack(w["bh"]), TN)
    for w in pre.values():
        ar = _mm(w["a_rb"], w["tt"])
        w["oa"] = stack(w["r_t"]) - ar[:, :P2]
        w["oc"] = w["rkv"] - ar[:, P2:]
        w["pz"] = bdiag * _mm(split_heads(w["a_t"]), w["z"], TN)
        w["kz"] = stack(w["kh"]) - _mm(w["a_ak"], w["z"], TN)
    for w in pre.values():
        w["q"] = bdiag * _mm(split_heads(w["v"]), w["kz"], TN)

    s = [s_ref[p] for p in range(NP)]
    for c, p in order:
        w = pre[c, p]
        w["s0"] = s[p]
        s[p] = s[p] * w["dec"] - _mm(s[p], w["pz"]) + w["q"]
    for c, p in order:
        w = pre[c, p]
        o_st = _mm(w["oa"], w["s0"], NT) + w["oc"]
        y_ref[c * C:(c + 1) * C, p * P2:(p + 1) * P2] = unstack(o_st)
    for p in range(NP):
        s_ref[p] = s[p]

    if rev:
        o = of_ref[...] + y_ref[...]
        bd = bd_ref[...]
        mu = _mm2(o, bd) * (1.0 / HD)
        oc = o - mu
        var = _mm2(oc * oc, bd) * (1.0 / HD)
        o = oc * lax.rsqrt(var + RW_GN_EPS) * lnw_ref[...] + lnb_ref[...]
        y_ref[...] = (o + bonus_ref[...]) * gs_ref[...]


def _rw_scan(rev, r, kk, v, lw, akk, kd, extra=()):
    rows = r.shape[0]
    nb = rows // MIX_ROWS
    order = _blk_order(rev, nb)
    P2 = 2 * RW_HD
    W = RW_HEADS * RW_HD
    blk = pl.BlockSpec((MIX_ROWS, W), lambda b: (order(b), 0))
    in_specs = [blk] * 6 + [pl.BlockSpec((MIX_ROWS, MIX_ROWS), lambda b: (0, 0))]
    args = [r, kk, v, lw, akk, kd, _seg_tri(MIX_ROWS, RW_CHUNK, rev)]
    if rev:
        o_fwd, bonus, gs, lnw, lnb, bd512 = extra
        vec = pl.BlockSpec((1, W), lambda b: (0, 0))
        in_specs += [blk, blk, blk, vec, vec, pl.BlockSpec((W, W), lambda b: (0, 0))]
        args += [o_fwd, bonus, gs, lnw, lnb, bd512]
    return pl.pallas_call(
        functools.partial(_rw_scan_kernel, rev),
        grid=(nb,),
        in_specs=in_specs,
        out_specs=blk,
        out_shape=jax.ShapeDtypeStruct((rows, W), F32),
        scratch_shapes=[pltpu.VMEM((RW_HEADS // 2, P2, P2), F32)],
        compiler_params=_cparams(("arbitrary",)),
        name="rw_rev" if rev else "rw_fwd",
    )(*args)


def _qk_prep_kernel(nq_ref, nk_ref, nv_ref, wq_ref, wkv_ref, nqn_ref, nkn_ref, wqn_ref, wkn_ref,
                    cos_ref, sin_ref, bd512_ref, bd128_ref,
                    naq_o, nak_o, nav_o, waq_o, wak_o, wav_o):
    bd512 = bd512_ref[...]
    bd128 = bd128_ref[...]

    def hnorm(x, g, bd, hd):
        return x * lax.rsqrt(_mm2(x * x, bd) * (1.0 / hd) + EPS) * g

    def rope(x, cos, sin):
        lane = lax.broadcasted_iota(jnp.int32, x.shape, 1)
        w = x.shape[1]
        partner = jnp.where((lane & 31) < 16, pltpu.roll(x, w - 16, axis=1), pltpu.roll(x, 16, axis=1))
        return x * cos + partner * sin

    def dup_heads(o_ref, x):
        lower = lax.broadcasted_iota(jnp.int32, x.shape, 1) < WA_HD
        swapped = pltpu.roll(x, WA_HD, axis=1)
        o_ref[0] = jnp.where(lower, x, swapped).astype(o_ref.dtype)
        o_ref[1] = jnp.where(lower, swapped, x).astype(o_ref.dtype)

    f32 = lambda ref: ref[...].astype(F32)
    naq_o[...] = (hnorm(f32(nq_ref), nqn_ref[...], bd512, NA_HD) * NA_HD ** -0.5).astype(BF16)
    nak_o[...] = hnorm(f32(nk_ref), nkn_ref[...], bd512, NA_HD).astype(BF16)
    nav_o[...] = nv_ref[...]
    cos = cos_ref[...]
    sin = sin_ref[...]
    wq = hnorm(f32(wq_ref), wqn_ref[...], bd512, WA_HD) * WA_HD ** -0.5
    wq = rope(wq, jnp.concatenate([cos] * 4, axis=1), jnp.concatenate([sin] * 4, axis=1))
    waq_o[...] = wq.astype(BF16)
    kv = f32(wkv_ref)
    dup_heads(wak_o, rope(hnorm(kv[:, :128], wkn_ref[...], bd128, WA_HD), cos, sin))
    dup_heads(wav_o, kv[:, 128:])


def _qk_prep(p, nqn, nkn, wqn, wkn, cos, sin, bd512, bd128):
    rows = p.shape[0]
    nb = rows // MIX_ROWS
    nab = OFF_NA // 512
    full = lambda shape: pl.BlockSpec(shape, lambda i: (0,) * len(shape))
    in_specs = [pl.BlockSpec((MIX_ROWS, 512), lambda i: (i, nab)),
                pl.BlockSpec((MIX_ROWS, 512), lambda i: (i, nab + 1)),
                pl.BlockSpec((MIX_ROWS, 512), lambda i: (i, nab + 2)),
                pl.BlockSpec((MIX_ROWS, 512), lambda i: (i, OFF_WA // 512)),
                pl.BlockSpec((MIX_ROWS, 256), lambda i: (i, (OFF_WA + 512) // 256)),
                full((1, 512)), full((1, 512)), full((1, 512)), full((1, 128)),
                pl.BlockSpec((MIX_ROWS, 128), lambda i: (i, 0)),
                pl.BlockSpec((MIX_ROWS, 128), lambda i: (i, 0)),
                full((512, 512)), full((128, 128))]
    flat = pl.BlockSpec((MIX_ROWS, 512), lambda i: (i, 0))
    flat_shape = jax.ShapeDtypeStruct((rows, 512), BF16)
    dup = pl.BlockSpec((WA_KV_HEADS, MIX_ROWS, 128), lambda i: (0, i, 0))
    dup_shape = jax.ShapeDtypeStruct((WA_KV_HEADS, rows, 128), BF16)
    return pl.pallas_call(
        _qk_prep_kernel,
        grid=(nb,),
        in_specs=in_specs,
        out_specs=[flat, flat, flat, flat, dup, dup],
        out_shape=[flat_shape, flat_shape, flat_shape, flat_shape, dup_shape, dup_shape],
        compiler_params=_cparams(("arbitrary",)),
        name="qk_prep",
    )(p, p, p, p, p, nqn, nkn, wqn, wkn, cos, sin, bd512, bd128)


def _na_bias_layout():
    wr, G, KR = NA_WIN_R, NA_GROUP, NA_KEY_ROWS
    off = (0, -(wr // 2), -(KR - G))
    out = {}
    for kind in range(3):
        for r in range(G):
            wstart = (0, r, KR - wr)[kind]
            for a in range(KR):
                inside = wstart <= a < wstart + wr
                out[kind, r, a] = (off[kind] + a - r + wr - 1) if inside else None
    return out


def _na_kernel(lc, nrows, q_ref, k_ref, v_ref, t_ref, o_ref, bias_s):
    W = GRID_W
    HD = NA_HD
    nq = NA_GROUP * W
    nk = NA_KEY_ROWS * W
    ngroups = nrows // NA_GROUP
    kc = k_ref[0:lc, :]
    vc = v_ref[0:lc, :]
    half = [lax.broadcasted_iota(jnp.int32, (1, 2 * HD), 1) // HD == hh for hh in range(2)]

    bias_s[:, :, :, 0:lc] = jnp.zeros((2, 3, nq, lc), F32)
    for (kind, r, a), ro in _na_bias_layout().items():
        c0 = lc + a * W
        for hh in range(2):
            tile = jnp.full((W, W), NEG, F32) if ro is None else t_ref[hh, ro, :, (a % 2) * W:(a % 2 + 1) * W]
            bias_s[hh, kind, r * W:(r + 1) * W, c0:c0 + W] = tile

    def attend(q, keys, vals, bias):
        out = None
        for hh in range(2):
            s = _mm(jnp.where(half[hh], q, 0), keys, NT)
            if bias is not None:
                s = s + bias(hh)
            p = jnp.exp(s - jnp.max(s, axis=-1, keepdims=True))
            o = _mm(p, jnp.where(half[hh], vals, 0)) / jnp.sum(p, axis=-1, keepdims=True)
            out = o if out is None else out + o
        return out

    o_ref[0:lc, :] = attend(q_ref[0:lc, :], kc, vc, None)

    def group(gi, carry):
        base = jnp.clip(gi * NA_GROUP - NA_WIN_R // 2, 0, nrows - NA_KEY_ROWS)
        kind = jnp.where(gi == 0, 0, jnp.where(gi == ngroups - 1, 2, 1))
        q0 = pl.multiple_of(lc + gi * nq, nq)
        k0 = pl.multiple_of(lc + base * W, W)
        keys = jnp.concatenate([kc, k_ref[pl.ds(k0, nk), :]], axis=0)
        vals = jnp.concatenate([vc, v_ref[pl.ds(k0, nk), :]], axis=0)
        o_ref[pl.ds(q0, nq), :] = attend(q_ref[pl.ds(q0, nq), :], keys, vals, lambda hh: bias_s[hh, kind])
        return carry

    lax.fori_loop(0, ngroups, group, 0, unroll=2)


def _na(q, k, v, t2, lc):
    rows = q.shape[0]
    nrows = (rows - lc) // GRID_W
    pblk = pl.BlockSpec((rows, 2 * NA_HD), lambda p: (0, p))
    return pl.pallas_call(
        functools.partial(_na_kernel, lc, nrows),
        grid=(NA_HEADS // 2,),
        in_specs=[pblk, pblk, pblk, pl.BlockSpec((2,) + t2.shape[1:], lambda p: (p, 0, 0, 0))],
        out_specs=pblk,
        out_shape=jax.ShapeDtypeStruct((rows, NA_HEADS * NA_HD), F32),
        scratch_shapes=[pltpu.VMEM((2, 3, NA_GROUP * GRID_W, lc + NA_KEY_ROWS * GRID_W), F32)],
        compiler_params=_cparams(("arbitrary",)),
        name="na_attn",
    )(q, k, v, t2)


def _wa_kernel(lc, t_len, q_ref, k_ref, v_ref, sink_ref, o_ref):
    G = WA_HEADS // WA_KV_HEADS
    B = WA_WINDOW
    nband = 3 * B
    HD = WA_HD
    b = pl.program_id(1)
    kc = k_ref[0:lc, :]
    vc = v_ref[0:lc, :]
    sink = sink_ref[...]
    lower = lax.broadcasted_iota(jnp.int32, (1, 2 * HD), 1) < HD

    def stacked_queries(r0, n):
        parts = []
        for g in range(G):
            qp = q_ref[r0:r0 + n, (g // 2) * 2 * HD:(g // 2 + 1) * 2 * HD]
            parts.append(jnp.where(lower if g % 2 == 0 else jnp.logical_not(lower), qp, 0))
        return jnp.concatenate(parts, axis=0)

    def store(r0, n, o):
        for gp in range(G // 2):
            o_ref[r0:r0 + n, gp * 2 * HD:(gp + 1) * 2 * HD] = jnp.where(
                lower, o[2 * gp * n:(2 * gp + 1) * n], o[(2 * gp + 1) * n:(2 * gp + 2) * n])

    @pl.when(b == 0)
    def _():
        sk = jnp.concatenate([jnp.broadcast_to(sink[g * B:g * B + 1, :], (lc, 1)) for g in range(G)], axis=0)
        s = _mm(stacked_queries(0, lc), kc, NT)
        m = jnp.maximum(jnp.max(s, axis=-1, keepdims=True), sk)
        e = jnp.exp(s - m)
        store(0, lc, _mm(e, vc) / (jnp.sum(e, axis=-1, keepdims=True) + jnp.exp(sk - m)))

    @pl.when(b > 0)
    def _():
        qoff = lax.broadcasted_iota(jnp.int32, (G * B, lc + nband), 0) & (B - 1)
        koff = lax.broadcasted_iota(jnp.int32, (G * B, lc + nband), 1) - lc
        for j in range(MIX_ROWS // B):
            n = (b - 1) * (MIX_ROWS // B) + j
            start = jnp.clip((n - 1) * B, 0, t_len - nband)
            k0 = pl.multiple_of(lc + start, B)
            keys = jnp.concatenate([kc, k_ref[pl.ds(k0, nband), :]], axis=0)
            vals = jnp.concatenate([vc, v_ref[pl.ds(k0, nband), :]], axis=0)
            valid = jnp.logical_or(koff < 0, jnp.abs((n * B + qoff) - (start + koff)) <= WA_WINDOW)
            s = jnp.where(valid, _mm(stacked_queries(j * B, B), keys, NT), NEG)
            m = jnp.maximum(jnp.max(s, axis=-1, keepdims=True), sink)
            p = jnp.exp(s - m)
            store(j * B, B, _mm(p, vals) / (jnp.sum(p, axis=-1, keepdims=True) + jnp.exp(sink - m)))


def _wa(q, k, v, sink_col, lc):
    rows = q.shape[0]
    G = WA_HEADS // WA_KV_HEADS
    kvblk = pl.BlockSpec((None, rows, 2 * WA_HD), lambda h, b: (h, 0, 0))
    qblk = pl.BlockSpec((MIX_ROWS, G * WA_HD), lambda h, b: (b, h))
    return pl.pallas_call(
        functools.partial(_wa_kernel, lc, rows - lc),
        grid=(WA_KV_HEADS, rows // MIX_ROWS),
        in_specs=[qblk, kvblk, kvblk,
                  pl.BlockSpec((None, G * WA_WINDOW, 1), lambda h, b: (h, 0, 0))],
        out_specs=qblk,
        out_shape=jax.ShapeDtypeStruct((rows, WA_HEADS * WA_HD), F32),
        compiler_params=_cparams(("arbitrary", "arbitrary")),
        name="wa_attn",
    )(q, k, v, sink_col)


def _block_diag(n, blk):
    idx = np.arange(n) // blk
    return jnp.asarray(idx[:, None] == idx[None, :], dtype=BF16)


def _na_bias_table(rpb):
    W, wc = GRID_W, NA_WIN_C
    j = np.arange(W)
    cstart = np.clip(j - wc // 2, 0, W - wc)
    cabs = np.arange(W)
    inwin = (cabs[None, :] >= cstart[:, None]) & (cabs[None, :] < cstart[:, None] + wc)
    cb = cabs[None, :] - j[:, None] + wc - 1
    nh, nro, nco = rpb.shape
    onehot = (cb[None] == np.arange(nco)[:, None, None]) & inwin[None]
    t = jnp.dot(rpb.reshape(nh * nro, nco), jnp.asarray(onehot.reshape(nco, W * W), F32),
                precision=lax.Precision.HIGHEST).reshape(nh, nro, W, W)
    t = t + jnp.asarray(np.where(inwin, 0.0, NEG), F32)
    return jnp.concatenate([t, t], axis=-1)


def _rope_tables(lc, t_len):
    pos = np.arange(t_len)
    n = WA_HD // 4
    inv = ROPE_BASE ** (-np.arange(n, dtype=np.float64) / n)
    ang_r = (pos // GRID_W)[:, None] * inv[None, :]
    ang_c = (pos % GRID_W)[:, None] * inv[None, :]
    cos = np.concatenate([np.cos(ang_r)] * 2 + [np.cos(ang_c)] * 2, axis=1)
    sin = np.concatenate([-np.sin(ang_r), np.sin(ang_r), -np.sin(ang_c), np.sin(ang_c)], axis=1)
    cos = np.concatenate([np.ones((lc, WA_HD)), cos], axis=0)
    sin = np.concatenate([np.zeros((lc, WA_HD)), sin], axis=0)
    return (jnp.asarray(np.tile(cos, (1, 2)), dtype=F32), jnp.asarray(np.tile(sin, (1, 2)), dtype=F32))


def _cast_kernel(x_ref, o_ref):
    o_ref[...] = x_ref[...].astype(o_ref.dtype)


def _cast_bf16(w, layer, bk, bn, src_col_block=None):
    _, k, n = w.shape
    col = src_col_block or (lambda j: j)
    return pl.pallas_call(
        _cast_kernel,
        grid=(k // bk, n // bn),
        in_specs=[pl.BlockSpec((None, bk, bn), lambda i, j: (layer, i, col(j)))],
        out_specs=pl.BlockSpec((bk, bn), lambda i, j: (i, j)),
        out_shape=jax.ShapeDtypeStruct((k, n), BF16),
        compiler_params=_cparams(("arbitrary", "arbitrary")),
        name="cast_bf16",
    )(w)


W_IN_BLK = 256
_W_IN_SRC = np.concatenate([np.arange(10, 18), np.arange(0, 10), np.arange(19, 25), np.arange(28, 60),
                            np.arange(25, 28), np.arange(18, 19)])


def _w_in_src_block(j):
    starts = (0, 8, 18, 24, 56, 59)
    src = j + int(_W_IN_SRC[0])
    for s in starts[1:]:
        src = jnp.where(j >= s, j + int(_W_IN_SRC[s]) - s, src)
    return src


def _lora_pad(w, row0):
    out = jnp.zeros((2, 256, w.shape[-1]), F32)
    for d in range(2):
        out = out.at[d, row0 + 64 * d:row0 + 64 * (d + 1)].set(w[d])
    return out


def _token_mixing(p, lc, layer, hg_lb, hg_norm, rw_shift, rw_w0, rw_w2, rw_a0, rw_a2, rw_kk, rw_ka, rw_rk,
                  rw_ln_w, rw_ln_b, na_qn, na_kn, na_rpb, wa_qn, wa_kn, wa_sink, tables):
    rows = p.shape[0]
    bd512, bd128, cos, sin = tables
    cum = jnp.cumsum(jax.nn.softmax(hg_lb.astype(F32), axis=1), axis=1)
    lbs = cum[:, layer] - cum[:, 0]
    o_f = _gla_dir(p, lbs[0:1], False)
    y_a = _gla_dir(p, lbs[1:2], True, o_f, hg_norm[None])
    prep = _rw_prep(p, rw_shift[:, :2048], rw_shift[:, 2048:], rw_w0, _lora_pad(rw_w2, 0), rw_a0,
                    _lora_pad(rw_a2, 128), rw_kk[None], rw_ka[None], rw_rk.reshape(1, -1), bd512)
    r, kk, v, gs, bonus, lw0, lw1, akk0, akk1, kd0, kd1 = prep
    o_f = _rw_scan(False, r, kk, v, lw0, akk0, kd0)
    y_b = _rw_scan(True, r, kk, v, lw1, akk1, kd1,
                   (o_f, bonus, gs, rw_ln_w[None], rw_ln_b[None], bd512))
    tile = lambda g, n: jnp.tile(g, n)[None]
    naq, nak, nav, waq, wak, wav = _qk_prep(p, tile(na_qn, 8), tile(na_kn, 8), tile(wa_qn, 8),
                                            tile(wa_kn, 2), cos, sin, bd512, bd128)
    y_c = _na(naq, nak, nav, _na_bias_table(na_rpb), lc)
    G = WA_HEADS // WA_KV_HEADS
    sink_col = jnp.repeat(wa_sink.reshape(WA_KV_HEADS, G), WA_WINDOW, axis=1)[..., None]
    y_d = _wa(waq, wak, wav, sink_col, lc)
    return y_a, y_b, y_c, y_d


def kernel(x, c, ctx, c_ctx, ada_w, ada_b, norm_ffn1, norm_mix, norm_ffn2, ffn1_wi, ffn1_wo, ffn2_wi, ffn2_wo, w_in, hg_lb, hg_norm, rw_shift, rw_w0, rw_w2, rw_a0, rw_a2, rw_kk, rw_ka, rw_rk, rw_ln_w, rw_ln_b, na_qn, na_kn, na_rpb, wa_qn, wa_kn, wa_sink, w_branch, w_out):
    assert x.shape[0] == 1 and ctx.shape[1] == MIX_ROWS
    lc = ctx.shape[1]
    t_len = x.shape[1]
    xa = jnp.concatenate([ctx[0], x[0]], axis=0)
    cc_t = jnp.stack([c[0], c_ctx], axis=1)
    mods = _ada_mods(cc_t, ada_w, ada_b).reshape(DEPTH, 2, N_MOD, D)
    tables = (_block_diag(512, 64), _block_diag(128, 64)) + _rope_tables(lc, t_len)
    for l in range(DEPTH):
        m = mods[l]
        xa = _ffn(xa, m[:, 0:3], norm_ffn1[l][None], _cast_bf16(ffn1_wi, l, D, 1024),
                  _cast_bf16(ffn1_wo, l, 512, D), lc)
        p = _win(xa, m[:, 3:6], norm_mix[l][None], _cast_bf16(w_in, l, D, W_IN_BLK, _w_in_src_block), lc)
        yb = _token_mixing(p, lc, l, hg_lb, hg_norm[l], rw_shift[l], rw_w0[l], rw_w2[l], rw_a0[l], rw_a2[l],
                           rw_kk[l], rw_ka[l], rw_rk[l], rw_ln_w[l], rw_ln_b[l], na_qn[l], na_kn[l],
                           na_rpb[l], wa_qn[l], wa_kn[l], wa_sink[l], tables)
        wb = _cast_bf16(w_branch.reshape(DEPTH, N_BRANCH * BR_W, D), l, 1024, D).reshape(N_BRANCH, BR_W, D)
        xa = _merge(xa, m[:, 3:6], yb, p, wb, _cast_bf16(w_out, l, 1024, D), lc)
        xa = _ffn(xa, m[:, 6:9], norm_ffn2[l][None], _cast_bf16(ffn2_wi, l, D, 1024),
                  _cast_bf16(ffn2_wo, l, 512, D), lc)
    return xa[lc:][None]
```

```python
import functools

import numpy as np
import jax
import jax.numpy as jnp
from jax import lax
from jax.experimental import pallas as pl
from jax.experimental.pallas import tpu as pltpu

F32 = jnp.float32
BF16 = jnp.bfloat16

D = 2048
DEPTH = 2
GRID_W = 64
EPS = 1e-6
D_FF = 5632
N_MOD = 9
N_BRANCH = 4
BR_W = 512
HG_HEADS, HG_DK = 4, 128
RW_HEADS, RW_HD = 8, 64
RW_LORA = 64
RW_GN_EPS = 64e-5
NA_HEADS, NA_HD = 8, 64
NA_WIN_R, NA_WIN_C = 8, 16
NA_GROUP = 4
NA_KEY_ROWS = NA_WIN_R + NA_GROUP - 1
WA_HEADS, WA_KV_HEADS, WA_HD = 8, 2, 64
WA_WINDOW = 128
ROPE_BASE = 10000.0

OFF_RW, OFF_HG, OFF_NA, OFF_GL, OFF_WA, OFF_LORA = 0, 2048, 4608, 6144, 14336, 15104
P_TOTAL = 15360

MIX_ROWS = 256
HALO = 16
GLA_CHUNK = 16
RW_CHUNK = 64
NEG = -1e30
LOG2E = 1.4426950408889634
VMEM_LIMIT = 60 * 1024 * 1024

NT = (((1,), (1,)), ((), ()))
TN = (((0,), (0,)), ((), ()))


def _cparams(sem):
    return pltpu.CompilerParams(dimension_semantics=sem, vmem_limit_bytes=VMEM_LIMIT)


def _mm(a, b, dims=None):
    a = a.astype(BF16)
    b = b.astype(BF16)
    if dims is None:
        return jnp.dot(a, b, preferred_element_type=F32)
    return lax.dot_general(a, b, dims, preferred_element_type=F32)


def _mm2(x, w_bf16):
    hi = x.astype(BF16)
    lo = (x - hi.astype(F32)).astype(BF16)
    return (jnp.dot(hi, w_bf16, preferred_element_type=F32)
            + jnp.dot(lo, w_bf16, preferred_element_type=F32))


def _mm_f32(a, b):
    a_hi, b_hi = a.astype(BF16), b.astype(BF16)
    a_lo = (a - a_hi.astype(F32)).astype(BF16)
    b_lo = (b - b_hi.astype(F32)).astype(BF16)
    dot = functools.partial(jnp.dot, preferred_element_type=F32)
    return dot(a_hi, b_hi) + (dot(a_hi, b_lo) + dot(a_lo, b_hi))


def _sigmoid(x):
    return 1.0 / (1.0 + jnp.exp(-x))


def _silu(x):
    return x * _sigmoid(x)


def _seg_cumsum(x, tri):
    w = x.shape[1]
    hi = x.astype(BF16)
    r1 = x - hi.astype(F32)
    mid = r1.astype(BF16)
    lo = (r1 - mid.astype(F32)).astype(BF16)
    y = jnp.dot(tri, jnp.concatenate([hi, mid, lo], axis=1), preferred_element_type=F32)
    return y[:, :w] + y[:, w:2 * w] + y[:, 2 * w:]


def _seg_tri(rows, seg, rev):
    r = np.arange(rows)
    same = (r[:, None] // seg) == (r[None, :] // seg)
    tri = (r[None, :] >= r[:, None]) if rev else (r[None, :] <= r[:, None])
    return jnp.asarray(same & tri, dtype=BF16)


def _ada_kernel(cc_ref, w_ref, b_ref, o_ref):
    s = _silu(cc_ref[...])
    w = w_ref[...]
    r0 = jnp.sum(w * s[:, 0:1], axis=0, keepdims=True)
    r1 = jnp.sum(w * s[:, 1:2], axis=0, keepdims=True)
    o_ref[...] = jnp.concatenate([r0, r1], axis=0) + b_ref[...]


def _ada_mods(cc_t, ada_w, ada_b):
    tn = 512
    nmod = ada_w.shape[-1]
    return pl.pallas_call(
        _ada_kernel,
        grid=(DEPTH, nmod // tn),
        in_specs=[pl.BlockSpec((D, 2), lambda l, j: (0, 0)),
                  pl.BlockSpec((None, D, tn), lambda l, j: (l, 0, j)),
                  pl.BlockSpec((None, 1, tn), lambda l, j: (l, 0, j))],
        out_specs=pl.BlockSpec((None, 2, tn), lambda l, j: (l, 0, j)),
        out_shape=jax.ShapeDtypeStruct((DEPTH, 2, nmod), F32),
        compiler_params=_cparams(("arbitrary", "arbitrary")),
        name="ada_mod",
    )(cc_t, ada_w, ada_b.reshape(DEPTH, 1, nmod))


def _cast_job(w, layer, bk, bn, src_col_block=None):
    return (w, layer, bk, bn, src_col_block)


def _cast_job_specs(jobs, grid):
    in_specs, out_specs, out_shapes, args = [], [], [], []
    for w, layer, bk, bn, src_col in jobs:
        _, k, n = w.shape
        ncol = n // bn
        nblk = (k // bk) * ncol
        assert nblk <= grid[0] * grid[1], (nblk, grid)
        col = src_col or (lambda c: c)

        def blk(i, j, nblk=nblk):
            return jnp.minimum(i * grid[1] + j, nblk - 1)

        in_specs.append(pl.BlockSpec(
            (None, bk, bn), lambda i, j, blk=blk, ncol=ncol, col=col, layer=layer:
            (layer, blk(i, j) // ncol, col(blk(i, j) % ncol))))
        out_specs.append(pl.BlockSpec((bk, bn), lambda i, j, blk=blk, ncol=ncol: (blk(i, j) // ncol, blk(i, j) % ncol)))
        out_shapes.append(jax.ShapeDtypeStruct((k, n), BF16))
        args.append(w)
    return in_specs, out_specs, out_shapes, args


def _run_cast_jobs(src_refs, dst_refs):
    for src, dst in zip(src_refs, dst_refs):
        dst[...] = src[...].astype(dst.dtype)


def _row_is_ctx(i, tm, lc):
    return (i * tm + lax.broadcasted_iota(jnp.int32, (tm, 1), 0)) < lc


def _mod_pick(mod_ref, j, is_ctx):
    return jnp.where(is_ctx, mod_ref[1, j:j + 1, :], mod_ref[0, j:j + 1, :])


def _store_modulated(xn_ref, x_ref, g_ref, mod_ref, i, tm, lc):
    def normed():
        x = x_ref[...]
        return x * lax.rsqrt(jnp.mean(x * x, axis=-1, keepdims=True) + EPS)

    @pl.when(i * tm < lc)
    def _():
        is_ctx = _row_is_ctx(i, tm, lc)
        y = normed() * g_ref[...]
        xn_ref[...] = (y * (1.0 + _mod_pick(mod_ref, 1, is_ctx)) + _mod_pick(mod_ref, 0, is_ctx)).astype(BF16)

    @pl.when(i * tm >= lc)
    def _():
        gain = g_ref[...] * (1.0 + mod_ref[0, 1:2, :])
        xn_ref[...] = (normed() * gain + mod_ref[0, 0:1, :]).astype(BF16)


def _ffn_kernel(lc, tm, nf, njobs, x_ref, mod_ref, g_ref, wa_ref, wb_ref, wo_ref, *rest):
    src_refs, (o_ref, *dst_refs, xn_ref) = rest[:njobs], rest[njobs:]
    i = pl.program_id(0)
    j = pl.program_id(1)
    _run_cast_jobs(src_refs, dst_refs)

    @pl.when(j == 0)
    def _():
        _store_modulated(xn_ref, x_ref, g_ref, mod_ref, i, tm, lc)
        o_ref[...] = jnp.zeros_like(o_ref)

    xn = xn_ref[...]
    a = jnp.dot(xn, wa_ref[...], preferred_element_type=F32)
    b = jnp.dot(xn, wb_ref[...], preferred_element_type=F32)
    h = (_silu(a) * b).astype(BF16)
    o_ref[...] += jnp.dot(h, wo_ref[...], preferred_element_type=F32)

    @pl.when(j == nf - 1)
    def _():
        o_ref[...] = x_ref[...] + 0.5 * _mod_pick(mod_ref, 2, _row_is_ctx(i, tm, lc)) * o_ref[...]


def _dense_tm(rows):
    for tm in (768, 512, 256):
        if rows % tm == 0:
            return tm
    raise ValueError(rows)


def _ffn(x, mod3, g, wi, wo, lc, jobs=()):
    rows = x.shape[0]
    tm = _dense_tm(rows)
    tf = 512
    nf = D_FF // tf
    grid = (rows // tm, nf)
    job_in, job_out, job_shape, job_args = _cast_job_specs(jobs, grid)
    return pl.pallas_call(
        functools.partial(_ffn_kernel, lc, tm, nf, len(jobs)),
        grid=grid,
        in_specs=[pl.BlockSpec((tm, D), lambda i, j: (i, 0)),
                  pl.BlockSpec((2, 3, D), lambda i, j: (0, 0, 0)),
                  pl.BlockSpec((1, D), lambda i, j: (0, 0)),
                  pl.BlockSpec((D, tf), lambda i, j: (0, j)),
                  pl.BlockSpec((D, tf), lambda i, j: (0, j + nf)),
                  pl.BlockSpec((tf, D), lambda i, j: (j, 0))] + job_in,
        out_specs=[pl.BlockSpec((tm, D), lambda i, j: (i, 0))] + job_out,
        out_shape=[jax.ShapeDtypeStruct((rows, D), F32)] + job_shape,
        scratch_shapes=[pltpu.VMEM((tm, D), BF16)],
        compiler_params=_cparams(("arbitrary", "arbitrary")),
        name="ffn",
    )(x, mod3, g, wi, wi, wo, *job_args)


def _win_kernel(lc, tm, njobs, x_ref, mod_ref, g_ref, w_ref, *rest):
    src_refs, (o_ref, *dst_refs, xn_ref) = rest[:njobs], rest[njobs:]
    i = pl.program_id(0)
    _run_cast_jobs(src_refs, dst_refs)

    @pl.when(pl.program_id(1) == 0)
    def _():
        _store_modulated(xn_ref, x_ref, g_ref, mod_ref, i, tm, lc)

    o_ref[...] = jnp.dot(xn_ref[...], w_ref[...], preferred_element_type=F32).astype(o_ref.dtype)


def _win(x, mod3, g, w, lc, jobs=()):
    rows = x.shape[0]
    tm = _dense_tm(rows)
    tn = 1536
    grid = (rows // tm, P_TOTAL // tn)
    job_in, job_out, job_shape, job_args = _cast_job_specs(jobs, grid)
    return pl.pallas_call(
        functools.partial(_win_kernel, lc, tm, len(jobs)),
        grid=grid,
        in_specs=[pl.BlockSpec((tm, D), lambda i, j: (i, 0)),
                  pl.BlockSpec((2, 3, D), lambda i, j: (0, 0, 0)),
                  pl.BlockSpec((1, D), lambda i, j: (0, 0)),
                  pl.BlockSpec((D, tn), lambda i, j: (0, j))] + job_in,
        out_specs=[pl.BlockSpec((tm, tn), lambda i, j: (i, j))] + job_out,
        out_shape=[jax.ShapeDtypeStruct((rows, P_TOTAL), BF16)] + job_shape,
        scratch_shapes=[pltpu.VMEM((tm, D), BF16)],
        compiler_params=_cparams(("arbitrary", "arbitrary")),
        name="w_in",
    )(x, mod3, g, w, *job_args)


def _merge_kernel(lc, tm, x_ref, mod_ref, ya_ref, yb_ref, yc_ref, yd_ref, gl_ref, wb_ref, wo_ref, o_ref, acc_ref):
    i = pl.program_id(0)
    n = pl.program_id(1)

    @pl.when(n == 0)
    def _():
        acc_ref[...] = jnp.zeros_like(acc_ref)

    for nn, y_ref in enumerate((ya_ref, yb_ref, yc_ref, yd_ref)):
        @pl.when(n == nn)
        def _():
            proj = jnp.dot(y_ref[...].astype(BF16), wb_ref[nn], preferred_element_type=F32)
            acc_ref[...] += _sigmoid(gl_ref[...].astype(F32)) * proj

    @pl.when(n == N_BRANCH - 1)
    def _():
        is_ctx = _row_is_ctx(i, tm, lc)
        y = jnp.dot(acc_ref[...].astype(BF16), wo_ref[...], preferred_element_type=F32)
        o_ref[...] = x_ref[...] + _mod_pick(mod_ref, 2, is_ctx) * y


def _merge(x, mod3, ys, p, wb, wo, lc):
    rows = x.shape[0]
    tm = 384 if rows % 384 == 0 else MIX_ROWS
    glb = OFF_GL // D
    yspec = pl.BlockSpec((tm, BR_W), lambda i, n: (i, 0))
    return pl.pallas_call(
        functools.partial(_merge_kernel, lc, tm),
        grid=(rows // tm, N_BRANCH),
        in_specs=[pl.BlockSpec((tm, D), lambda i, n: (i, 0)),
                  pl.BlockSpec((2, 3, D), lambda i, n: (0, 0, 0)),
                  yspec, yspec, yspec, yspec,
                  pl.BlockSpec((tm, D), lambda i, n: (i, glb + n)),
                  pl.BlockSpec((N_BRANCH, BR_W, D), lambda i, n: (0, 0, 0)),
                  pl.BlockSpec((D, D), lambda i, n: (0, 0))],
        out_specs=pl.BlockSpec((tm, D), lambda i, n: (i, 0)),
        out_shape=jax.ShapeDtypeStruct((rows, D), F32),
        scratch_shapes=[pltpu.VMEM((tm, D), F32)],
        compiler_params=_cparams(("arbitrary", "arbitrary")),
        name="merge",
    )(x, mod3, *ys, p, wb, wo)


def _gla_kernel(rev, *refs):
    if rev:
        q_ref, f_ref, i_ref, lb_ref, tri_ref, of_ref, g_ref, nw_ref, y_ref, st_ref = refs
    else:
        q_ref, f_ref, i_ref, lb_ref, tri_ref, y_ref, st_ref = refs
    C = GLA_CHUNK
    R = MIX_ROWS
    NH, DK = HG_HEADS, HG_DK

    @pl.when(pl.program_id(0) == 0)
    def _():
        st_ref[...] = jnp.zeros_like(st_ref)

    lb = lb_ref[...]
    f_all = lb + (1.0 - lb) * _sigmoid(f_ref[...].astype(F32))
    b_all = _seg_cumsum(jnp.log(f_all), tri_ref[...])
    SUB = C // 2
    sub_pos = lax.broadcasted_iota(jnp.int32, (1, SUB, 1), 1)
    chunks = list(range(R // C - 1, -1, -1) if rev else range(R // C))

    heads = []
    for h in range(NH):
        hs = slice(h * DK, (h + 1) * DK)
        b = b_all[:, hs]
        k = 1.0 - f_all[:, hs]
        q = q_ref[:, hs].astype(F32)
        v = i_ref[:, hs].astype(F32)
        def halves(t):
            t3 = t.reshape(R // C, C, DK)
            lo, hi = t3[:, :SUB], t3[:, SUB:]
            return (hi, lo) if rev else (lo, hi)

        def pair_term(qq, bq, keys, ok):
            bk_, kk_, vk_ = keys
            diff = bq - bk_
            e = jnp.exp2(diff if ok is None else jnp.where(ok, diff, NEG))
            return jnp.sum(qq * kk_ * e, axis=-1, keepdims=True) * vk_

        (qn, qf), (bn, bf), (kn, kf), (vn, vf) = halves(q), halves(b * LOG2E), halves(k), halves(v)
        near, far = (bn, kn, vn), (bf, kf, vf)
        o_n = jnp.sum(qn * kn, axis=-1, keepdims=True) * vn
        o_f = jnp.sum(qf * kf, axis=-1, keepdims=True) * vf + pair_term(qf, bf, near, None)
        for d in range(1, SUB):
            sh = (SUB - d) if rev else d
            ok = (sub_pos < SUB - d) if rev else (sub_pos >= d)
            near_r = tuple(pltpu.roll(t, sh, axis=1) for t in near)
            far_r = tuple(jnp.where(ok, pltpu.roll(t, sh, axis=1), tn) for t, tn in zip(far, near_r))
            o_n = o_n + pair_term(qn, bn, near_r, ok)
            o_f = o_f + pair_term(qf, bf, far_r, None) + pair_term(qf, bf, near_r, ok)
        o = jnp.concatenate([o_f, o_n] if rev else [o_n, o_f], axis=1).reshape(R, DK)
        upd, dec = {}, {}
        for c in chunks:
            sl = slice(c * C, (c + 1) * C)
            b_end = b[sl][0:1] if rev else b[sl][C - 1:C]
            dec[c] = jnp.exp(b_end)
            upd[c] = _mm(v[sl], k[sl] * jnp.exp(b_end - b[sl]), TN)
        heads.append(dict(o=o, qb=q * jnp.exp(b), upd=upd, dec=dec, st=st_ref[h]))

    for c in chunks:
        for w in heads:
            w[c] = w["st"]
            w["st"] = w["st"] * w["dec"][c] + w["upd"][c]
    outs = []
    for h, w in enumerate(heads):
        st_ref[h] = w["st"]
        inter = {c: _mm(w["qb"][c * C:(c + 1) * C], w[c], NT) for c in chunks}
        outs.append(w["o"] + jnp.concatenate([inter[c] for c in range(R // C)], axis=0))

    for h, o in enumerate(outs):
        hs = slice(h * DK, (h + 1) * DK)
        if rev:
            o = of_ref[:, hs] + o
            o = o * lax.rsqrt(jnp.mean(o * o, axis=-1, keepdims=True) + EPS) * nw_ref[:, hs]
            o = o * _silu(g_ref[:, hs].astype(F32))
        y_ref[:, hs] = o


def _blk_order(rev, nb):
    if rev:
        return lambda b: jnp.where(b == 0, 0, nb - b)
    return lambda b: b


def _gla_dir(p, lb, rev, o_fwd=None, norm_w=None):
    rows = p.shape[0]
    nb = rows // MIX_ROWS
    order = _blk_order(rev, nb)
    W = HG_HEADS * HG_DK
    cb = OFF_HG // W

    def col(n):
        return pl.BlockSpec((MIX_ROWS, W), lambda b: (order(b), cb + n))

    vec = pl.BlockSpec((1, W), lambda b: (0, 0))
    in_specs = [col(0), col(2 if rev else 1), col(3), vec,
                pl.BlockSpec((MIX_ROWS, MIX_ROWS), lambda b: (0, 0))]
    args = [p, p, p, lb, _seg_tri(MIX_ROWS, GLA_CHUNK, rev)]
    if rev:
        in_specs += [pl.BlockSpec((MIX_ROWS, W), lambda b: (order(b), 0)), col(4), vec]
        args += [o_fwd, p, norm_w]
    return pl.pallas_call(
        functools.partial(_gla_kernel, rev),
        grid=(nb,),
        in_specs=in_specs,
        out_specs=pl.BlockSpec((MIX_ROWS, W), lambda b: (order(b), 0)),
        out_shape=jax.ShapeDtypeStruct((rows, W), F32),
        scratch_shapes=[pltpu.VMEM((HG_HEADS, HG_DK, HG_DK), F32)],
        compiler_params=_cparams(("arbitrary",)),
        name="gla_rev" if rev else "gla_fwd",
    )(*args)


def _rw_prep_kernel(nb, main_ref, lora_ref, pm_ref, nm_ref, pl_ref, nl_ref, tm_ref, tl_ref,
                    w0_ref, w2_ref, a0_ref, a2_ref, kkw_ref, ka_ref, rk_ref, bd_ref,
                    r_o, kk_o, v_o, gs_o, bonus_o, lw0_o, lw1_o, akk0_o, akk1_o, kd0_o, kd1_o):
    i = pl.program_id(0)
    has_prev = (i >= 2).astype(F32)
    has_next = jnp.logical_and(i != 0, i != nb - 1).astype(F32)
    rowi = lax.broadcasted_iota(jnp.int32, (MIX_ROWS, 1), 0)

    def shift(x, prev_blk, next_blk, taps):
        up = jnp.where(rowi == 0, prev_blk[HALO - 1:HALO, :] * has_prev, pltpu.roll(x, 1, axis=0))
        dn = jnp.where(rowi == MIX_ROWS - 1, next_blk[0:1, :] * has_next,
                       pltpu.roll(x, MIX_ROWS - 1, axis=0))
        return taps[0:1] * up + taps[1:2] * x + taps[2:3] * dn

    f32 = lambda ref: ref[...].astype(F32)
    main = shift(f32(main_ref), f32(pm_ref), f32(nm_ref), tm_ref[...])
    lora = shift(f32(lora_ref), f32(pl_ref), f32(nl_ref), tl_ref[...])
    W = RW_HEADS * RW_HD
    r, k, v, g = (main[:, n * W:(n + 1) * W] for n in range(4))
    bd = bd_ref[...]
    kk = k * kkw_ref[...]
    kk = kk * lax.rsqrt(_mm2(kk * kk, bd) + EPS)
    tl = jnp.tanh(lora)
    kds = []
    for d, (lw_o, akk_o, kd_o) in enumerate(((lw0_o, akk0_o, kd0_o), (lw1_o, akk1_o, kd1_o))):
        z = -(w0_ref[d:d + 1, :] + _mm_f32(tl, w2_ref[d]))
        softplus = jnp.maximum(z, 0.0) + jnp.log(1.0 + jnp.exp(-jnp.abs(z)))
        lw_o[...] = -jnp.exp(-softplus - 0.5)
        a = _sigmoid(a0_ref[d:d + 1, :] + _mm_f32(lora, a2_ref[d]))
        kd = k * (1.0 + (a - 1.0) * ka_ref[...])
        kds.append(kd)
        kd_o[...] = kd.astype(kd_o.dtype)
        akk_o[...] = (a * kk).astype(akk_o.dtype)
    r_o[...] = r.astype(r_o.dtype)
    kk_o[...] = kk.astype(kk_o.dtype)
    v_o[...] = v.astype(v_o.dtype)
    gs_o[...] = _sigmoid(g)
    bonus_o[...] = _mm2(r * (kds[0] + kds[1]) * rk_ref[...], bd) * v


def _rw_prep(p, taps_m, taps_l, w0, w2p, a0, a2p, kkw, ka, rk, bd512):
    rows = p.shape[0]
    nb = rows // MIX_ROWS
    W = RW_HEADS * RW_HD
    n8 = rows // HALO
    lb = OFF_LORA // 256
    per = MIX_ROWS // HALO
    full = lambda shape: pl.BlockSpec(shape, lambda i: (0,) * len(shape))
    in_specs = [pl.BlockSpec((MIX_ROWS, 4 * W), lambda i: (i, 0)),
                pl.BlockSpec((MIX_ROWS, 256), lambda i: (i, lb)),
                pl.BlockSpec((HALO, 4 * W), lambda i: (jnp.maximum(i * per - 1, 0), 0)),
                pl.BlockSpec((HALO, 4 * W), lambda i: (jnp.minimum((i + 1) * per, n8 - 1), 0)),
                pl.BlockSpec((HALO, 256), lambda i: (jnp.maximum(i * per - 1, 0), lb)),
                pl.BlockSpec((HALO, 256), lambda i: (jnp.minimum((i + 1) * per, n8 - 1), lb)),
                full((3, 4 * W)), full((3, 256)),
                full((2, W)), full((2, 256, W)), full((2, W)), full((2, 256, W)),
                full((1, W)), full((1, W)), full((1, W)), full((W, W))]
    f32_out = jax.ShapeDtypeStruct((rows, W), F32)
    mxu_out = jax.ShapeDtypeStruct((rows, W), BF16)
    return pl.pallas_call(
        functools.partial(_rw_prep_kernel, nb),
        grid=(nb,),
        in_specs=in_specs,
        out_specs=[pl.BlockSpec((MIX_ROWS, W), lambda i: (i, 0))] * 11,
        out_shape=[mxu_out] * 3 + [f32_out] * 4 + [mxu_out] * 4,
        compiler_params=_cparams(("arbitrary",)),
        name="rw_prep",
    )(p, p, p, p, p, p, taps_m, taps_l, w0, w2p, a0, a2p, kkw, ka, rk, bd512)


def _rw_scan_kernel(rev, *refs):
    if rev:
        (r_ref, kk_ref, v_ref, lw_ref, akk_ref, kd_ref, tri_ref, of_ref, bonus_ref, gs_ref,
         lnw_ref, lnb_ref, bd_ref, y_ref, s_ref) = refs
    else:
        r_ref, kk_ref, v_ref, lw_ref, akk_ref, kd_ref, tri_ref, y_ref, s_ref = refs
    C = RW_CHUNK
    HD = RW_HD
    P2 = 2 * HD

    NP = RW_HEADS // 2

    @pl.when(pl.program_id(0) == 0)
    def _():
        s_ref[...] = jnp.zeros_like(s_ref)

    lw = lw_ref[...]
    cum = _seg_cumsum(lw, tri_ref[...])
    cum_prev = cum - lw

    lane = lax.broadcasted_iota(jnp.int32, (1, P2), 1)
    h0 = lane < HD
    ri = lax.broadcasted_iota(jnp.int32, (P2, P2), 0)
    ci = lax.broadcasted_iota(jnp.int32, (P2, P2), 1)
    same = (ri // C) == (ci // C)
    rt, cs = ri % C, ci % C
    strict = jnp.logical_and(same, (rt < cs) if rev else (rt > cs))
    incl = jnp.logical_and(same, (rt <= cs) if rev else (rt >= cs))
    eye = (ri == ci).astype(F32)
    bdiag = ((ri // HD) == (ci // HD)).astype(F32)

    def stack(x):
        return jnp.concatenate([x, x], axis=0)

    def split_heads(x):
        return jnp.concatenate([jnp.where(h0, x, 0.0), jnp.where(h0, 0.0, x)], axis=0)

    def unstack(x):
        return jnp.where(h0, x[:C], x[C:])

    chunks = list(range(MIX_ROWS // C - 1, -1, -1) if rev else range(MIX_ROWS // C))
    order = [(c, p) for c in chunks for p in range(NP)]

    pre = {}
    for c, p in order:
        sl = slice(c * C, (c + 1) * C)
        pp = slice(p * P2, (p + 1) * P2)
        cum_c = cum[sl, pp]
        cend = cum_c[0:1] if rev else cum_c[C - 1:C]
        e_neg = jnp.exp(-cum_c)
        e_end = jnp.exp(cend - cum_c)
        a_t = kk_ref[sl, pp].astype(F32) * jnp.exp(cum_prev[sl, pp])
        r_t = r_ref[sl, pp].astype(F32) * jnp.exp(cum_c)
        akk = akk_ref[sl, pp].astype(F32)
        kd = kd_ref[sl, pp].astype(F32)
        v = v_ref[sl, pp]
        pre[c, p] = dict(a_t=a_t, r_t=r_t, v=v, kh=kd * e_end, bh=akk * e_end, dec=jnp.exp(cend),
                         lhs=jnp.concatenate([split_heads(a_t), split_heads(r_t)], axis=0),
                         rhs=jnp.concatenate([stack(akk * e_neg), stack(kd * e_neg)], axis=0))
    for w in pre.values():
        g = _mm(w["lhs"], w["rhs"], NT)
        w["m"] = jnp.where(strict, g[:P2, :P2], 0.0)
        w["a_ak"] = jnp.where(strict, g[:P2, P2:], 0.0)
        w["a_rb"] = jnp.where(incl, g[P2:, :P2], 0.0)
        w["a_rk"] = jnp.where(incl, g[P2:, P2:], 0.0)
        w["p"] = eye - w["m"]
    for w in pre.values():
        av = _mm(jnp.concatenate([w["a_ak"], w["a_rk"]], axis=0), stack(w["v"]))
        w["av"], w["rkv"] = av[:P2], av[P2:]
    for w in pre.values():
        w["m"] = _mm(w["m"], w["m"])
    for _ in range(4):
        for w in pre.values():
            pm = _mm(jnp.concatenate([w["p"], w["m"]], axis=0), w["m"])
            w["p"] = w["p"] + pm[:P2]
            w["m"] = pm[P2:]
    for w in pre.values():
        w["p"] = w["p"] + _mm(w["p"], w["m"])
    for w in pre.values():
        w["tt"] = _mm(w["p"], jnp.concatenate([stack(w["a_t"]), w["av"]], axis=1))
        w["z"] = _mm(w["p"], stack(w["bh"]), TN)
    for w in pre.values():
        ar = _mm(w["a_rb"], w["tt"])
        w["oa"] = stack(w["r_t"]) - ar[:, :P2]
        w["oc"] = w["rkv"] - ar[:, P2:]
        w["pz"] = bdiag * _mm(split_heads(w["a_t"]), w["z"], TN)
        w["kz"] = stack(w["kh"]) - _mm(w["a_ak"], w["z"], TN)
    for w in pre.values():
        w["q"] = bdiag * _mm(split_heads(w["v"]), w["kz"], TN)

    s = [s_ref[p] for p in range(NP)]
    for c, p in order:
        w = pre[c, p]
        w["s0"] = s[p]
        s[p] = s[p] * w["dec"] - _mm(s[p], w["pz"]) + w["q"]
    for c, p in order:
        w = pre[c, p]
        o_st = _mm(w["oa"], w["s0"], NT) + w["oc"]
        y_ref[c * C:(c + 1) * C, p * P2:(p + 1) * P2] = unstack(o_st)
    for p in range(NP):
        s_ref[p] = s[p]

    if rev:
        o = of_ref[...] + y_ref[...]
        bd = bd_ref[...]
        mu = _mm2(o, bd) * (1.0 / HD)
        oc = o - mu
        var = _mm2(oc * oc, bd) * (1.0 / HD)
        o = oc * lax.rsqrt(var + RW_GN_EPS) * lnw_ref[...] + lnb_ref[...]
        y_ref[...] = (o + bonus_ref[...]) * gs_ref[...]


def _rw_scan(rev, r, kk, v, lw, akk, kd, extra=()):
    rows = r.shape[0]
    nb = rows // MIX_ROWS
    order = _blk_order(rev, nb)
    P2 = 2 * RW_HD
    W = RW_HEADS * RW_HD
    blk = pl.BlockSpec((MIX_ROWS, W), lambda b: (order(b), 0))
    in_specs = [blk] * 6 + [pl.BlockSpec((MIX_ROWS, MIX_ROWS), lambda b: (0, 0))]
    args = [r, kk, v, lw, akk, kd, _seg_tri(MIX_ROWS, RW_CHUNK, rev)]
    if rev:
        o_fwd, bonus, gs, lnw, lnb, bd512 = extra
        vec = pl.BlockSpec((1, W), lambda b: (0, 0))
        in_specs += [blk, blk, blk, vec, vec, pl.BlockSpec((W, W), lambda b: (0, 0))]
        args += [o_fwd, bonus, gs, lnw, lnb, bd512]
    return pl.pallas_call(
        functools.partial(_rw_scan_kernel, rev),
        grid=(nb,),
        in_specs=in_specs,
        out_specs=blk,
        out_shape=jax.ShapeDtypeStruct((rows, W), F32),
        scratch_shapes=[pltpu.VMEM((RW_HEADS // 2, P2, P2), F32)],
        compiler_params=_cparams(("arbitrary",)),
        name="rw_rev" if rev else "rw_fwd",
    )(*args)


def _qk_prep_kernel(nq_ref, nk_ref, nv_ref, wq_ref, wkv_ref, nqn_ref, nkn_ref, wqn_ref, wkn_ref,
                    cos_ref, sin_ref, bd512_ref, bd128_ref,
                    naq_o, nak_o, nav_o, waq_o, wak_o, wav_o):
    bd512 = bd512_ref[...]
    bd128 = bd128_ref[...]

    def hnorm(x, g, bd, hd):
        return x * lax.rsqrt(_mm2(x * x, bd) * (1.0 / hd) + EPS) * g

    def rope(x, cos, sin):
        lane = lax.broadcasted_iota(jnp.int32, x.shape, 1)
        w = x.shape[1]
        partner = jnp.where((lane & 31) < 16, pltpu.roll(x, w - 16, axis=1), pltpu.roll(x, 16, axis=1))
        return x * cos + partner * sin

    def dup_heads(o_ref, x):
        lower = lax.broadcasted_iota(jnp.int32, x.shape, 1) < WA_HD
        swapped = pltpu.roll(x, WA_HD, axis=1)
        o_ref[0] = jnp.where(lower, x, swapped).astype(o_ref.dtype)
        o_ref[1] = jnp.where(lower, swapped, x).astype(o_ref.dtype)

    f32 = lambda ref: ref[...].astype(F32)
    naq_o[...] = (hnorm(f32(nq_ref), nqn_ref[...], bd512, NA_HD) * NA_HD ** -0.5).astype(BF16)
    nak_o[...] = hnorm(f32(nk_ref), nkn_ref[...], bd512, NA_HD).astype(BF16)
    nav_o[...] = nv_ref[...]
    cos = cos_ref[...]
    sin = sin_ref[...]
    wq = hnorm(f32(wq_ref), wqn_ref[...], bd512, WA_HD) * WA_HD ** -0.5
    wq = rope(wq, jnp.concatenate([cos] * 4, axis=1), jnp.concatenate([sin] * 4, axis=1))
    waq_o[...] = wq.astype(BF16)
    kv = f32(wkv_ref)
    dup_heads(wak_o, rope(hnorm(kv[:, :128], wkn_ref[...], bd128, WA_HD), cos, sin))
    dup_heads(wav_o, kv[:, 128:])


def _qk_prep(p, nqn, nkn, wqn, wkn, cos, sin, bd512, bd128):
    rows = p.shape[0]
    nb = rows // MIX_ROWS
    nab = OFF_NA // 512
    full = lambda shape: pl.BlockSpec(shape, lambda i: (0,) * len(shape))
    in_specs = [pl.BlockSpec((MIX_ROWS, 512), lambda i: (i, nab)),
                pl.BlockSpec((MIX_ROWS, 512), lambda i: (i, nab + 1)),
                pl.BlockSpec((MIX_ROWS, 512), lambda i: (i, nab + 2)),
                pl.BlockSpec((MIX_ROWS, 512), lambda i: (i, OFF_WA // 512)),
                pl.BlockSpec((MIX_ROWS, 256), lambda i: (i, (OFF_WA + 512) // 256)),
                full((1, 512)), full((1, 512)), full((1, 512)), full((1, 128)),
                pl.BlockSpec((MIX_ROWS, 128), lambda i: (i, 0)),
                pl.BlockSpec((MIX_ROWS, 128), lambda i: (i, 0)),
                full((512, 512)), full((128, 128))]
    flat = pl.BlockSpec((MIX_ROWS, 512), lambda i: (i, 0))
    flat_shape = jax.ShapeDtypeStruct((rows, 512), BF16)
    dup = pl.BlockSpec((WA_KV_HEADS, MIX_ROWS, 128), lambda i: (0, i, 0))
    dup_shape = jax.ShapeDtypeStruct((WA_KV_HEADS, rows, 128), BF16)
    return pl.pallas_call(
        _qk_prep_kernel,
        grid=(nb,),
        in_specs=in_specs,
        out_specs=[flat, flat, flat, flat, dup, dup],
        out_shape=[flat_shape, flat_shape, flat_shape, flat_shape, dup_shape, dup_shape],
        compiler_params=_cparams(("arbitrary",)),
        name="qk_prep",
    )(p, p, p, p, p, nqn, nkn, wqn, wkn, cos, sin, bd512, bd128)


def _na_bias_layout():
    wr, G, KR = NA_WIN_R, NA_GROUP, NA_KEY_ROWS
    off = (0, -(wr // 2), -(KR - G))
    out = {}
    for kind in range(3):
        for r in range(G):
            wstart = (0, r, KR - wr)[kind]
            for a in range(KR):
                inside = wstart <= a < wstart + wr
                out[kind, r, a] = (off[kind] + a - r + wr - 1) if inside else None
    return out


def _na_kernel(lc, nrows, q_ref, k_ref, v_ref, t_ref, o_ref, bias_s):
    W = GRID_W
    HD = NA_HD
    nq = NA_GROUP * W
    nk = NA_KEY_ROWS * W
    ngroups = nrows // NA_GROUP
    kc = k_ref[0:lc, :]
    vc = v_ref[0:lc, :]
    half = [lax.broadcasted_iota(jnp.int32, (1, 2 * HD), 1) // HD == hh for hh in range(2)]

    bias_s[:, :, :, 0:lc] = jnp.zeros((2, 3, nq, lc), F32)
    for (kind, r, a), ro in _na_bias_layout().items():
        c0 = lc + a * W
        for hh in range(2):
            tile = jnp.full((W, W), NEG, F32) if ro is None else t_ref[hh, ro, :, (a % 2) * W:(a % 2 + 1) * W]
            bias_s[hh, kind, r * W:(r + 1) * W, c0:c0 + W] = tile

    def attend(q, keys, vals, bias):
        out = None
        for hh in range(2):
            s = _mm(jnp.where(half[hh], q, 0), keys, NT)
            if bias is not None:
                s = s + bias(hh)
            p = jnp.exp(s - jnp.max(s, axis=-1, keepdims=True))
            o = _mm(p, jnp.where(half[hh], vals, 0)) / jnp.sum(p, axis=-1, keepdims=True)
            out = o if out is None else out + o
        return out

    o_ref[0:lc, :] = attend(q_ref[0:lc, :], kc, vc, None)

    def group(gi, carry):
        base = jnp.clip(gi * NA_GROUP - NA_WIN_R // 2, 0, nrows - NA_KEY_ROWS)
        kind = jnp.where(gi == 0, 0, jnp.where(gi == ngroups - 1, 2, 1))
        q0 = pl.multiple_of(lc + gi * nq, nq)
        k0 = pl.multiple_of(lc + base * W, W)
        keys = jnp.concatenate([kc, k_ref[pl.ds(k0, nk), :]], axis=0)
        vals = jnp.concatenate([vc, v_ref[pl.ds(k0, nk), :]], axis=0)
        o_ref[pl.ds(q0, nq), :] = attend(q_ref[pl.ds(q0, nq), :], keys, vals, lambda hh: bias_s[hh, kind])
        return carry

    lax.fori_loop(0, ngroups, group, 0, unroll=2)


def _na(q, k, v, t2, lc):
    rows = q.shape[0]
    nrows = (rows - lc) // GRID_W
    pblk = pl.BlockSpec((rows, 2 * NA_HD), lambda p: (0, p))
    return pl.pallas_call(
        functools.partial(_na_kernel, lc, nrows),
        grid=(NA_HEADS // 2,),
        in_specs=[pblk, pblk, pblk, pl.BlockSpec((2,) + t2.shape[1:], lambda p: (p, 0, 0, 0))],
        out_specs=pblk,
        out_shape=jax.ShapeDtypeStruct((rows, NA_HEADS * NA_HD), F32),
        scratch_shapes=[pltpu.VMEM((2, 3, NA_GROUP * GRID_W, lc + NA_KEY_ROWS * GRID_W), F32)],
        compiler_params=_cparams(("arbitrary",)),
        name="na_attn",
    )(q, k, v, t2)


def _wa_kernel(lc, t_len, q_ref, k_ref, v_ref, sink_ref, o_ref):
    G = WA_HEADS // WA_KV_HEADS
    B = WA_WINDOW
    nband = 3 * B
    HD = WA_HD
    b = pl.program_id(1)
    kc = k_ref[0:lc, :]
    vc = v_ref[0:lc, :]
    sink = sink_ref[...]
    lower = lax.broadcasted_iota(jnp.int32, (1, 2 * HD), 1) < HD

    def stacked_queries(r0, n):
        parts = []
        for g in range(G):
            qp = q_ref[r0:r0 + n, (g // 2) * 2 * HD:(g // 2 + 1) * 2 * HD]
            parts.append(jnp.where(lower if g % 2 == 0 else jnp.logical_not(lower), qp, 0))
        return jnp.concatenate(parts, axis=0)

    def store(r0, n, o):
        for gp in range(G // 2):
            o_ref[r0:r0 + n, gp * 2 * HD:(gp + 1) * 2 * HD] = jnp.where(
                lower, o[2 * gp * n:(2 * gp + 1) * n], o[(2 * gp + 1) * n:(2 * gp + 2) * n])

    @pl.when(b == 0)
    def _():
        sk = jnp.concatenate([jnp.broadcast_to(sink[g * B:g * B + 1, :], (lc, 1)) for g in range(G)], axis=0)
        s = _mm(stacked_queries(0, lc), kc, NT)
        m = jnp.maximum(jnp.max(s, axis=-1, keepdims=True), sk)
        e = jnp.exp(s - m)
        store(0, lc, _mm(e, vc) / (jnp.sum(e, axis=-1, keepdims=True) + jnp.exp(sk - m)))

    @pl.when(b > 0)
    def _():
        qoff = lax.broadcasted_iota(jnp.int32, (G * B, lc + nband), 0) & (B - 1)
        koff = lax.broadcasted_iota(jnp.int32, (G * B, lc + nband), 1) - lc
        for j in range(MIX_ROWS // B):
            n = (b - 1) * (MIX_ROWS // B) + j
            start = jnp.clip((n - 1) * B, 0, t_len - nband)
            k0 = pl.multiple_of(lc + start, B)
            keys = jnp.concatenate([kc, k_ref[pl.ds(k0, nband), :]], axis=0)
            vals = jnp.concatenate([vc, v_ref[pl.ds(k0, nband), :]], axis=0)
            valid = jnp.logical_or(koff < 0, jnp.abs((n * B + qoff) - (start + koff)) <= WA_WINDOW)
            s = jnp.where(valid, _mm(stacked_queries(j * B, B), keys, NT), NEG)
            m = jnp.maximum(jnp.max(s, axis=-1, keepdims=True), sink)
            p = jnp.exp(s - m)
            store(j * B, B, _mm(p, vals) / (jnp.sum(p, axis=-1, keepdims=True) + jnp.exp(sink - m)))


def _wa(q, k, v, sink_col, lc):
    rows = q.shape[0]
    G = WA_HEADS // WA_KV_HEADS
    kvblk = pl.BlockSpec((None, rows, 2 * WA_HD), lambda h, b: (h, 0, 0))
    qblk = pl.BlockSpec((MIX_ROWS, G * WA_HD), lambda h, b: (b, h))
    return pl.pallas_call(
        functools.partial(_wa_kernel, lc, rows - lc),
        grid=(WA_KV_HEADS, rows // MIX_ROWS),
        in_specs=[qblk, kvblk, kvblk,
                  pl.BlockSpec((None, G * WA_WINDOW, 1), lambda h, b: (h, 0, 0))],
        out_specs=qblk,
        out_shape=jax.ShapeDtypeStruct((rows, WA_HEADS * WA_HD), F32),
        compiler_params=_cparams(("arbitrary", "arbitrary")),
        name="wa_attn",
    )(q, k, v, sink_col)


def _block_diag(n, blk):
    idx = np.arange(n) // blk
    return jnp.asarray(idx[:, None] == idx[None, :], dtype=BF16)


def _na_bias_table(rpb):
    W, wc = GRID_W, NA_WIN_C
    j = np.arange(W)
    cstart = np.clip(j - wc // 2, 0, W - wc)
    cabs = np.arange(W)
    inwin = (cabs[None, :] >= cstart[:, None]) & (cabs[None, :] < cstart[:, None] + wc)
    cb = cabs[None, :] - j[:, None] + wc - 1
    nh, nro, nco = rpb.shape
    onehot = (cb[None] == np.arange(nco)[:, None, None]) & inwin[None]
    t = jnp.dot(rpb.reshape(nh * nro, nco), jnp.asarray(onehot.reshape(nco, W * W), F32),
                precision=lax.Precision.HIGHEST).reshape(nh, nro, W, W)
    t = t + jnp.asarray(np.where(inwin, 0.0, NEG), F32)
    return jnp.concatenate([t, t], axis=-1)


def _rope_tables(lc, t_len):
    pos = np.arange(t_len)
    n = WA_HD // 4
    inv = ROPE_BASE ** (-np.arange(n, dtype=np.float64) / n)
    ang_r = (pos // GRID_W)[:, None] * inv[None, :]
    ang_c = (pos % GRID_W)[:, None] * inv[None, :]
    cos = np.concatenate([np.cos(ang_r)] * 2 + [np.cos(ang_c)] * 2, axis=1)
    sin = np.concatenate([-np.sin(ang_r), np.sin(ang_r), -np.sin(ang_c), np.sin(ang_c)], axis=1)
    cos = np.concatenate([np.ones((lc, WA_HD)), cos], axis=0)
    sin = np.concatenate([np.zeros((lc, WA_HD)), sin], axis=0)
    return (jnp.asarray(np.tile(cos, (1, 2)), dtype=F32), jnp.asarray(np.tile(sin, (1, 2)), dtype=F32))


def _cast_kernel(x_ref, o_ref):
    o_ref[...] = x_ref[...].astype(o_ref.dtype)


def _cast_bf16(w, layer, bk, bn, src_col_block=None):
    _, k, n = w.shape
    col = src_col_block or (lambda j: j)
    return pl.pallas_call(
        _cast_kernel,
        grid=(k // bk, n // bn),
        in_specs=[pl.BlockSpec((None, bk, bn), lambda i, j: (layer, i, col(j)))],
        out_specs=pl.BlockSpec((bk, bn), lambda i, j: (i, j)),
        out_shape=jax.ShapeDtypeStruct((k, n), BF16),
        compiler_params=_cparams(("arbitrary", "arbitrary")),
        name="cast_bf16",
    )(w)


W_IN_BLK = 256
_W_IN_SRC = np.concatenate([np.arange(10, 18), np.arange(0, 10), np.arange(19, 25), np.arange(28, 60),
                            np.arange(25, 28), np.arange(18, 19)])


def _w_in_src_block(j):
    starts = (0, 8, 18, 24, 56, 59)
    src = j + int(_W_IN_SRC[0])
    for s in starts[1:]:
        src = jnp.where(j >= s, j + int(_W_IN_SRC[s]) - s, src)
    return src


def _lora_pad(w, row0):
    out = jnp.zeros((2, 256, w.shape[-1]), F32)
    for d in range(2):
        out = out.at[d, row0 + 64 * d:row0 + 64 * (d + 1)].set(w[d])
    return out


def _token_mixing(p, lc, layer, hg_lb, hg_norm, rw_shift, rw_w0, rw_w2, rw_a0, rw_a2, rw_kk, rw_ka, rw_rk,
                  rw_ln_w, rw_ln_b, na_qn, na_kn, na_rpb, wa_qn, wa_kn, wa_sink, tables):
    rows = p.shape[0]
    bd512, bd128, cos, sin = tables
    cum = jnp.cumsum(jax.nn.softmax(hg_lb.astype(F32), axis=1), axis=1)
    lbs = cum[:, layer] - cum[:, 0]
    o_f = _gla_dir(p, lbs[0:1], False)
    y_a = _gla_dir(p, lbs[1:2], True, o_f, hg_norm[None])
    prep = _rw_prep(p, rw_shift[:, :2048], rw_shift[:, 2048:], rw_w0, _lora_pad(rw_w2, 0), rw_a0,
                    _lora_pad(rw_a2, 128), rw_kk[None], rw_ka[None], rw_rk.reshape(1, -1), bd512)
    r, kk, v, gs, bonus, lw0, lw1, akk0, akk1, kd0, kd1 = prep
    o_f = _rw_scan(False, r, kk, v, lw0, akk0, kd0)
    y_b = _rw_scan(True, r, kk, v, lw1, akk1, kd1,
                   (o_f, bonus, gs, rw_ln_w[None], rw_ln_b[None], bd512))
    tile = lambda g, n: jnp.tile(g, n)[None]
    naq, nak, nav, waq, wak, wav = _qk_prep(p, tile(na_qn, 8), tile(na_kn, 8), tile(wa_qn, 8),
                                            tile(wa_kn, 2), cos, sin, bd512, bd128)
    y_c = _na(naq, nak, nav, _na_bias_table(na_rpb), lc)
    G = WA_HEADS // WA_KV_HEADS
    sink_col = jnp.repeat(wa_sink.reshape(WA_KV_HEADS, G), WA_WINDOW, axis=1)[..., None]
    y_d = _wa(waq, wak, wav, sink_col, lc)
    return y_a, y_b, y_c, y_d


def kernel(x, c, ctx, c_ctx, ada_w, ada_b, norm_ffn1, norm_mix, norm_ffn2, ffn1_wi, ffn1_wo, ffn2_wi, ffn2_wo, w_in, hg_lb, hg_norm, rw_shift, rw_w0, rw_w2, rw_a0, rw_a2, rw_kk, rw_ka, rw_rk, rw_ln_w, rw_ln_b, na_qn, na_kn, na_rpb, wa_qn, wa_kn, wa_sink, w_branch, w_out):
    assert x.shape[0] == 1 and ctx.shape[1] == MIX_ROWS
    lc = ctx.shape[1]
    t_len = x.shape[1]
    xa = jnp.concatenate([ctx[0], x[0]], axis=0)
    cc_t = jnp.stack([c[0], c_ctx], axis=1)
    mods = _ada_mods(cc_t, ada_w, ada_b).reshape(DEPTH, 2, N_MOD, D)
    tables = (_block_diag(512, 64), _block_diag(128, 64)) + _rope_tables(lc, t_len)
    half = W_IN_BLK // 2
    w_in_src = lambda c: 2 * _w_in_src_block(c // 2) + c % 2
    wi1, wo1 = _cast_bf16(ffn1_wi, 0, D, 1024), _cast_bf16(ffn1_wo, 0, 512, D)
    for l in range(DEPTH):
        m = mods[l]
        xa, w_in_l = _ffn(xa, m[:, 0:3], norm_ffn1[l][None], wi1, wo1, lc,
                          jobs=[_cast_job(w_in, l, D, half, w_in_src)])
        jobs = [_cast_job(ffn2_wi, l, D, half), _cast_job(ffn2_wo, l, 64, D)]
        if l + 1 < DEPTH:
            jobs += [_cast_job(ffn1_wi, l + 1, D, half), _cast_job(ffn1_wo, l + 1, 64, D)]
        p, wi2, wo2, *nxt = _win(xa, m[:, 3:6], norm_mix[l][None], w_in_l, lc, jobs=jobs)
        if nxt:
            wi1, wo1 = nxt
        yb = _token_mixing(p, lc, l, hg_lb, hg_norm[l], rw_shift[l], rw_w0[l], rw_w2[l], rw_a0[l], rw_a2[l],
                           rw_kk[l], rw_ka[l], rw_rk[l], rw_ln_w[l], rw_ln_b[l], na_qn[l], na_kn[l],
                           na_rpb[l], wa_qn[l], wa_kn[l], wa_sink[l], tables)
        wb = _cast_bf16(w_branch.reshape(DEPTH, N_BRANCH * BR_W, D), l, 1024, D).reshape(N_BRANCH, BR_W, D)
        xa = _merge(xa, m[:, 3:6], yb, p, wb, _cast_bf16(w_out, l, 1024, D), lc)
        xa, = _ffn(xa, m[:, 6:9], norm_ffn2[l][None], wi2, wo2, lc)
    return xa[lc:][None]
```

```python
import functools

import numpy as np
import jax
import jax.numpy as jnp
from jax import lax
from jax.experimental import pallas as pl
from jax.experimental.pallas import tpu as pltpu

F32 = jnp.float32
BF16 = jnp.bfloat16

D = 2048
DEPTH = 2
GRID_W = 64
EPS = 1e-6
D_FF = 5632
N_MOD = 9
N_BRANCH = 4
BR_W = 512
HG_HEADS, HG_DK = 4, 128
RW_HEADS, RW_HD = 8, 64
RW_LORA = 64
RW_GN_EPS = 64e-5
NA_HEADS, NA_HD = 8, 64
NA_WIN_R, NA_WIN_C = 8, 16
NA_GROUP = 4
NA_KEY_ROWS = NA_WIN_R + NA_GROUP - 1
WA_HEADS, WA_KV_HEADS, WA_HD = 8, 2, 64
WA_WINDOW = 128
ROPE_BASE = 10000.0

OFF_RW, OFF_HG, OFF_NA, OFF_GL, OFF_WA, OFF_LORA = 0, 2048, 4608, 6144, 14336, 15104
P_TOTAL = 15360

MIX_ROWS = 256
HALO = 16
GLA_CHUNK = 16
RW_CHUNK = 64
NEG = -1e30
LOG2E = 1.4426950408889634
VMEM_LIMIT = 60 * 1024 * 1024

NT = (((1,), (1,)), ((), ()))
TN = (((0,), (0,)), ((), ()))


def _cparams(sem):
    return pltpu.CompilerParams(dimension_semantics=sem, vmem_limit_bytes=VMEM_LIMIT)


def _mm(a, b, dims=None):
    a = a.astype(BF16)
    b = b.astype(BF16)
    if dims is None:
        return jnp.dot(a, b, preferred_element_type=F32)
    return lax.dot_general(a, b, dims, preferred_element_type=F32)


def _mm2(x, w_bf16):
    hi = x.astype(BF16)
    lo = (x - hi.astype(F32)).astype(BF16)
    return (jnp.dot(hi, w_bf16, preferred_element_type=F32)
            + jnp.dot(lo, w_bf16, preferred_element_type=F32))


def _mm_f32(a, b):
    a_hi, b_hi = a.astype(BF16), b.astype(BF16)
    a_lo = (a - a_hi.astype(F32)).astype(BF16)
    b_lo = (b - b_hi.astype(F32)).astype(BF16)
    dot = functools.partial(jnp.dot, preferred_element_type=F32)
    return dot(a_hi, b_hi) + (dot(a_hi, b_lo) + dot(a_lo, b_hi))


def _sigmoid(x):
    return 1.0 / (1.0 + jnp.exp(-x))


def _silu(x):
    return x * _sigmoid(x)


def _seg_cumsum(x, tri):
    w = x.shape[1]
    hi = x.astype(BF16)
    r1 = x - hi.astype(F32)
    mid = r1.astype(BF16)
    lo = (r1 - mid.astype(F32)).astype(BF16)
    y = jnp.dot(tri, jnp.concatenate([hi, mid, lo], axis=1), preferred_element_type=F32)
    return y[:, :w] + y[:, w:2 * w] + y[:, 2 * w:]


def _seg_tri(rows, seg, rev):
    r = np.arange(rows)
    same = (r[:, None] // seg) == (r[None, :] // seg)
    tri = (r[None, :] >= r[:, None]) if rev else (r[None, :] <= r[:, None])
    return jnp.asarray(same & tri, dtype=BF16)


def _ada_kernel(cc_ref, w_ref, b_ref, o_ref):
    s = _silu(cc_ref[...])
    w = w_ref[...]
    r0 = jnp.sum(w * s[:, 0:1], axis=0, keepdims=True)
    r1 = jnp.sum(w * s[:, 1:2], axis=0, keepdims=True)
    o_ref[...] = jnp.concatenate([r0, r1], axis=0) + b_ref[...]


def _ada_mods(cc_t, ada_w, ada_b):
    tn = 512
    nmod = ada_w.shape[-1]
    return pl.pallas_call(
        _ada_kernel,
        grid=(DEPTH, nmod // tn),
        in_specs=[pl.BlockSpec((D, 2), lambda l, j: (0, 0)),
                  pl.BlockSpec((None, D, tn), lambda l, j: (l, 0, j)),
                  pl.BlockSpec((None, 1, tn), lambda l, j: (l, 0, j))],
        out_specs=pl.BlockSpec((None, 2, tn), lambda l, j: (l, 0, j)),
        out_shape=jax.ShapeDtypeStruct((DEPTH, 2, nmod), F32),
        compiler_params=_cparams(("arbitrary", "arbitrary")),
        name="ada_mod",
    )(cc_t, ada_w, ada_b.reshape(DEPTH, 1, nmod))


def _cast_job(w, layer, bk, bn, src_col_block=None):
    return (w, layer, bk, bn, src_col_block)


def _cast_job_specs(jobs, grid):
    in_specs, out_specs, out_shapes, args = [], [], [], []
    for w, layer, bk, bn, src_col in jobs:
        _, k, n = w.shape
        ncol = n // bn
        nblk = (k // bk) * ncol
        assert nblk <= grid[0] * grid[1], (nblk, grid)
        col = src_col or (lambda c: c)

        def blk(i, j, nblk=nblk):
            return jnp.minimum(i * grid[1] + j, nblk - 1)

        in_specs.append(pl.BlockSpec(
            (None, bk, bn), lambda i, j, blk=blk, ncol=ncol, col=col, layer=layer:
            (layer, blk(i, j) // ncol, col(blk(i, j) % ncol))))
        out_specs.append(pl.BlockSpec((bk, bn), lambda i, j, blk=blk, ncol=ncol: (blk(i, j) // ncol, blk(i, j) % ncol)))
        out_shapes.append(jax.ShapeDtypeStruct((k, n), BF16))
        args.append(w)
    return in_specs, out_specs, out_shapes, args


def _run_cast_jobs(src_refs, dst_refs):
    for src, dst in zip(src_refs, dst_refs):
        dst[...] = src[...].astype(dst.dtype)


def _row_is_ctx(i, tm, lc):
    return (i * tm + lax.broadcasted_iota(jnp.int32, (tm, 1), 0)) < lc


def _mod_pick(mod_ref, j, is_ctx):
    return jnp.where(is_ctx, mod_ref[1, j:j + 1, :], mod_ref[0, j:j + 1, :])


def _store_modulated(xn_ref, x_ref, g_ref, mod_ref, i, tm, lc):
    def normed():
        x = x_ref[...]
        return x * lax.rsqrt(jnp.mean(x * x, axis=-1, keepdims=True) + EPS)

    @pl.when(i * tm < lc)
    def _():
        is_ctx = _row_is_ctx(i, tm, lc)
        y = normed() * g_ref[...]
        xn_ref[...] = (y * (1.0 + _mod_pick(mod_ref, 1, is_ctx)) + _mod_pick(mod_ref, 0, is_ctx)).astype(BF16)

    @pl.when(i * tm >= lc)
    def _():
        gain = g_ref[...] * (1.0 + mod_ref[0, 1:2, :])
        xn_ref[...] = (normed() * gain + mod_ref[0, 0:1, :]).astype(BF16)


def _ffn_kernel(lc, tm, nf, njobs, x_ref, mod_ref, g_ref, wa_ref, wb_ref, wo_ref, *rest):
    src_refs, (o_ref, *dst_refs, xn_ref) = rest[:njobs], rest[njobs:]
    i = pl.program_id(0)
    j = pl.program_id(1)
    _run_cast_jobs(src_refs, dst_refs)

    @pl.when(j == 0)
    def _():
        _store_modulated(xn_ref, x_ref, g_ref, mod_ref, i, tm, lc)
        o_ref[...] = jnp.zeros_like(o_ref)

    xn = xn_ref[...]
    a = jnp.dot(xn, wa_ref[...], preferred_element_type=F32)
    b = jnp.dot(xn, wb_ref[...], preferred_element_type=F32)
    h = (_silu(a) * b).astype(BF16)
    o_ref[...] += jnp.dot(h, wo_ref[...], preferred_element_type=F32)

    @pl.when(j == nf - 1)
    def _():
        o_ref[...] = x_ref[...] + 0.5 * _mod_pick(mod_ref, 2, _row_is_ctx(i, tm, lc)) * o_ref[...]


def _dense_tm(rows):
    for tm in (768, 512, 256):
        if rows % tm == 0:
            return tm
    raise ValueError(rows)


def _ffn(x, mod3, g, wi, wo, lc, jobs=()):
    rows = x.shape[0]
    tm = _dense_tm(rows)
    tf = 512
    nf = D_FF // tf
    grid = (rows // tm, nf)
    job_in, job_out, job_shape, job_args = _cast_job_specs(jobs, grid)
    return pl.pallas_call(
        functools.partial(_ffn_kernel, lc, tm, nf, len(jobs)),
        grid=grid,
        in_specs=[pl.BlockSpec((tm, D), lambda i, j: (i, 0)),
                  pl.BlockSpec((2, 3, D), lambda i, j: (0, 0, 0)),
                  pl.BlockSpec((1, D), lambda i, j: (0, 0)),
                  pl.BlockSpec((D, tf), lambda i, j: (0, j)),
                  pl.BlockSpec((D, tf), lambda i, j: (0, j + nf)),
                  pl.BlockSpec((tf, D), lambda i, j: (j, 0))] + job_in,
        out_specs=[pl.BlockSpec((tm, D), lambda i, j: (i, 0))] + job_out,
        out_shape=[jax.ShapeDtypeStruct((rows, D), F32)] + job_shape,
        scratch_shapes=[pltpu.VMEM((tm, D), BF16)],
        compiler_params=_cparams(("arbitrary", "arbitrary")),
        name="ffn",
    )(x, mod3, g, wi, wi, wo, *job_args)


def _win_kernel(lc, tm, njobs, x_ref, mod_ref, g_ref, w_ref, *rest):
    src_refs, (o_ref, *dst_refs, xn_ref) = rest[:njobs], rest[njobs:]
    i = pl.program_id(0)
    _run_cast_jobs(src_refs, dst_refs)

    @pl.when(pl.program_id(1) == 0)
    def _():
        _store_modulated(xn_ref, x_ref, g_ref, mod_ref, i, tm, lc)

    o_ref[...] = jnp.dot(xn_ref[...], w_ref[...], preferred_element_type=F32).astype(o_ref.dtype)


def _win(x, mod3, g, w, lc, jobs=()):
    rows = x.shape[0]
    tm = _dense_tm(rows)
    tn = 1536
    grid = (rows // tm, P_TOTAL // tn)
    job_in, job_out, job_shape, job_args = _cast_job_specs(jobs, grid)
    return pl.pallas_call(
        functools.partial(_win_kernel, lc, tm, len(jobs)),
        grid=grid,
        in_specs=[pl.BlockSpec((tm, D), lambda i, j: (i, 0)),
                  pl.BlockSpec((2, 3, D), lambda i, j: (0, 0, 0)),
                  pl.BlockSpec((1, D), lambda i, j: (0, 0)),
                  pl.BlockSpec((D, tn), lambda i, j: (0, j))] + job_in,
        out_specs=[pl.BlockSpec((tm, tn), lambda i, j: (i, j))] + job_out,
        out_shape=[jax.ShapeDtypeStruct((rows, P_TOTAL), BF16)] + job_shape,
        scratch_shapes=[pltpu.VMEM((tm, D), BF16)],
        compiler_params=_cparams(("arbitrary", "arbitrary")),
        name="w_in",
    )(x, mod3, g, w, *job_args)


def _merge_kernel(lc, tm, x_ref, mod_ref, ya_ref, yb_ref, yc_ref, yd_ref, gl_ref, wb_ref, wo_ref, o_ref, acc_ref):
    i = pl.program_id(0)
    n = pl.program_id(1)

    @pl.when(n == 0)
    def _():
        acc_ref[...] = jnp.zeros_like(acc_ref)

    for nn, y_ref in enumerate((ya_ref, yb_ref, yc_ref, yd_ref)):
        @pl.when(n == nn)
        def _():
            proj = jnp.dot(y_ref[...].astype(BF16), wb_ref[nn], preferred_element_type=F32)
            acc_ref[...] += _sigmoid(gl_ref[...].astype(F32)) * proj

    @pl.when(n == N_BRANCH - 1)
    def _():
        is_ctx = _row_is_ctx(i, tm, lc)
        y = jnp.dot(acc_ref[...].astype(BF16), wo_ref[...], preferred_element_type=F32)
        o_ref[...] = x_ref[...] + _mod_pick(mod_ref, 2, is_ctx) * y


def _merge(x, mod3, ys, p, wb, wo, lc):
    rows = x.shape[0]
    tm = 384 if rows % 384 == 0 else MIX_ROWS
    glb = OFF_GL // D
    yspec = pl.BlockSpec((tm, BR_W), lambda i, n: (i, 0))
    return pl.pallas_call(
        functools.partial(_merge_kernel, lc, tm),
        grid=(rows // tm, N_BRANCH),
        in_specs=[pl.BlockSpec((tm, D), lambda i, n: (i, 0)),
                  pl.BlockSpec((2, 3, D), lambda i, n: (0, 0, 0)),
                  yspec, yspec, yspec, yspec,
                  pl.BlockSpec((tm, D), lambda i, n: (i, glb + n)),
                  pl.BlockSpec((N_BRANCH, BR_W, D), lambda i, n: (0, 0, 0)),
                  pl.BlockSpec((D, D), lambda i, n: (0, 0))],
        out_specs=pl.BlockSpec((tm, D), lambda i, n: (i, 0)),
        out_shape=jax.ShapeDtypeStruct((rows, D), F32),
        scratch_shapes=[pltpu.VMEM((tm, D), F32)],
        compiler_params=_cparams(("arbitrary", "arbitrary")),
        name="merge",
    )(x, mod3, *ys, p, wb, wo)


def _gla_kernel(rev, *refs):
    if rev:
        q_ref, f_ref, i_ref, lb_ref, tri_ref, of_ref, g_ref, nw_ref, y_ref, st_ref = refs
    else:
        q_ref, f_ref, i_ref, lb_ref, tri_ref, y_ref, st_ref = refs
    C = GLA_CHUNK
    R = MIX_ROWS
    NH, DK = HG_HEADS, HG_DK

    @pl.when(pl.program_id(0) == 0)
    def _():
        st_ref[...] = jnp.zeros_like(st_ref)

    yield
    lb = lb_ref[...]
    f_all = lb + (1.0 - lb) * _sigmoid(f_ref[...].astype(F32))
    b_all = _seg_cumsum(jnp.log(f_all), tri_ref[...])
    SUB = C // 2
    sub_pos = lax.broadcasted_iota(jnp.int32, (1, SUB, 1), 1)
    chunks = list(range(R // C - 1, -1, -1) if rev else range(R // C))

    heads = []
    for h in range(NH):
        hs = slice(h * DK, (h + 1) * DK)
        b = b_all[:, hs]
        k = 1.0 - f_all[:, hs]
        q = q_ref[:, hs].astype(F32)
        v = i_ref[:, hs].astype(F32)
        def halves(t):
            t3 = t.reshape(R // C, C, DK)
            lo, hi = t3[:, :SUB], t3[:, SUB:]
            return (hi, lo) if rev else (lo, hi)

        def pair_term(qq, bq, keys, ok):
            bk_, kk_, vk_ = keys
            diff = bq - bk_
            e = jnp.exp2(diff if ok is None else jnp.where(ok, diff, NEG))
            return jnp.sum(qq * kk_ * e, axis=-1, keepdims=True) * vk_

        (qn, qf), (bn, bf), (kn, kf), (vn, vf) = halves(q), halves(b * LOG2E), halves(k), halves(v)
        near, far = (bn, kn, vn), (bf, kf, vf)
        o_n = jnp.sum(qn * kn, axis=-1, keepdims=True) * vn
        o_f = jnp.sum(qf * kf, axis=-1, keepdims=True) * vf + pair_term(qf, bf, near, None)
        for d in range(1, SUB):
            sh = (SUB - d) if rev else d
            ok = (sub_pos < SUB - d) if rev else (sub_pos >= d)
            near_r = tuple(pltpu.roll(t, sh, axis=1) for t in near)
            far_r = tuple(jnp.where(ok, pltpu.roll(t, sh, axis=1), tn) for t, tn in zip(far, near_r))
            o_n = o_n + pair_term(qn, bn, near_r, ok)
            o_f = o_f + pair_term(qf, bf, far_r, None) + pair_term(qf, bf, near_r, ok)
        o = jnp.concatenate([o_f, o_n] if rev else [o_n, o_f], axis=1).reshape(R, DK)
        upd, dec = {}, {}
        for c in chunks:
            sl = slice(c * C, (c + 1) * C)
            b_end = b[sl][0:1] if rev else b[sl][C - 1:C]
            dec[c] = jnp.exp(b_end)
            upd[c] = _mm(v[sl], k[sl] * jnp.exp(b_end - b[sl]), TN)
        heads.append(dict(o=o, qb=q * jnp.exp(b), upd=upd, dec=dec, st=st_ref[h]))

    for c in chunks:
        for w in heads:
            w[c] = w["st"]
            w["st"] = w["st"] * w["dec"][c] + w["upd"][c]
    outs = []
    for h, w in enumerate(heads):
        st_ref[h] = w["st"]
        inter = {c: _mm(w["qb"][c * C:(c + 1) * C], w[c], NT) for c in chunks}
        outs.append(w["o"] + jnp.concatenate([inter[c] for c in range(R // C)], axis=0))

    for h, o in enumerate(outs):
        hs = slice(h * DK, (h + 1) * DK)
        if rev:
            o = of_ref[:, hs] + o
            o = o * lax.rsqrt(jnp.mean(o * o, axis=-1, keepdims=True) + EPS) * nw_ref[:, hs]
            o = o * _silu(g_ref[:, hs].astype(F32))
        y_ref[:, hs] = o


def _blk_order(rev, nb):
    if rev:
        return lambda b: jnp.where(b == 0, 0, nb - b)
    return lambda b: b


def _gla_parts(p, lb, rev, o_fwd=None, norm_w=None):
    rows = p.shape[0]
    nb = rows // MIX_ROWS
    order = _blk_order(rev, nb)
    W = HG_HEADS * HG_DK
    cb = OFF_HG // W

    def col(n):
        return pl.BlockSpec((MIX_ROWS, W), lambda b: (order(b), cb + n))

    vec = pl.BlockSpec((1, W), lambda b: (0, 0))
    in_specs = [col(0), col(2 if rev else 1), col(3), vec,
                pl.BlockSpec((MIX_ROWS, MIX_ROWS), lambda b: (0, 0))]
    args = [p, p, p, lb, _seg_tri(MIX_ROWS, GLA_CHUNK, rev)]
    if rev:
        in_specs += [pl.BlockSpec((MIX_ROWS, W), lambda b: (order(b), 0)), col(4), vec]
        args += [o_fwd, p, norm_w]
    return dict(body=functools.partial(_gla_kernel, rev), in_specs=in_specs, args=args,
                out_spec=pl.BlockSpec((MIX_ROWS, W), lambda b: (order(b), 0)),
                out_shape=jax.ShapeDtypeStruct((rows, W), F32),
                scratch=pltpu.VMEM((HG_HEADS, HG_DK, HG_DK), F32))


def _run_scans(parts, rows, name):
    n_in = [len(p["in_specs"]) for p in parts]

    def kernel(*refs):
        pos = 0
        ins = []
        for n in n_in:
            ins.append(refs[pos:pos + n])
            pos += n
        outs = refs[pos:pos + len(parts)]
        scratch = refs[pos + len(parts):]
        bodies = [p["body"](*i, o, s) for p, i, o, s in zip(parts, ins, outs, scratch)]
        for body in bodies:
            next(body)
        for body in bodies:
            for _ in body:
                pass

    return pl.pallas_call(
        kernel,
        grid=(rows // MIX_ROWS,),
        in_specs=[s for p in parts for s in p["in_specs"]],
        out_specs=[p["out_spec"] for p in parts],
        out_shape=[p["out_shape"] for p in parts],
        scratch_shapes=[p["scratch"] for p in parts],
        compiler_params=_cparams(("arbitrary",)),
        name=name,
    )(*[a for p in parts for a in p["args"]])


def _rw_prep_kernel(nb, main_ref, lora_ref, pm_ref, nm_ref, pl_ref, nl_ref, tm_ref, tl_ref,
                    w0_ref, w2_ref, a0_ref, a2_ref, kkw_ref, ka_ref, rk_ref, bd_ref,
                    r_o, kk_o, v_o, gs_o, bonus_o, lw0_o, lw1_o, akk0_o, akk1_o, kd0_o, kd1_o):
    i = pl.program_id(0)
    has_prev = (i >= 2).astype(F32)
    has_next = jnp.logical_and(i != 0, i != nb - 1).astype(F32)
    rowi = lax.broadcasted_iota(jnp.int32, (MIX_ROWS, 1), 0)

    def shift(x, prev_blk, next_blk, taps):
        up = jnp.where(rowi == 0, prev_blk[HALO - 1:HALO, :] * has_prev, pltpu.roll(x, 1, axis=0))
        dn = jnp.where(rowi == MIX_ROWS - 1, next_blk[0:1, :] * has_next,
                       pltpu.roll(x, MIX_ROWS - 1, axis=0))
        return taps[0:1] * up + taps[1:2] * x + taps[2:3] * dn

    f32 = lambda ref: ref[...].astype(F32)
    main = shift(f32(main_ref), f32(pm_ref), f32(nm_ref), tm_ref[...])
    lora = shift(f32(lora_ref), f32(pl_ref), f32(nl_ref), tl_ref[...])
    W = RW_HEADS * RW_HD
    r, k, v, g = (main[:, n * W:(n + 1) * W] for n in range(4))
    bd = bd_ref[...]
    kk = k * kkw_ref[...]
    kk = kk * lax.rsqrt(_mm2(kk * kk, bd) + EPS)
    tl = jnp.tanh(lora)
    kds = []
    for d, (lw_o, akk_o, kd_o) in enumerate(((lw0_o, akk0_o, kd0_o), (lw1_o, akk1_o, kd1_o))):
        z = -(w0_ref[d:d + 1, :] + _mm_f32(tl, w2_ref[d]))
        softplus = jnp.maximum(z, 0.0) + jnp.log(1.0 + jnp.exp(-jnp.abs(z)))
        lw_o[...] = -jnp.exp(-softplus - 0.5)
        a = _sigmoid(a0_ref[d:d + 1, :] + _mm_f32(lora, a2_ref[d]))
        kd = k * (1.0 + (a - 1.0) * ka_ref[...])
        kds.append(kd)
        kd_o[...] = kd.astype(kd_o.dtype)
        akk_o[...] = (a * kk).astype(akk_o.dtype)
    r_o[...] = r.astype(r_o.dtype)
    kk_o[...] = kk.astype(kk_o.dtype)
    v_o[...] = v.astype(v_o.dtype)
    gs_o[...] = _sigmoid(g)
    bonus_o[...] = _mm2(r * (kds[0] + kds[1]) * rk_ref[...], bd) * v


def _rw_prep(p, taps_m, taps_l, w0, w2p, a0, a2p, kkw, ka, rk, bd512):
    rows = p.shape[0]
    nb = rows // MIX_ROWS
    W = RW_HEADS * RW_HD
    n8 = rows // HALO
    lb = OFF_LORA // 256
    per = MIX_ROWS // HALO
    full = lambda shape: pl.BlockSpec(shape, lambda i: (0,) * len(shape))
    in_specs = [pl.BlockSpec((MIX_ROWS, 4 * W), lambda i: (i, 0)),
                pl.BlockSpec((MIX_ROWS, 256), lambda i: (i, lb)),
                pl.BlockSpec((HALO, 4 * W), lambda i: (jnp.maximum(i * per - 1, 0), 0)),
                pl.BlockSpec((HALO, 4 * W), lambda i: (jnp.minimum((i + 1) * per, n8 - 1), 0)),
                pl.BlockSpec((HALO, 256), lambda i: (jnp.maximum(i * per - 1, 0), lb)),
                pl.BlockSpec((HALO, 256), lambda i: (jnp.minimum((i + 1) * per, n8 - 1), lb)),
                full((3, 4 * W)), full((3, 256)),
                full((2, W)), full((2, 256, W)), full((2, W)), full((2, 256, W)),
                full((1, W)), full((1, W)), full((1, W)), full((W, W))]
    f32_out = jax.ShapeDtypeStruct((rows, W), F32)
    mxu_out = jax.ShapeDtypeStruct((rows, W), BF16)
    return pl.pallas_call(
        functools.partial(_rw_prep_kernel, nb),
        grid=(nb,),
        in_specs=in_specs,
        out_specs=[pl.BlockSpec((MIX_ROWS, W), lambda i: (i, 0))] * 11,
        out_shape=[mxu_out] * 3 + [f32_out] * 4 + [mxu_out] * 4,
        compiler_params=_cparams(("arbitrary",)),
        name="rw_prep",
    )(p, p, p, p, p, p, taps_m, taps_l, w0, w2p, a0, a2p, kkw, ka, rk, bd512)


def _rw_scan_kernel(rev, *refs):
    if rev:
        (r_ref, kk_ref, v_ref, lw_ref, akk_ref, kd_ref, tri_ref, of_ref, bonus_ref, gs_ref,
         lnw_ref, lnb_ref, bd_ref, y_ref, s_ref) = refs
    else:
        r_ref, kk_ref, v_ref, lw_ref, akk_ref, kd_ref, tri_ref, y_ref, s_ref = refs
    C = RW_CHUNK
    HD = RW_HD
    P2 = 2 * HD

    NP = RW_HEADS // 2

    @pl.when(pl.program_id(0) == 0)
    def _():
        s_ref[...] = jnp.zeros_like(s_ref)

    yield
    lw = lw_ref[...]
    cum = _seg_cumsum(lw, tri_ref[...])
    cum_prev = cum - lw

    lane = lax.broadcasted_iota(jnp.int32, (1, P2), 1)
    h0 = lane < HD
    ri = lax.broadcasted_iota(jnp.int32, (P2, P2), 0)
    ci = lax.broadcasted_iota(jnp.int32, (P2, P2), 1)
    same = (ri // C) == (ci // C)
    rt, cs = ri % C, ci % C
    strict = jnp.logical_and(same, (rt < cs) if rev else (rt > cs))
    incl = jnp.logical_and(same, (rt <= cs) if rev else (rt >= cs))
    eye = (ri == ci).astype(F32)
    bdiag = ((ri // HD) == (ci // HD)).astype(F32)

    def stack(x):
        return jnp.concatenate([x, x], axis=0)

    def split_heads(x):
        return jnp.concatenate([jnp.where(h0, x, 0.0), jnp.where(h0, 0.0, x)], axis=0)

    def unstack(x):
        return jnp.where(h0, x[:C], x[C:])

    chunks = list(range(MIX_ROWS // C - 1, -1, -1) if rev else range(MIX_ROWS // C))
    order = [(c, p) for c in chunks for p in range(NP)]

    pre = {}
    for c, p in order:
        sl = slice(c * C, (c + 1) * C)
        pp = slice(p * P2, (p + 1) * P2)
        cum_c = cum[sl, pp]
        cend = cum_c[0:1] if rev else cum_c[C - 1:C]
        e_neg = jnp.exp(-cum_c)
        e_end = jnp.exp(cend - cum_c)
        a_t = kk_ref[sl, pp].astype(F32) * jnp.exp(cum_prev[sl, pp])
        r_t = r_ref[sl, pp].astype(F32) * jnp.exp(cum_c)
        akk = akk_ref[sl, pp].astype(F32)
        kd = kd_ref[sl, pp].astype(F32)
        v = v_ref[sl, pp]
        pre[c, p] = dict(a_t=a_t, r_t=r_t, v=v, kh=kd * e_end, bh=akk * e_end, dec=jnp.exp(cend),
                         lhs=jnp.concatenate([split_heads(a_t), split_heads(r_t)], axis=0),
                         rhs=jnp.concatenate([stack(akk * e_neg), stack(kd * e_neg)], axis=0))
    for w in pre.values():
        g = _mm(w["lhs"], w["rhs"], NT)
        w["m"] = jnp.where(strict, g[:P2, :P2], 0.0)
        w["a_ak"] = jnp.where(strict, g[:P2, P2:], 0.0)
        w["a_rb"] = jnp.where(incl, g[P2:, :P2], 0.0)
        w["a_rk"] = jnp.where(incl, g[P2:, P2:], 0.0)
        w["p"] = eye - w["m"]
    for w in pre.values():
        av = _mm(jnp.concatenate([w["a_ak"], w["a_rk"]], axis=0), stack(w["v"]))
        w["av"], w["rkv"] = av[:P2], av[P2:]
    for w in pre.values():
        w["m"] = _mm(w["m"], w["m"])
    for _ in range(4):
        for w in pre.values():
            pm = _mm(jnp.concatenate([w["p"], w["m"]], axis=0), w["m"])
            w["p"] = w["p"] + pm[:P2]
            w["m"] = pm[P2:]
    for w in pre.values():
        w["p"] = w["p"] + _mm(w["p"], w["m"])
    for w in pre.values():
        w["tt"] = _mm(w["p"], jnp.concatenate([stack(w["a_t"]), w["av"]], axis=1))
        w["z"] = _mm(w["p"], stack(w["bh"]), TN)
    for w in pre.values():
        ar = _mm(w["a_rb"], w["tt"])
        w["oa"] = stack(w["r_t"]) - ar[:, :P2]
        w["oc"] = w["rkv"] - ar[:, P2:]
        w["pz"] = bdiag * _mm(split_heads(w["a_t"]), w["z"], TN)
        w["kz"] = stack(w["kh"]) - _mm(w["a_ak"], w["z"], TN)
    for w in pre.values():
        w["q"] = bdiag * _mm(split_heads(w["v"]), w["kz"], TN)

    s = [s_ref[p] for p in range(NP)]
    for c, p in order:
        w = pre[c, p]
        w["s0"] = s[p]
        s[p] = s[p] * w["dec"] - _mm(s[p], w["pz"]) + w["q"]
    for c, p in order:
        w = pre[c, p]
        o_st = _mm(w["oa"], w["s0"], NT) + w["oc"]
        y_ref[c * C:(c + 1) * C, p * P2:(p + 1) * P2] = unstack(o_st)
    for p in range(NP):
        s_ref[p] = s[p]

    if rev:
        o = of_ref[...] + y_ref[...]
        bd = bd_ref[...]
        mu = _mm2(o, bd) * (1.0 / HD)
        oc = o - mu
        var = _mm2(oc * oc, bd) * (1.0 / HD)
        o = oc * lax.rsqrt(var + RW_GN_EPS) * lnw_ref[...] + lnb_ref[...]
        y_ref[...] = (o + bonus_ref[...]) * gs_ref[...]


def _rw_parts(rev, r, kk, v, lw, akk, kd, extra=()):
    rows = r.shape[0]
    nb = rows // MIX_ROWS
    order = _blk_order(rev, nb)
    P2 = 2 * RW_HD
    W = RW_HEADS * RW_HD
    blk = pl.BlockSpec((MIX_ROWS, W), lambda b: (order(b), 0))
    in_specs = [blk] * 6 + [pl.BlockSpec((MIX_ROWS, MIX_ROWS), lambda b: (0, 0))]
    args = [r, kk, v, lw, akk, kd, _seg_tri(MIX_ROWS, RW_CHUNK, rev)]
    if rev:
        o_fwd, bonus, gs, lnw, lnb, bd512 = extra
        vec = pl.BlockSpec((1, W), lambda b: (0, 0))
        in_specs += [blk, blk, blk, vec, vec, pl.BlockSpec((W, W), lambda b: (0, 0))]
        args += [o_fwd, bonus, gs, lnw, lnb, bd512]
    return dict(body=functools.partial(_rw_scan_kernel, rev), in_specs=in_specs, args=args, out_spec=blk,
                out_shape=jax.ShapeDtypeStruct((rows, W), F32),
                scratch=pltpu.VMEM((RW_HEADS // 2, P2, P2), F32))


def _qk_prep_kernel(nq_ref, nk_ref, nv_ref, wq_ref, wkv_ref, nqn_ref, nkn_ref, wqn_ref, wkn_ref,
                    cos_ref, sin_ref, bd512_ref, bd128_ref,
                    naq_o, nak_o, nav_o, waq_o, wak_o, wav_o):
    bd512 = bd512_ref[...]
    bd128 = bd128_ref[...]

    def hnorm(x, g, bd, hd):
        return x * lax.rsqrt(_mm2(x * x, bd) * (1.0 / hd) + EPS) * g

    def rope(x, cos, sin):
        lane = lax.broadcasted_iota(jnp.int32, x.shape, 1)
        w = x.shape[1]
        partner = jnp.where((lane & 31) < 16, pltpu.roll(x, w - 16, axis=1), pltpu.roll(x, 16, axis=1))
        return x * cos + partner * sin

    def dup_heads(o_ref, x):
        lower = lax.broadcasted_iota(jnp.int32, x.shape, 1) < WA_HD
        swapped = pltpu.roll(x, WA_HD, axis=1)
        o_ref[0] = jnp.where(lower, x, swapped).astype(o_ref.dtype)
        o_ref[1] = jnp.where(lower, swapped, x).astype(o_ref.dtype)

    f32 = lambda ref: ref[...].astype(F32)
    naq_o[...] = (hnorm(f32(nq_ref), nqn_ref[...], bd512, NA_HD) * NA_HD ** -0.5).astype(BF16)
    nak_o[...] = hnorm(f32(nk_ref), nkn_ref[...], bd512, NA_HD).astype(BF16)
    nav_o[...] = nv_ref[...]
    cos = cos_ref[...]
    sin = sin_ref[...]
    wq = hnorm(f32(wq_ref), wqn_ref[...], bd512, WA_HD) * WA_HD ** -0.5
    wq = rope(wq, jnp.concatenate([cos] * 4, axis=1), jnp.concatenate([sin] * 4, axis=1))
    waq_o[...] = wq.astype(BF16)
    kv = f32(wkv_ref)
    dup_heads(wak_o, rope(hnorm(kv[:, :128], wkn_ref[...], bd128, WA_HD), cos, sin))
    dup_heads(wav_o, kv[:, 128:])


def _qk_prep(p, nqn, nkn, wqn, wkn, cos, sin, bd512, bd128):
    rows = p.shape[0]
    nb = rows // MIX_ROWS
    nab = OFF_NA // 512
    full = lambda shape: pl.BlockSpec(shape, lambda i: (0,) * len(shape))
    in_specs = [pl.BlockSpec((MIX_ROWS, 512), lambda i: (i, nab)),
                pl.BlockSpec((MIX_ROWS, 512), lambda i: (i, nab + 1)),
                pl.BlockSpec((MIX_ROWS, 512), lambda i: (i, nab + 2)),
                pl.BlockSpec((MIX_ROWS, 512), lambda i: (i, OFF_WA // 512)),
                pl.BlockSpec((MIX_ROWS, 256), lambda i: (i, (OFF_WA + 512) // 256)),
                full((1, 512)), full((1, 512)), full((1, 512)), full((1, 128)),
                pl.BlockSpec((MIX_ROWS, 128), lambda i: (i, 0)),
                pl.BlockSpec((MIX_ROWS, 128), lambda i: (i, 0)),
                full((512, 512)), full((128, 128))]
    flat = pl.BlockSpec((MIX_ROWS, 512), lambda i: (i, 0))
    flat_shape = jax.ShapeDtypeStruct((rows, 512), BF16)
    dup = pl.BlockSpec((WA_KV_HEADS, MIX_ROWS, 128), lambda i: (0, i, 0))
    dup_shape = jax.ShapeDtypeStruct((WA_KV_HEADS, rows, 128), BF16)
    return pl.pallas_call(
        _qk_prep_kernel,
        grid=(nb,),
        in_specs=in_specs,
        out_specs=[flat, flat, flat, flat, dup, dup],
        out_shape=[flat_shape, flat_shape, flat_shape, flat_shape, dup_shape, dup_shape],
        compiler_params=_cparams(("arbitrary",)),
        name="qk_prep",
    )(p, p, p, p, p, nqn, nkn, wqn, wkn, cos, sin, bd512, bd128)


def _na_bias_layout():
    wr, G, KR = NA_WIN_R, NA_GROUP, NA_KEY_ROWS
    off = (0, -(wr // 2), -(KR - G))
    out = {}
    for kind in range(3):
        for r in range(G):
            wstart = (0, r, KR - wr)[kind]
            for a in range(KR):
                inside = wstart <= a < wstart + wr
                out[kind, r, a] = (off[kind] + a - r + wr - 1) if inside else None
    return out


def _na_kernel(lc, nrows, q_ref, k_ref, v_ref, t_ref, o_ref, bias_s):
    W = GRID_W
    HD = NA_HD
    nq = NA_GROUP * W
    nk = NA_KEY_ROWS * W
    ngroups = nrows // NA_GROUP
    kc = k_ref[0:lc, :]
    vc = v_ref[0:lc, :]
    half = [lax.broadcasted_iota(jnp.int32, (1, 2 * HD), 1) // HD == hh for hh in range(2)]

    bias_s[:, :, :, 0:lc] = jnp.zeros((2, 3, nq, lc), F32)
    for (kind, r, a), ro in _na_bias_layout().items():
        c0 = lc + a * W
        for hh in range(2):
            tile = jnp.full((W, W), NEG, F32) if ro is None else t_ref[hh, ro, :, (a % 2) * W:(a % 2 + 1) * W]
            bias_s[hh, kind, r * W:(r + 1) * W, c0:c0 + W] = tile

    def attend(q, keys, vals, bias):
        out = None
        for hh in range(2):
            s = _mm(jnp.where(half[hh], q, 0), keys, NT)
            if bias is not None:
                s = s + bias(hh)
            p = jnp.exp(s - jnp.max(s, axis=-1, keepdims=True))
            o = _mm(p, jnp.where(half[hh], vals, 0)) / jnp.sum(p, axis=-1, keepdims=True)
            out = o if out is None else out + o
        return out

    o_ref[0:lc, :] = attend(q_ref[0:lc, :], kc, vc, None)

    def group(gi, carry):
        base = jnp.clip(gi * NA_GROUP - NA_WIN_R // 2, 0, nrows - NA_KEY_ROWS)
        kind = jnp.where(gi == 0, 0, jnp.where(gi == ngroups - 1, 2, 1))
        q0 = pl.multiple_of(lc + gi * nq, nq)
        k0 = pl.multiple_of(lc + base * W, W)
        keys = jnp.concatenate([kc, k_ref[pl.ds(k0, nk), :]], axis=0)
        vals = jnp.concatenate([vc, v_ref[pl.ds(k0, nk), :]], axis=0)
        o_ref[pl.ds(q0, nq), :] = attend(q_ref[pl.ds(q0, nq), :], keys, vals, lambda hh: bias_s[hh, kind])
        return carry

    lax.fori_loop(0, ngroups, group, 0, unroll=2)


def _na(q, k, v, t2, lc):
    rows = q.shape[0]
    nrows = (rows - lc) // GRID_W
    pblk = pl.BlockSpec((rows, 2 * NA_HD), lambda p: (0, p))
    return pl.pallas_call(
        functools.partial(_na_kernel, lc, nrows),
        grid=(NA_HEADS // 2,),
        in_specs=[pblk, pblk, pblk, pl.BlockSpec((2,) + t2.shape[1:], lambda p: (p, 0, 0, 0))],
        out_specs=pblk,
        out_shape=jax.ShapeDtypeStruct((rows, NA_HEADS * NA_HD), F32),
        scratch_shapes=[pltpu.VMEM((2, 3, NA_GROUP * GRID_W, lc + NA_KEY_ROWS * GRID_W), F32)],
        compiler_params=_cparams(("arbitrary",)),
        name="na_attn",
    )(q, k, v, t2)


def _wa_kernel(lc, t_len, q_ref, k_ref, v_ref, sink_ref, o_ref):
    G = WA_HEADS // WA_KV_HEADS
    B = WA_WINDOW
    nband = 3 * B
    HD = WA_HD
    b = pl.program_id(1)
    kc = k_ref[0:lc, :]
    vc = v_ref[0:lc, :]
    sink = sink_ref[...]
    lower = lax.broadcasted_iota(jnp.int32, (1, 2 * HD), 1) < HD

    def stacked_queries(r0, n):
        parts = []
        for g in range(G):
            qp = q_ref[r0:r0 + n, (g // 2) * 2 * HD:(g // 2 + 1) * 2 * HD]
            parts.append(jnp.where(lower if g % 2 == 0 else jnp.logical_not(lower), qp, 0))
        return jnp.concatenate(parts, axis=0)

    def store(r0, n, o):
        for gp in range(G // 2):
            o_ref[r0:r0 + n, gp * 2 * HD:(gp + 1) * 2 * HD] = jnp.where(
                lower, o[2 * gp * n:(2 * gp + 1) * n], o[(2 * gp + 1) * n:(2 * gp + 2) * n])

    @pl.when(b == 0)
    def _():
        sk = jnp.concatenate([jnp.broadcast_to(sink[g * B:g * B + 1, :], (lc, 1)) for g in range(G)], axis=0)
        s = _mm(stacked_queries(0, lc), kc, NT)
        m = jnp.maximum(jnp.max(s, axis=-1, keepdims=True), sk)
        e = jnp.exp(s - m)
        store(0, lc, _mm(e, vc) / (jnp.sum(e, axis=-1, keepdims=True) + jnp.exp(sk - m)))

    @pl.when(b > 0)
    def _():
        qoff = lax.broadcasted_iota(jnp.int32, (G * B, lc + nband), 0) & (B - 1)
        koff = lax.broadcasted_iota(jnp.int32, (G * B, lc + nband), 1) - lc
        for j in range(MIX_ROWS // B):
            n = (b - 1) * (MIX_ROWS // B) + j
            start = jnp.clip((n - 1) * B, 0, t_len - nband)
            k0 = pl.multiple_of(lc + start, B)
            keys = jnp.concatenate([kc, k_ref[pl.ds(k0, nband), :]], axis=0)
            vals = jnp.concatenate([vc, v_ref[pl.ds(k0, nband), :]], axis=0)
            valid = jnp.logical_or(koff < 0, jnp.abs((n * B + qoff) - (start + koff)) <= WA_WINDOW)
            s = jnp.where(valid, _mm(stacked_queries(j * B, B), keys, NT), NEG)
            m = jnp.maximum(jnp.max(s, axis=-1, keepdims=True), sink)
            p = jnp.exp(s - m)
            store(j * B, B, _mm(p, vals) / (jnp.sum(p, axis=-1, keepdims=True) + jnp.exp(sink - m)))


def _wa(q, k, v, sink_col, lc):
    rows = q.shape[0]
    G = WA_HEADS // WA_KV_HEADS
    kvblk = pl.BlockSpec((None, rows, 2 * WA_HD), lambda h, b: (h, 0, 0))
    qblk = pl.BlockSpec((MIX_ROWS, G * WA_HD), lambda h, b: (b, h))
    return pl.pallas_call(
        functools.partial(_wa_kernel, lc, rows - lc),
        grid=(WA_KV_HEADS, rows // MIX_ROWS),
        in_specs=[qblk, kvblk, kvblk,
                  pl.BlockSpec((None, G * WA_WINDOW, 1), lambda h, b: (h, 0, 0))],
        out_specs=qblk,
        out_shape=jax.ShapeDtypeStruct((rows, WA_HEADS * WA_HD), F32),
        compiler_params=_cparams(("arbitrary", "arbitrary")),
        name="wa_attn",
    )(q, k, v, sink_col)


def _block_diag(n, blk):
    idx = np.arange(n) // blk
    return jnp.asarray(idx[:, None] == idx[None, :], dtype=BF16)


def _na_bias_table(rpb):
    W, wc = GRID_W, NA_WIN_C
    j = np.arange(W)
    cstart = np.clip(j - wc // 2, 0, W - wc)
    cabs = np.arange(W)
    inwin = (cabs[None, :] >= cstart[:, None]) & (cabs[None, :] < cstart[:, None] + wc)
    cb = cabs[None, :] - j[:, None] + wc - 1
    nh, nro, nco = rpb.shape
    onehot = (cb[None] == np.arange(nco)[:, None, None]) & inwin[None]
    t = jnp.dot(rpb.reshape(nh * nro, nco), jnp.asarray(onehot.reshape(nco, W * W), F32),
                precision=lax.Precision.HIGHEST).reshape(nh, nro, W, W)
    t = t + jnp.asarray(np.where(inwin, 0.0, NEG), F32)
    return jnp.concatenate([t, t], axis=-1)


def _rope_tables(lc, t_len):
    pos = np.arange(t_len)
    n = WA_HD // 4
    inv = ROPE_BASE ** (-np.arange(n, dtype=np.float64) / n)
    ang_r = (pos // GRID_W)[:, None] * inv[None, :]
    ang_c = (pos % GRID_W)[:, None] * inv[None, :]
    cos = np.concatenate([np.cos(ang_r)] * 2 + [np.cos(ang_c)] * 2, axis=1)
    sin = np.concatenate([-np.sin(ang_r), np.sin(ang_r), -np.sin(ang_c), np.sin(ang_c)], axis=1)
    cos = np.concatenate([np.ones((lc, WA_HD)), cos], axis=0)
    sin = np.concatenate([np.zeros((lc, WA_HD)), sin], axis=0)
    return (jnp.asarray(np.tile(cos, (1, 2)), dtype=F32), jnp.asarray(np.tile(sin, (1, 2)), dtype=F32))


def _cast_kernel(x_ref, o_ref):
    o_ref[...] = x_ref[...].astype(o_ref.dtype)


def _cast_bf16(w, layer, bk, bn, src_col_block=None):
    _, k, n = w.shape
    col = src_col_block or (lambda j: j)
    return pl.pallas_call(
        _cast_kernel,
        grid=(k // bk, n // bn),
        in_specs=[pl.BlockSpec((None, bk, bn), lambda i, j: (layer, i, col(j)))],
        out_specs=pl.BlockSpec((bk, bn), lambda i, j: (i, j)),
        out_shape=jax.ShapeDtypeStruct((k, n), BF16),
        compiler_params=_cparams(("arbitrary", "arbitrary")),
        name="cast_bf16",
    )(w)


W_IN_BLK = 256
_W_IN_SRC = np.concatenate([np.arange(10, 18), np.arange(0, 10), np.arange(19, 25), np.arange(28, 60),
                            np.arange(25, 28), np.arange(18, 19)])


def _w_in_src_block(j):
    starts = (0, 8, 18, 24, 56, 59)
    src = j + int(_W_IN_SRC[0])
    for s in starts[1:]:
        src = jnp.where(j >= s, j + int(_W_IN_SRC[s]) - s, src)
    return src


def _lora_pad(w, row0):
    out = jnp.zeros((2, 256, w.shape[-1]), F32)
    for d in range(2):
        out = out.at[d, row0 + 64 * d:row0 + 64 * (d + 1)].set(w[d])
    return out


def _token_mixing(p, lc, layer, hg_lb, hg_norm, rw_shift, rw_w0, rw_w2, rw_a0, rw_a2, rw_kk, rw_ka, rw_rk,
                  rw_ln_w, rw_ln_b, na_qn, na_kn, na_rpb, wa_qn, wa_kn, wa_sink, tables):
    rows = p.shape[0]
    bd512, bd128, cos, sin = tables
    cum = jnp.cumsum(jax.nn.softmax(hg_lb.astype(F32), axis=1), axis=1)
    lbs = cum[:, layer] - cum[:, 0]
    prep = _rw_prep(p, rw_shift[:, :2048], rw_shift[:, 2048:], rw_w0, _lora_pad(rw_w2, 0), rw_a0,
                    _lora_pad(rw_a2, 128), rw_kk[None], rw_ka[None], rw_rk.reshape(1, -1), bd512)
    r, kk, v, gs, bonus, lw0, lw1, akk0, akk1, kd0, kd1 = prep
    oa_f, ob_f = _run_scans([_gla_parts(p, lbs[0:1], False), _rw_parts(False, r, kk, v, lw0, akk0, kd0)],
                            rows, "scan_fwd")
    y_a, y_b = _run_scans([_gla_parts(p, lbs[1:2], True, oa_f, hg_norm[None]),
                           _rw_parts(True, r, kk, v, lw1, akk1, kd1,
                                     (ob_f, bonus, gs, rw_ln_w[None], rw_ln_b[None], bd512))],
                          rows, "scan_rev")
    tile = lambda g, n: jnp.tile(g, n)[None]
    naq, nak, nav, waq, wak, wav = _qk_prep(p, tile(na_qn, 8), tile(na_kn, 8), tile(wa_qn, 8),
                                            tile(wa_kn, 2), cos, sin, bd512, bd128)
    y_c = _na(naq, nak, nav, _na_bias_table(na_rpb), lc)
    G = WA_HEADS // WA_KV_HEADS
    sink_col = jnp.repeat(wa_sink.reshape(WA_KV_HEADS, G), WA_WINDOW, axis=1)[..., None]
    y_d = _wa(waq, wak, wav, sink_col, lc)
    return y_a, y_b, y_c, y_d


def kernel(x, c, ctx, c_ctx, ada_w, ada_b, norm_ffn1, norm_mix, norm_ffn2, ffn1_wi, ffn1_wo, ffn2_wi, ffn2_wo, w_in, hg_lb, hg_norm, rw_shift, rw_w0, rw_w2, rw_a0, rw_a2, rw_kk, rw_ka, rw_rk, rw_ln_w, rw_ln_b, na_qn, na_kn, na_rpb, wa_qn, wa_kn, wa_sink, w_branch, w_out):
    assert x.shape[0] == 1 and ctx.shape[1] == MIX_ROWS
    lc = ctx.shape[1]
    t_len = x.shape[1]
    xa = jnp.concatenate([ctx[0], x[0]], axis=0)
    cc_t = jnp.stack([c[0], c_ctx], axis=1)
    mods = _ada_mods(cc_t, ada_w, ada_b).reshape(DEPTH, 2, N_MOD, D)
    tables = (_block_diag(512, 64), _block_diag(128, 64)) + _rope_tables(lc, t_len)
    half = W_IN_BLK // 2
    w_in_src = lambda c: 2 * _w_in_src_block(c // 2) + c % 2
    wi1, wo1 = _cast_bf16(ffn1_wi, 0, D, 1024), _cast_bf16(ffn1_wo, 0, 512, D)
    for l in range(DEPTH):
        m = mods[l]
        xa, w_in_l = _ffn(xa, m[:, 0:3], norm_ffn1[l][None], wi1, wo1, lc,
                          jobs=[_cast_job(w_in, l, D, half, w_in_src)])
        jobs = [_cast_job(w_branch.reshape(DEPTH, N_BRANCH * BR_W, D), l, 256, 256), _cast_job(w_out, l, 256, 256),
                _cast_job(ffn2_wi, l, D, half), _cast_job(ffn2_wo, l, 64, D)]
        if l + 1 < DEPTH:
            jobs += [_cast_job(ffn1_wi, l + 1, D, half), _cast_job(ffn1_wo, l + 1, 64, D)]
        p, wb, w_out_l, wi2, wo2, *nxt = _win(xa, m[:, 3:6], norm_mix[l][None], w_in_l, lc, jobs=jobs)
        if nxt:
            wi1, wo1 = nxt
        yb = _token_mixing(p, lc, l, hg_lb, hg_norm[l], rw_shift[l], rw_w0[l], rw_w2[l], rw_a0[l], rw_a2[l],
                           rw_kk[l], rw_ka[l], rw_rk[l], rw_ln_w[l], rw_ln_b[l], na_qn[l], na_kn[l],
                           na_rpb[l], wa_qn[l], wa_kn[l], wa_sink[l], tables)
        xa = _merge(xa, m[:, 3:6], yb, p, wb.reshape(N_BRANCH, BR_W, D), w_out_l, lc)
        xa, = _ffn(xa, m[:, 6:9], norm_ffn2[l][None], wi2, wo2, lc)
    return xa[lc:][None]
```

```python
import functools

import numpy as np
import jax
import jax.numpy as jnp
from jax import lax
from jax.experimental import pallas as pl
from jax.experimental.pallas import tpu as pltpu

F32 = jnp.float32
BF16 = jnp.bfloat16

D = 2048
DEPTH = 2
GRID_W = 64
EPS = 1e-6
D_FF = 5632
N_MOD = 9
N_BRANCH = 4
BR_W = 512
HG_HEADS, HG_DK = 4, 128
RW_HEADS, RW_HD = 8, 64
RW_LORA = 64
RW_GN_EPS = 64e-5
NA_HEADS, NA_HD = 8, 64
NA_WIN_R, NA_WIN_C = 8, 16
NA_GROUP = 4
NA_KEY_ROWS = NA_WIN_R + NA_GROUP - 1
WA_HEADS, WA_KV_HEADS, WA_HD = 8, 2, 64
WA_WINDOW = 128
ROPE_BASE = 10000.0

OFF_RW, OFF_HG, OFF_NA, OFF_GL, OFF_WA, OFF_LORA = 0, 2048, 4608, 6144, 14336, 15104
P_TOTAL = 15360

MIX_ROWS = 256
HALO = 16
GLA_CHUNK = 16
RW_CHUNK = 64
NEG = -1e30
LOG2E = 1.4426950408889634
VMEM_LIMIT = 60 * 1024 * 1024

NT = (((1,), (1,)), ((), ()))
TN = (((0,), (0,)), ((), ()))


def _cparams(sem):
    return pltpu.CompilerParams(dimension_semantics=sem, vmem_limit_bytes=VMEM_LIMIT)


def _mm(a, b, dims=None):
    a = a.astype(BF16)
    b = b.astype(BF16)
    if dims is None:
        return jnp.dot(a, b, preferred_element_type=F32)
    return lax.dot_general(a, b, dims, preferred_element_type=F32)


def _mm2(x, w_bf16):
    hi = x.astype(BF16)
    lo = (x - hi.astype(F32)).astype(BF16)
    return (jnp.dot(hi, w_bf16, preferred_element_type=F32)
            + jnp.dot(lo, w_bf16, preferred_element_type=F32))


def _mm_f32(a, b):
    a_hi, b_hi = a.astype(BF16), b.astype(BF16)
    a_lo = (a - a_hi.astype(F32)).astype(BF16)
    b_lo = (b - b_hi.astype(F32)).astype(BF16)
    dot = functools.partial(jnp.dot, preferred_element_type=F32)
    return dot(a_hi, b_hi) + (dot(a_hi, b_lo) + dot(a_lo, b_hi))


def _sigmoid(x):
    return 1.0 / (1.0 + jnp.exp(-x))


def _silu(x):
    return x * _sigmoid(x)


def _seg_cumsum(x, tri):
    w = x.shape[1]
    hi = x.astype(BF16)
    r1 = x - hi.astype(F32)
    mid = r1.astype(BF16)
    lo = (r1 - mid.astype(F32)).astype(BF16)
    y = jnp.dot(tri, jnp.concatenate([hi, mid, lo], axis=1), preferred_element_type=F32)
    return y[:, :w] + y[:, w:2 * w] + y[:, 2 * w:]


def _seg_tri(rows, seg, rev):
    r = np.arange(rows)
    same = (r[:, None] // seg) == (r[None, :] // seg)
    tri = (r[None, :] >= r[:, None]) if rev else (r[None, :] <= r[:, None])
    return jnp.asarray(same & tri, dtype=BF16)


def _ada_kernel(cc_ref, w_ref, b_ref, o_ref):
    s = _silu(cc_ref[...])
    w = w_ref[...]
    r0 = jnp.sum(w * s[:, 0:1], axis=0, keepdims=True)
    r1 = jnp.sum(w * s[:, 1:2], axis=0, keepdims=True)
    o_ref[...] = jnp.concatenate([r0, r1], axis=0) + b_ref[...]


def _ada_mods(cc_t, ada_w, ada_b):
    tn = 1024
    nmod = ada_w.shape[-1]
    return pl.pallas_call(
        _ada_kernel,
        grid=(DEPTH, nmod // tn),
        in_specs=[pl.BlockSpec((D, 2), lambda l, j: (0, 0)),
                  pl.BlockSpec((None, D, tn), lambda l, j: (l, 0, j)),
                  pl.BlockSpec((None, 1, tn), lambda l, j: (l, 0, j))],
        out_specs=pl.BlockSpec((None, 2, tn), lambda l, j: (l, 0, j)),
        out_shape=jax.ShapeDtypeStruct((DEPTH, 2, nmod), F32),
        compiler_params=_cparams(("arbitrary", "arbitrary")),
        name="ada_mod",
    )(cc_t, ada_w, ada_b.reshape(DEPTH, 1, nmod))


def _cast_job(w, layer, bk, bn, src_col_block=None):
    return (w, layer, bk, bn, src_col_block)


def _cast_job_specs(jobs, grid):
    in_specs, out_specs, out_shapes, args = [], [], [], []
    for w, layer, bk, bn, src_col in jobs:
        _, k, n = w.shape
        ncol = n // bn
        nblk = (k // bk) * ncol
        assert nblk <= grid[0] * grid[1], (nblk, grid)
        col = src_col or (lambda c: c)

        def blk(i, j, nblk=nblk):
            return jnp.minimum(i * grid[1] + j, nblk - 1)

        in_specs.append(pl.BlockSpec(
            (None, bk, bn), lambda i, j, blk=blk, ncol=ncol, col=col, layer=layer:
            (layer, blk(i, j) // ncol, col(blk(i, j) % ncol))))
        out_specs.append(pl.BlockSpec((bk, bn), lambda i, j, blk=blk, ncol=ncol: (blk(i, j) // ncol, blk(i, j) % ncol)))
        out_shapes.append(jax.ShapeDtypeStruct((k, n), BF16))
        args.append(w)
    return in_specs, out_specs, out_shapes, args


def _run_cast_jobs(src_refs, dst_refs):
    for src, dst in zip(src_refs, dst_refs):
        dst[...] = src[...].astype(dst.dtype)


def _row_is_ctx(i, tm, lc):
    return (i * tm + lax.broadcasted_iota(jnp.int32, (tm, 1), 0)) < lc


def _mod_pick(mod_ref, j, is_ctx):
    return jnp.where(is_ctx, mod_ref[1, j:j + 1, :], mod_ref[0, j:j + 1, :])


def _store_modulated(xn_ref, x_ref, g_ref, mod_ref, i, tm, lc):
    def normed():
        x = x_ref[...]
        return x * lax.rsqrt(jnp.mean(x * x, axis=-1, keepdims=True) + EPS)

    @pl.when(i * tm < lc)
    def _():
        is_ctx = _row_is_ctx(i, tm, lc)
        y = normed() * g_ref[...]
        xn_ref[...] = (y * (1.0 + _mod_pick(mod_ref, 1, is_ctx)) + _mod_pick(mod_ref, 0, is_ctx)).astype(BF16)

    @pl.when(i * tm >= lc)
    def _():
        gain = g_ref[...] * (1.0 + mod_ref[0, 1:2, :])
        xn_ref[...] = (normed() * gain + mod_ref[0, 0:1, :]).astype(BF16)


def _ffn_kernel(lc, tm, nf, njobs, x_ref, mod_ref, g_ref, wa_ref, wb_ref, wo_ref, *rest):
    src_refs, (o_ref, *dst_refs, xn_ref) = rest[:njobs], rest[njobs:]
    i = pl.program_id(0)
    j = pl.program_id(1)
    _run_cast_jobs(src_refs, dst_refs)

    @pl.when(j == 0)
    def _():
        _store_modulated(xn_ref, x_ref, g_ref, mod_ref, i, tm, lc)
        o_ref[...] = jnp.zeros_like(o_ref)

    xn = xn_ref[...]
    a = jnp.dot(xn, wa_ref[...], preferred_element_type=F32)
    b = jnp.dot(xn, wb_ref[...], preferred_element_type=F32)
    h = (_silu(a) * b).astype(BF16)
    o_ref[...] += jnp.dot(h, wo_ref[...], preferred_element_type=F32)

    @pl.when(j == nf - 1)
    def _():
        o_ref[...] = x_ref[...] + 0.5 * _mod_pick(mod_ref, 2, _row_is_ctx(i, tm, lc)) * o_ref[...]


def _dense_tm(rows):
    for tm in (768, 512, 256):
        if rows % tm == 0:
            return tm
    raise ValueError(rows)


def _ffn(x, mod3, g, wi, wo, lc, jobs=()):
    rows = x.shape[0]
    tm = _dense_tm(rows)
    tf = 512
    nf = D_FF // tf
    grid = (rows // tm, nf)
    job_in, job_out, job_shape, job_args = _cast_job_specs(jobs, grid)
    return pl.pallas_call(
        functools.partial(_ffn_kernel, lc, tm, nf, len(jobs)),
        grid=grid,
        in_specs=[pl.BlockSpec((tm, D), lambda i, j: (i, 0)),
                  pl.BlockSpec((2, 3, D), lambda i, j: (0, 0, 0)),
                  pl.BlockSpec((1, D), lambda i, j: (0, 0)),
                  pl.BlockSpec((D, tf), lambda i, j: (0, j)),
                  pl.BlockSpec((D, tf), lambda i, j: (0, j + nf)),
                  pl.BlockSpec((tf, D), lambda i, j: (j, 0))] + job_in,
        out_specs=[pl.BlockSpec((tm, D), lambda i, j: (i, 0))] + job_out,
        out_shape=[jax.ShapeDtypeStruct((rows, D), F32)] + job_shape,
        scratch_shapes=[pltpu.VMEM((tm, D), BF16)],
        compiler_params=_cparams(("arbitrary", "arbitrary")),
        name="ffn",
    )(x, mod3, g, wi, wi, wo, *job_args)


def _win_kernel(lc, tm, njobs, x_ref, mod_ref, g_ref, w_ref, *rest):
    src_refs, (o_ref, *dst_refs, xn_ref) = rest[:njobs], rest[njobs:]
    i = pl.program_id(0)
    _run_cast_jobs(src_refs, dst_refs)

    @pl.when(pl.program_id(1) == 0)
    def _():
        _store_modulated(xn_ref, x_ref, g_ref, mod_ref, i, tm, lc)

    o_ref[...] = jnp.dot(xn_ref[...], w_ref[...], preferred_element_type=F32).astype(o_ref.dtype)


def _win(x, mod3, g, w, lc, jobs=()):
    rows = x.shape[0]
    tm = _dense_tm(rows)
    tn = 1536
    grid = (rows // tm, P_TOTAL // tn)
    job_in, job_out, job_shape, job_args = _cast_job_specs(jobs, grid)
    return pl.pallas_call(
        functools.partial(_win_kernel, lc, tm, len(jobs)),
        grid=grid,
        in_specs=[pl.BlockSpec((tm, D), lambda i, j: (i, 0)),
                  pl.BlockSpec((2, 3, D), lambda i, j: (0, 0, 0)),
                  pl.BlockSpec((1, D), lambda i, j: (0, 0)),
                  pl.BlockSpec((D, tn), lambda i, j: (0, j))] + job_in,
        out_specs=[pl.BlockSpec((tm, tn), lambda i, j: (i, j))] + job_out,
        out_shape=[jax.ShapeDtypeStruct((rows, P_TOTAL), BF16)] + job_shape,
        scratch_shapes=[pltpu.VMEM((tm, D), BF16)],
        compiler_params=_cparams(("arbitrary", "arbitrary")),
        name="w_in",
    )(x, mod3, g, w, *job_args)


def _merge_kernel(lc, tm, x_ref, mod_ref, ya_ref, yb_ref, yc_ref, yd_ref, gl0_ref, gl1_ref, wb_ref, wo_ref,
                  o_ref, acc_ref):
    i = pl.program_id(0)
    n = pl.program_id(1)
    y_refs = (ya_ref, yb_ref, yc_ref, yd_ref)
    gl_refs = (gl0_ref, gl1_ref)

    for nn in range(N_BRANCH // 2):
        @pl.when(n == nn)
        def _():
            gated = None
            for k in range(2):
                br = 2 * nn + k
                proj = jnp.dot(y_refs[br][...].astype(BF16), wb_ref[br], preferred_element_type=F32)
                term = _sigmoid(gl_refs[k][...].astype(F32)) * proj
                gated = term if gated is None else gated + term
            if nn == 0:
                acc_ref[...] = gated
            else:
                acc_ref[...] += gated

    @pl.when(n == N_BRANCH // 2 - 1)
    def _():
        is_ctx = _row_is_ctx(i, tm, lc)
        y = jnp.dot(acc_ref[...].astype(BF16), wo_ref[...], preferred_element_type=F32)
        o_ref[...] = x_ref[...] + _mod_pick(mod_ref, 2, is_ctx) * y


def _merge(x, mod3, ys, p, wb, wo, lc):
    rows = x.shape[0]
    tm = 384 if rows % 384 == 0 else MIX_ROWS
    glb = OFF_GL // D
    yspec = pl.BlockSpec((tm, BR_W), lambda i, n: (i, 0))
    return pl.pallas_call(
        functools.partial(_merge_kernel, lc, tm),
        grid=(rows // tm, N_BRANCH // 2),
        in_specs=[pl.BlockSpec((tm, D), lambda i, n: (i, 0)),
                  pl.BlockSpec((2, 3, D), lambda i, n: (0, 0, 0)),
                  yspec, yspec, yspec, yspec,
                  pl.BlockSpec((tm, D), lambda i, n: (i, glb + 2 * n)),
                  pl.BlockSpec((tm, D), lambda i, n: (i, glb + 2 * n + 1)),
                  pl.BlockSpec((N_BRANCH, BR_W, D), lambda i, n: (0, 0, 0)),
                  pl.BlockSpec((D, D), lambda i, n: (0, 0))],
        out_specs=pl.BlockSpec((tm, D), lambda i, n: (i, 0)),
        out_shape=jax.ShapeDtypeStruct((rows, D), F32),
        scratch_shapes=[pltpu.VMEM((tm, D), F32)],
        compiler_params=_cparams(("arbitrary", "arbitrary")),
        name="merge",
    )(x, mod3, *ys, p, p, wb, wo)


def _gla_kernel(rev, *refs):
    if rev:
        q_ref, f_ref, i_ref, lb_ref, tri_ref, of_ref, g_ref, nw_ref, y_ref, st_ref = refs
    else:
        q_ref, f_ref, i_ref, lb_ref, tri_ref, y_ref, st_ref = refs
    C = GLA_CHUNK
    R = MIX_ROWS
    NH, DK = HG_HEADS, HG_DK

    @pl.when(pl.program_id(0) == 0)
    def _():
        st_ref[...] = jnp.zeros_like(st_ref)

    yield
    lb = lb_ref[...]
    f_all = lb + (1.0 - lb) * _sigmoid(f_ref[...].astype(F32))
    b_all = _seg_cumsum(jnp.log(f_all), tri_ref[...])
    SUB = C // 2
    sub_pos = lax.broadcasted_iota(jnp.int32, (1, SUB, 1), 1)
    chunks = list(range(R // C - 1, -1, -1) if rev else range(R // C))

    heads = []
    for h in range(NH):
        hs = slice(h * DK, (h + 1) * DK)
        b = b_all[:, hs]
        k = 1.0 - f_all[:, hs]
        q = q_ref[:, hs].astype(F32)
        v = i_ref[:, hs].astype(F32)
        def halves(t):
            t3 = t.reshape(R // C, C, DK)
            lo, hi = t3[:, :SUB], t3[:, SUB:]
            return (hi, lo) if rev else (lo, hi)

        def pair_term(qq, bq, keys, ok):
            bk_, kk_, vk_ = keys
            diff = bq - bk_
            e = jnp.exp2(diff if ok is None else jnp.where(ok, diff, NEG))
            return jnp.sum(qq * kk_ * e, axis=-1, keepdims=True) * vk_

        (qn, qf), (bn, bf), (kn, kf), (vn, vf) = halves(q), halves(b * LOG2E), halves(k), halves(v)
        near, far = (bn, kn, vn), (bf, kf, vf)
        o_n = jnp.sum(qn * kn, axis=-1, keepdims=True) * vn
        o_f = jnp.sum(qf * kf, axis=-1, keepdims=True) * vf + pair_term(qf, bf, near, None)
        for d in range(1, SUB):
            sh = (SUB - d) if rev else d
            ok = (sub_pos < SUB - d) if rev else (sub_pos >= d)
            near_r = tuple(pltpu.roll(t, sh, axis=1) for t in near)
            far_r = tuple(jnp.where(ok, pltpu.roll(t, sh, axis=1), tn) for t, tn in zip(far, near_r))
            o_n = o_n + pair_term(qn, bn, near_r, ok)
            o_f = o_f + pair_term(qf, bf, far_r, None) + pair_term(qf, bf, near_r, ok)
        o = jnp.concatenate([o_f, o_n] if rev else [o_n, o_f], axis=1).reshape(R, DK)
        upd, dec = {}, {}
        for c in chunks:
            sl = slice(c * C, (c + 1) * C)
            b_end = b[sl][0:1] if rev else b[sl][C - 1:C]
            dec[c] = jnp.exp(b_end)
            upd[c] = _mm(v[sl], k[sl] * jnp.exp(b_end - b[sl]), TN)
        heads.append(dict(o=o, qb=q * jnp.exp(b), upd=upd, dec=dec, st=st_ref[h]))

    for c in chunks:
        for w in heads:
            w[c] = w["st"]
            w["st"] = w["st"] * w["dec"][c] + w["upd"][c]
    outs = []
    for h, w in enumerate(heads):
        st_ref[h] = w["st"]
        inter = {c: _mm(w["qb"][c * C:(c + 1) * C], w[c], NT) for c in chunks}
        outs.append(w["o"] + jnp.concatenate([inter[c] for c in range(R // C)], axis=0))

    for h, o in enumerate(outs):
        hs = slice(h * DK, (h + 1) * DK)
        if rev:
            o = of_ref[:, hs] + o
            o = o * lax.rsqrt(jnp.mean(o * o, axis=-1, keepdims=True) + EPS) * nw_ref[:, hs]
            o = o * _silu(g_ref[:, hs].astype(F32))
        y_ref[:, hs] = o


def _blk_order(rev, nb):
    if rev:
        return lambda b: jnp.where(b == 0, 0, nb - b)
    return lambda b: b


def _gla_parts(p, lb, rev, o_fwd=None, norm_w=None):
    rows = p.shape[0]
    nb = rows // MIX_ROWS
    order = _blk_order(rev, nb)
    W = HG_HEADS * HG_DK
    cb = OFF_HG // W

    def col(n):
        return pl.BlockSpec((MIX_ROWS, W), lambda b: (order(b), cb + n))

    vec = pl.BlockSpec((1, W), lambda b: (0, 0))
    in_specs = [col(0), col(2 if rev else 1), col(3), vec,
                pl.BlockSpec((MIX_ROWS, MIX_ROWS), lambda b: (0, 0))]
    args = [p, p, p, lb, _seg_tri(MIX_ROWS, GLA_CHUNK, rev)]
    if rev:
        in_specs += [pl.BlockSpec((MIX_ROWS, W), lambda b: (order(b), 0)), col(4), vec]
        args += [o_fwd, p, norm_w]
    return dict(body=functools.partial(_gla_kernel, rev), in_specs=in_specs, args=args,
                out_spec=pl.BlockSpec((MIX_ROWS, W), lambda b: (order(b), 0)),
                out_shape=jax.ShapeDtypeStruct((rows, W), F32),
                scratch=pltpu.VMEM((HG_HEADS, HG_DK, HG_DK), F32))


def _run_scans(parts, rows, name):
    n_in = [len(p["in_specs"]) for p in parts]

    def kernel(*refs):
        pos = 0
        ins = []
        for n in n_in:
            ins.append(refs[pos:pos + n])
            pos += n
        outs = refs[pos:pos + len(parts)]
        scratch = refs[pos + len(parts):]
        bodies = [p["body"](*i, o, s) for p, i, o, s in zip(parts, ins, outs, scratch)]
        for body in bodies:
            next(body)
        for body in bodies:
            for _ in body:
                pass

    return pl.pallas_call(
        kernel,
        grid=(rows // MIX_ROWS,),
        in_specs=[s for p in parts for s in p["in_specs"]],
        out_specs=[p["out_spec"] for p in parts],
        out_shape=[p["out_shape"] for p in parts],
        scratch_shapes=[p["scratch"] for p in parts],
        compiler_params=_cparams(("arbitrary",)),
        name=name,
    )(*[a for p in parts for a in p["args"]])


def _rw_prep_kernel(nb, main_ref, lora_ref, pm_ref, nm_ref, pl_ref, nl_ref, tm_ref, tl_ref,
                    w0_ref, w2_ref, a0_ref, a2_ref, kkw_ref, ka_ref, rk_ref, bd_ref,
                    r_o, kk_o, v_o, gs_o, bonus_o, lw0_o, lw1_o, akk0_o, akk1_o, kd0_o, kd1_o):
    i = pl.program_id(0)
    has_prev = (i >= 2).astype(F32)
    has_next = jnp.logical_and(i != 0, i != nb - 1).astype(F32)
    rowi = lax.broadcasted_iota(jnp.int32, (MIX_ROWS, 1), 0)

    def shift(x, prev_blk, next_blk, taps):
        up = jnp.where(rowi == 0, prev_blk[HALO - 1:HALO, :] * has_prev, pltpu.roll(x, 1, axis=0))
        dn = jnp.where(rowi == MIX_ROWS - 1, next_blk[0:1, :] * has_next,
                       pltpu.roll(x, MIX_ROWS - 1, axis=0))
        return taps[0:1] * up + taps[1:2] * x + taps[2:3] * dn

    f32 = lambda ref: ref[...].astype(F32)
    main = shift(f32(main_ref), f32(pm_ref), f32(nm_ref), tm_ref[...])
    lora = shift(f32(lora_ref), f32(pl_ref), f32(nl_ref), tl_ref[...])
    W = RW_HEADS * RW_HD
    r, k, v, g = (main[:, n * W:(n + 1) * W] for n in range(4))
    bd = bd_ref[...]
    kk = k * kkw_ref[...]
    kk = kk * lax.rsqrt(_mm2(kk * kk, bd) + EPS)
    tl = jnp.tanh(lora)
    kds = []
    for d, (lw_o, akk_o, kd_o) in enumerate(((lw0_o, akk0_o, kd0_o), (lw1_o, akk1_o, kd1_o))):
        z = -(w0_ref[d:d + 1, :] + _mm_f32(tl, w2_ref[d]))
        softplus = jnp.maximum(z, 0.0) + jnp.log(1.0 + jnp.exp(-jnp.abs(z)))
        lw_o[...] = -jnp.exp(-softplus - 0.5)
        a = _sigmoid(a0_ref[d:d + 1, :] + _mm_f32(lora, a2_ref[d]))
        kd = k * (1.0 + (a - 1.0) * ka_ref[...])
        kds.append(kd)
        kd_o[...] = kd.astype(kd_o.dtype)
        akk_o[...] = (a * kk).astype(akk_o.dtype)
    r_o[...] = r.astype(r_o.dtype)
    kk_o[...] = kk.astype(kk_o.dtype)
    v_o[...] = v.astype(v_o.dtype)
    gs_o[...] = _sigmoid(g)
    bonus_o[...] = _mm2(r * (kds[0] + kds[1]) * rk_ref[...], bd) * v


def _rw_prep(p, taps_m, taps_l, w0, w2p, a0, a2p, kkw, ka, rk, bd512):
    rows = p.shape[0]
    nb = rows // MIX_ROWS
    W = RW_HEADS * RW_HD
    n8 = rows // HALO
    lb = OFF_LORA // 256
    per = MIX_ROWS // HALO
    full = lambda shape: pl.BlockSpec(shape, lambda i: (0,) * len(shape))
    in_specs = [pl.BlockSpec((MIX_ROWS, 4 * W), lambda i: (i, 0)),
                pl.BlockSpec((MIX_ROWS, 256), lambda i: (i, lb)),
                pl.BlockSpec((HALO, 4 * W), lambda i: (jnp.maximum(i * per - 1, 0), 0)),
                pl.BlockSpec((HALO, 4 * W), lambda i: (jnp.minimum((i + 1) * per, n8 - 1), 0)),
                pl.BlockSpec((HALO, 256), lambda i: (jnp.maximum(i * per - 1, 0), lb)),
                pl.BlockSpec((HALO, 256), lambda i: (jnp.minimum((i + 1) * per, n8 - 1), lb)),
                full((3, 4 * W)), full((3, 256)),
                full((2, W)), full((2, 256, W)), full((2, W)), full((2, 256, W)),
                full((1, W)), full((1, W)), full((1, W)), full((W, W))]
    f32_out = jax.ShapeDtypeStruct((rows, W), F32)
    mxu_out = jax.ShapeDtypeStruct((rows, W), BF16)
    return pl.pallas_call(
        functools.partial(_rw_prep_kernel, nb),
        grid=(nb,),
        in_specs=in_specs,
        out_specs=[pl.BlockSpec((MIX_ROWS, W), lambda i: (i, 0))] * 11,
        out_shape=[mxu_out] * 3 + [f32_out] * 4 + [mxu_out] * 4,
        compiler_params=_cparams(("arbitrary",)),
        name="rw_prep",
    )(p, p, p, p, p, p, taps_m, taps_l, w0, w2p, a0, a2p, kkw, ka, rk, bd512)


def _rw_scan_kernel(rev, *refs):
    if rev:
        (r_ref, kk_ref, v_ref, lw_ref, akk_ref, kd_ref, tri_ref, of_ref, bonus_ref, gs_ref,
         lnw_ref, lnb_ref, bd_ref, y_ref, s_ref) = refs
    else:
        r_ref, kk_ref, v_ref, lw_ref, akk_ref, kd_ref, tri_ref, y_ref, s_ref = refs
    C = RW_CHUNK
    HD = RW_HD
    P2 = 2 * HD

    NP = RW_HEADS // 2

    @pl.when(pl.program_id(0) == 0)
    def _():
        s_ref[...] = jnp.zeros_like(s_ref)

    yield
    lw = lw_ref[...]
    cum = _seg_cumsum(lw, tri_ref[...])
    cum_prev = cum - lw

    lane = lax.broadcasted_iota(jnp.int32, (1, P2), 1)
    h0 = lane < HD
    ri = lax.broadcasted_iota(jnp.int32, (P2, P2), 0)
    ci = lax.broadcasted_iota(jnp.int32, (P2, P2), 1)
    same = (ri // C) == (ci // C)
    rt, cs = ri % C, ci % C
    strict = jnp.logical_and(same, (rt < cs) if rev else (rt > cs))
    incl = jnp.logical_and(same, (rt <= cs) if rev else (rt >= cs))
    eye = (ri == ci).astype(F32)
    bdiag = ((ri // HD) == (ci // HD)).astype(F32)

    def stack(x):
        return jnp.concatenate([x, x], axis=0)

    def split_heads(x):
        return jnp.concatenate([jnp.where(h0, x, 0.0), jnp.where(h0, 0.0, x)], axis=0)

    def unstack(x):
        return jnp.where(h0, x[:C], x[C:])

    chunks = list(range(MIX_ROWS // C - 1, -1, -1) if rev else range(MIX_ROWS // C))
    order = [(c, p) for c in chunks for p in range(NP)]

    pre = {}
    for c, p in order:
        sl = slice(c * C, (c + 1) * C)
        pp = slice(p * P2, (p + 1) * P2)
        cum_c = cum[sl, pp]
        cend = cum_c[0:1] if rev else cum_c[C - 1:C]
        e_neg = jnp.exp(-cum_c)
        e_end = jnp.exp(cend - cum_c)
        a_t = kk_ref[sl, pp].astype(F32) * jnp.exp(cum_prev[sl, pp])
        r_t = r_ref[sl, pp].astype(F32) * jnp.exp(cum_c)
        akk = akk_ref[sl, pp].astype(F32)
        kd = kd_ref[sl, pp].astype(F32)
        v = v_ref[sl, pp]
        pre[c, p] = dict(a_t=a_t, r_t=r_t, v=v, kh=kd * e_end, bh=akk * e_end, dec=jnp.exp(cend),
                         lhs=jnp.concatenate([split_heads(a_t), split_heads(r_t)], axis=0),
                         rhs=jnp.concatenate([stack(akk * e_neg), stack(kd * e_neg)], axis=0))
    for w in pre.values():
        g = _mm(w["lhs"], w["rhs"], NT)
        w["m"] = jnp.where(strict, g[:P2, :P2], 0.0)
        w["a_ak"] = jnp.where(strict, g[:P2, P2:], 0.0)
        w["a_rb"] = jnp.where(incl, g[P2:, :P2], 0.0)
        w["a_rk"] = jnp.where(incl, g[P2:, P2:], 0.0)
        w["p"] = eye - w["m"]
    for w in pre.values():
        av = _mm(jnp.concatenate([w["a_ak"], w["a_rk"]], axis=0), stack(w["v"]))
        w["av"], w["rkv"] = av[:P2], av[P2:]
    for w in pre.values():
        w["m"] = _mm(w["m"], w["m"])
    for _ in range(4):
        for w in pre.values():
            pm = _mm(jnp.concatenate([w["p"], w["m"]], axis=0), w["m"])
            w["p"] = w["p"] + pm[:P2]
            w["m"] = pm[P2:]
    for w in pre.values():
        w["p"] = w["p"] + _mm(w["p"], w["m"])
    for w in pre.values():
        w["tt"] = _mm(w["p"], jnp.concatenate([stack(w["a_t"]), w["av"]], axis=1))
        w["z"] = _mm(w["p"], stack(w["bh"]), TN)
    for w in pre.values():
        ar = _mm(w["a_rb"], w["tt"])
        w["oa"] = stack(w["r_t"]) - ar[:, :P2]
        w["oc"] = w["rkv"] - ar[:, P2:]
        w["pz"] = bdiag * _mm(split_heads(w["a_t"]), w["z"], TN)
        w["kz"] = stack(w["kh"]) - _mm(w["a_ak"], w["z"], TN)
    for w in pre.values():
        w["q"] = bdiag * _mm(split_heads(w["v"]), w["kz"], TN)

    s = [s_ref[p] for p in range(NP)]
    for c, p in order:
        w = pre[c, p]
        w["s0"] = s[p]
        s[p] = s[p] * w["dec"] - _mm(s[p], w["pz"]) + w["q"]
    for c, p in order:
        w = pre[c, p]
        o_st = _mm(w["oa"], w["s0"], NT) + w["oc"]
        y_ref[c * C:(c + 1) * C, p * P2:(p + 1) * P2] = unstack(o_st)
    for p in range(NP):
        s_ref[p] = s[p]

    if rev:
        o = of_ref[...] + y_ref[...]
        bd = bd_ref[...]
        mu = _mm2(o, bd) * (1.0 / HD)
        oc = o - mu
        var = _mm2(oc * oc, bd) * (1.0 / HD)
        o = oc * lax.rsqrt(var + RW_GN_EPS) * lnw_ref[...] + lnb_ref[...]
        y_ref[...] = (o + bonus_ref[...]) * gs_ref[...]


def _rw_parts(rev, r, kk, v, lw, akk, kd, extra=()):
    rows = r.shape[0]
    nb = rows // MIX_ROWS
    order = _blk_order(rev, nb)
    P2 = 2 * RW_HD
    W = RW_HEADS * RW_HD
    blk = pl.BlockSpec((MIX_ROWS, W), lambda b: (order(b), 0))
    in_specs = [blk] * 6 + [pl.BlockSpec((MIX_ROWS, MIX_ROWS), lambda b: (0, 0))]
    args = [r, kk, v, lw, akk, kd, _seg_tri(MIX_ROWS, RW_CHUNK, rev)]
    if rev:
        o_fwd, bonus, gs, lnw, lnb, bd512 = extra
        vec = pl.BlockSpec((1, W), lambda b: (0, 0))
        in_specs += [blk, blk, blk, vec, vec, pl.BlockSpec((W, W), lambda b: (0, 0))]
        args += [o_fwd, bonus, gs, lnw, lnb, bd512]
    return dict(body=functools.partial(_rw_scan_kernel, rev), in_specs=in_specs, args=args, out_spec=blk,
                out_shape=jax.ShapeDtypeStruct((rows, W), F32),
                scratch=pltpu.VMEM((RW_HEADS // 2, P2, P2), F32))


def _qk_prep_kernel(nq_ref, nk_ref, nv_ref, wq_ref, wkv_ref, nqn_ref, nkn_ref, wqn_ref, wkn_ref,
                    cos_ref, sin_ref, bd512_ref, bd128_ref,
                    naq_o, nak_o, nav_o, waq_o, wak_o, wav_o):
    bd512 = bd512_ref[...]
    bd128 = bd128_ref[...]

    def hnorm(x, g, bd, hd):
        return x * lax.rsqrt(_mm2(x * x, bd) * (1.0 / hd) + EPS) * g

    def rope(x, cos, sin):
        lane = lax.broadcasted_iota(jnp.int32, x.shape, 1)
        w = x.shape[1]
        partner = jnp.where((lane & 31) < 16, pltpu.roll(x, w - 16, axis=1), pltpu.roll(x, 16, axis=1))
        return x * cos + partner * sin

    def dup_heads(o_ref, x):
        lower = lax.broadcasted_iota(jnp.int32, x.shape, 1) < WA_HD
        swapped = pltpu.roll(x, WA_HD, axis=1)
        o_ref[0] = jnp.where(lower, x, swapped).astype(o_ref.dtype)
        o_ref[1] = jnp.where(lower, swapped, x).astype(o_ref.dtype)

    f32 = lambda ref: ref[...].astype(F32)
    naq_o[...] = (hnorm(f32(nq_ref), nqn_ref[...], bd512, NA_HD) * NA_HD ** -0.5).astype(BF16)
    nak_o[...] = hnorm(f32(nk_ref), nkn_ref[...], bd512, NA_HD).astype(BF16)
    nav_o[...] = nv_ref[...]
    cos = cos_ref[...]
    sin = sin_ref[...]
    wq = hnorm(f32(wq_ref), wqn_ref[...], bd512, WA_HD) * WA_HD ** -0.5
    wq = rope(wq, jnp.concatenate([cos] * 4, axis=1), jnp.concatenate([sin] * 4, axis=1))
    waq_o[...] = wq.astype(BF16)
    kv = f32(wkv_ref)
    dup_heads(wak_o, rope(hnorm(kv[:, :128], wkn_ref[...], bd128, WA_HD), cos, sin))
    dup_heads(wav_o, kv[:, 128:])


def _qk_prep(p, nqn, nkn, wqn, wkn, cos, sin, bd512, bd128):
    rows = p.shape[0]
    nb = rows // MIX_ROWS
    nab = OFF_NA // 512
    full = lambda shape: pl.BlockSpec(shape, lambda i: (0,) * len(shape))
    in_specs = [pl.BlockSpec((MIX_ROWS, 512), lambda i: (i, nab)),
                pl.BlockSpec((MIX_ROWS, 512), lambda i: (i, nab + 1)),
                pl.BlockSpec((MIX_ROWS, 512), lambda i: (i, nab + 2)),
                pl.BlockSpec((MIX_ROWS, 512), lambda i: (i, OFF_WA // 512)),
                pl.BlockSpec((MIX_ROWS, 256), lambda i: (i, (OFF_WA + 512) // 256)),
                full((1, 512)), full((1, 512)), full((1, 512)), full((1, 128)),
                pl.BlockSpec((MIX_ROWS, 128), lambda i: (i, 0)),
                pl.BlockSpec((MIX_ROWS, 128), lambda i: (i, 0)),
                full((512, 512)), full((128, 128))]
    flat = pl.BlockSpec((MIX_ROWS, 512), lambda i: (i, 0))
    flat_shape = jax.ShapeDtypeStruct((rows, 512), BF16)
    dup = pl.BlockSpec((WA_KV_HEADS, MIX_ROWS, 128), lambda i: (0, i, 0))
    dup_shape = jax.ShapeDtypeStruct((WA_KV_HEADS, rows, 128), BF16)
    return pl.pallas_call(
        _qk_prep_kernel,
        grid=(nb,),
        in_specs=in_specs,
        out_specs=[flat, flat, flat, flat, dup, dup],
        out_shape=[flat_shape, flat_shape, flat_shape, flat_shape, dup_shape, dup_shape],
        compiler_params=_cparams(("arbitrary",)),
        name="qk_prep",
    )(p, p, p, p, p, nqn, nkn, wqn, wkn, cos, sin, bd512, bd128)


def _na_bias_layout():
    wr, G, KR = NA_WIN_R, NA_GROUP, NA_KEY_ROWS
    off = (0, -(wr // 2), -(KR - G))
    out = {}
    for kind in range(3):
        for r in range(G):
            wstart = (0, r, KR - wr)[kind]
            for a in range(KR):
                inside = wstart <= a < wstart + wr
                out[kind, r, a] = (off[kind] + a - r + wr - 1) if inside else None
    return out


def _na_kernel(lc, nrows, q_ref, k_ref, v_ref, t_ref, o_ref, bias_s):
    W = GRID_W
    HD = NA_HD
    nq = NA_GROUP * W
    nk = NA_KEY_ROWS * W
    ngroups = nrows // NA_GROUP
    kc = k_ref[0:lc, :]
    vc = v_ref[0:lc, :]
    half = [lax.broadcasted_iota(jnp.int32, (1, 2 * HD), 1) // HD == hh for hh in range(2)]

    bias_s[:, :, :, 0:lc] = jnp.zeros((2, 3, nq, lc), F32)
    for (kind, r, a), ro in _na_bias_layout().items():
        c0 = lc + a * W
        for hh in range(2):
            tile = jnp.full((W, W), NEG, F32) if ro is None else t_ref[hh, ro, :, (a % 2) * W:(a % 2 + 1) * W]
            bias_s[hh, kind, r * W:(r + 1) * W, c0:c0 + W] = tile

    def attend(q, keys, vals, bias):
        out = None
        for hh in range(2):
            s = _mm(jnp.where(half[hh], q, 0), keys, NT)
            if bias is not None:
                s = s + bias(hh)
            p = jnp.exp(s - jnp.max(s, axis=-1, keepdims=True))
            o = _mm(p, jnp.where(half[hh], vals, 0)) / jnp.sum(p, axis=-1, keepdims=True)
            out = o if out is None else out + o
        return out

    o_ref[0:lc, :] = attend(q_ref[0:lc, :], kc, vc, None)

    def group(gi, carry):
        base = jnp.clip(gi * NA_GROUP - NA_WIN_R // 2, 0, nrows - NA_KEY_ROWS)
        kind = jnp.where(gi == 0, 0, jnp.where(gi == ngroups - 1, 2, 1))
        q0 = pl.multiple_of(lc + gi * nq, nq)
        k0 = pl.multiple_of(lc + base * W, W)
        keys = jnp.concatenate([kc, k_ref[pl.ds(k0, nk), :]], axis=0)
        vals = jnp.concatenate([vc, v_ref[pl.ds(k0, nk), :]], axis=0)
        o_ref[pl.ds(q0, nq), :] = attend(q_ref[pl.ds(q0, nq), :], keys, vals, lambda hh: bias_s[hh, kind])
        return carry

    lax.fori_loop(0, ngroups, group, 0, unroll=2)


def _na(q, k, v, t2, lc):
    rows = q.shape[0]
    nrows = (rows - lc) // GRID_W
    pblk = pl.BlockSpec((rows, 2 * NA_HD), lambda p: (0, p))
    return pl.pallas_call(
        functools.partial(_na_kernel, lc, nrows),
        grid=(NA_HEADS // 2,),
        in_specs=[pblk, pblk, pblk, pl.BlockSpec((2,) + t2.shape[1:], lambda p: (p, 0, 0, 0))],
        out_specs=pblk,
        out_shape=jax.ShapeDtypeStruct((rows, NA_HEADS * NA_HD), F32),
        scratch_shapes=[pltpu.VMEM((2, 3, NA_GROUP * GRID_W, lc + NA_KEY_ROWS * GRID_W), F32)],
        compiler_params=_cparams(("arbitrary",)),
        name="na_attn",
    )(q, k, v, t2)


def _wa_kernel(lc, t_len, q_ref, k_ref, v_ref, sink_ref, o_ref):
    G = WA_HEADS // WA_KV_HEADS
    B = WA_WINDOW
    nband = 3 * B
    HD = WA_HD
    b = pl.program_id(1)
    kc = k_ref[0:lc, :]
    vc = v_ref[0:lc, :]
    sink = sink_ref[...]
    lower = lax.broadcasted_iota(jnp.int32, (1, 2 * HD), 1) < HD

    def stacked_queries(r0, n):
        parts = []
        for g in range(G):
            qp = q_ref[r0:r0 + n, (g // 2) * 2 * HD:(g // 2 + 1) * 2 * HD]
            parts.append(jnp.where(lower if g % 2 == 0 else jnp.logical_not(lower), qp, 0))
        return jnp.concatenate(parts, axis=0)

    def store(r0, n, o):
        for gp in range(G // 2):
            o_ref[r0:r0 + n, gp * 2 * HD:(gp + 1) * 2 * HD] = jnp.where(
                lower, o[2 * gp * n:(2 * gp + 1) * n], o[(2 * gp + 1) * n:(2 * gp + 2) * n])

    @pl.when(b == 0)
    def _():
        sk = jnp.concatenate([jnp.broadcast_to(sink[g * B:g * B + 1, :], (lc, 1)) for g in range(G)], axis=0)
        s = _mm(stacked_queries(0, lc), kc, NT)
        m = jnp.maximum(jnp.max(s, axis=-1, keepdims=True), sk)
        e = jnp.exp(s - m)
        store(0, lc, _mm(e, vc) / (jnp.sum(e, axis=-1, keepdims=True) + jnp.exp(sk - m)))

    @pl.when(b > 0)
    def _():
        qoff = lax.broadcasted_iota(jnp.int32, (G * B, lc + nband), 0) & (B - 1)
        koff = lax.broadcasted_iota(jnp.int32, (G * B, lc + nband), 1) - lc
        for j in range(MIX_ROWS // B):
            n = (b - 1) * (MIX_ROWS // B) + j
            start = jnp.clip((n - 1) * B, 0, t_len - nband)
            k0 = pl.multiple_of(lc + start, B)
            keys = jnp.concatenate([kc, k_ref[pl.ds(k0, nband), :]], axis=0)
            vals = jnp.concatenate([vc, v_ref[pl.ds(k0, nband), :]], axis=0)
            valid = jnp.logical_or(koff < 0, jnp.abs((n * B + qoff) - (start + koff)) <= WA_WINDOW)
            s = jnp.where(valid, _mm(stacked_queries(j * B, B), keys, NT), NEG)
            m = jnp.maximum(jnp.max(s, axis=-1, keepdims=True), sink)
            p = jnp.exp(s - m)
            store(j * B, B, _mm(p, vals) / (jnp.sum(p, axis=-1, keepdims=True) + jnp.exp(sink - m)))


def _wa(q, k, v, sink_col, lc):
    rows = q.shape[0]
    G = WA_HEADS // WA_KV_HEADS
    kvblk = pl.BlockSpec((None, rows, 2 * WA_HD), lambda h, b: (h, 0, 0))
    qblk = pl.BlockSpec((MIX_ROWS, G * WA_HD), lambda h, b: (b, h))
    return pl.pallas_call(
        functools.partial(_wa_kernel, lc, rows - lc),
        grid=(WA_KV_HEADS, rows // MIX_ROWS),
        in_specs=[qblk, kvblk, kvblk,
                  pl.BlockSpec((None, G * WA_WINDOW, 1), lambda h, b: (h, 0, 0))],
        out_specs=qblk,
        out_shape=jax.ShapeDtypeStruct((rows, WA_HEADS * WA_HD), F32),
        compiler_params=_cparams(("arbitrary", "arbitrary")),
        name="wa_attn",
    )(q, k, v, sink_col)


def _block_diag(n, blk):
    idx = np.arange(n) // blk
    return jnp.asarray(idx[:, None] == idx[None, :], dtype=BF16)


def _na_bias_table(rpb):
    W, wc = GRID_W, NA_WIN_C
    j = np.arange(W)
    cstart = np.clip(j - wc // 2, 0, W - wc)
    cabs = np.arange(W)
    inwin = (cabs[None, :] >= cstart[:, None]) & (cabs[None, :] < cstart[:, None] + wc)
    cb = cabs[None, :] - j[:, None] + wc - 1
    nh, nro, nco = rpb.shape
    onehot = (cb[None] == np.arange(nco)[:, None, None]) & inwin[None]
    t = jnp.dot(rpb.reshape(nh * nro, nco), jnp.asarray(onehot.reshape(nco, W * W), F32),
                precision=lax.Precision.HIGHEST).reshape(nh, nro, W, W)
    t = t + jnp.asarray(np.where(inwin, 0.0, NEG), F32)
    return jnp.concatenate([t, t], axis=-1)


def _rope_tables(lc, t_len):
    pos = np.arange(t_len)
    n = WA_HD // 4
    inv = ROPE_BASE ** (-np.arange(n, dtype=np.float64) / n)
    ang_r = (pos // GRID_W)[:, None] * inv[None, :]
    ang_c = (pos % GRID_W)[:, None] * inv[None, :]
    cos = np.concatenate([np.cos(ang_r)] * 2 + [np.cos(ang_c)] * 2, axis=1)
    sin = np.concatenate([-np.sin(ang_r), np.sin(ang_r), -np.sin(ang_c), np.sin(ang_c)], axis=1)
    cos = np.concatenate([np.ones((lc, WA_HD)), cos], axis=0)
    sin = np.concatenate([np.zeros((lc, WA_HD)), sin], axis=0)
    return (jnp.asarray(np.tile(cos, (1, 2)), dtype=F32), jnp.asarray(np.tile(sin, (1, 2)), dtype=F32))


def _cast_kernel(x_ref, o_ref):
    o_ref[...] = x_ref[...].astype(o_ref.dtype)


def _cast_bf16(w, layer, bk, bn, src_col_block=None):
    _, k, n = w.shape
    col = src_col_block or (lambda j: j)
    return pl.pallas_call(
        _cast_kernel,
        grid=(k // bk, n // bn),
        in_specs=[pl.BlockSpec((None, bk, bn), lambda i, j: (layer, i, col(j)))],
        out_specs=pl.BlockSpec((bk, bn), lambda i, j: (i, j)),
        out_shape=jax.ShapeDtypeStruct((k, n), BF16),
        compiler_params=_cparams(("arbitrary", "arbitrary")),
        name="cast_bf16",
    )(w)


W_IN_BLK = 256
_W_IN_SRC = np.concatenate([np.arange(10, 18), np.arange(0, 10), np.arange(19, 25), np.arange(28, 60),
                            np.arange(25, 28), np.arange(18, 19)])


def _w_in_src_block(j):
    starts = (0, 8, 18, 24, 56, 59)
    src = j + int(_W_IN_SRC[0])
    for s in starts[1:]:
        src = jnp.where(j >= s, j + int(_W_IN_SRC[s]) - s, src)
    return src


def _lora_pad(w, row0):
    out = jnp.zeros((2, 256, w.shape[-1]), F32)
    for d in range(2):
        out = out.at[d, row0 + 64 * d:row0 + 64 * (d + 1)].set(w[d])
    return out


def _token_mixing(p, lc, layer, hg_lb, hg_norm, rw_shift, rw_w0, rw_w2, rw_a0, rw_a2, rw_kk, rw_ka, rw_rk,
                  rw_ln_w, rw_ln_b, na_qn, na_kn, na_rpb, wa_qn, wa_kn, wa_sink, tables):
    rows = p.shape[0]
    bd512, bd128, cos, sin = tables
    cum = jnp.cumsum(jax.nn.softmax(hg_lb.astype(F32), axis=1), axis=1)
    lbs = cum[:, layer] - cum[:, 0]
    prep = _rw_prep(p, rw_shift[:, :2048], rw_shift[:, 2048:], rw_w0, _lora_pad(rw_w2, 0), rw_a0,
                    _lora_pad(rw_a2, 128), rw_kk[None], rw_ka[None], rw_rk.reshape(1, -1), bd512)
    r, kk, v, gs, bonus, lw0, lw1, akk0, akk1, kd0, kd1 = prep
    oa_f, ob_f = _run_scans([_gla_parts(p, lbs[0:1], False), _rw_parts(False, r, kk, v, lw0, akk0, kd0)],
                            rows, "scan_fwd")
    y_a, y_b = _run_scans([_gla_parts(p, lbs[1:2], True, oa_f, hg_norm[None]),
                           _rw_parts(True, r, kk, v, lw1, akk1, kd1,
                                     (ob_f, bonus, gs, rw_ln_w[None], rw_ln_b[None], bd512))],
                          rows, "scan_rev")
    tile = lambda g, n: jnp.tile(g, n)[None]
    naq, nak, nav, waq, wak, wav = _qk_prep(p, tile(na_qn, 8), tile(na_kn, 8), tile(wa_qn, 8),
                                            tile(wa_kn, 2), cos, sin, bd512, bd128)
    y_c = _na(naq, nak, nav, _na_bias_table(na_rpb), lc)
    G = WA_HEADS // WA_KV_HEADS
    sink_col = jnp.repeat(wa_sink.reshape(WA_KV_HEADS, G), WA_WINDOW, axis=1)[..., None]
    y_d = _wa(waq, wak, wav, sink_col, lc)
    return y_a, y_b, y_c, y_d


def kernel(x, c, ctx, c_ctx, ada_w, ada_b, norm_ffn1, norm_mix, norm_ffn2, ffn1_wi, ffn1_wo, ffn2_wi, ffn2_wo, w_in, hg_lb, hg_norm, rw_shift, rw_w0, rw_w2, rw_a0, rw_a2, rw_kk, rw_ka, rw_rk, rw_ln_w, rw_ln_b, na_qn, na_kn, na_rpb, wa_qn, wa_kn, wa_sink, w_branch, w_out):
    assert x.shape[0] == 1 and ctx.shape[1] == MIX_ROWS
    lc = ctx.shape[1]
    t_len = x.shape[1]
    xa = jnp.concatenate([ctx[0], x[0]], axis=0)
    cc_t = jnp.stack([c[0], c_ctx], axis=1)
    mods = _ada_mods(cc_t, ada_w, ada_b).reshape(DEPTH, 2, N_MOD, D)
    tables = (_block_diag(512, 64), _block_diag(128, 64)) + _rope_tables(lc, t_len)
    half = W_IN_BLK // 2
    w_in_src = lambda c: 2 * _w_in_src_block(c // 2) + c % 2
    wi1, wo1 = _cast_bf16(ffn1_wi, 0, D, 1024), _cast_bf16(ffn1_wo, 0, 512, D)
    for l in range(DEPTH):
        m = mods[l]
        xa, w_in_l = _ffn(xa, m[:, 0:3], norm_ffn1[l][None], wi1, wo1, lc,
                          jobs=[_cast_job(w_in, l, D, half, w_in_src)])
        jobs = [_cast_job(w_branch.reshape(DEPTH, N_BRANCH * BR_W, D), l, 256, 256), _cast_job(w_out, l, 256, 256),
                _cast_job(ffn2_wi, l, D, half), _cast_job(ffn2_wo, l, 64, D)]
        if l + 1 < DEPTH:
            jobs += [_cast_job(ffn1_wi, l + 1, D, half), _cast_job(ffn1_wo, l + 1, 64, D)]
        p, wb, w_out_l, wi2, wo2, *nxt = _win(xa, m[:, 3:6], norm_mix[l][None], w_in_l, lc, jobs=jobs)
        if nxt:
            wi1, wo1 = nxt
        yb = _token_mixing(p, lc, l, hg_lb, hg_norm[l], rw_shift[l], rw_w0[l], rw_w2[l], rw_a0[l], rw_a2[l],
                           rw_kk[l], rw_ka[l], rw_rk[l], rw_ln_w[l], rw_ln_b[l], na_qn[l], na_kn[l],
                           na_rpb[l], wa_qn[l], wa_kn[l], wa_sink[l], tables)
        xa = _merge(xa, m[:, 3:6], yb, p, wb.reshape(N_BRANCH, BR_W, D), w_out_l, lc)
        xa, = _ffn(xa, m[:, 6:9], norm_ffn2[l][None], wi2, wo2, lc)
    return xa[lc:][None]
```

```python
import functools

import numpy as np
import jax
import jax.numpy as jnp
from jax import lax
from jax.experimental import pallas as pl
from jax.experimental.pallas import tpu as pltpu

F32 = jnp.float32
BF16 = jnp.bfloat16

D = 2048
DEPTH = 2
GRID_W = 64
EPS = 1e-6
D_FF = 5632
N_MOD = 9
N_BRANCH = 4
BR_W = 512
HG_HEADS, HG_DK = 4, 128
RW_HEADS, RW_HD = 8, 64
RW_GN_EPS = 64e-5
NA_HEADS, NA_HD = 8, 64
NA_WIN_R, NA_WIN_C = 8, 16
NA_GROUP = 4
NA_KEY_ROWS = NA_WIN_R + NA_GROUP - 1
WA_HEADS, WA_KV_HEADS, WA_HD = 8, 2, 64
WA_WINDOW = 128
ROPE_BASE = 10000.0

OFF_RW, OFF_HG, OFF_NA, OFF_GL, OFF_WA, OFF_LORA = 0, 2048, 4608, 6144, 14336, 15104
P_TOTAL = 15360

MIX_ROWS = 256
HALO = 16
GLA_CHUNK = 16
RW_CHUNK = 64
NEG = -1e30
LOG2E = 1.4426950408889634
VMEM_LIMIT = 60 * 1024 * 1024

NT = (((1,), (1,)), ((), ()))
TN = (((0,), (0,)), ((), ()))


def _cparams(sem):
    return pltpu.CompilerParams(dimension_semantics=sem, vmem_limit_bytes=VMEM_LIMIT)


def _mm(a, b, dims=None):
    a = a.astype(BF16)
    b = b.astype(BF16)
    if dims is None:
        return jnp.dot(a, b, preferred_element_type=F32)
    return lax.dot_general(a, b, dims, preferred_element_type=F32)


def _mm2(x, w_bf16):
    hi = x.astype(BF16)
    lo = (x - hi.astype(F32)).astype(BF16)
    return (jnp.dot(hi, w_bf16, preferred_element_type=F32)
            + jnp.dot(lo, w_bf16, preferred_element_type=F32))


def _mm_f32(a, b):
    a_hi, b_hi = a.astype(BF16), b.astype(BF16)
    a_lo = (a - a_hi.astype(F32)).astype(BF16)
    b_lo = (b - b_hi.astype(F32)).astype(BF16)
    dot = functools.partial(jnp.dot, preferred_element_type=F32)
    return dot(a_hi, b_hi) + (dot(a_hi, b_lo) + dot(a_lo, b_hi))


def _sigmoid(x):
    return 1.0 / (1.0 + jnp.exp(-x))


def _silu(x):
    return x * _sigmoid(x)


def _seg_cumsum(x, tri):
    w = x.shape[1]
    hi = x.astype(BF16)
    r1 = x - hi.astype(F32)
    mid = r1.astype(BF16)
    lo = (r1 - mid.astype(F32)).astype(BF16)
    y = jnp.dot(tri, jnp.concatenate([hi, mid, lo], axis=1), preferred_element_type=F32)
    return y[:, :w] + y[:, w:2 * w] + y[:, 2 * w:]


def _seg_tri(rows, seg, rev):
    r = np.arange(rows)
    same = (r[:, None] // seg) == (r[None, :] // seg)
    tri = (r[None, :] >= r[:, None]) if rev else (r[None, :] <= r[:, None])
    return jnp.asarray(same & tri, dtype=BF16)


def _ada_kernel(cc_ref, w_ref, b_ref, o_ref):
    s = _silu(cc_ref[...])
    w = w_ref[...]
    r0 = jnp.sum(w * s[:, 0:1], axis=0, keepdims=True)
    r1 = jnp.sum(w * s[:, 1:2], axis=0, keepdims=True)
    o_ref[...] = jnp.concatenate([r0, r1], axis=0) + b_ref[...]


def _ada_mods(cc_t, ada_w, ada_b):
    tn = 1024
    nmod = ada_w.shape[-1]
    return pl.pallas_call(
        _ada_kernel,
        grid=(DEPTH, nmod // tn),
        in_specs=[pl.BlockSpec((D, 2), lambda l, j: (0, 0)),
                  pl.BlockSpec((None, D, tn), lambda l, j: (l, 0, j)),
                  pl.BlockSpec((None, 1, tn), lambda l, j: (l, 0, j))],
        out_specs=pl.BlockSpec((None, 2, tn), lambda l, j: (l, 0, j)),
        out_shape=jax.ShapeDtypeStruct((DEPTH, 2, nmod), F32),
        compiler_params=_cparams(("arbitrary", "arbitrary")),
        name="ada_mod",
    )(cc_t, ada_w, ada_b.reshape(DEPTH, 1, nmod))


def _cast_job(w, layer, bk, bn, src_col_block=None):
    return (w, layer, bk, bn, src_col_block)


def _cast_job_specs(jobs, grid):
    in_specs, out_specs, out_shapes, args = [], [], [], []
    for w, layer, bk, bn, src_col in jobs:
        _, k, n = w.shape
        ncol = n // bn
        nblk = (k // bk) * ncol
        assert nblk <= grid[0] * grid[1], (nblk, grid)
        col = src_col or (lambda c: c)

        def blk(i, j, nblk=nblk):
            return jnp.minimum(i * grid[1] + j, nblk - 1)

        in_specs.append(pl.BlockSpec(
            (None, bk, bn), lambda i, j, blk=blk, ncol=ncol, col=col, layer=layer:
            (layer, blk(i, j) // ncol, col(blk(i, j) % ncol))))
        out_specs.append(pl.BlockSpec((bk, bn), lambda i, j, blk=blk, ncol=ncol: (blk(i, j) // ncol, blk(i, j) % ncol)))
        out_shapes.append(jax.ShapeDtypeStruct((k, n), BF16))
        args.append(w)
    return in_specs, out_specs, out_shapes, args


def _run_cast_jobs(src_refs, dst_refs):
    for src, dst in zip(src_refs, dst_refs):
        dst[...] = src[...].astype(dst.dtype)


def _row_is_ctx(i, tm, lc):
    return (i * tm + lax.broadcasted_iota(jnp.int32, (tm, 1), 0)) < lc


def _mod_pick(mod_ref, j, is_ctx):
    return jnp.where(is_ctx, mod_ref[1, j:j + 1, :], mod_ref[0, j:j + 1, :])


def _store_modulated(xn_ref, x_ref, g_ref, mod_ref, i, tm, lc):
    def normed():
        x = x_ref[...]
        return x * lax.rsqrt(jnp.mean(x * x, axis=-1, keepdims=True) + EPS)

    @pl.when(i * tm < lc)
    def _():
        is_ctx = _row_is_ctx(i, tm, lc)
        y = normed() * g_ref[...]
        xn_ref[...] = (y * (1.0 + _mod_pick(mod_ref, 1, is_ctx)) + _mod_pick(mod_ref, 0, is_ctx)).astype(BF16)

    @pl.when(i * tm >= lc)
    def _():
        gain = g_ref[...] * (1.0 + mod_ref[0, 1:2, :])
        xn_ref[...] = (normed() * gain + mod_ref[0, 0:1, :]).astype(BF16)


def _ffn_kernel(lc, tm, nf, njobs, x_ref, mod_ref, g_ref, wa_ref, wb_ref, wo_ref, *rest):
    src_refs, (o_ref, *dst_refs, xn_ref) = rest[:njobs], rest[njobs:]
    i = pl.program_id(0)
    j = pl.program_id(1)
    _run_cast_jobs(src_refs, dst_refs)

    @pl.when(j == 0)
    def _():
        _store_modulated(xn_ref, x_ref, g_ref, mod_ref, i, tm, lc)
        o_ref[...] = jnp.zeros_like(o_ref)

    xn = xn_ref[...]
    a = jnp.dot(xn, wa_ref[...], preferred_element_type=F32)
    b = jnp.dot(xn, wb_ref[...], preferred_element_type=F32)
    h = (_silu(a) * b).astype(BF16)
    o_ref[...] += jnp.dot(h, wo_ref[...], preferred_element_type=F32)

    @pl.when(j == nf - 1)
    def _():
        o_ref[...] = x_ref[...] + 0.5 * _mod_pick(mod_ref, 2, _row_is_ctx(i, tm, lc)) * o_ref[...]


def _dense_tm(rows):
    for tm in (768, 512, 256):
        if rows % tm == 0:
            return tm
    raise ValueError(rows)


def _ffn(x, mod3, g, wi, wo, lc, jobs=()):
    rows = x.shape[0]
    tm = _dense_tm(rows)
    tf = 512
    nf = D_FF // tf
    grid = (rows // tm, nf)
    job_in, job_out, job_shape, job_args = _cast_job_specs(jobs, grid)
    return pl.pallas_call(
        functools.partial(_ffn_kernel, lc, tm, nf, len(jobs)),
        grid=grid,
        in_specs=[pl.BlockSpec((tm, D), lambda i, j: (i, 0)),
                  pl.BlockSpec((2, 3, D), lambda i, j: (0, 0, 0)),
                  pl.BlockSpec((1, D), lambda i, j: (0, 0)),
                  pl.BlockSpec((D, tf), lambda i, j: (0, j)),
                  pl.BlockSpec((D, tf), lambda i, j: (0, j + nf)),
                  pl.BlockSpec((tf, D), lambda i, j: (j, 0))] + job_in,
        out_specs=[pl.BlockSpec((tm, D), lambda i, j: (i, 0))] + job_out,
        out_shape=[jax.ShapeDtypeStruct((rows, D), F32)] + job_shape,
        scratch_shapes=[pltpu.VMEM((tm, D), BF16)],
        compiler_params=_cparams(("arbitrary", "arbitrary")),
        name="ffn",
    )(x, mod3, g, wi, wi, wo, *job_args)


def _win_kernel(lc, tm, njobs, x_ref, mod_ref, g_ref, w_ref, *rest):
    src_refs, (o_ref, *dst_refs, xn_ref) = rest[:njobs], rest[njobs:]
    i = pl.program_id(0)
    _run_cast_jobs(src_refs, dst_refs)

    @pl.when(pl.program_id(1) == 0)
    def _():
        _store_modulated(xn_ref, x_ref, g_ref, mod_ref, i, tm, lc)

    o_ref[...] = jnp.dot(xn_ref[...], w_ref[...], preferred_element_type=F32).astype(o_ref.dtype)


def _win(x, mod3, g, w, lc, jobs=()):
    rows = x.shape[0]
    tm = _dense_tm(rows)
    tn = 1536
    grid = (rows // tm, P_TOTAL // tn)
    job_in, job_out, job_shape, job_args = _cast_job_specs(jobs, grid)
    return pl.pallas_call(
        functools.partial(_win_kernel, lc, tm, len(jobs)),
        grid=grid,
        in_specs=[pl.BlockSpec((tm, D), lambda i, j: (i, 0)),
                  pl.BlockSpec((2, 3, D), lambda i, j: (0, 0, 0)),
                  pl.BlockSpec((1, D), lambda i, j: (0, 0)),
                  pl.BlockSpec((D, tn), lambda i, j: (0, j))] + job_in,
        out_specs=[pl.BlockSpec((tm, tn), lambda i, j: (i, j))] + job_out,
        out_shape=[jax.ShapeDtypeStruct((rows, P_TOTAL), BF16)] + job_shape,
        scratch_shapes=[pltpu.VMEM((tm, D), BF16)],
        compiler_params=_cparams(("arbitrary", "arbitrary")),
        name="w_in",
    )(x, mod3, g, w, *job_args)


def _merge_kernel(lc, tm, x_ref, mod_ref, ya_ref, yb_ref, yc_ref, yd_ref, gl0_ref, gl1_ref, wb_ref, wo_ref,
                  o_ref, acc_ref):
    i = pl.program_id(0)
    n = pl.program_id(1)
    y_refs = (ya_ref, yb_ref, yc_ref, yd_ref)
    gl_refs = (gl0_ref, gl1_ref)

    for nn in range(N_BRANCH // 2):
        @pl.when(n == nn)
        def _():
            gated = None
            for k in range(2):
                br = 2 * nn + k
                proj = jnp.dot(y_refs[br][...].astype(BF16), wb_ref[br], preferred_element_type=F32)
                term = _sigmoid(gl_refs[k][...].astype(F32)) * proj
                gated = term if gated is None else gated + term
            if nn == 0:
                acc_ref[...] = gated
            else:
                acc_ref[...] += gated

    @pl.when(n == N_BRANCH // 2 - 1)
    def _():
        is_ctx = _row_is_ctx(i, tm, lc)
        y = jnp.dot(acc_ref[...].astype(BF16), wo_ref[...], preferred_element_type=F32)
        o_ref[...] = x_ref[...] + _mod_pick(mod_ref, 2, is_ctx) * y


def _merge(x, mod3, ys, p, wb, wo, lc):
    rows = x.shape[0]
    tm = 384 if rows % 384 == 0 else MIX_ROWS
    glb = OFF_GL // D
    yspec = pl.BlockSpec((tm, BR_W), lambda i, n: (i, 0))
    return pl.pallas_call(
        functools.partial(_merge_kernel, lc, tm),
        grid=(rows // tm, N_BRANCH // 2),
        in_specs=[pl.BlockSpec((tm, D), lambda i, n: (i, 0)),
                  pl.BlockSpec((2, 3, D), lambda i, n: (0, 0, 0)),
                  yspec, yspec, yspec, yspec,
                  pl.BlockSpec((tm, D), lambda i, n: (i, glb + 2 * n)),
                  pl.BlockSpec((tm, D), lambda i, n: (i, glb + 2 * n + 1)),
                  pl.BlockSpec((N_BRANCH, BR_W, D), lambda i, n: (0, 0, 0)),
                  pl.BlockSpec((D, D), lambda i, n: (0, 0))],
        out_specs=pl.BlockSpec((tm, D), lambda i, n: (i, 0)),
        out_shape=jax.ShapeDtypeStruct((rows, D), F32),
        scratch_shapes=[pltpu.VMEM((tm, D), F32)],
        compiler_params=_cparams(("arbitrary", "arbitrary")),
        name="merge",
    )(x, mod3, *ys, p, p, wb, wo)


def _gla_kernel(rev, *refs):
    if rev:
        q_ref, f_ref, i_ref, lb_ref, tri_ref, of_ref, g_ref, nw_ref, y_ref, st_ref = refs
    else:
        q_ref, f_ref, i_ref, lb_ref, tri_ref, y_ref, st_ref = refs
    C = GLA_CHUNK
    R = MIX_ROWS
    NH, DK = HG_HEADS, HG_DK

    @pl.when(pl.program_id(0) == 0)
    def _():
        st_ref[...] = jnp.zeros_like(st_ref)

    yield
    lb = lb_ref[...]
    f_all = lb + (1.0 - lb) * _sigmoid(f_ref[...].astype(F32))
    b_all = _seg_cumsum(jnp.log(f_all), tri_ref[...])
    SUB = C // 2
    sub_pos = lax.broadcasted_iota(jnp.int32, (1, SUB, 1), 1)
    chunks = list(range(R // C - 1, -1, -1) if rev else range(R // C))

    heads = []
    for h in range(NH):
        hs = slice(h * DK, (h + 1) * DK)
        b = b_all[:, hs]
        k = 1.0 - f_all[:, hs]
        q = q_ref[:, hs].astype(F32)
        v = i_ref[:, hs].astype(F32)
        def halves(t):
            t3 = t.reshape(R // C, C, DK)
            lo, hi = t3[:, :SUB], t3[:, SUB:]
            return (hi, lo) if rev else (lo, hi)

        def pair_term(qq, bq, keys, ok):
            bk_, kk_, vk_ = keys
            diff = bq - bk_
            e = jnp.exp2(diff if ok is None else jnp.where(ok, diff, NEG))
            return jnp.sum(qq * kk_ * e, axis=-1, keepdims=True) * vk_

        (qn, qf), (bn, bf), (kn, kf), (vn, vf) = halves(q), halves(b * LOG2E), halves(k), halves(v)
        near, far = (bn, kn, vn), (bf, kf, vf)
        o_n = jnp.sum(qn * kn, axis=-1, keepdims=True) * vn
        o_f = jnp.sum(qf * kf, axis=-1, keepdims=True) * vf + pair_term(qf, bf, near, None)
        for d in range(1, SUB):
            sh = (SUB - d) if rev else d
            ok = (sub_pos < SUB - d) if rev else (sub_pos >= d)
            near_r = tuple(pltpu.roll(t, sh, axis=1) for t in near)
            far_r = tuple(jnp.where(ok, pltpu.roll(t, sh, axis=1), tn) for t, tn in zip(far, near_r))
            o_n = o_n + pair_term(qn, bn, near_r, ok)
            o_f = o_f + pair_term(qf, bf, far_r, None) + pair_term(qf, bf, near_r, ok)
        o = jnp.concatenate([o_f, o_n] if rev else [o_n, o_f], axis=1).reshape(R, DK)
        upd, dec = {}, {}
        for c in chunks:
            sl = slice(c * C, (c + 1) * C)
            b_end = b[sl][0:1] if rev else b[sl][C - 1:C]
            dec[c] = jnp.exp(b_end)
            upd[c] = _mm(v[sl], k[sl] * jnp.exp(b_end - b[sl]), TN)
        heads.append(dict(o=o, qb=q * jnp.exp(b), upd=upd, dec=dec, st=st_ref[h]))

    for c in chunks:
        for w in heads:
            w[c] = w["st"]
            w["st"] = w["st"] * w["dec"][c] + w["upd"][c]
    outs = []
    for h, w in enumerate(heads):
        st_ref[h] = w["st"]
        inter = {c: _mm(w["qb"][c * C:(c + 1) * C], w[c], NT) for c in chunks}
        outs.append(w["o"] + jnp.concatenate([inter[c] for c in range(R // C)], axis=0))

    for h, o in enumerate(outs):
        hs = slice(h * DK, (h + 1) * DK)
        if rev:
            o = of_ref[:, hs] + o
            o = o * lax.rsqrt(jnp.mean(o * o, axis=-1, keepdims=True) + EPS) * nw_ref[:, hs]
            o = o * _silu(g_ref[:, hs].astype(F32))
        y_ref[:, hs] = o


def _blk_order(rev, nb):
    if rev:
        return lambda b: jnp.where(b == 0, 0, nb - b)
    return lambda b: b


def _gla_parts(p, lb, rev, o_fwd=None, norm_w=None):
    rows = p.shape[0]
    nb = rows // MIX_ROWS
    order = _blk_order(rev, nb)
    W = HG_HEADS * HG_DK
    cb = OFF_HG // W

    def col(n):
        return pl.BlockSpec((MIX_ROWS, W), lambda b: (order(b), cb + n))

    vec = pl.BlockSpec((1, W), lambda b: (0, 0))
    in_specs = [col(0), col(2 if rev else 1), col(3), vec,
                pl.BlockSpec((MIX_ROWS, MIX_ROWS), lambda b: (0, 0))]
    args = [p, p, p, lb, _seg_tri(MIX_ROWS, GLA_CHUNK, rev)]
    if rev:
        in_specs += [pl.BlockSpec((MIX_ROWS, W), lambda b: (order(b), 0)), col(4), vec]
        args += [o_fwd, p, norm_w]
    return dict(body=functools.partial(_gla_kernel, rev), in_specs=in_specs, args=args,
                out_spec=pl.BlockSpec((MIX_ROWS, W), lambda b: (order(b), 0)),
                out_shape=jax.ShapeDtypeStruct((rows, W), F32),
                scratch=pltpu.VMEM((HG_HEADS, HG_DK, HG_DK), F32))


def _run_scans(parts, rows, name):
    n_in = [len(p["in_specs"]) for p in parts]

    def kernel(*refs):
        pos = 0
        ins = []
        for n in n_in:
            ins.append(refs[pos:pos + n])
            pos += n
        outs = refs[pos:pos + len(parts)]
        scratch = refs[pos + len(parts):]
        bodies = [p["body"](*i, o, s) for p, i, o, s in zip(parts, ins, outs, scratch)]
        for body in bodies:
            next(body)
        for body in bodies:
            for _ in body:
                pass

    return pl.pallas_call(
        kernel,
        grid=(rows // MIX_ROWS,),
        in_specs=[s for p in parts for s in p["in_specs"]],
        out_specs=[p["out_spec"] for p in parts],
        out_shape=[p["out_shape"] for p in parts],
        scratch_shapes=[p["scratch"] for p in parts],
        compiler_params=_cparams(("arbitrary",)),
        name=name,
    )(*[a for p in parts for a in p["args"]])


def _rw_prep_kernel(nb, main_ref, lora_ref, pm_ref, nm_ref, pl_ref, nl_ref, tm_ref, tl_ref,
                    w0_ref, w2_ref, a0_ref, a2_ref, kkw_ref, ka_ref, rk_ref, bd_ref,
                    r_o, kk_o, v_o, gs_o, bonus_o, lw0_o, lw1_o, akk0_o, akk1_o, kd0_o, kd1_o):
    i = pl.program_id(0)
    has_prev = (i >= 2).astype(F32)
    has_next = jnp.logical_and(i != 0, i != nb - 1).astype(F32)
    rowi = lax.broadcasted_iota(jnp.int32, (MIX_ROWS, 1), 0)

    def shift(x, prev_blk, next_blk, taps):
        up = jnp.where(rowi == 0, prev_blk[HALO - 1:HALO, :] * has_prev, pltpu.roll(x, 1, axis=0))
        dn = jnp.where(rowi == MIX_ROWS - 1, next_blk[0:1, :] * has_next,
                       pltpu.roll(x, MIX_ROWS - 1, axis=0))
        return taps[0:1] * up + taps[1:2] * x + taps[2:3] * dn

    f32 = lambda ref: ref[...].astype(F32)
    main = shift(f32(main_ref), f32(pm_ref), f32(nm_ref), tm_ref[...])
    lora = shift(f32(lora_ref), f32(pl_ref), f32(nl_ref), tl_ref[...])
    W = RW_HEADS * RW_HD
    r, k, v, g = (main[:, n * W:(n + 1) * W] for n in range(4))
    bd = bd_ref[...]
    kk = k * kkw_ref[...]
    kk = kk * lax.rsqrt(_mm2(kk * kk, bd) + EPS)
    tl = jnp.tanh(lora)
    kds = []
    for d, (lw_o, akk_o, kd_o) in enumerate(((lw0_o, akk0_o, kd0_o), (lw1_o, akk1_o, kd1_o))):
        z = -(w0_ref[d:d + 1, :] + _mm_f32(tl, w2_ref[d]))
        softplus = jnp.maximum(z, 0.0) + jnp.log(1.0 + jnp.exp(-jnp.abs(z)))
        lw_o[...] = -jnp.exp(-softplus - 0.5)
        a = _sigmoid(a0_ref[d:d + 1, :] + _mm_f32(lora, a2_ref[d]))
        kd = k * (1.0 + (a - 1.0) * ka_ref[...])
        kds.append(kd)
        kd_o[...] = kd.astype(kd_o.dtype)
        akk_o[...] = (a * kk).astype(akk_o.dtype)
    r_o[...] = r.astype(r_o.dtype)
    kk_o[...] = kk.astype(kk_o.dtype)
    v_o[...] = v.astype(v_o.dtype)
    gs_o[...] = _sigmoid(g)
    bonus_o[...] = _mm2(r * (kds[0] + kds[1]) * rk_ref[...], bd) * v


def _rw_prep(p, taps_m, taps_l, w0, w2p, a0, a2p, kkw, ka, rk, bd512):
    rows = p.shape[0]
    nb = rows // MIX_ROWS
    W = RW_HEADS * RW_HD
    n_halo = rows // HALO
    lb = OFF_LORA // 256
    per = MIX_ROWS // HALO
    full = lambda shape: pl.BlockSpec(shape, lambda i: (0,) * len(shape))
    in_specs = [pl.BlockSpec((MIX_ROWS, 4 * W), lambda i: (i, 0)),
                pl.BlockSpec((MIX_ROWS, 256), lambda i: (i, lb)),
                pl.BlockSpec((HALO, 4 * W), lambda i: (jnp.maximum(i * per - 1, 0), 0)),
                pl.BlockSpec((HALO, 4 * W), lambda i: (jnp.minimum((i + 1) * per, n_halo - 1), 0)),
                pl.BlockSpec((HALO, 256), lambda i: (jnp.maximum(i * per - 1, 0), lb)),
                pl.BlockSpec((HALO, 256), lambda i: (jnp.minimum((i + 1) * per, n_halo - 1), lb)),
                full((3, 4 * W)), full((3, 256)),
                full((2, W)), full((2, 256, W)), full((2, W)), full((2, 256, W)),
                full((1, W)), full((1, W)), full((1, W)), full((W, W))]
    f32_out = jax.ShapeDtypeStruct((rows, W), F32)
    mxu_out = jax.ShapeDtypeStruct((rows, W), BF16)
    return pl.pallas_call(
        functools.partial(_rw_prep_kernel, nb),
        grid=(nb,),
        in_specs=in_specs,
        out_specs=[pl.BlockSpec((MIX_ROWS, W), lambda i: (i, 0))] * 11,
        out_shape=[mxu_out] * 3 + [f32_out] * 4 + [mxu_out] * 4,
        compiler_params=_cparams(("arbitrary",)),
        name="rw_prep",
    )(p, p, p, p, p, p, taps_m, taps_l, w0, w2p, a0, a2p, kkw, ka, rk, bd512)


def _rw_scan_kernel(rev, *refs):
    if rev:
        (r_ref, kk_ref, v_ref, lw_ref, akk_ref, kd_ref, tri_ref, of_ref, bonus_ref, gs_ref,
         lnw_ref, lnb_ref, bd_ref, y_ref, s_ref) = refs
    else:
        r_ref, kk_ref, v_ref, lw_ref, akk_ref, kd_ref, tri_ref, y_ref, s_ref = refs
    C = RW_CHUNK
    HD = RW_HD
    P2 = 2 * HD

    NP = RW_HEADS // 2

    @pl.when(pl.program_id(0) == 0)
    def _():
        s_ref[...] = jnp.zeros_like(s_ref)

    yield
    lw = lw_ref[...]
    cum = _seg_cumsum(lw, tri_ref[...])
    cum_prev = cum - lw

    lane = lax.broadcasted_iota(jnp.int32, (1, P2), 1)
    h0 = lane < HD
    ri = lax.broadcasted_iota(jnp.int32, (P2, P2), 0)
    ci = lax.broadcasted_iota(jnp.int32, (P2, P2), 1)
    same = (ri // C) == (ci // C)
    rt, cs = ri % C, ci % C
    strict = jnp.logical_and(same, (rt < cs) if rev else (rt > cs))
    incl = jnp.logical_and(same, (rt <= cs) if rev else (rt >= cs))
    eye = (ri == ci).astype(F32)

    def stack(x):
        return jnp.concatenate([x, x], axis=0)

    def split_heads(x):
        return jnp.concatenate([jnp.where(h0, x, 0.0), jnp.where(h0, 0.0, x)], axis=0)

    def unstack(x):
        return jnp.where(h0, x[:C], x[C:])

    chunks = list(range(MIX_ROWS // C - 1, -1, -1) if rev else range(MIX_ROWS // C))
    order = [(c, p) for c in chunks for p in range(NP)]

    pre = {}
    for c, p in order:
        sl = slice(c * C, (c + 1) * C)
        pp = slice(p * P2, (p + 1) * P2)
        cum_c = cum[sl, pp]
        cend = cum_c[0:1] if rev else cum_c[C - 1:C]
        e_neg = jnp.exp(-cum_c)
        e_end = jnp.exp(cend - cum_c)
        a_t = kk_ref[sl, pp].astype(F32) * jnp.exp(cum_prev[sl, pp])
        r_t = r_ref[sl, pp].astype(F32) * jnp.exp(cum_c)
        akk = akk_ref[sl, pp].astype(F32)
        kd = kd_ref[sl, pp].astype(F32)
        v = v_ref[sl, pp]
        pre[c, p] = dict(a_t=a_t, r_t=r_t, v=v, kh=kd * e_end, bh=akk * e_end, dec=jnp.exp(cend),
                         lhs=jnp.concatenate([split_heads(a_t), split_heads(r_t)], axis=0),
                         rhs=jnp.concatenate([stack(akk * e_neg), stack(kd * e_neg)], axis=0))
    for w in pre.values():
        g = _mm(w["lhs"], w["rhs"], NT)
        w["m"] = jnp.where(strict, g[:P2, :P2], 0.0)
        w["a_ak"] = jnp.where(strict, g[:P2, P2:], 0.0)
        w["a_rb"] = jnp.where(incl, g[P2:, :P2], 0.0)
        w["a_rk"] = jnp.where(incl, g[P2:, P2:], 0.0)
        w["p"] = eye - w["m"]
    for w in pre.values():
        av = _mm(jnp.concatenate([w["a_ak"], w["a_rk"]], axis=0), stack(w["v"]))
        w["av"], w["rkv"] = av[:P2], av[P2:]
    for w in pre.values():
        w["m"] = _mm(w["m"], w["m"])
    for _ in range(4):
        for w in pre.values():
            pm = _mm(jnp.concatenate([w["p"], w["m"]], axis=0), w["m"])
            w["p"] = w["p"] + pm[:P2]
            w["m"] = pm[P2:]
    for w in pre.values():
        w["p"] = w["p"] + _mm(w["p"], w["m"])
    for w in pre.values():
        w["tt"] = _mm(w["p"], jnp.concatenate([stack(w["a_t"]), w["av"]], axis=1))
        w["z"] = _mm(w["p"], split_heads(w["bh"]), TN)
    for w in pre.values():
        ar = _mm(w["a_rb"], w["tt"])
        w["oa"] = stack(w["r_t"]) - ar[:, :P2]
        w["oc"] = w["rkv"] - ar[:, P2:]
        w["pz"] = _mm(split_heads(w["a_t"]), w["z"], TN)
        w["kz"] = split_heads(w["kh"]) - _mm(w["a_ak"], w["z"], TN)
    for w in pre.values():
        w["q"] = _mm(split_heads(w["v"]), w["kz"], TN)

    s = [s_ref[p] for p in range(NP)]
    for c, p in order:
        w = pre[c, p]
        w["s0"] = s[p]
        s[p] = s[p] * w["dec"] - _mm(s[p], w["pz"]) + w["q"]
    for c, p in order:
        w = pre[c, p]
        o_st = _mm(w["oa"], w["s0"], NT) + w["oc"]
        y_ref[c * C:(c + 1) * C, p * P2:(p + 1) * P2] = unstack(o_st)
    for p in range(NP):
        s_ref[p] = s[p]

    if rev:
        o = of_ref[...] + y_ref[...]
        bd = bd_ref[...]
        mu = _mm2(o, bd) * (1.0 / HD)
        oc = o - mu
        var = _mm2(oc * oc, bd) * (1.0 / HD)
        o = oc * lax.rsqrt(var + RW_GN_EPS) * lnw_ref[...] + lnb_ref[...]
        y_ref[...] = (o + bonus_ref[...]) * gs_ref[...]


def _rw_parts(rev, r, kk, v, lw, akk, kd, extra=()):
    rows = r.shape[0]
    nb = rows // MIX_ROWS
    order = _blk_order(rev, nb)
    P2 = 2 * RW_HD
    W = RW_HEADS * RW_HD
    blk = pl.BlockSpec((MIX_ROWS, W), lambda b: (order(b), 0))
    in_specs = [blk] * 6 + [pl.BlockSpec((MIX_ROWS, MIX_ROWS), lambda b: (0, 0))]
    args = [r, kk, v, lw, akk, kd, _seg_tri(MIX_ROWS, RW_CHUNK, rev)]
    if rev:
        o_fwd, bonus, gs, lnw, lnb, bd512 = extra
        vec = pl.BlockSpec((1, W), lambda b: (0, 0))
        in_specs += [blk, blk, blk, vec, vec, pl.BlockSpec((W, W), lambda b: (0, 0))]
        args += [o_fwd, bonus, gs, lnw, lnb, bd512]
    return dict(body=functools.partial(_rw_scan_kernel, rev), in_specs=in_specs, args=args, out_spec=blk,
                out_shape=jax.ShapeDtypeStruct((rows, W), F32),
                scratch=pltpu.VMEM((RW_HEADS // 2, P2, P2), F32))


def _qk_prep_kernel(nq_ref, nk_ref, nv_ref, wq_ref, wkv_ref, nqn_ref, nkn_ref, wqn_ref, wkn_ref,
                    cos_ref, sin_ref, bd512_ref, bd128_ref,
                    naq_o, nak_o, nav_o, waq_o, wak_o, wav_o):
    bd512 = bd512_ref[...]
    bd128 = bd128_ref[...]

    def hnorm(x, g, bd, hd):
        return x * lax.rsqrt(_mm2(x * x, bd) * (1.0 / hd) + EPS) * g

    def rope(x, cos, sin):
        lane = lax.broadcasted_iota(jnp.int32, x.shape, 1)
        w = x.shape[1]
        partner = jnp.where((lane & 31) < 16, pltpu.roll(x, w - 16, axis=1), pltpu.roll(x, 16, axis=1))
        return x * cos + partner * sin

    def dup_heads(o_ref, x):
        lower = lax.broadcasted_iota(jnp.int32, x.shape, 1) < WA_HD
        swapped = pltpu.roll(x, WA_HD, axis=1)
        o_ref[0] = jnp.where(lower, x, swapped).astype(o_ref.dtype)
        o_ref[1] = jnp.where(lower, swapped, x).astype(o_ref.dtype)

    f32 = lambda ref: ref[...].astype(F32)
    naq_o[...] = (hnorm(f32(nq_ref), nqn_ref[...], bd512, NA_HD) * NA_HD ** -0.5).astype(BF16)
    nak_o[...] = hnorm(f32(nk_ref), nkn_ref[...], bd512, NA_HD).astype(BF16)
    nav_o[...] = nv_ref[...]
    cos = cos_ref[...]
    sin = sin_ref[...]
    wq = hnorm(f32(wq_ref), wqn_ref[...], bd512, WA_HD) * WA_HD ** -0.5
    wq = rope(wq, jnp.concatenate([cos] * 4, axis=1), jnp.concatenate([sin] * 4, axis=1))
    waq_o[...] = wq.astype(BF16)
    kv = f32(wkv_ref)
    dup_heads(wak_o, rope(hnorm(kv[:, :128], wkn_ref[...], bd128, WA_HD), cos, sin))
    dup_heads(wav_o, kv[:, 128:])


def _qk_prep(p, nqn, nkn, wqn, wkn, cos, sin, bd512, bd128):
    rows = p.shape[0]
    nb = rows // MIX_ROWS
    nab = OFF_NA // 512
    full = lambda shape: pl.BlockSpec(shape, lambda i: (0,) * len(shape))
    in_specs = [pl.BlockSpec((MIX_ROWS, 512), lambda i: (i, nab)),
                pl.BlockSpec((MIX_ROWS, 512), lambda i: (i, nab + 1)),
                pl.BlockSpec((MIX_ROWS, 512), lambda i: (i, nab + 2)),
                pl.BlockSpec((MIX_ROWS, 512), lambda i: (i, OFF_WA // 512)),
                pl.BlockSpec((MIX_ROWS, 256), lambda i: (i, (OFF_WA + 512) // 256)),
                full((1, 512)), full((1, 512)), full((1, 512)), full((1, 128)),
                pl.BlockSpec((MIX_ROWS, 128), lambda i: (i, 0)),
                pl.BlockSpec((MIX_ROWS, 128), lambda i: (i, 0)),
                full((512, 512)), full((128, 128))]
    flat = pl.BlockSpec((MIX_ROWS, 512), lambda i: (i, 0))
    flat_shape = jax.ShapeDtypeStruct((rows, 512), BF16)
    dup = pl.BlockSpec((WA_KV_HEADS, MIX_ROWS, 128), lambda i: (0, i, 0))
    dup_shape = jax.ShapeDtypeStruct((WA_KV_HEADS, rows, 128), BF16)
    return pl.pallas_call(
        _qk_prep_kernel,
        grid=(nb,),
        in_specs=in_specs,
        out_specs=[flat, flat, flat, flat, dup, dup],
        out_shape=[flat_shape, flat_shape, flat_shape, flat_shape, dup_shape, dup_shape],
        compiler_params=_cparams(("arbitrary",)),
        name="qk_prep",
    )(p, p, p, p, p, nqn, nkn, wqn, wkn, cos, sin, bd512, bd128)


def _na_bias_layout():
    wr, G, KR = NA_WIN_R, NA_GROUP, NA_KEY_ROWS
    off = (0, -(wr // 2), -(KR - G))
    out = {}
    for kind in range(3):
        for r in range(G):
            wstart = (0, r, KR - wr)[kind]
            for a in range(KR):
                inside = wstart <= a < wstart + wr
                out[kind, r, a] = (off[kind] + a - r + wr - 1) if inside else None
    return out


def _na_kernel(lc, nrows, q_ref, k_ref, v_ref, t_ref, o_ref, bias_s):
    W = GRID_W
    HD = NA_HD
    nq = NA_GROUP * W
    nk = NA_KEY_ROWS * W
    ngroups = nrows // NA_GROUP
    kc = k_ref[0:lc, :]
    vc = v_ref[0:lc, :]
    half = [lax.broadcasted_iota(jnp.int32, (1, 2 * HD), 1) // HD == hh for hh in range(2)]

    bias_s[:, :, :, 0:lc] = jnp.zeros((2, 3, nq, lc), F32)
    for (kind, r, a), ro in _na_bias_layout().items():
        c0 = lc + a * W
        for hh in range(2):
            tile = jnp.full((W, W), NEG, F32) if ro is None else t_ref[hh, ro, :, (a % 2) * W:(a % 2 + 1) * W]
            bias_s[hh, kind, r * W:(r + 1) * W, c0:c0 + W] = tile

    def attend(q, keys, vals, bias):
        out = None
        for hh in range(2):
            s = _mm(jnp.where(half[hh], q, 0), keys, NT)
            if bias is not None:
                s = s + bias(hh)
            p = jnp.exp(s - jnp.max(s, axis=-1, keepdims=True))
            o = _mm(p, jnp.where(half[hh], vals, 0)) / jnp.sum(p, axis=-1, keepdims=True)
            out = o if out is None else out + o
        return out

    o_ref[0:lc, :] = attend(q_ref[0:lc, :], kc, vc, None)

    def group(gi, carry):
        base = jnp.clip(gi * NA_GROUP - NA_WIN_R // 2, 0, nrows - NA_KEY_ROWS)
        kind = jnp.where(gi == 0, 0, jnp.where(gi == ngroups - 1, 2, 1))
        q0 = pl.multiple_of(lc + gi * nq, nq)
        k0 = pl.multiple_of(lc + base * W, W)
        keys = jnp.concatenate([kc, k_ref[pl.ds(k0, nk), :]], axis=0)
        vals = jnp.concatenate([vc, v_ref[pl.ds(k0, nk), :]], axis=0)
        o_ref[pl.ds(q0, nq), :] = attend(q_ref[pl.ds(q0, nq), :], keys, vals, lambda hh: bias_s[hh, kind])
        return carry

    lax.fori_loop(0, ngroups, group, 0, unroll=4)


def _na(q, k, v, t2, lc):
    rows = q.shape[0]
    nrows = (rows - lc) // GRID_W
    pblk = pl.BlockSpec((rows, 2 * NA_HD), lambda p: (0, p))
    return pl.pallas_call(
        functools.partial(_na_kernel, lc, nrows),
        grid=(NA_HEADS // 2,),
        in_specs=[pblk, pblk, pblk, pl.BlockSpec((2,) + t2.shape[1:], lambda p: (p, 0, 0, 0))],
        out_specs=pblk,
        out_shape=jax.ShapeDtypeStruct((rows, NA_HEADS * NA_HD), F32),
        scratch_shapes=[pltpu.VMEM((2, 3, NA_GROUP * GRID_W, lc + NA_KEY_ROWS * GRID_W), F32)],
        compiler_params=_cparams(("arbitrary",)),
        name="na_attn",
    )(q, k, v, t2)


def _wa_kernel(lc, t_len, q_ref, k_ref, v_ref, sink_ref, o_ref):
    G = WA_HEADS // WA_KV_HEADS
    B = WA_WINDOW
    nband = 3 * B
    HD = WA_HD
    b = pl.program_id(1)
    kc = k_ref[0:lc, :]
    vc = v_ref[0:lc, :]
    sink = sink_ref[...]
    lower = lax.broadcasted_iota(jnp.int32, (1, 2 * HD), 1) < HD

    def stacked_queries(r0, n):
        parts = []
        for g in range(G):
            qp = q_ref[r0:r0 + n, (g // 2) * 2 * HD:(g // 2 + 1) * 2 * HD]
            parts.append(jnp.where(lower if g % 2 == 0 else jnp.logical_not(lower), qp, 0))
        return jnp.concatenate(parts, axis=0)

    def store(r0, n, o):
        for gp in range(G // 2):
            o_ref[r0:r0 + n, gp * 2 * HD:(gp + 1) * 2 * HD] = jnp.where(
                lower, o[2 * gp * n:(2 * gp + 1) * n], o[(2 * gp + 1) * n:(2 * gp + 2) * n])

    @pl.when(b == 0)
    def _():
        sk = jnp.concatenate([jnp.broadcast_to(sink[g * B:g * B + 1, :], (lc, 1)) for g in range(G)], axis=0)
        s = _mm(stacked_queries(0, lc), kc, NT)
        m = jnp.maximum(jnp.max(s, axis=-1, keepdims=True), sk)
        e = jnp.exp(s - m)
        store(0, lc, _mm(e, vc) / (jnp.sum(e, axis=-1, keepdims=True) + jnp.exp(sk - m)))

    @pl.when(b > 0)
    def _():
        qoff = lax.broadcasted_iota(jnp.int32, (G * B, lc + nband), 0) & (B - 1)
        koff = lax.broadcasted_iota(jnp.int32, (G * B, lc + nband), 1) - lc
        for j in range(MIX_ROWS // B):
            n = (b - 1) * (MIX_ROWS // B) + j
            start = jnp.clip((n - 1) * B, 0, t_len - nband)
            k0 = pl.multiple_of(lc + start, B)
            keys = jnp.concatenate([kc, k_ref[pl.ds(k0, nband), :]], axis=0)
            vals = jnp.concatenate([vc, v_ref[pl.ds(k0, nband), :]], axis=0)
            valid = jnp.logical_or(koff < 0, jnp.abs((n * B + qoff) - (start + koff)) <= WA_WINDOW)
            s = jnp.where(valid, _mm(stacked_queries(j * B, B), keys, NT), NEG)
            m = jnp.maximum(jnp.max(s, axis=-1, keepdims=True), sink)
            p = jnp.exp(s - m)
            store(j * B, B, _mm(p, vals) / (jnp.sum(p, axis=-1, keepdims=True) + jnp.exp(sink - m)))


def _wa(q, k, v, sink_col, lc):
    rows = q.shape[0]
    G = WA_HEADS // WA_KV_HEADS
    kvblk = pl.BlockSpec((None, rows, 2 * WA_HD), lambda h, b: (h, 0, 0))
    qblk = pl.BlockSpec((MIX_ROWS, G * WA_HD), lambda h, b: (b, h))
    return pl.pallas_call(
        functools.partial(_wa_kernel, lc, rows - lc),
        grid=(WA_KV_HEADS, rows // MIX_ROWS),
        in_specs=[qblk, kvblk, kvblk,
                  pl.BlockSpec((None, G * WA_WINDOW, 1), lambda h, b: (h, 0, 0))],
        out_specs=qblk,
        out_shape=jax.ShapeDtypeStruct((rows, WA_HEADS * WA_HD), F32),
        compiler_params=_cparams(("arbitrary", "arbitrary")),
        name="wa_attn",
    )(q, k, v, sink_col)


def _block_diag(n, blk):
    idx = np.arange(n) // blk
    return jnp.asarray(idx[:, None] == idx[None, :], dtype=BF16)


def _na_bias_table(rpb):
    W, wc = GRID_W, NA_WIN_C
    j = np.arange(W)
    cstart = np.clip(j - wc // 2, 0, W - wc)
    cabs = np.arange(W)
    inwin = (cabs[None, :] >= cstart[:, None]) & (cabs[None, :] < cstart[:, None] + wc)
    cb = cabs[None, :] - j[:, None] + wc - 1
    nh, nro, nco = rpb.shape
    onehot = (cb[None] == np.arange(nco)[:, None, None]) & inwin[None]
    t = jnp.dot(rpb.reshape(nh * nro, nco), jnp.asarray(onehot.reshape(nco, W * W), F32),
                precision=lax.Precision.HIGHEST).reshape(nh, nro, W, W)
    t = t + jnp.asarray(np.where(inwin, 0.0, NEG), F32)
    return jnp.concatenate([t, t], axis=-1)


def _rope_tables(lc, t_len):
    pos = np.arange(t_len)
    n = WA_HD // 4
    inv = ROPE_BASE ** (-np.arange(n, dtype=np.float64) / n)
    ang_r = (pos // GRID_W)[:, None] * inv[None, :]
    ang_c = (pos % GRID_W)[:, None] * inv[None, :]
    cos = np.concatenate([np.cos(ang_r)] * 2 + [np.cos(ang_c)] * 2, axis=1)
    sin = np.concatenate([-np.sin(ang_r), np.sin(ang_r), -np.sin(ang_c), np.sin(ang_c)], axis=1)
    cos = np.concatenate([np.ones((lc, WA_HD)), cos], axis=0)
    sin = np.concatenate([np.zeros((lc, WA_HD)), sin], axis=0)
    return (jnp.asarray(np.tile(cos, (1, 2)), dtype=F32), jnp.asarray(np.tile(sin, (1, 2)), dtype=F32))


def _cast_kernel(x_ref, o_ref):
    o_ref[...] = x_ref[...].astype(o_ref.dtype)


def _cast_bf16(w, layer, bk, bn, src_col_block=None):
    _, k, n = w.shape
    col = src_col_block or (lambda j: j)
    return pl.pallas_call(
        _cast_kernel,
        grid=(k // bk, n // bn),
        in_specs=[pl.BlockSpec((None, bk, bn), lambda i, j: (layer, i, col(j)))],
        out_specs=pl.BlockSpec((bk, bn), lambda i, j: (i, j)),
        out_shape=jax.ShapeDtypeStruct((k, n), BF16),
        compiler_params=_cparams(("arbitrary", "arbitrary")),
        name="cast_bf16",
    )(w)


W_IN_BLK = 256
_W_IN_SRC = np.concatenate([np.arange(10, 18), np.arange(0, 10), np.arange(19, 25), np.arange(28, 60),
                            np.arange(25, 28), np.arange(18, 19)])


def _w_in_src_block(j):
    starts = (0, 8, 18, 24, 56, 59)
    src = j + int(_W_IN_SRC[0])
    for s in starts[1:]:
        src = jnp.where(j >= s, j + int(_W_IN_SRC[s]) - s, src)
    return src


def _lora_pad(w, row0):
    out = jnp.zeros((2, 256, w.shape[-1]), F32)
    for d in range(2):
        out = out.at[d, row0 + 64 * d:row0 + 64 * (d + 1)].set(w[d])
    return out


def _token_mixing(p, lc, layer, hg_lb, hg_norm, rw_shift, rw_w0, rw_w2, rw_a0, rw_a2, rw_kk, rw_ka, rw_rk,
                  rw_ln_w, rw_ln_b, na_qn, na_kn, na_rpb, wa_qn, wa_kn, wa_sink, tables):
    rows = p.shape[0]
    bd512, bd128, cos, sin = tables
    cum = jnp.cumsum(jax.nn.softmax(hg_lb.astype(F32), axis=1), axis=1)
    lbs = cum[:, layer] - cum[:, 0]
    prep = _rw_prep(p, rw_shift[:, :2048], rw_shift[:, 2048:], rw_w0, _lora_pad(rw_w2, 0), rw_a0,
                    _lora_pad(rw_a2, 128), rw_kk[None], rw_ka[None], rw_rk.reshape(1, -1), bd512)
    r, kk, v, gs, bonus, lw0, lw1, akk0, akk1, kd0, kd1 = prep
    oa_f, ob_f = _run_scans([_gla_parts(p, lbs[0:1], False), _rw_parts(False, r, kk, v, lw0, akk0, kd0)],
                            rows, "scan_fwd")
    y_a, y_b = _run_scans([_gla_parts(p, lbs[1:2], True, oa_f, hg_norm[None]),
                           _rw_parts(True, r, kk, v, lw1, akk1, kd1,
                                     (ob_f, bonus, gs, rw_ln_w[None], rw_ln_b[None], bd512))],
                          rows, "scan_rev")
    tile = lambda g, n: jnp.tile(g, n)[None]
    naq, nak, nav, waq, wak, wav = _qk_prep(p, tile(na_qn, 8), tile(na_kn, 8), tile(wa_qn, 8),
                                            tile(wa_kn, 2), cos, sin, bd512, bd128)
    y_c = _na(naq, nak, nav, _na_bias_table(na_rpb), lc)
    G = WA_HEADS // WA_KV_HEADS
    sink_col = jnp.repeat(wa_sink.reshape(WA_KV_HEADS, G), WA_WINDOW, axis=1)[..., None]
    y_d = _wa(waq, wak, wav, sink_col, lc)
    return y_a, y_b, y_c, y_d


def kernel(x, c, ctx, c_ctx, ada_w, ada_b, norm_ffn1, norm_mix, norm_ffn2, ffn1_wi, ffn1_wo, ffn2_wi, ffn2_wo, w_in, hg_lb, hg_norm, rw_shift, rw_w0, rw_w2, rw_a0, rw_a2, rw_kk, rw_ka, rw_rk, rw_ln_w, rw_ln_b, na_qn, na_kn, na_rpb, wa_qn, wa_kn, wa_sink, w_branch, w_out):
    assert x.shape[0] == 1 and ctx.shape[1] == MIX_ROWS
    lc = ctx.shape[1]
    t_len = x.shape[1]
    xa = jnp.concatenate([ctx[0], x[0]], axis=0)
    cc_t = jnp.stack([c[0], c_ctx], axis=1)
    mods = _ada_mods(cc_t, ada_w, ada_b).reshape(DEPTH, 2, N_MOD, D)
    tables = (_block_diag(512, 64), _block_diag(128, 64)) + _rope_tables(lc, t_len)
    half = W_IN_BLK // 2
    w_in_src = lambda c: 2 * _w_in_src_block(c // 2) + c % 2
    wi1, wo1 = _cast_bf16(ffn1_wi, 0, D, 1024), _cast_bf16(ffn1_wo, 0, 512, D)
    for l in range(DEPTH):
        m = mods[l]
        xa, w_in_l = _ffn(xa, m[:, 0:3], norm_ffn1[l][None], wi1, wo1, lc,
                          jobs=[_cast_job(w_in, l, D, half, w_in_src)])
        jobs = [_cast_job(w_branch.reshape(DEPTH, N_BRANCH * BR_W, D), l, 256, 256), _cast_job(w_out, l, 256, 256),
                _cast_job(ffn2_wi, l, D, half), _cast_job(ffn2_wo, l, 64, D)]
        if l + 1 < DEPTH:
            jobs += [_cast_job(ffn1_wi, l + 1, D, half), _cast_job(ffn1_wo, l + 1, 64, D)]
        p, wb, w_out_l, wi2, wo2, *nxt = _win(xa, m[:, 3:6], norm_mix[l][None], w_in_l, lc, jobs=jobs)
        if nxt:
            wi1, wo1 = nxt
        yb = _token_mixing(p, lc, l, hg_lb, hg_norm[l], rw_shift[l], rw_w0[l], rw_w2[l], rw_a0[l], rw_a2[l],
                           rw_kk[l], rw_ka[l], rw_rk[l], rw_ln_w[l], rw_ln_b[l], na_qn[l], na_kn[l],
                           na_rpb[l], wa_qn[l], wa_kn[l], wa_sink[l], tables)
        xa = _merge(xa, m[:, 3:6], yb, p, wb.reshape(N_BRANCH, BR_W, D), w_out_l, lc)
        xa, = _ffn(xa, m[:, 6:9], norm_ffn2[l][None], wi2, wo2, lc)
    return xa[lc:][None]
```

```python
import functools

import numpy as np
import jax
import jax.numpy as jnp
from jax import lax
from jax.experimental import pallas as pl
from jax.experimental.pallas import tpu as pltpu

F32 = jnp.float32
BF16 = jnp.bfloat16

D = 2048
DEPTH = 2
GRID_W = 64
EPS = 1e-6
D_FF = 5632
N_MOD = 9
N_BRANCH = 4
BR_W = 512
HG_HEADS, HG_DK = 4, 128
RW_HEADS, RW_HD = 8, 64
RW_GN_EPS = 64e-5
NA_HEADS, NA_HD = 8, 64
NA_WIN_R, NA_WIN_C = 8, 16
NA_GROUP = 4
NA_KEY_ROWS = NA_WIN_R + NA_GROUP - 1
WA_HEADS, WA_KV_HEADS, WA_HD = 8, 2, 64
WA_WINDOW = 128
ROPE_BASE = 10000.0

OFF_RW, OFF_HG, OFF_NA, OFF_GL, OFF_WA, OFF_LORA = 0, 2048, 4608, 6144, 14336, 15104
P_TOTAL = 15360

MIX_ROWS = 256
HALO = 16
GLA_CHUNK = 16
RW_CHUNK = 64
NEG = -1e30
LOG2E = 1.4426950408889634
VMEM_LIMIT = 60 * 1024 * 1024

NT = (((1,), (1,)), ((), ()))
TN = (((0,), (0,)), ((), ()))


def _cparams(sem):
    return pltpu.CompilerParams(dimension_semantics=sem, vmem_limit_bytes=VMEM_LIMIT)


def _mm(a, b, dims=None):
    a = a.astype(BF16)
    b = b.astype(BF16)
    if dims is None:
        return jnp.dot(a, b, preferred_element_type=F32)
    return lax.dot_general(a, b, dims, preferred_element_type=F32)


def _mm2(x, w_bf16):
    hi = x.astype(BF16)
    lo = (x - hi.astype(F32)).astype(BF16)
    return (jnp.dot(hi, w_bf16, preferred_element_type=F32)
            + jnp.dot(lo, w_bf16, preferred_element_type=F32))


def _mm_f32(a, b, dims=None):
    a = a.astype(F32)
    b = b.astype(F32)
    a_hi, b_hi = a.astype(BF16), b.astype(BF16)
    a_lo = (a - a_hi.astype(F32)).astype(BF16)
    b_lo = (b - b_hi.astype(F32)).astype(BF16)
    dot = functools.partial(_mm, dims=dims)
    return dot(a_hi, b_hi) + (dot(a_hi, b_lo) + dot(a_lo, b_hi))


def _sigmoid(x):
    return 1.0 / (1.0 + jnp.exp(-x))


def _silu(x):
    return x * _sigmoid(x)


def _seg_cumsum(x, tri):
    w = x.shape[1]
    hi = x.astype(BF16)
    r1 = x - hi.astype(F32)
    mid = r1.astype(BF16)
    lo = (r1 - mid.astype(F32)).astype(BF16)
    y = jnp.dot(tri, jnp.concatenate([hi, mid, lo], axis=1), preferred_element_type=F32)
    return y[:, :w] + y[:, w:2 * w] + y[:, 2 * w:]


def _seg_tri(rows, seg, rev):
    r = np.arange(rows)
    same = (r[:, None] // seg) == (r[None, :] // seg)
    tri = (r[None, :] >= r[:, None]) if rev else (r[None, :] <= r[:, None])
    return jnp.asarray(same & tri, dtype=BF16)


def _ada_kernel(cc_ref, w_ref, b_ref, o_ref):
    s = _silu(cc_ref[...])
    w = w_ref[...]
    r0 = jnp.sum(w * s[:, 0:1], axis=0, keepdims=True)
    r1 = jnp.sum(w * s[:, 1:2], axis=0, keepdims=True)
    o_ref[...] = jnp.concatenate([r0, r1], axis=0) + b_ref[...]


def _ada_mods(cc_t, ada_w, ada_b):
    tn = 1024
    nmod = ada_w.shape[-1]
    return pl.pallas_call(
        _ada_kernel,
        grid=(DEPTH, nmod // tn),
        in_specs=[pl.BlockSpec((D, 2), lambda l, j: (0, 0)),
                  pl.BlockSpec((None, D, tn), lambda l, j: (l, 0, j)),
                  pl.BlockSpec((None, 1, tn), lambda l, j: (l, 0, j))],
        out_specs=pl.BlockSpec((None, 2, tn), lambda l, j: (l, 0, j)),
        out_shape=jax.ShapeDtypeStruct((DEPTH, 2, nmod), F32),
        compiler_params=_cparams(("arbitrary", "arbitrary")),
        name="ada_mod",
    )(cc_t, ada_w, ada_b.reshape(DEPTH, 1, nmod))


def _cast_job(w, layer, bk, bn, src_col_block=None):
    return (w, layer, bk, bn, src_col_block)


def _cast_job_specs(jobs, grid):
    in_specs, out_specs, out_shapes, args = [], [], [], []
    for w, layer, bk, bn, src_col in jobs:
        _, k, n = w.shape
        ncol = n // bn
        nblk = (k // bk) * ncol
        assert nblk <= grid[0] * grid[1], (nblk, grid)
        col = src_col or (lambda c: c)

        def blk(i, j, nblk=nblk):
            return jnp.minimum(i * grid[1] + j, nblk - 1)

        in_specs.append(pl.BlockSpec(
            (None, bk, bn), lambda i, j, blk=blk, ncol=ncol, col=col, layer=layer:
            (layer, blk(i, j) // ncol, col(blk(i, j) % ncol))))
        out_specs.append(pl.BlockSpec((bk, bn), lambda i, j, blk=blk, ncol=ncol: (blk(i, j) // ncol, blk(i, j) % ncol)))
        out_shapes.append(jax.ShapeDtypeStruct((k, n), BF16))
        args.append(w)
    return in_specs, out_specs, out_shapes, args


def _run_cast_jobs(src_refs, dst_refs):
    for src, dst in zip(src_refs, dst_refs):
        dst[...] = src[...].astype(dst.dtype)


def _row_is_ctx(i, tm, lc):
    return (i * tm + lax.broadcasted_iota(jnp.int32, (tm, 1), 0)) < lc


def _mod_pick(mod_ref, j, is_ctx):
    return jnp.where(is_ctx, mod_ref[1, j:j + 1, :], mod_ref[0, j:j + 1, :])


def _store_modulated(xn_ref, x_ref, g_ref, mod_ref, i, tm, lc):
    def normed():
        x = x_ref[...]
        return x * lax.rsqrt(jnp.mean(x * x, axis=-1, keepdims=True) + EPS)

    @pl.when(i * tm < lc)
    def _():
        is_ctx = _row_is_ctx(i, tm, lc)
        y = normed() * g_ref[...]
        xn_ref[...] = (y * (1.0 + _mod_pick(mod_ref, 1, is_ctx)) + _mod_pick(mod_ref, 0, is_ctx)).astype(BF16)

    @pl.when(i * tm >= lc)
    def _():
        gain = g_ref[...] * (1.0 + mod_ref[0, 1:2, :])
        xn_ref[...] = (normed() * gain + mod_ref[0, 0:1, :]).astype(BF16)


def _ffn_kernel(lc, tm, nf, njobs, x_ref, mod_ref, g_ref, wa_ref, wb_ref, wo_ref, *rest):
    src_refs, (o_ref, *dst_refs, xn_ref) = rest[:njobs], rest[njobs:]
    i = pl.program_id(0)
    j = pl.program_id(1)
    _run_cast_jobs(src_refs, dst_refs)

    @pl.when(j == 0)
    def _():
        _store_modulated(xn_ref, x_ref, g_ref, mod_ref, i, tm, lc)
        o_ref[...] = jnp.zeros_like(o_ref)

    xn = xn_ref[...]
    a = jnp.dot(xn, wa_ref[...], preferred_element_type=F32)
    b = jnp.dot(xn, wb_ref[...], preferred_element_type=F32)
    h = (_silu(a) * b).astype(BF16)
    o_ref[...] += jnp.dot(h, wo_ref[...], preferred_element_type=F32)

    @pl.when(j == nf - 1)
    def _():
        o_ref[...] = x_ref[...] + 0.5 * _mod_pick(mod_ref, 2, _row_is_ctx(i, tm, lc)) * o_ref[...]


def _dense_tm(rows):
    for tm in (768, 512, 256):
        if rows % tm == 0:
            return tm
    raise ValueError(rows)


def _ffn(x, mod3, g, wi, wo, lc, jobs=()):
    rows = x.shape[0]
    tm = _dense_tm(rows)
    tf = 512
    nf = D_FF // tf
    grid = (rows // tm, nf)
    job_in, job_out, job_shape, job_args = _cast_job_specs(jobs, grid)
    return pl.pallas_call(
        functools.partial(_ffn_kernel, lc, tm, nf, len(jobs)),
        grid=grid,
        in_specs=[pl.BlockSpec((tm, D), lambda i, j: (i, 0)),
                  pl.BlockSpec((2, 3, D), lambda i, j: (0, 0, 0)),
                  pl.BlockSpec((1, D), lambda i, j: (0, 0)),
                  pl.BlockSpec((D, tf), lambda i, j: (0, j)),
                  pl.BlockSpec((D, tf), lambda i, j: (0, j + nf)),
                  pl.BlockSpec((tf, D), lambda i, j: (j, 0))] + job_in,
        out_specs=[pl.BlockSpec((tm, D), lambda i, j: (i, 0))] + job_out,
        out_shape=[jax.ShapeDtypeStruct((rows, D), F32)] + job_shape,
        scratch_shapes=[pltpu.VMEM((tm, D), BF16)],
        compiler_params=_cparams(("arbitrary", "arbitrary")),
        name="ffn",
    )(x, mod3, g, wi, wi, wo, *job_args)


def _win_kernel(lc, tm, njobs, x_ref, mod_ref, g_ref, w_ref, *rest):
    src_refs, (o_ref, *dst_refs, xn_ref) = rest[:njobs], rest[njobs:]
    i = pl.program_id(0)
    _run_cast_jobs(src_refs, dst_refs)

    @pl.when(pl.program_id(1) == 0)
    def _():
        _store_modulated(xn_ref, x_ref, g_ref, mod_ref, i, tm, lc)

    o_ref[...] = jnp.dot(xn_ref[...], w_ref[...], preferred_element_type=F32).astype(o_ref.dtype)


def _win(x, mod3, g, w, lc, jobs=()):
    rows = x.shape[0]
    tm = _dense_tm(rows)
    tn = 1536
    grid = (rows // tm, P_TOTAL // tn)
    job_in, job_out, job_shape, job_args = _cast_job_specs(jobs, grid)
    return pl.pallas_call(
        functools.partial(_win_kernel, lc, tm, len(jobs)),
        grid=grid,
        in_specs=[pl.BlockSpec((tm, D), lambda i, j: (i, 0)),
                  pl.BlockSpec((2, 3, D), lambda i, j: (0, 0, 0)),
                  pl.BlockSpec((1, D), lambda i, j: (0, 0)),
                  pl.BlockSpec((D, tn), lambda i, j: (0, j))] + job_in,
        out_specs=[pl.BlockSpec((tm, tn), lambda i, j: (i, j))] + job_out,
        out_shape=[jax.ShapeDtypeStruct((rows, P_TOTAL), BF16)] + job_shape,
        scratch_shapes=[pltpu.VMEM((tm, D), BF16)],
        compiler_params=_cparams(("arbitrary", "arbitrary")),
        name="w_in",
    )(x, mod3, g, w, *job_args)


def _merge_kernel(lc, tm, x_ref, mod_ref, ya_ref, yb_ref, yc_ref, yd_ref, gl0_ref, gl1_ref, wb_ref, wo_ref,
                  o_ref, acc_ref):
    i = pl.program_id(0)
    n = pl.program_id(1)
    y_refs = (ya_ref, yb_ref, yc_ref, yd_ref)
    gl_refs = (gl0_ref, gl1_ref)

    for nn in range(N_BRANCH // 2):
        @pl.when(n == nn)
        def _():
            gated = None
            for k in range(2):
                br = 2 * nn + k
                proj = jnp.dot(y_refs[br][...].astype(BF16), wb_ref[br], preferred_element_type=F32)
                term = _sigmoid(gl_refs[k][...].astype(F32)) * proj
                gated = term if gated is None else gated + term
            if nn == 0:
                acc_ref[...] = gated
            else:
                acc_ref[...] += gated

    @pl.when(n == N_BRANCH // 2 - 1)
    def _():
        is_ctx = _row_is_ctx(i, tm, lc)
        y = jnp.dot(acc_ref[...].astype(BF16), wo_ref[...], preferred_element_type=F32)
        o_ref[...] = x_ref[...] + _mod_pick(mod_ref, 2, is_ctx) * y


def _merge(x, mod3, ys, p, wb, wo, lc):
    rows = x.shape[0]
    tm = 384 if rows % 384 == 0 else MIX_ROWS
    glb = OFF_GL // D
    yspec = pl.BlockSpec((tm, BR_W), lambda i, n: (i, 0))
    return pl.pallas_call(
        functools.partial(_merge_kernel, lc, tm),
        grid=(rows // tm, N_BRANCH // 2),
        in_specs=[pl.BlockSpec((tm, D), lambda i, n: (i, 0)),
                  pl.BlockSpec((2, 3, D), lambda i, n: (0, 0, 0)),
                  yspec, yspec, yspec, yspec,
                  pl.BlockSpec((tm, D), lambda i, n: (i, glb + 2 * n)),
                  pl.BlockSpec((tm, D), lambda i, n: (i, glb + 2 * n + 1)),
                  pl.BlockSpec((N_BRANCH, BR_W, D), lambda i, n: (0, 0, 0)),
                  pl.BlockSpec((D, D), lambda i, n: (0, 0))],
        out_specs=pl.BlockSpec((tm, D), lambda i, n: (i, 0)),
        out_shape=jax.ShapeDtypeStruct((rows, D), F32),
        scratch_shapes=[pltpu.VMEM((tm, D), F32)],
        compiler_params=_cparams(("arbitrary", "arbitrary")),
        name="merge",
    )(x, mod3, *ys, p, p, wb, wo)


def _gla_kernel(rev, *refs):
    if rev:
        q_ref, f_ref, i_ref, lb_ref, tri_ref, of_ref, g_ref, nw_ref, y_ref, st_ref = refs
    else:
        q_ref, f_ref, i_ref, lb_ref, tri_ref, y_ref, st_ref = refs
    C = GLA_CHUNK
    R = MIX_ROWS
    NH, DK = HG_HEADS, HG_DK

    @pl.when(pl.program_id(0) == 0)
    def _():
        st_ref[...] = jnp.zeros_like(st_ref)

    yield
    lb = lb_ref[...]
    f_all = lb + (1.0 - lb) * _sigmoid(f_ref[...].astype(F32))
    b_all = _seg_cumsum(jnp.log(f_all), tri_ref[...])
    SUB = C // 2
    sub_pos = lax.broadcasted_iota(jnp.int32, (1, SUB, 1), 1)
    chunks = list(range(R // C - 1, -1, -1) if rev else range(R // C))

    heads = []
    for h in range(NH):
        hs = slice(h * DK, (h + 1) * DK)
        b = b_all[:, hs]
        k = 1.0 - f_all[:, hs]
        q = q_ref[:, hs].astype(F32)
        v = i_ref[:, hs].astype(F32)
        def halves(t):
            t3 = t.reshape(R // C, C, DK)
            lo, hi = t3[:, :SUB], t3[:, SUB:]
            return (hi, lo) if rev else (lo, hi)

        def pair_term(qq, bq, keys, ok):
            bk_, kk_, vk_ = keys
            diff = bq - bk_
            e = jnp.exp2(diff if ok is None else jnp.where(ok, diff, NEG))
            return jnp.sum(qq * kk_ * e, axis=-1, keepdims=True) * vk_

        (qn, qf), (bn, bf), (kn, kf), (vn, vf) = halves(q), halves(b * LOG2E), halves(k), halves(v)
        near, far = (bn, kn, vn), (bf, kf, vf)
        o_n = jnp.sum(qn * kn, axis=-1, keepdims=True) * vn
        o_f = jnp.sum(qf * kf, axis=-1, keepdims=True) * vf + pair_term(qf, bf, near, None)
        for d in range(1, SUB):
            sh = (SUB - d) if rev else d
            ok = (sub_pos < SUB - d) if rev else (sub_pos >= d)
            near_r = tuple(pltpu.roll(t, sh, axis=1) for t in near)
            far_r = tuple(jnp.where(ok, pltpu.roll(t, sh, axis=1), tn) for t, tn in zip(far, near_r))
            o_n = o_n + pair_term(qn, bn, near_r, ok)
            o_f = o_f + pair_term(qf, bf, far_r, None) + pair_term(qf, bf, near_r, ok)
        o = jnp.concatenate([o_f, o_n] if rev else [o_n, o_f], axis=1).reshape(R, DK)
        upd, dec = {}, {}
        for c in chunks:
            sl = slice(c * C, (c + 1) * C)
            b_end = b[sl][0:1] if rev else b[sl][C - 1:C]
            dec[c] = jnp.exp(b_end)
            upd[c] = _mm(v[sl], k[sl] * jnp.exp(b_end - b[sl]), TN)
        heads.append(dict(o=o, qb=q * jnp.exp(b), upd=upd, dec=dec, st=st_ref[h]))

    for c in chunks:
        for w in heads:
            w[c] = w["st"]
            w["st"] = w["st"] * w["dec"][c] + w["upd"][c]
    outs = []
    for h, w in enumerate(heads):
        st_ref[h] = w["st"]
        inter = {c: _mm(w["qb"][c * C:(c + 1) * C], w[c], NT) for c in chunks}
        outs.append(w["o"] + jnp.concatenate([inter[c] for c in range(R // C)], axis=0))

    for h, o in enumerate(outs):
        hs = slice(h * DK, (h + 1) * DK)
        if rev:
            o = of_ref[:, hs] + o
            o = o * lax.rsqrt(jnp.mean(o * o, axis=-1, keepdims=True) + EPS) * nw_ref[:, hs]
            o = o * _silu(g_ref[:, hs].astype(F32))
        y_ref[:, hs] = o


def _blk_order(rev, nb):
    if rev:
        return lambda b: jnp.where(b == 0, 0, nb - b)
    return lambda b: b


def _gla_parts(p, lb, rev, o_fwd=None, norm_w=None):
    rows = p.shape[0]
    nb = rows // MIX_ROWS
    order = _blk_order(rev, nb)
    W = HG_HEADS * HG_DK
    cb = OFF_HG // W

    def col(n):
        return pl.BlockSpec((MIX_ROWS, W), lambda b: (order(b), cb + n))

    vec = pl.BlockSpec((1, W), lambda b: (0, 0))
    in_specs = [col(0), col(2 if rev else 1), col(3), vec,
                pl.BlockSpec((MIX_ROWS, MIX_ROWS), lambda b: (0, 0))]
    args = [p, p, p, lb, _seg_tri(MIX_ROWS, GLA_CHUNK, rev)]
    if rev:
        in_specs += [pl.BlockSpec((MIX_ROWS, W), lambda b: (order(b), 0)), col(4), vec]
        args += [o_fwd, p, norm_w]
    return dict(body=functools.partial(_gla_kernel, rev), in_specs=in_specs, args=args,
                out_spec=pl.BlockSpec((MIX_ROWS, W), lambda b: (order(b), 0)),
                out_shape=jax.ShapeDtypeStruct((rows, W), F32),
                scratch=pltpu.VMEM((HG_HEADS, HG_DK, HG_DK), F32))


def _run_scans(parts, rows, name):
    n_in = [len(p["in_specs"]) for p in parts]

    def kernel(*refs):
        pos = 0
        ins = []
        for n in n_in:
            ins.append(refs[pos:pos + n])
            pos += n
        outs = refs[pos:pos + len(parts)]
        scratch = refs[pos + len(parts):]
        bodies = [p["body"](*i, o, s) for p, i, o, s in zip(parts, ins, outs, scratch)]
        for body in bodies:
            next(body)
        for body in bodies:
            for _ in body:
                pass

    return pl.pallas_call(
        kernel,
        grid=(rows // MIX_ROWS,),
        in_specs=[s for p in parts for s in p["in_specs"]],
        out_specs=[p["out_spec"] for p in parts],
        out_shape=[p["out_shape"] for p in parts],
        scratch_shapes=[p["scratch"] for p in parts],
        compiler_params=_cparams(("arbitrary",)),
        name=name,
    )(*[a for p in parts for a in p["args"]])


def _rw_prep_kernel(nb, main_ref, lora_ref, pm_ref, nm_ref, pl_ref, nl_ref, tm_ref, tl_ref,
                    w0_ref, w2_ref, a0_ref, a2_ref, kkw_ref, ka_ref, rk_ref, bd_ref,
                    r_o, kk_o, v_o, gs_o, bonus_o, lw0_o, lw1_o, akk0_o, akk1_o, kd0_o, kd1_o):
    i = pl.program_id(0)
    has_prev = (i >= 2).astype(F32)
    has_next = jnp.logical_and(i != 0, i != nb - 1).astype(F32)
    rowi = lax.broadcasted_iota(jnp.int32, (MIX_ROWS, 1), 0)

    def shift(x, prev_blk, next_blk, taps):
        up = jnp.where(rowi == 0, prev_blk[HALO - 1:HALO, :] * has_prev, pltpu.roll(x, 1, axis=0))
        dn = jnp.where(rowi == MIX_ROWS - 1, next_blk[0:1, :] * has_next,
                       pltpu.roll(x, MIX_ROWS - 1, axis=0))
        return taps[0:1] * up + taps[1:2] * x + taps[2:3] * dn

    f32 = lambda ref: ref[...].astype(F32)
    main = shift(f32(main_ref), f32(pm_ref), f32(nm_ref), tm_ref[...])
    lora = shift(f32(lora_ref), f32(pl_ref), f32(nl_ref), tl_ref[...])
    W = RW_HEADS * RW_HD
    r, k, v, g = (main[:, n * W:(n + 1) * W] for n in range(4))
    bd = bd_ref[...]
    kk = k * kkw_ref[...]
    kk = kk * lax.rsqrt(_mm2(kk * kk, bd) + EPS)
    tl = jnp.tanh(lora)
    kds = []
    for d, (lw_o, akk_o, kd_o) in enumerate(((lw0_o, akk0_o, kd0_o), (lw1_o, akk1_o, kd1_o))):
        z = -(w0_ref[d:d + 1, :] + _mm_f32(tl, w2_ref[d]))
        softplus = jnp.maximum(z, 0.0) + jnp.log(1.0 + jnp.exp(-jnp.abs(z)))
        lw_o[...] = -jnp.exp(-softplus - 0.5)
        a = _sigmoid(a0_ref[d:d + 1, :] + _mm_f32(lora, a2_ref[d]))
        kd = k * (1.0 + (a - 1.0) * ka_ref[...])
        kds.append(kd)
        kd_o[...] = kd.astype(kd_o.dtype)
        akk_o[...] = (a * kk).astype(akk_o.dtype)
    r_o[...] = r.astype(r_o.dtype)
    kk_o[...] = kk.astype(kk_o.dtype)
    v_o[...] = v.astype(v_o.dtype)
    gs_o[...] = _sigmoid(g)
    bonus_o[...] = _mm2(r * (kds[0] + kds[1]) * rk_ref[...], bd) * v


def _rw_prep(p, taps_m, taps_l, w0, w2p, a0, a2p, kkw, ka, rk, bd512):
    rows = p.shape[0]
    nb = rows // MIX_ROWS
    W = RW_HEADS * RW_HD
    n_halo = rows // HALO
    lb = OFF_LORA // 256
    per = MIX_ROWS // HALO
    full = lambda shape: pl.BlockSpec(shape, lambda i: (0,) * len(shape))
    in_specs = [pl.BlockSpec((MIX_ROWS, 4 * W), lambda i: (i, 0)),
                pl.BlockSpec((MIX_ROWS, 256), lambda i: (i, lb)),
                pl.BlockSpec((HALO, 4 * W), lambda i: (jnp.maximum(i * per - 1, 0), 0)),
                pl.BlockSpec((HALO, 4 * W), lambda i: (jnp.minimum((i + 1) * per, n_halo - 1), 0)),
                pl.BlockSpec((HALO, 256), lambda i: (jnp.maximum(i * per - 1, 0), lb)),
                pl.BlockSpec((HALO, 256), lambda i: (jnp.minimum((i + 1) * per, n_halo - 1), lb)),
                full((3, 4 * W)), full((3, 256)),
                full((2, W)), full((2, 256, W)), full((2, W)), full((2, 256, W)),
                full((1, W)), full((1, W)), full((1, W)), full((W, W))]
    f32_out = jax.ShapeDtypeStruct((rows, W), F32)
    mxu_out = jax.ShapeDtypeStruct((rows, W), BF16)
    return pl.pallas_call(
        functools.partial(_rw_prep_kernel, nb),
        grid=(nb,),
        in_specs=in_specs,
        out_specs=[pl.BlockSpec((MIX_ROWS, W), lambda i: (i, 0))] * 11,
        out_shape=[mxu_out] * 3 + [f32_out] * 4 + [mxu_out] * 4,
        compiler_params=_cparams(("arbitrary",)),
        name="rw_prep",
    )(p, p, p, p, p, p, taps_m, taps_l, w0, w2p, a0, a2p, kkw, ka, rk, bd512)


def _rw_scan_kernel(rev, *refs):
    if rev:
        (r_ref, kk_ref, v_ref, lw_ref, akk_ref, kd_ref, tri_ref, of_ref, bonus_ref, gs_ref,
         lnw_ref, lnb_ref, bd_ref, y_ref, s_ref) = refs
    else:
        r_ref, kk_ref, v_ref, lw_ref, akk_ref, kd_ref, tri_ref, y_ref, s_ref = refs
    C = RW_CHUNK
    HD = RW_HD
    P2 = 2 * HD

    NP = RW_HEADS // 2

    @pl.when(pl.program_id(0) == 0)
    def _():
        s_ref[...] = jnp.zeros_like(s_ref)

    yield
    lw = lw_ref[...]
    cum = _seg_cumsum(lw, tri_ref[...])
    cum_prev = cum - lw

    lane = lax.broadcasted_iota(jnp.int32, (1, P2), 1)
    h0 = lane < HD
    ri = lax.broadcasted_iota(jnp.int32, (P2, P2), 0)
    ci = lax.broadcasted_iota(jnp.int32, (P2, P2), 1)
    same = (ri // C) == (ci // C)
    rt, cs = ri % C, ci % C
    strict = jnp.logical_and(same, (rt < cs) if rev else (rt > cs))
    incl = jnp.logical_and(same, (rt <= cs) if rev else (rt >= cs))
    eye = (ri == ci).astype(F32)

    def stack(x):
        return jnp.concatenate([x, x], axis=0)

    def split_heads(x):
        return jnp.concatenate([jnp.where(h0, x, 0.0), jnp.where(h0, 0.0, x)], axis=0)

    def unstack(x):
        return jnp.where(h0, x[:C], x[C:])

    chunks = list(range(MIX_ROWS // C - 1, -1, -1) if rev else range(MIX_ROWS // C))
    order = [(c, p) for c in chunks for p in range(NP)]

    pre = {}
    for c, p in order:
        sl = slice(c * C, (c + 1) * C)
        pp = slice(p * P2, (p + 1) * P2)
        cum_c = cum[sl, pp]
        cend = cum_c[0:1] if rev else cum_c[C - 1:C]
        e_neg = jnp.exp(-cum_c)
        e_end = jnp.exp(cend - cum_c)
        a_t = kk_ref[sl, pp].astype(F32) * jnp.exp(cum_prev[sl, pp])
        r_t = r_ref[sl, pp].astype(F32) * jnp.exp(cum_c)
        akk = akk_ref[sl, pp].astype(F32)
        kd = kd_ref[sl, pp].astype(F32)
        v = v_ref[sl, pp]
        pre[c, p] = dict(a_t=a_t, r_t=r_t, v=v, kh=kd * e_end, bh=akk * e_end, dec=jnp.exp(cend),
                         lhs=jnp.concatenate([split_heads(a_t), split_heads(r_t)], axis=0),
                         rhs=jnp.concatenate([stack(akk * e_neg), stack(kd * e_neg)], axis=0))
    for w in pre.values():
        g = _mm_f32(w["lhs"], w["rhs"], NT)
        w["m"] = jnp.where(strict, g[:P2, :P2], 0.0)
        w["a_ak"] = jnp.where(strict, g[:P2, P2:], 0.0)
        w["a_rb"] = jnp.where(incl, g[P2:, :P2], 0.0)
        w["a_rk"] = jnp.where(incl, g[P2:, P2:], 0.0)
        w["p"] = eye - w["m"]
    for w in pre.values():
        av = _mm(jnp.concatenate([w["a_ak"], w["a_rk"]], axis=0), stack(w["v"]))
        w["av"], w["rkv"] = av[:P2], av[P2:]
    for w in pre.values():
        w["m"] = _mm_f32(w["m"], w["m"])
    for _ in range(4):
        for w in pre.values():
            pm = _mm_f32(jnp.concatenate([w["p"], w["m"]], axis=0), w["m"])
            w["p"] = w["p"] + pm[:P2]
            w["m"] = pm[P2:]
    for w in pre.values():
        w["p"] = w["p"] + _mm_f32(w["p"], w["m"])
    for w in pre.values():
        w["tt"] = _mm(w["p"], jnp.concatenate([stack(w["a_t"]), w["av"]], axis=1))
        w["z"] = _mm_f32(w["p"], split_heads(w["bh"]), TN)
    for w in pre.values():
        ar = _mm(w["a_rb"], w["tt"])
        w["oa"] = stack(w["r_t"]) - ar[:, :P2]
        w["oc"] = w["rkv"] - ar[:, P2:]
        w["pz"] = _mm_f32(split_heads(w["a_t"]), w["z"], TN)
        w["kz"] = split_heads(w["kh"]) - _mm_f32(w["a_ak"], w["z"], TN)
    for w in pre.values():
        w["q"] = _mm_f32(split_heads(w["v"]), w["kz"], TN)

    s = [s_ref[p] for p in range(NP)]
    for c, p in order:
        w = pre[c, p]
        w["s0"] = s[p]
        s[p] = s[p] * w["dec"] - _mm_f32(s[p], w["pz"]) + w["q"]
    for c, p in order:
        w = pre[c, p]
        o_st = _mm(w["oa"], w["s0"], NT) + w["oc"]
        y_ref[c * C:(c + 1) * C, p * P2:(p + 1) * P2] = unstack(o_st)
    for p in range(NP):
        s_ref[p] = s[p]

    if rev:
        o = of_ref[...] + y_ref[...]
        bd = bd_ref[...]
        mu = _mm2(o, bd) * (1.0 / HD)
        oc = o - mu
        var = _mm2(oc * oc, bd) * (1.0 / HD)
        o = oc * lax.rsqrt(var + RW_GN_EPS) * lnw_ref[...] + lnb_ref[...]
        y_ref[...] = (o + bonus_ref[...]) * gs_ref[...]


def _rw_parts(rev, r, kk, v, lw, akk, kd, extra=()):
    rows = r.shape[0]
    nb = rows // MIX_ROWS
    order = _blk_order(rev, nb)
    P2 = 2 * RW_HD
    W = RW_HEADS * RW_HD
    blk = pl.BlockSpec((MIX_ROWS, W), lambda b: (order(b), 0))
    in_specs = [blk] * 6 + [pl.BlockSpec((MIX_ROWS, MIX_ROWS), lambda b: (0, 0))]
    args = [r, kk, v, lw, akk, kd, _seg_tri(MIX_ROWS, RW_CHUNK, rev)]
    if rev:
        o_fwd, bonus, gs, lnw, lnb, bd512 = extra
        vec = pl.BlockSpec((1, W), lambda b: (0, 0))
        in_specs += [blk, blk, blk, vec, vec, pl.BlockSpec((W, W), lambda b: (0, 0))]
        args += [o_fwd, bonus, gs, lnw, lnb, bd512]
    return dict(body=functools.partial(_rw_scan_kernel, rev), in_specs=in_specs, args=args, out_spec=blk,
                out_shape=jax.ShapeDtypeStruct((rows, W), F32),
                scratch=pltpu.VMEM((RW_HEADS // 2, P2, P2), F32))


def _qk_prep_kernel(nq_ref, nk_ref, nv_ref, wq_ref, wkv_ref, nqn_ref, nkn_ref, wqn_ref, wkn_ref,
                    cos_ref, sin_ref, bd512_ref, bd128_ref,
                    naq_o, nak_o, nav_o, waq_o, wak_o, wav_o):
    bd512 = bd512_ref[...]
    bd128 = bd128_ref[...]

    def hnorm(x, g, bd, hd):
        return x * lax.rsqrt(_mm2(x * x, bd) * (1.0 / hd) + EPS) * g

    def rope(x, cos, sin):
        lane = lax.broadcasted_iota(jnp.int32, x.shape, 1)
        w = x.shape[1]
        partner = jnp.where((lane & 31) < 16, pltpu.roll(x, w - 16, axis=1), pltpu.roll(x, 16, axis=1))
        return x * cos + partner * sin

    def dup_heads(o_ref, x):
        lower = lax.broadcasted_iota(jnp.int32, x.shape, 1) < WA_HD
        swapped = pltpu.roll(x, WA_HD, axis=1)
        o_ref[0] = jnp.where(lower, x, swapped).astype(o_ref.dtype)
        o_ref[1] = jnp.where(lower, swapped, x).astype(o_ref.dtype)

    f32 = lambda ref: ref[...].astype(F32)
    naq_o[...] = (hnorm(f32(nq_ref), nqn_ref[...], bd512, NA_HD) * NA_HD ** -0.5).astype(BF16)
    nak_o[...] = hnorm(f32(nk_ref), nkn_ref[...], bd512, NA_HD).astype(BF16)
    nav_o[...] = nv_ref[...]
    cos = cos_ref[...]
    sin = sin_ref[...]
    wq = hnorm(f32(wq_ref), wqn_ref[...], bd512, WA_HD) * WA_HD ** -0.5
    wq = rope(wq, jnp.concatenate([cos] * 4, axis=1), jnp.concatenate([sin] * 4, axis=1))
    waq_o[...] = wq.astype(BF16)
    kv = f32(wkv_ref)
    dup_heads(wak_o, rope(hnorm(kv[:, :128], wkn_ref[...], bd128, WA_HD), cos, sin))
    dup_heads(wav_o, kv[:, 128:])


def _qk_prep(p, nqn, nkn, wqn, wkn, cos, sin, bd512, bd128):
    rows = p.shape[0]
    nb = rows // MIX_ROWS
    nab = OFF_NA // 512
    full = lambda shape: pl.BlockSpec(shape, lambda i: (0,) * len(shape))
    in_specs = [pl.BlockSpec((MIX_ROWS, 512), lambda i: (i, nab)),
                pl.BlockSpec((MIX_ROWS, 512), lambda i: (i, nab + 1)),
                pl.BlockSpec((MIX_ROWS, 512), lambda i: (i, nab + 2)),
                pl.BlockSpec((MIX_ROWS, 512), lambda i: (i, OFF_WA // 512)),
                pl.BlockSpec((MIX_ROWS, 256), lambda i: (i, (OFF_WA + 512) // 256)),
                full((1, 512)), full((1, 512)), full((1, 512)), full((1, 128)),
                pl.BlockSpec((MIX_ROWS, 128), lambda i: (i, 0)),
                pl.BlockSpec((MIX_ROWS, 128), lambda i: (i, 0)),
                full((512, 512)), full((128, 128))]
    flat = pl.BlockSpec((MIX_ROWS, 512), lambda i: (i, 0))
    flat_shape = jax.ShapeDtypeStruct((rows, 512), BF16)
    dup = pl.BlockSpec((WA_KV_HEADS, MIX_ROWS, 128), lambda i: (0, i, 0))
    dup_shape = jax.ShapeDtypeStruct((WA_KV_HEADS, rows, 128), BF16)
    return pl.pallas_call(
        _qk_prep_kernel,
        grid=(nb,),
        in_specs=in_specs,
        out_specs=[flat, flat, flat, flat, dup, dup],
        out_shape=[flat_shape, flat_shape, flat_shape, flat_shape, dup_shape, dup_shape],
        compiler_params=_cparams(("arbitrary",)),
        name="qk_prep",
    )(p, p, p, p, p, nqn, nkn, wqn, wkn, cos, sin, bd512, bd128)


def _na_bias_layout():
    wr, G, KR = NA_WIN_R, NA_GROUP, NA_KEY_ROWS
    off = (0, -(wr // 2), -(KR - G))
    out = {}
    for kind in range(3):
        for r in range(G):
            wstart = (0, r, KR - wr)[kind]
            for a in range(KR):
                inside = wstart <= a < wstart + wr
                out[kind, r, a] = (off[kind] + a - r + wr - 1) if inside else None
    return out


def _na_kernel(lc, nrows, q_ref, k_ref, v_ref, t_ref, o_ref, bias_s):
    W = GRID_W
    HD = NA_HD
    nq = NA_GROUP * W
    nk = NA_KEY_ROWS * W
    ngroups = nrows // NA_GROUP
    kc = k_ref[0:lc, :]
    vc = v_ref[0:lc, :]
    half = [lax.broadcasted_iota(jnp.int32, (1, 2 * HD), 1) // HD == hh for hh in range(2)]

    bias_s[:, :, :, 0:lc] = jnp.zeros((2, 3, nq, lc), F32)
    for (kind, r, a), ro in _na_bias_layout().items():
        c0 = lc + a * W
        for hh in range(2):
            tile = jnp.full((W, W), NEG, F32) if ro is None else t_ref[hh, ro, :, (a % 2) * W:(a % 2 + 1) * W]
            bias_s[hh, kind, r * W:(r + 1) * W, c0:c0 + W] = tile

    def attend(q, keys, vals, bias):
        out = None
        for hh in range(2):
            s = _mm(jnp.where(half[hh], q, 0), keys, NT)
            if bias is not None:
                s = s + bias(hh)
            p = jnp.exp(s - jnp.max(s, axis=-1, keepdims=True))
            o = _mm(p, jnp.where(half[hh], vals, 0)) / jnp.sum(p, axis=-1, keepdims=True)
            out = o if out is None else out + o
        return out

    o_ref[0:lc, :] = attend(q_ref[0:lc, :], kc, vc, None)

    def group(gi, carry):
        base = jnp.clip(gi * NA_GROUP - NA_WIN_R // 2, 0, nrows - NA_KEY_ROWS)
        kind = jnp.where(gi == 0, 0, jnp.where(gi == ngroups - 1, 2, 1))
        q0 = pl.multiple_of(lc + gi * nq, nq)
        k0 = pl.multiple_of(lc + base * W, W)
        keys = jnp.concatenate([kc, k_ref[pl.ds(k0, nk), :]], axis=0)
        vals = jnp.concatenate([vc, v_ref[pl.ds(k0, nk), :]], axis=0)
        o_ref[pl.ds(q0, nq), :] = attend(q_ref[pl.ds(q0, nq), :], keys, vals, lambda hh: bias_s[hh, kind])
        return carry

    lax.fori_loop(0, ngroups, group, 0, unroll=4)


def _na(q, k, v, t2, lc):
    rows = q.shape[0]
    nrows = (rows - lc) // GRID_W
    pblk = pl.BlockSpec((rows, 2 * NA_HD), lambda p: (0, p))
    return pl.pallas_call(
        functools.partial(_na_kernel, lc, nrows),
        grid=(NA_HEADS // 2,),
        in_specs=[pblk, pblk, pblk, pl.BlockSpec((2,) + t2.shape[1:], lambda p: (p, 0, 0, 0))],
        out_specs=pblk,
        out_shape=jax.ShapeDtypeStruct((rows, NA_HEADS * NA_HD), F32),
        scratch_shapes=[pltpu.VMEM((2, 3, NA_GROUP * GRID_W, lc + NA_KEY_ROWS * GRID_W), F32)],
        compiler_params=_cparams(("arbitrary",)),
        name="na_attn",
    )(q, k, v, t2)


def _wa_kernel(lc, t_len, q_ref, k_ref, v_ref, sink_ref, o_ref):
    G = WA_HEADS // WA_KV_HEADS
    B = WA_WINDOW
    nband = 3 * B
    HD = WA_HD
    b = pl.program_id(1)
    kc = k_ref[0:lc, :]
    vc = v_ref[0:lc, :]
    sink = sink_ref[...]
    lower = lax.broadcasted_iota(jnp.int32, (1, 2 * HD), 1) < HD

    def stacked_queries(r0, n):
        parts = []
        for g in range(G):
            qp = q_ref[r0:r0 + n, (g // 2) * 2 * HD:(g // 2 + 1) * 2 * HD]
            parts.append(jnp.where(lower if g % 2 == 0 else jnp.logical_not(lower), qp, 0))
        return jnp.concatenate(parts, axis=0)

    def store(r0, n, o):
        for gp in range(G // 2):
            o_ref[r0:r0 + n, gp * 2 * HD:(gp + 1) * 2 * HD] = jnp.where(
                lower, o[2 * gp * n:(2 * gp + 1) * n], o[(2 * gp + 1) * n:(2 * gp + 2) * n])

    @pl.when(b == 0)
    def _():
        sk = jnp.concatenate([jnp.broadcast_to(sink[g * B:g * B + 1, :], (lc, 1)) for g in range(G)], axis=0)
        s = _mm(stacked_queries(0, lc), kc, NT)
        m = jnp.maximum(jnp.max(s, axis=-1, keepdims=True), sk)
        e = jnp.exp(s - m)
        store(0, lc, _mm(e, vc) / (jnp.sum(e, axis=-1, keepdims=True) + jnp.exp(sk - m)))

    @pl.when(b > 0)
    def _():
        qoff = lax.broadcasted_iota(jnp.int32, (G * B, lc + nband), 0) & (B - 1)
        koff = lax.broadcasted_iota(jnp.int32, (G * B, lc + nband), 1) - lc
        for j in range(MIX_ROWS // B):
            n = (b - 1) * (MIX_ROWS // B) + j
            start = jnp.clip((n - 1) * B, 0, t_len - nband)
            k0 = pl.multiple_of(lc + start, B)
            keys = jnp.concatenate([kc, k_ref[pl.ds(k0, nband), :]], axis=0)
            vals = jnp.concatenate([vc, v_ref[pl.ds(k0, nband), :]], axis=0)
            valid = jnp.logical_or(koff < 0, jnp.abs((n * B + qoff) - (start + koff)) <= WA_WINDOW)
            s = jnp.where(valid, _mm(stacked_queries(j * B, B), keys, NT), NEG)
            m = jnp.maximum(jnp.max(s, axis=-1, keepdims=True), sink)
            p = jnp.exp(s - m)
            store(j * B, B, _mm(p, vals) / (jnp.sum(p, axis=-1, keepdims=True) + jnp.exp(sink - m)))


def _wa(q, k, v, sink_col, lc):
    rows = q.shape[0]
    G = WA_HEADS // WA_KV_HEADS
    kvblk = pl.BlockSpec((None, rows, 2 * WA_HD), lambda h, b: (h, 0, 0))
    qblk = pl.BlockSpec((MIX_ROWS, G * WA_HD), lambda h, b: (b, h))
    return pl.pallas_call(
        functools.partial(_wa_kernel, lc, rows - lc),
        grid=(WA_KV_HEADS, rows // MIX_ROWS),
        in_specs=[qblk, kvblk, kvblk,
                  pl.BlockSpec((None, G * WA_WINDOW, 1), lambda h, b: (h, 0, 0))],
        out_specs=qblk,
        out_shape=jax.ShapeDtypeStruct((rows, WA_HEADS * WA_HD), F32),
        compiler_params=_cparams(("arbitrary", "arbitrary")),
        name="wa_attn",
    )(q, k, v, sink_col)


def _block_diag(n, blk):
    idx = np.arange(n) // blk
    return jnp.asarray(idx[:, None] == idx[None, :], dtype=BF16)


def _na_bias_table(rpb):
    W, wc = GRID_W, NA_WIN_C
    j = np.arange(W)
    cstart = np.clip(j - wc // 2, 0, W - wc)
    cabs = np.arange(W)
    inwin = (cabs[None, :] >= cstart[:, None]) & (cabs[None, :] < cstart[:, None] + wc)
    cb = cabs[None, :] - j[:, None] + wc - 1
    nh, nro, nco = rpb.shape
    onehot = (cb[None] == np.arange(nco)[:, None, None]) & inwin[None]
    t = jnp.dot(rpb.reshape(nh * nro, nco), jnp.asarray(onehot.reshape(nco, W * W), F32),
                precision=lax.Precision.HIGHEST).reshape(nh, nro, W, W)
    t = t + jnp.asarray(np.where(inwin, 0.0, NEG), F32)
    return jnp.concatenate([t, t], axis=-1)


def _rope_tables(lc, t_len):
    pos = np.arange(t_len)
    n = WA_HD // 4
    inv = ROPE_BASE ** (-np.arange(n, dtype=np.float64) / n)
    ang_r = (pos // GRID_W)[:, None] * inv[None, :]
    ang_c = (pos % GRID_W)[:, None] * inv[None, :]
    cos = np.concatenate([np.cos(ang_r)] * 2 + [np.cos(ang_c)] * 2, axis=1)
    sin = np.concatenate([-np.sin(ang_r), np.sin(ang_r), -np.sin(ang_c), np.sin(ang_c)], axis=1)
    cos = np.concatenate([np.ones((lc, WA_HD)), cos], axis=0)
    sin = np.concatenate([np.zeros((lc, WA_HD)), sin], axis=0)
    return (jnp.asarray(np.tile(cos, (1, 2)), dtype=F32), jnp.asarray(np.tile(sin, (1, 2)), dtype=F32))


def _cast_kernel(x_ref, o_ref):
    o_ref[...] = x_ref[...].astype(o_ref.dtype)


def _cast_bf16(w, layer, bk, bn, src_col_block=None):
    _, k, n = w.shape
    col = src_col_block or (lambda j: j)
    return pl.pallas_call(
        _cast_kernel,
        grid=(k // bk, n // bn),
        in_specs=[pl.BlockSpec((None, bk, bn), lambda i, j: (layer, i, col(j)))],
        out_specs=pl.BlockSpec((bk, bn), lambda i, j: (i, j)),
        out_shape=jax.ShapeDtypeStruct((k, n), BF16),
        compiler_params=_cparams(("arbitrary", "arbitrary")),
        name="cast_bf16",
    )(w)


W_IN_BLK = 256
_W_IN_SRC = np.concatenate([np.arange(10, 18), np.arange(0, 10), np.arange(19, 25), np.arange(28, 60),
                            np.arange(25, 28), np.arange(18, 19)])


def _w_in_src_block(j):
    starts = (0, 8, 18, 24, 56, 59)
    src = j + int(_W_IN_SRC[0])
    for s in starts[1:]:
        src = jnp.where(j >= s, j + int(_W_IN_SRC[s]) - s, src)
    return src


def _lora_pad(w, row0):
    out = jnp.zeros((2, 256, w.shape[-1]), F32)
    for d in range(2):
        out = out.at[d, row0 + 64 * d:row0 + 64 * (d + 1)].set(w[d])
    return out


def _token_mixing(p, lc, layer, hg_lb, hg_norm, rw_shift, rw_w0, rw_w2, rw_a0, rw_a2, rw_kk, rw_ka, rw_rk,
                  rw_ln_w, rw_ln_b, na_qn, na_kn, na_rpb, wa_qn, wa_kn, wa_sink, tables):
    rows = p.shape[0]
    bd512, bd128, cos, sin = tables
    cum = jnp.cumsum(jax.nn.softmax(hg_lb.astype(F32), axis=1), axis=1)
    lbs = cum[:, layer] - cum[:, 0]
    prep = _rw_prep(p, rw_shift[:, :2048], rw_shift[:, 2048:], rw_w0, _lora_pad(rw_w2, 0), rw_a0,
                    _lora_pad(rw_a2, 128), rw_kk[None], rw_ka[None], rw_rk.reshape(1, -1), bd512)
    r, kk, v, gs, bonus, lw0, lw1, akk0, akk1, kd0, kd1 = prep
    oa_f, ob_f = _run_scans([_gla_parts(p, lbs[0:1], False), _rw_parts(False, r, kk, v, lw0, akk0, kd0)],
                            rows, "scan_fwd")
    y_a, y_b = _run_scans([_gla_parts(p, lbs[1:2], True, oa_f, hg_norm[None]),
                           _rw_parts(True, r, kk, v, lw1, akk1, kd1,
                                     (ob_f, bonus, gs, rw_ln_w[None], rw_ln_b[None], bd512))],
                          rows, "scan_rev")
    tile = lambda g, n: jnp.tile(g, n)[None]
    naq, nak, nav, waq, wak, wav = _qk_prep(p, tile(na_qn, 8), tile(na_kn, 8), tile(wa_qn, 8),
                                            tile(wa_kn, 2), cos, sin, bd512, bd128)
    y_c = _na(naq, nak, nav, _na_bias_table(na_rpb), lc)
    G = WA_HEADS // WA_KV_HEADS
    sink_col = jnp.repeat(wa_sink.reshape(WA_KV_HEADS, G), WA_WINDOW, axis=1)[..., None]
    y_d = _wa(waq, wak, wav, sink_col, lc)
    return y_a, y_b, y_c, y_d


def kernel(x, c, ctx, c_ctx, ada_w, ada_b, norm_ffn1, norm_mix, norm_ffn2, ffn1_wi, ffn1_wo, ffn2_wi, ffn2_wo, w_in, hg_lb, hg_norm, rw_shift, rw_w0, rw_w2, rw_a0, rw_a2, rw_kk, rw_ka, rw_rk, rw_ln_w, rw_ln_b, na_qn, na_kn, na_rpb, wa_qn, wa_kn, wa_sink, w_branch, w_out):
    assert x.shape[0] == 1 and ctx.shape[1] == MIX_ROWS
    lc = ctx.shape[1]
    t_len = x.shape[1]
    xa = jnp.concatenate([ctx[0], x[0]], axis=0)
    cc_t = jnp.stack([c[0], c_ctx], axis=1)
    mods = _ada_mods(cc_t, ada_w, ada_b).reshape(DEPTH, 2, N_MOD, D)
    tables = (_block_diag(512, 64), _block_diag(128, 64)) + _rope_tables(lc, t_len)
    half = W_IN_BLK // 2
    w_in_src = lambda c: 2 * _w_in_src_block(c // 2) + c % 2
    wi1, wo1 = _cast_bf16(ffn1_wi, 0, D, 1024), _cast_bf16(ffn1_wo, 0, 512, D)
    for l in range(DEPTH):
        m = mods[l]
        xa, w_in_l = _ffn(xa, m[:, 0:3], norm_ffn1[l][None], wi1, wo1, lc,
                          jobs=[_cast_job(w_in, l, D, half, w_in_src)])
        jobs = [_cast_job(w_branch.reshape(DEPTH, N_BRANCH * BR_W, D), l, 256, 256), _cast_job(w_out, l, 256, 256),
                _cast_job(ffn2_wi, l, D, half), _cast_job(ffn2_wo, l, 64, D)]
        if l + 1 < DEPTH:
            jobs += [_cast_job(ffn1_wi, l + 1, D, half), _cast_job(ffn1_wo, l + 1, 64, D)]
        p, wb, w_out_l, wi2, wo2, *nxt = _win(xa, m[:, 3:6], norm_mix[l][None], w_in_l, lc, jobs=jobs)
        if nxt:
            wi1, wo1 = nxt
        yb = _token_mixing(p, lc, l, hg_lb, hg_norm[l], rw_shift[l], rw_w0[l], rw_w2[l], rw_a0[l], rw_a2[l],
                           rw_kk[l], rw_ka[l], rw_rk[l], rw_ln_w[l], rw_ln_b[l], na_qn[l], na_kn[l],
                           na_rpb[l], wa_qn[l], wa_kn[l], wa_sink[l], tables)
        xa = _merge(xa, m[:, 3:6], yb, p, wb.reshape(N_BRANCH, BR_W, D), w_out_l, lc)
        xa, = _ffn(xa, m[:, 6:9], norm_ffn2[l][None], wi2, wo2, lc)
    return xa[lc:][None]
```
